```python
import math
import jax, jax.numpy as jnp
from jax import lax
import numpy as np

D_MODEL = 1024
BATCH = 2
SEQ = 8192
DEPTH = 2
DEC_BATCH = 32
DEC_SEQ = 4
PAST_LEN = 8192
PAGE_SIZE = 128

HEAD_DIM = 64
N_ATTN_HEADS = D_MODEL // 128
ATTN_WIDTH = N_ATTN_HEADS * HEAD_DIM
CONV_DIM = D_MODEL - ATTN_WIDTH
CONV_K = 3
MIX_WIDTH = ATTN_WIDTH + CONV_DIM
PROJ_WIDTH = 3 * ATTN_WIDTH + 3 * CONV_DIM
SPLITS = (ATTN_WIDTH, 2 * ATTN_WIDTH, 3 * ATTN_WIDTH,
          3 * ATTN_WIDTH + CONV_DIM, 3 * ATTN_WIDTH + 2 * CONV_DIM)
DILATED_PATTERNS = ((128, 1), (512, 4), (2048, 16))
MAX_WINDOW = 2048
SEG_KEYS = 128
Q_BLOCK = 128
N_BUCKETS = 32
MAX_DISTANCE = 2048
N_GROUPS = 4
EXPERTS_PER_GROUP = 8
N_EXPERTS = N_GROUPS * EXPERTS_PER_GROUP
TOP_K = 2
D_EXPERT = D_MODEL // 4
EPS = 1e-6
NEG = -1e30

kernel_name = "hymba_conv_dilated_attn_hmoe_step"


def rms_norm(x, g):
    xf = x.astype(jnp.float32)
    y = xf * lax.rsqrt(jnp.mean(xf * xf, axis=-1, keepdims=True) + EPS)
    return (y * g.astype(jnp.float32)).astype(x.dtype)


def rel_bucket(dist):
    max_exact = N_BUCKETS // 2
    d_f = jnp.maximum(dist, 1).astype(jnp.float32)
    large = max_exact + (jnp.log(d_f / max_exact) / math.log(MAX_DISTANCE / max_exact)
                         * (N_BUCKETS - max_exact)).astype(jnp.int32)
    large = jnp.minimum(large, N_BUCKETS - 1)
    return jnp.where(dist < max_exact, dist, large)


def split_projection(proj):
    return jnp.split(proj, SPLITS, axis=-1)


def short_conv(u_ext, conv_w, n):
    y = conv_w[0] * u_ext[:, 0:n]
    for j in range(1, CONV_K):
        y = y + conv_w[j] * u_ext[:, j:j + n]
    return y


def merge_branches(branches):
    lse = jnp.stack([b[1] for b in branches], axis=0)
    w = jax.nn.softmax(lse, axis=0)
    out = w[0][..., None] * branches[0][0]
    for i in range(1, len(branches)):
        out = out + w[i][..., None] * branches[i][0]
    return out


def dilated_branch_prompt(q, k, v, rel_bias, dil):
    B, S, H, C = q.shape
    L = S // dil
    nb = -(-L // Q_BLOCK)
    Lp = nb * Q_BLOCK
    scale = 1.0 / math.sqrt(C)

    def to_sub(t):
        t = t.reshape(B, L, dil, H, C).transpose(0, 2, 1, 3, 4)
        return jnp.pad(t, ((0, 0), (0, 0), (0, Lp - L), (0, 0), (0, 0)))

    qs, ks, vs = to_sub(q), to_sub(k), to_sub(v)
    qb = qs.reshape(B, dil, nb, Q_BLOCK, H, C)

    def band(t):
        tp = jnp.pad(t, ((0, 0), (0, 0), (Q_BLOCK, 0), (0, 0), (0, 0)))
        prev = tp[:, :, :Lp].reshape(B, dil, nb, Q_BLOCK, H, C)
        cur = t.reshape(B, dil, nb, Q_BLOCK, H, C)
        return jnp.concatenate([prev, cur], axis=3)

    kb, vb = band(ks), band(vs)
    logits = jnp.einsum('brnqhc,brnkhc->brnhqk', qb, kb,
                        preferred_element_type=jnp.float32) * scale
    q_loc = jnp.arange(Q_BLOCK)[:, None]
    k_loc = jnp.arange(2 * Q_BLOCK)[None, :]
    steps = q_loc + Q_BLOCK - k_loc
    in_band = (steps >= 0) & (steps <= SEG_KEYS)
    key_idx = jnp.arange(nb)[:, None, None] * Q_BLOCK + k_loc[None] - Q_BLOCK
    valid = in_band[None] & (key_idx >= 0)
    bias = rel_bias[rel_bucket(jnp.clip(steps, 0, SEG_KEYS) * dil)].astype(jnp.float32)
    logits = logits + bias.transpose(2, 0, 1)[None, None, None]
    logits = jnp.where(valid[None, None, :, None], logits, NEG)
    m = jnp.max(logits, axis=-1, keepdims=True)
    p = jnp.exp(logits - m)
    den = jnp.sum(p, axis=-1, keepdims=True)
    out = jnp.einsum('brnhqk,brnkhc->brnqhc', p / den, vb.astype(jnp.float32))
    lse = (m + jnp.log(den))[..., 0]
    out = out.reshape(B, dil, Lp, H, C)[:, :, :L].transpose(0, 2, 1, 3, 4).reshape(B, S, H, C)
    lse = lse.transpose(0, 1, 2, 4, 3).reshape(B, dil, Lp, H)[:, :, :L]
    lse = lse.transpose(0, 2, 1, 3).reshape(B, S, H)
    return out, lse


def dilated_branch_sample(q, k_all, v_all, rel_bias, dil, w_buf):
    T = q.shape[1]
    C = q.shape[-1]
    scale = 1.0 / math.sqrt(C)
    steps = jnp.arange(SEG_KEYS + 1)
    idx = w_buf + jnp.arange(T)[:, None] - steps[None, :] * dil
    valid = idx >= 0
    idx = jnp.maximum(idx, 0)
    kg = k_all[:, idx]
    vg = v_all[:, idx]
    logits = jnp.einsum('bthc,btjhc->bthj', q, kg,
                        preferred_element_type=jnp.float32) * scale
    bias = rel_bias[rel_bucket(steps * dil)].astype(jnp.float32)
    logits = logits + bias.T[None, None]
    logits = jnp.where(valid[None, :, None, :], logits, NEG)
    m = jnp.max(logits, axis=-1, keepdims=True)
    p = jnp.exp(logits - m)
    den = jnp.sum(p, axis=-1, keepdims=True)
    out = jnp.einsum('bthj,btjhc->bthc', p / den, vg.astype(jnp.float32))
    lse = (m + jnp.log(den))[..., 0]
    return out, lse


def token_mix_prompt(xn, w_in, conv_w, w_out, rel_bias):
    B, S, _ = xn.shape
    q, k, v, gb, gc, h = split_projection(xn @ w_in)
    q = q.reshape(B, S, N_ATTN_HEADS, HEAD_DIM)
    k = k.reshape(B, S, N_ATTN_HEADS, HEAD_DIM)
    v = v.reshape(B, S, N_ATTN_HEADS, HEAD_DIM)
    branches = [dilated_branch_prompt(q, k, v, rel_bias, dil) for _, dil in DILATED_PATTERNS]
    attn = merge_branches(branches).reshape(B, S, ATTN_WIDTH).astype(xn.dtype)
    u = gc * h
    u_ext = jnp.pad(u, ((0, 0), (CONV_K - 1, 0), (0, 0)))
    conv = gb * short_conv(u_ext, conv_w, S)
    y = jnp.concatenate([attn, conv], axis=-1) @ w_out
    w_keep = min(MAX_WINDOW, S)
    return y, k[:, S - w_keep:], v[:, S - w_keep:], u_ext[:, S:]


def token_mix_sample(xn, cache_k, cache_v, conv_state, w_in, conv_w, w_out, rel_bias):
    DB, T, _ = xn.shape
    w_buf = cache_k.shape[1]
    q, k, v, gb, gc, h = split_projection(xn @ w_in)
    q = q.reshape(DB, T, N_ATTN_HEADS, HEAD_DIM)
    k = k.reshape(DB, T, N_ATTN_HEADS, HEAD_DIM)
    v = v.reshape(DB, T, N_ATTN_HEADS, HEAD_DIM)
    k_all = jnp.concatenate([cache_k, k], axis=1)
    v_all = jnp.concatenate([cache_v, v], axis=1)
    branches = [dilated_branch_sample(q, k_all, v_all, rel_bias, dil, w_buf)
                for _, dil in DILATED_PATTERNS]
    attn = merge_branches(branches).reshape(DB, T, ATTN_WIDTH).astype(xn.dtype)
    u = gc * h
    u_ext = jnp.concatenate([conv_state, u], axis=1)
    conv = gb * short_conv(u_ext, conv_w, T)
    y = jnp.concatenate([attn, conv], axis=-1) @ w_out
    return y, k, v, u_ext[:, T:]


def hier_moe(xn, w_rg, w_re, w_gate, w_up, w_down):
    shp = xn.shape
    xt = xn.reshape(-1, shp[-1])
    n = xt.shape[0]
    g_logits = (xt @ w_rg).astype(jnp.float32)
    g_prob = jax.nn.softmax(g_logits, axis=-1)
    grp = jnp.argmax(g_logits, axis=-1)
    p_grp = jnp.take_along_axis(g_prob, grp[:, None], axis=1)
    e_logits = (xt @ w_re).astype(jnp.float32).reshape(n, N_GROUPS, EXPERTS_PER_GROUP)
    e_in = jnp.take_along_axis(e_logits, grp[:, None, None], axis=1)[:, 0]
    top_p, top_i = lax.top_k(jax.nn.softmax(e_in, axis=-1), TOP_K)
    top_p = top_p / jnp.sum(top_p, axis=-1, keepdims=True)
    expert_id = grp[:, None] * EXPERTS_PER_GROUP + top_i
    gates = p_grp * top_p
    combine = jnp.sum(jax.nn.one_hot(expert_id, N_EXPERTS, dtype=jnp.float32)
                      * gates[..., None], axis=1).astype(xt.dtype)
    y = jnp.zeros_like(xt)
    for e in range(N_EXPERTS):
        hdn = jax.nn.silu(xt @ w_gate[e]) * (xt @ w_up[e])
        y = y + (combine[:, e:e + 1] * hdn) @ w_down[e]
    return y.reshape(shp)


def setup_inputs(seed: int = 0) -> dict:
    key = jax.random.key(seed)
    ks = jax.random.split(key, 18)
    w_buf = min(MAX_WINDOW, PAST_LEN)
    nrm = jax.random.normal
    f32 = jnp.float32
    return {
        "x_prompt": nrm(ks[0], (BATCH, SEQ, D_MODEL), f32),
        "x_sample": nrm(ks[1], (DEC_BATCH, DEC_SEQ, D_MODEL), f32),
        "cache_k": nrm(ks[2], (DEPTH, DEC_BATCH, w_buf, N_ATTN_HEADS, HEAD_DIM), f32),
        "cache_v": nrm(ks[3], (DEPTH, DEC_BATCH, w_buf, N_ATTN_HEADS, HEAD_DIM), f32),
        "state_conv": nrm(ks[4], (DEPTH, DEC_BATCH, CONV_K - 1, CONV_DIM), f32),
        "rel_bias": 0.5 * nrm(ks[5], (N_BUCKETS, N_ATTN_HEADS), f32),
        "norm_mix": 1.0 + 0.02 * nrm(ks[6], (DEPTH, D_MODEL), f32),
        "norm_ffn": 1.0 + 0.02 * nrm(ks[7], (DEPTH, D_MODEL), f32),
        "norm_final": 1.0 + 0.02 * nrm(ks[8], (D_MODEL,), f32),
        "w_in": nrm(ks[9], (DEPTH, D_MODEL, PROJ_WIDTH), f32) * D_MODEL ** -0.5,
        "conv_w": nrm(ks[10], (DEPTH, CONV_K, CONV_DIM), f32) * CONV_K ** -0.5,
        "w_out": nrm(ks[11], (DEPTH, MIX_WIDTH, D_MODEL), f32) * MIX_WIDTH ** -0.5,
        "w_router_group": nrm(ks[12], (DEPTH, D_MODEL, N_GROUPS), f32) * D_MODEL ** -0.5,
        "w_router_expert": nrm(ks[13], (DEPTH, D_MODEL, N_EXPERTS), f32) * D_MODEL ** -0.5,
        "w_gate": nrm(ks[14], (DEPTH, N_EXPERTS, D_MODEL, D_EXPERT), f32) * D_MODEL ** -0.5,
        "w_up": nrm(ks[15], (DEPTH, N_EXPERTS, D_MODEL, D_EXPERT), f32) * D_MODEL ** -0.5,
        "w_down": nrm(ks[16], (DEPTH, N_EXPERTS, D_EXPERT, D_MODEL), f32) * D_EXPERT ** -0.5,
    }


def reference(x_prompt, x_sample, cache_k, cache_v, state_conv, rel_bias, norm_mix, norm_ffn,
              norm_final, w_in, conv_w, w_out, w_router_group, w_router_expert, w_gate, w_up,
              w_down):
    hp, hs = x_prompt, x_sample
    pk, pv, pc, sk, sv, sc = [], [], [], [], [], []
    for l in range(DEPTH):
        y, k_w, v_w, c_w = token_mix_prompt(rms_norm(hp, norm_mix[l]), w_in[l], conv_w[l],
                                            w_out[l], rel_bias)
        hp = hp + y
        hp = hp + hier_moe(rms_norm(hp, norm_ffn[l]), w_router_group[l], w_router_expert[l],
                           w_gate[l], w_up[l], w_down[l])
        pk.append(k_w); pv.append(v_w); pc.append(c_w)
        y, k_n, v_n, c_n = token_mix_sample(rms_norm(hs, norm_mix[l]), cache_k[l], cache_v[l],
                                            state_conv[l], w_in[l], conv_w[l], w_out[l], rel_bias)
        hs = hs + y
        hs = hs + hier_moe(rms_norm(hs, norm_ffn[l]), w_router_group[l], w_router_expert[l],
                           w_gate[l], w_up[l], w_down[l])
        sk.append(k_n); sv.append(v_n); sc.append(c_n)
    y_prompt = rms_norm(hp, norm_final)
    y_sample = rms_norm(hs, norm_final)
    return (y_prompt, y_sample, jnp.stack(pk), jnp.stack(pv), jnp.stack(pc),
            jnp.stack(sk), jnp.stack(sv), jnp.stack(sc))
```

```python
import functools
import math

import jax
import jax.numpy as jnp
from jax import lax
from jax.experimental import pallas as pl
from jax.experimental.pallas import tpu as pltpu

F32 = jnp.float32
BF16 = jnp.bfloat16

D_MODEL = 1024
N_HEADS = 8
HEAD_DIM = 64
ATTN_WIDTH = N_HEADS * HEAD_DIM
CONV_DIM = D_MODEL - ATTN_WIDTH
CONV_K = 3
PROJ_WIDTH = 3 * ATTN_WIDTH + 3 * CONV_DIM
DILATIONS = (1, 4, 16)
SEG_KEYS = 128
Q_BLOCK = 128
MAX_WINDOW = 2048
N_BUCKETS = 32
MAX_DISTANCE = 2048
N_GROUPS = 4
EXPERTS_PER_GROUP = 8
N_EXPERTS = N_GROUPS * EXPERTS_PER_GROUP
D_EXPERT = D_MODEL // 4
EPS = 1e-6
NEG = -1e30
SCALE = 1.0 / math.sqrt(HEAD_DIM)

LANES = 128
SUBLANES = 8
N_SLABS = ATTN_WIDTH // LANES
VMEM_LIMIT = 48 * 1024 * 1024


def _params(*sem):
    return pltpu.CompilerParams(dimension_semantics=sem, vmem_limit_bytes=VMEM_LIMIT)


def _rms(x, g):
    return x * lax.rsqrt(jnp.mean(x * x, axis=-1, keepdims=True) + EPS) * g


def _rel_bucket(dist):
    max_exact = N_BUCKETS // 2
    d_f = jnp.maximum(dist, 1).astype(F32)
    large = max_exact + (jnp.log(d_f / max_exact) / math.log(MAX_DISTANCE / max_exact)
                         * (N_BUCKETS - max_exact)).astype(jnp.int32)
    large = jnp.minimum(large, N_BUCKETS - 1)
    return jnp.where(dist < max_exact, dist, large)


def _in_proj_kernel(x_ref, g_ref, w_ref, cw_ref, q_ref, k_ref, v_ref, conv_ref, ulast_ref,
                    ext_ref, *, tm, tiles_per_seq):
    i = pl.program_id(0)
    xn = _rms(x_ref[...], g_ref[...])
    proj = jnp.dot(xn.astype(BF16), w_ref[...], preferred_element_type=F32)
    for s in range(N_SLABS):
        q_ref[s] = proj[:, s * LANES:(s + 1) * LANES]
        k_ref[s] = proj[:, ATTN_WIDTH + s * LANES:ATTN_WIDTH + (s + 1) * LANES]
        v_ref[s] = proj[:, 2 * ATTN_WIDTH + s * LANES:2 * ATTN_WIDTH + (s + 1) * LANES]
    c0 = 3 * ATTN_WIDTH
    gb = proj[:, c0:c0 + CONV_DIM]
    u = proj[:, c0 + CONV_DIM:c0 + 2 * CONV_DIM] * proj[:, c0 + 2 * CONV_DIM:c0 + 3 * CONV_DIM]

    @pl.when(i % tiles_per_seq == 0)
    def _():
        ext_ref[0:SUBLANES, :] = jnp.zeros((SUBLANES, CONV_DIM), F32)

    ext_ref[SUBLANES:SUBLANES + tm, :] = u
    u1 = ext_ref[SUBLANES - 1:SUBLANES - 1 + tm, :]
    u2 = ext_ref[SUBLANES - 2:SUBLANES - 2 + tm, :]
    cw = cw_ref[...]
    conv_ref[...] = gb * (cw[0:1] * u2 + cw[1:2] * u1 + cw[2:3] * u)
    ulast_ref[0] = ext_ref[tm + SUBLANES - 2:tm + SUBLANES, :]
    ext_ref[0:SUBLANES, :] = ext_ref[tm:tm + SUBLANES, :]


def _in_proj(x, g, w_bf16, conv_w, seq_len, tm=256):
    n = x.shape[0]
    n_seq = n // seq_len
    slab = jax.ShapeDtypeStruct((N_SLABS, n, LANES), F32)
    slab_spec = pl.BlockSpec((N_SLABS, tm, LANES), lambda i: (0, i, 0))
    tiles_per_seq = seq_len // tm
    return pl.pallas_call(
        functools.partial(_in_proj_kernel, tm=tm, tiles_per_seq=tiles_per_seq),
        grid=(n // tm,),
        in_specs=[pl.BlockSpec((tm, D_MODEL), lambda i: (i, 0)),
                  pl.BlockSpec((1, D_MODEL), lambda i: (0, 0)),
                  pl.BlockSpec((D_MODEL, PROJ_WIDTH), lambda i: (0, 0)),
                  pl.BlockSpec((CONV_K, CONV_DIM), lambda i: (0, 0))],
        out_specs=[slab_spec, slab_spec, slab_spec,
                   pl.BlockSpec((tm, CONV_DIM), lambda i: (i, 0)),
                   pl.BlockSpec((1, CONV_K - 1, CONV_DIM), lambda i: (i // tiles_per_seq, 0, 0))],
        out_shape=[slab, slab, slab,
                   jax.ShapeDtypeStruct((n, CONV_DIM), F32),
                   jax.ShapeDtypeStruct((n_seq, CONV_K - 1, CONV_DIM), F32)],
        scratch_shapes=[pltpu.VMEM((tm + SUBLANES, CONV_DIM), F32)],
        compiler_params=_params("arbitrary"),
        name="in_proj_prompt",
    )(x, g, w_bf16, conv_w)


def _attn_kernel(q_ref, k_ref, v_ref, bias_ref, o_ref, kcat, vcat, oacc, lacc, *, sb):
    j = pl.program_id(2)

    @pl.when(j == 0)
    def _():
        kcat[0:sb, :] = jnp.zeros((sb, LANES), F32)
        vcat[0:sb, :] = jnp.zeros((sb, LANES), F32)

    @pl.when(j > 0)
    def _():
        kcat[0:sb, :] = kcat[sb:2 * sb, :]
        vcat[0:sb, :] = vcat[sb:2 * sb, :]

    kcat[sb:2 * sb, :] = k_ref[...]
    vcat[sb:2 * sb, :] = v_ref[...]

    lane = lax.broadcasted_iota(jnp.int32, (Q_BLOCK, LANES), 1)
    low = lane < HEAD_DIM
    ones = jnp.ones((2 * Q_BLOCK, LANES), BF16)

    for di, d in enumerate(DILATIONS):
        nq = sb // (Q_BLOCK * d)

        def unit(u, carry, di=di, d=d, nq=nq):
            r = u // nq
            n = u % nq
            qstart = r + d * Q_BLOCK * n
            kstart = sb + qstart - d * Q_BLOCK
            q = q_ref[pl.ds(qstart, Q_BLOCK, stride=d), :] * SCALE
            kk = kcat[pl.ds(kstart, 2 * Q_BLOCK, stride=d), :].astype(BF16)
            vv = vcat[pl.ds(kstart, 2 * Q_BLOCK, stride=d), :].astype(BF16)
            qm = jnp.concatenate([jnp.where(low, q, 0.0), jnp.where(low, 0.0, q)],
                                 axis=0).astype(BF16)
            s = lax.dot_general(qm, kk, (((1,), (1,)), ((), ())),
                                preferred_element_type=F32)
            first = jnp.logical_and(j == 0, n == 0).astype(jnp.int32)
            s = s + bias_ref[di, first]
            m = jnp.max(s, axis=-1, keepdims=True)
            p = jnp.exp(s - m).astype(BF16)
            pv = jnp.dot(p, jnp.concatenate([vv, ones], axis=1),
                         preferred_element_type=F32)
            o_sel = jnp.where(low, pv[0:Q_BLOCK, 0:LANES], pv[Q_BLOCK:, 0:LANES])
            l_sel = jnp.where(low, pv[0:Q_BLOCK, LANES:], pv[Q_BLOCK:, LANES:])
            m_sel = jnp.where(low, jnp.broadcast_to(m[0:Q_BLOCK], (Q_BLOCK, LANES)),
                              jnp.broadcast_to(m[Q_BLOCK:], (Q_BLOCK, LANES)))
            oacc[di, pl.ds(qstart, Q_BLOCK, stride=d), :] = o_sel / l_sel
            lacc[di, pl.ds(qstart, Q_BLOCK, stride=d), :] = m_sel + jnp.log(l_sel)
            return carry

        lax.fori_loop(0, sb // Q_BLOCK, unit, 0)

    def merge(c, carry):
        rows = pl.ds(pl.multiple_of(c * 256, 256), 256)
        l0, l1, l2 = lacc[0, rows, :], lacc[1, rows, :], lacc[2, rows, :]
        mm = jnp.maximum(jnp.maximum(l0, l1), l2)
        w0, w1, w2 = jnp.exp(l0 - mm), jnp.exp(l1 - mm), jnp.exp(l2 - mm)
        num = w0 * oacc[0, rows, :] + w1 * oacc[1, rows, :] + w2 * oacc[2, rows, :]
        o_ref[rows, :] = num / (w0 + w1 + w2)
        return carry

    lax.fori_loop(0, sb // 256, merge, 0)


def _prompt_bias_tables(rel_bias):
    q_loc = jnp.arange(Q_BLOCK)[:, None]
    k_loc = jnp.arange(2 * Q_BLOCK)[None, :]
    steps = q_loc + Q_BLOCK - k_loc
    in_band = (steps >= 0) & (steps <= SEG_KEYS)
    tables = []
    for d in DILATIONS:
        bias = rel_bias[_rel_bucket(jnp.clip(steps, 0, SEG_KEYS) * d)].astype(F32)
        bias = bias.transpose(2, 0, 1)
        per_first = []
        for first in (False, True):
            valid = in_band & (k_loc >= Q_BLOCK) if first else in_band
            t = jnp.where(valid[None], bias, NEG)
            per_first.append(t.reshape(N_SLABS, 2 * Q_BLOCK, 2 * Q_BLOCK))
        tables.append(jnp.stack(per_first))
    return jnp.stack(tables)


def _attention_prompt(q, k, v, bias_tab, n_seq, seq_len, sb=2048):
    n = q.shape[1]
    nsb = seq_len // sb
    blk = pl.BlockSpec((None, sb, LANES), lambda b, s, j: (s, b * nsb + j, 0))
    return pl.pallas_call(
        functools.partial(_attn_kernel, sb=sb),
        grid=(n_seq, N_SLABS, nsb),
        in_specs=[blk, blk, blk,
                  pl.BlockSpec((len(DILATIONS), 2, None, 2 * Q_BLOCK, 2 * Q_BLOCK),
                               lambda b, s, j: (0, 0, s, 0, 0))],
        out_specs=blk,
        out_shape=jax.ShapeDtypeStruct((N_SLABS, n, LANES), F32),
        scratch_shapes=[pltpu.VMEM((2 * sb, LANES), F32), pltpu.VMEM((2 * sb, LANES), F32),
                        pltpu.VMEM((len(DILATIONS), sb, LANES), F32),
                        pltpu.VMEM((len(DILATIONS), sb, LANES), F32)],
        compiler_params=_params("arbitrary", "arbitrary", "arbitrary"),
        name="attention_prompt",
    )(q, k, v, bias_tab)


def _route(lg):
    lane = lax.broadcasted_iota(jnp.int32, lg.shape, 1)
    big = jnp.int32(4 * LANES)
    is_g = (lane >= N_EXPERTS) & (lane < N_EXPERTS + N_GROUPS)
    glog = jnp.where(is_g, lg, -jnp.inf)
    gmax = jnp.max(glog, axis=-1, keepdims=True)
    gsum = jnp.sum(jnp.exp(glog - gmax), axis=-1, keepdims=True)
    gidx = jnp.min(jnp.where(glog == gmax, lane, big), axis=-1, keepdims=True) - N_EXPERTS
    p_grp = 1.0 / gsum
    in_grp = (lane >= gidx * EXPERTS_PER_GROUP) & (lane < (gidx + 1) * EXPERTS_PER_GROUP)
    elog = jnp.where(in_grp, lg, -jnp.inf)
    emax = jnp.max(elog, axis=-1, keepdims=True)
    eexp = jnp.exp(elog - emax)
    eprob = eexp / jnp.sum(eexp, axis=-1, keepdims=True)
    eprob = jnp.where(in_grp, eprob, -1.0)
    p1 = jnp.max(eprob, axis=-1, keepdims=True)
    i1 = jnp.min(jnp.where(eprob == p1, lane, big), axis=-1, keepdims=True)
    rest = jnp.where(lane == i1, -1.0, eprob)
    p2 = jnp.max(rest, axis=-1, keepdims=True)
    i2 = jnp.min(jnp.where(rest == p2, lane, big), axis=-1, keepdims=True)
    tot = p1 + p2
    g1 = p_grp * (p1 / tot)
    g2 = p_grp * (p2 / tot)
    return jnp.where(lane == i1, g1, 0.0) + jnp.where(lane == i2, g2, 0.0)


def _out_proj_kernel(attn_ref, conv_ref, h_ref, wo_ref, g_ref, wr_ref, h1_ref, xn_ref, comb_ref):
    mix = jnp.concatenate([attn_ref[s] for s in range(N_SLABS)] + [conv_ref[...]], axis=1)
    h1 = h_ref[...] + jnp.dot(mix.astype(BF16), wo_ref[...], preferred_element_type=F32)
    h1_ref[...] = h1
    xn = _rms(h1, g_ref[...])
    xn_ref[...] = xn.astype(BF16)
    lg = jnp.dot(xn, wr_ref[...], preferred_element_type=F32, precision=lax.Precision.HIGHEST)
    comb_ref[...] = _route(lg)


def _out_proj(attn, conv, h, wo_bf16, g, w_router, tm):
    n = h.shape[0]
    return pl.pallas_call(
        _out_proj_kernel,
        grid=(n // tm,),
        in_specs=[pl.BlockSpec((N_SLABS, tm, LANES), lambda i: (0, i, 0)),
                  pl.BlockSpec((tm, CONV_DIM), lambda i: (i, 0)),
                  pl.BlockSpec((tm, D_MODEL), lambda i: (i, 0)),
                  pl.BlockSpec((D_MODEL, D_MODEL), lambda i: (0, 0)),
                  pl.BlockSpec((1, D_MODEL), lambda i: (0, 0)),
                  pl.BlockSpec((D_MODEL, LANES), lambda i: (0, 0))],
        out_specs=[pl.BlockSpec((tm, D_MODEL), lambda i: (i, 0)),
                   pl.BlockSpec((tm, D_MODEL), lambda i: (i, 0)),
                   pl.BlockSpec((tm, LANES), lambda i: (i, 0))],
        out_shape=[jax.ShapeDtypeStruct((n, D_MODEL), F32),
                   jax.ShapeDtypeStruct((n, D_MODEL), BF16),
                   jax.ShapeDtypeStruct((n, LANES), F32)],
        compiler_params=_params("arbitrary"),
        name="out_proj",
    )(attn, conv, h, wo_bf16, g, w_router)


def _moe_kernel(x_ref, comb_ref, wgu_ref, wd_ref, h_ref, gf_ref, o_ref, *, final_norm):
    e = pl.program_id(1)

    @pl.when(e == 0)
    def _():
        o_ref[...] = h_ref[...]

    x = x_ref[...]
    gu = jnp.dot(x, wgu_ref[...], preferred_element_type=F32)
    gate = gu[:, :D_EXPERT]
    hdn = gate * jax.nn.sigmoid(gate) * gu[:, D_EXPERT:]
    comb = comb_ref[...]
    lane = lax.broadcasted_iota(jnp.int32, comb.shape, 1)
    w = jnp.sum(jnp.where(lane == e, comb, 0.0), axis=-1, keepdims=True)
    o_ref[...] += jnp.dot((w * hdn).astype(BF16), wd_ref[...], preferred_element_type=F32)

    if final_norm:
        @pl.when(e == N_EXPERTS - 1)
        def _():
            o_ref[...] = _rms(o_ref[...], gf_ref[...])


def _moe(xn, comb, wgu, wd, h, g_final, tm, final_norm):
    n = h.shape[0]
    return pl.pallas_call(
        functools.partial(_moe_kernel, final_norm=final_norm),
        grid=(n // tm, N_EXPERTS),
        in_specs=[pl.BlockSpec((tm, D_MODEL), lambda i, e: (i, 0)),
                  pl.BlockSpec((tm, LANES), lambda i, e: (i, 0)),
                  pl.BlockSpec((None, D_MODEL, 2 * D_EXPERT), lambda i, e: (e, 0, 0)),
                  pl.BlockSpec((None, D_EXPERT, D_MODEL), lambda i, e: (e, 0, 0)),
                  pl.BlockSpec((tm, D_MODEL), lambda i, e: (i, 0)),
                  pl.BlockSpec((1, D_MODEL), lambda i, e: (0, 0))],
        out_specs=pl.BlockSpec((tm, D_MODEL), lambda i, e: (i, 0)),
        out_shape=jax.ShapeDtypeStruct((n, D_MODEL), F32),
        compiler_params=_params("arbitrary", "arbitrary"),
        name="moe_experts",
    )(xn, comb, wgu, wd, h, g_final)


def _in_proj_sample_kernel(x_ref, g_ref, w_ref, o_ref):
    xn = _rms(x_ref[...], g_ref[...])
    o_ref[...] = jnp.dot(xn.astype(BF16), w_ref[...], preferred_element_type=F32)


def _in_proj_sample(x, g, w_bf16):
    n = x.shape[0]
    return pl.pallas_call(
        _in_proj_sample_kernel,
        out_shape=jax.ShapeDtypeStruct((n, PROJ_WIDTH), F32),
        compiler_params=pltpu.CompilerParams(vmem_limit_bytes=VMEM_LIMIT),
        name="in_proj_sample",
    )(x, g, w_bf16)


def _sample_mix_kernel(q_ref, kn_ref, vn_ref, k4_ref, v4_ref, k16_ref, v16_ref,
                       bc_ref, bn_ref, gb_ref, gc_ref, hc_ref, st_ref, cw_ref,
                       o_ref, conv_ref, st_out_ref, *, t_new):
    kn = kn_ref[0]
    vn = vn_ref[0]
    n1 = SEG_KEYS // 4
    for t in range(t_new):
        q = q_ref[0, t] * SCALE
        outs, lses = [], []
        for di in range(len(DILATIONS)):
            if di == 0:
                kc = k4_ref[0, 0, pl.ds(3 * n1, n1)].reshape(SEG_KEYS, N_HEADS, HEAD_DIM)
                vc = v4_ref[0, 0, pl.ds(3 * n1, n1)].reshape(SEG_KEYS, N_HEADS, HEAD_DIM)
            elif di == 1:
                kc = k4_ref[0, 0, :, t]
                vc = v4_ref[0, 0, :, t]
            else:
                kc = k16_ref[0, 0, :, t]
                vc = v16_ref[0, 0, :, t]
            s_c = jnp.sum(kc * q[None], axis=-1, keepdims=True) + bc_ref[di, t]
            s_n = jnp.sum(kn * q[None], axis=-1, keepdims=True) + bn_ref[di, t]
            m = jnp.maximum(jnp.max(s_c, axis=0, keepdims=True),
                            jnp.max(s_n, axis=0, keepdims=True))
            p_c = jnp.exp(s_c - m)
            p_n = jnp.exp(s_n - m)
            den = jnp.sum(p_c, axis=0, keepdims=True) + jnp.sum(p_n, axis=0, keepdims=True)
            num = (jnp.sum(p_c * vc, axis=0, keepdims=True)
                   + jnp.sum(p_n * vn, axis=0, keepdims=True))
            outs.append(num / den)
            lses.append(m + jnp.log(den))
        mm = jnp.maximum(jnp.maximum(lses[0], lses[1]), lses[2])
        w = [jnp.exp(l - mm) for l in lses]
        o = (w[0] * outs[0] + w[1] * outs[1] + w[2] * outs[2]) / (w[0] + w[1] + w[2])
        o_ref[0, t] = o[0]
    cw = cw_ref[...]
    u = gc_ref[0] * hc_ref[0]
    st = st_ref[0]
    rows = [st[j:j + 1] for j in range(CONV_K - 1)] + [u[t:t + 1] for t in range(t_new)]
    gb = gb_ref[0]
    conv_ref[0] = jnp.concatenate(
        [gb[t:t + 1] * sum(cw[j:j + 1] * rows[t + j] for j in range(CONV_K)) for t in range(t_new)],
        axis=0)
    st_out_ref[0] = jnp.concatenate(rows[t_new:], axis=0)


def _sample_bias_tables(rel_bias, t_new):
    t_idx = jnp.arange(t_new)
    cache, new = [], []
    for di, d in enumerate(DILATIONS):
        if di == 0:
            dist = SEG_KEYS + t_idx[:, None] - jnp.arange(SEG_KEYS)[None, :]
            ok = dist <= SEG_KEYS
            b = rel_bias[_rel_bucket(jnp.clip(dist, 0, SEG_KEYS))]
            cache.append(jnp.where(ok[..., None], b, NEG))
            dn = t_idx[:, None] - t_idx[None, :]
            bn = rel_bias[_rel_bucket(jnp.clip(dn, 0, SEG_KEYS))]
            new.append(jnp.where((dn >= 0)[..., None], bn, NEG))
        else:
            steps = SEG_KEYS - jnp.arange(SEG_KEYS)
            b = rel_bias[_rel_bucket(steps * d)]
            cache.append(jnp.broadcast_to(b[None], (t_new, SEG_KEYS, N_HEADS)))
            bn = jnp.broadcast_to(rel_bias[0][None, None], (t_new, t_new, N_HEADS))
            new.append(jnp.where((t_idx[:, None] == t_idx[None, :])[..., None], bn, NEG))
    return (jnp.stack(cache).astype(F32)[..., None], jnp.stack(new).astype(F32)[..., None])


def _sample_mix(q, kn, vn, cache_k, cache_v, layer, bias_c, bias_n, gb, gc, hc, state, conv_w):
    db, t_new = q.shape[0], q.shape[1]
    depth, _, w_buf = cache_k.shape[0], cache_k.shape[1], cache_k.shape[2]
    c4 = cache_k.reshape(depth, db, w_buf // 4, 4, N_HEADS, HEAD_DIM)
    v4 = cache_v.reshape(depth, db, w_buf // 4, 4, N_HEADS, HEAD_DIM)
    c16 = cache_k.reshape(depth, db, w_buf // 16, 16, N_HEADS, HEAD_DIM)
    v16 = cache_v.reshape(depth, db, w_buf // 16, 16, N_HEADS, HEAD_DIM)
    tok = pl.BlockSpec((1, t_new, N_HEADS, HEAD_DIM), lambda b: (b, 0, 0, 0))
    row = pl.BlockSpec((1, t_new, CONV_DIM), lambda b: (b, 0, 0))
    spec4 = pl.BlockSpec((1, 1, SEG_KEYS, 4, N_HEADS, HEAD_DIM), lambda b: (layer, b, 3, 0, 0, 0))
    spec16 = pl.BlockSpec((1, 1, SEG_KEYS, t_new, N_HEADS, HEAD_DIM), lambda b: (layer, b, 0, 0, 0, 0))
    return pl.pallas_call(
        functools.partial(_sample_mix_kernel, t_new=t_new),
        grid=(db,),
        in_specs=[tok, tok, tok, spec4, spec4, spec16, spec16,
                  pl.BlockSpec(bias_c.shape, lambda b: (0,) * bias_c.ndim),
                  pl.BlockSpec(bias_n.shape, lambda b: (0,) * bias_n.ndim),
                  row, row, row,
                  pl.BlockSpec((1, CONV_K - 1, CONV_DIM), lambda b: (b, 0, 0)),
                  pl.BlockSpec((CONV_K, CONV_DIM), lambda b: (0, 0))],
        out_specs=[tok, row, pl.BlockSpec((1, CONV_K - 1, CONV_DIM), lambda b: (b, 0, 0))],
        out_shape=[jax.ShapeDtypeStruct((db, t_new, N_HEADS, HEAD_DIM), F32),
                   jax.ShapeDtypeStruct((db, t_new, CONV_DIM), F32),
                   jax.ShapeDtypeStruct((db, CONV_K - 1, CONV_DIM), F32)],
        compiler_params=_params("arbitrary"),
        name="sample_mix",
    )(q, kn, vn, c4, v4, c16, v16, bias_c, bias_n, gb, gc, hc, state, conv_w)


def _to_slabs(x):
    n = x.shape[0]
    return x.reshape(n, N_SLABS, LANES).transpose(1, 0, 2)


def _from_slabs(x):
    return x.transpose(1, 0, 2).reshape(x.shape[1], ATTN_WIDTH)


def kernel(x_prompt, x_sample, cache_k, cache_v, state_conv, rel_bias, norm_mix, norm_ffn,
           norm_final, w_in, conv_w, w_out, w_router_group, w_router_expert, w_gate, w_up,
           w_down):
    batch, seq, _ = x_prompt.shape
    db, t_new, _ = x_sample.shape
    depth = w_in.shape[0]
    w_keep = min(MAX_WINDOW, seq)

    hp = x_prompt.reshape(batch * seq, D_MODEL)
    hs = x_sample.reshape(db * t_new, D_MODEL)
    bias_prompt = _prompt_bias_tables(rel_bias)
    bias_c, bias_n = _sample_bias_tables(rel_bias, t_new)
    g_final = norm_final.reshape(1, D_MODEL)

    pk, pv, pc, sk, sv, sc = [], [], [], [], [], []
    for l in range(depth):
        w_in_b = w_in[l].astype(BF16)
        w_out_b = w_out[l].astype(BF16)
        w_router = jnp.pad(jnp.concatenate([w_router_expert[l], w_router_group[l]], axis=1),
                           ((0, 0), (0, LANES - N_EXPERTS - N_GROUPS)))
        wgu = jnp.concatenate([w_gate[l], w_up[l]], axis=-1).astype(BF16)
        wd = w_down[l].astype(BF16)
        g_mix = norm_mix[l].reshape(1, D_MODEL)
        g_ffn = norm_ffn[l].reshape(1, D_MODEL)
        last = l == depth - 1

        q, k, v, conv, u_last = _in_proj(hp, g_mix, w_in_b, conv_w[l], seq)
        attn = _attention_prompt(q, k, v, bias_prompt, batch, seq)
        h1, xn, comb = _out_proj(attn, conv, hp, w_out_b, g_ffn, w_router, tm=256)
        hp = _moe(xn, comb, wgu, wd, h1, g_final, tm=1024, final_norm=last)
        k_nat = _from_slabs(k).reshape(batch, seq, N_HEADS, HEAD_DIM)
        v_nat = _from_slabs(v).reshape(batch, seq, N_HEADS, HEAD_DIM)
        pk.append(k_nat[:, seq - w_keep:])
        pv.append(v_nat[:, seq - w_keep:])
        pc.append(u_last)

        proj = _in_proj_sample(hs, g_mix, w_in_b)
        qs = proj[:, :ATTN_WIDTH].reshape(db, t_new, N_HEADS, HEAD_DIM)
        ks = proj[:, ATTN_WIDTH:2 * ATTN_WIDTH].reshape(db, t_new, N_HEADS, HEAD_DIM)
        vs = proj[:, 2 * ATTN_WIDTH:3 * ATTN_WIDTH].reshape(db, t_new, N_HEADS, HEAD_DIM)
        c0 = 3 * ATTN_WIDTH
        gb = proj[:, c0:c0 + CONV_DIM].reshape(db, t_new, CONV_DIM)
        gc = proj[:, c0 + CONV_DIM:c0 + 2 * CONV_DIM].reshape(db, t_new, CONV_DIM)
        hc = proj[:, c0 + 2 * CONV_DIM:].reshape(db, t_new, CONV_DIM)
        attn_s, conv_s, state_s = _sample_mix(qs, ks, vs, cache_k, cache_v, l, bias_c, bias_n,
                                              gb, gc, hc, state_conv[l], conv_w[l])
        h1s, xns, combs = _out_proj(_to_slabs(attn_s.reshape(db * t_new, ATTN_WIDTH)),
                                    conv_s.reshape(db * t_new, CONV_DIM), hs, w_out_b, g_ffn,
                                    w_router, tm=db * t_new)
        hs = _moe(xns, combs, wgu, wd, h1s, g_final, tm=db * t_new, final_norm=last)
        sk.append(ks)
        sv.append(vs)
        sc.append(state_s)

    return (hp.reshape(batch, seq, D_MODEL), hs.reshape(db, t_new, D_MODEL),
            jnp.stack(pk), jnp.stack(pv), jnp.stack(pc),
            jnp.stack(sk), jnp.stack(sv), jnp.stack(sc))
```

```python
import functools
import math

import jax
import jax.numpy as jnp
from jax import lax
from jax.experimental import pallas as pl
from jax.experimental.pallas import tpu as pltpu

F32 = jnp.float32
BF16 = jnp.bfloat16

D_MODEL = 1024
N_HEADS = 8
HEAD_DIM = 64
ATTN_WIDTH = N_HEADS * HEAD_DIM
CONV_DIM = D_MODEL - ATTN_WIDTH
CONV_K = 3
PROJ_WIDTH = 3 * ATTN_WIDTH + 3 * CONV_DIM
DILATIONS = (1, 4, 16)
SEG_KEYS = 128
Q_BLOCK = 128
MAX_WINDOW = 2048
N_BUCKETS = 32
MAX_DISTANCE = 2048
N_GROUPS = 4
EXPERTS_PER_GROUP = 8
N_EXPERTS = N_GROUPS * EXPERTS_PER_GROUP
D_EXPERT = D_MODEL // 4
EPS = 1e-6
NEG = -1e30
SCALE = 1.0 / math.sqrt(HEAD_DIM)

LANES = 128
SUBLANES = 8
N_SLABS = ATTN_WIDTH // LANES
VMEM_LIMIT = 48 * 1024 * 1024
UNITS_PER_TRIP = 4


def _params(*sem):
    return pltpu.CompilerParams(dimension_semantics=sem, vmem_limit_bytes=VMEM_LIMIT)


def _rms(x, g):
    return x * lax.rsqrt(jnp.mean(x * x, axis=-1, keepdims=True) + EPS) * g


def _rel_bucket(dist):
    max_exact = N_BUCKETS // 2
    d_f = jnp.maximum(dist, 1).astype(F32)
    large = max_exact + (jnp.log(d_f / max_exact) / math.log(MAX_DISTANCE / max_exact)
                         * (N_BUCKETS - max_exact)).astype(jnp.int32)
    large = jnp.minimum(large, N_BUCKETS - 1)
    return jnp.where(dist < max_exact, dist, large)


def _in_proj_kernel(x_ref, g_ref, w_ref, cw_ref, q_ref, k_ref, v_ref, conv_ref, ulast_ref,
                    ext_ref, *, tm, tiles_per_seq):
    i = pl.program_id(0)
    xn = _rms(x_ref[...], g_ref[...])
    proj = jnp.dot(xn.astype(BF16), w_ref[...], preferred_element_type=F32)
    for s in range(N_SLABS):
        q_ref[s] = proj[:, s * LANES:(s + 1) * LANES]
        k_ref[s] = proj[:, ATTN_WIDTH + s * LANES:ATTN_WIDTH + (s + 1) * LANES]
        v_ref[s] = proj[:, 2 * ATTN_WIDTH + s * LANES:2 * ATTN_WIDTH + (s + 1) * LANES]
    c0 = 3 * ATTN_WIDTH
    gb = proj[:, c0:c0 + CONV_DIM]
    u = proj[:, c0 + CONV_DIM:c0 + 2 * CONV_DIM] * proj[:, c0 + 2 * CONV_DIM:c0 + 3 * CONV_DIM]

    @pl.when(i % tiles_per_seq == 0)
    def _():
        ext_ref[0:SUBLANES, :] = jnp.zeros((SUBLANES, CONV_DIM), F32)

    ext_ref[SUBLANES:SUBLANES + tm, :] = u
    u1 = ext_ref[SUBLANES - 1:SUBLANES - 1 + tm, :]
    u2 = ext_ref[SUBLANES - 2:SUBLANES - 2 + tm, :]
    cw = cw_ref[...]
    conv_ref[...] = gb * (cw[0:1] * u2 + cw[1:2] * u1 + cw[2:3] * u)
    ulast_ref[0] = ext_ref[tm + SUBLANES - 2:tm + SUBLANES, :]
    ext_ref[0:SUBLANES, :] = ext_ref[tm:tm + SUBLANES, :]


def _in_proj(x, g, w_bf16, conv_w, seq_len, tm=256):
    n = x.shape[0]
    n_seq = n // seq_len
    slab = jax.ShapeDtypeStruct((N_SLABS, n, LANES), F32)
    slab_spec = pl.BlockSpec((N_SLABS, tm, LANES), lambda i: (0, i, 0))
    tiles_per_seq = seq_len // tm
    return pl.pallas_call(
        functools.partial(_in_proj_kernel, tm=tm, tiles_per_seq=tiles_per_seq),
        grid=(n // tm,),
        in_specs=[pl.BlockSpec((tm, D_MODEL), lambda i: (i, 0)),
                  pl.BlockSpec((1, D_MODEL), lambda i: (0, 0)),
                  pl.BlockSpec((D_MODEL, PROJ_WIDTH), lambda i: (0, 0)),
                  pl.BlockSpec((CONV_K, CONV_DIM), lambda i: (0, 0))],
        out_specs=[slab_spec, slab_spec, slab_spec,
                   pl.BlockSpec((tm, CONV_DIM), lambda i: (i, 0)),
                   pl.BlockSpec((1, CONV_K - 1, CONV_DIM), lambda i: (i // tiles_per_seq, 0, 0))],
        out_shape=[slab, slab, slab,
                   jax.ShapeDtypeStruct((n, CONV_DIM), F32),
                   jax.ShapeDtypeStruct((n_seq, CONV_K - 1, CONV_DIM), F32)],
        scratch_shapes=[pltpu.VMEM((tm + SUBLANES, CONV_DIM), F32)],
        compiler_params=_params("arbitrary"),
        name="in_proj_prompt",
    )(x, g, w_bf16, conv_w)


def _attn_kernel(q_ref, k_ref, v_ref, bias_ref, o_ref, kcat, vcat, oacc, lacc, *, sb):
    j = pl.program_id(2)

    @pl.when(j == 0)
    def _():
        kcat[0:sb, :] = jnp.zeros((sb, LANES), F32)
        vcat[0:sb, :] = jnp.zeros((sb, LANES), F32)

    @pl.when(j > 0)
    def _():
        kcat[0:sb, :] = kcat[sb:2 * sb, :]
        vcat[0:sb, :] = vcat[sb:2 * sb, :]

    kcat[sb:2 * sb, :] = k_ref[...]
    vcat[sb:2 * sb, :] = v_ref[...]

    lane = lax.broadcasted_iota(jnp.int32, (Q_BLOCK, LANES), 1)
    low = lane < HEAD_DIM
    ones = jnp.ones((2 * Q_BLOCK, LANES), BF16)

    for di, d in enumerate(DILATIONS):
        nq = sb // (Q_BLOCK * d)

        def unit(u, di=di, d=d, nq=nq):
            r = u // nq
            n = u % nq
            qstart = r + d * Q_BLOCK * n
            kstart = sb + qstart - d * Q_BLOCK
            q = q_ref[pl.ds(qstart, Q_BLOCK, stride=d), :] * SCALE
            kk = kcat[pl.ds(kstart, 2 * Q_BLOCK, stride=d), :].astype(BF16)
            vv = vcat[pl.ds(kstart, 2 * Q_BLOCK, stride=d), :].astype(BF16)
            qm = jnp.concatenate([jnp.where(low, q, 0.0), jnp.where(low, 0.0, q)],
                                 axis=0).astype(BF16)
            s = lax.dot_general(qm, kk, (((1,), (1,)), ((), ())),
                                preferred_element_type=F32)
            first = jnp.logical_and(j == 0, n == 0).astype(jnp.int32)
            s = s + bias_ref[di, first]
            m = jnp.max(s, axis=-1, keepdims=True)
            p = jnp.exp(s - m).astype(BF16)
            pv = jnp.dot(p, jnp.concatenate([vv, ones], axis=1),
                         preferred_element_type=F32)
            o_sel = jnp.where(low, pv[0:Q_BLOCK, 0:LANES], pv[Q_BLOCK:, 0:LANES])
            l_sel = jnp.where(low, pv[0:Q_BLOCK, LANES:], pv[Q_BLOCK:, LANES:])
            m_sel = jnp.where(low, jnp.broadcast_to(m[0:Q_BLOCK], (Q_BLOCK, LANES)),
                              jnp.broadcast_to(m[Q_BLOCK:], (Q_BLOCK, LANES)))
            oacc[di, pl.ds(qstart, Q_BLOCK, stride=d), :] = o_sel / l_sel
            lacc[di, pl.ds(qstart, Q_BLOCK, stride=d), :] = m_sel + jnp.log(l_sel)

        def group(g, carry, unit=unit):
            for uu in range(UNITS_PER_TRIP):
                unit(g * UNITS_PER_TRIP + uu)
            return carry

        lax.fori_loop(0, sb // Q_BLOCK // UNITS_PER_TRIP, group, 0)

    def merge(c, carry):
        rows = pl.ds(pl.multiple_of(c * 256, 256), 256)
        l0, l1, l2 = lacc[0, rows, :], lacc[1, rows, :], lacc[2, rows, :]
        mm = jnp.maximum(jnp.maximum(l0, l1), l2)
        w0, w1, w2 = jnp.exp(l0 - mm), jnp.exp(l1 - mm), jnp.exp(l2 - mm)
        num = w0 * oacc[0, rows, :] + w1 * oacc[1, rows, :] + w2 * oacc[2, rows, :]
        o_ref[rows, :] = num / (w0 + w1 + w2)
        return carry

    lax.fori_loop(0, sb // 256, merge, 0)


def _step_bias(rel_bias):
    steps = jnp.arange(SEG_KEYS + 1)
    return [rel_bias[_rel_bucket(steps * d)].astype(F32) for d in DILATIONS]


def _prompt_bias_tables(step_bias):
    width = 3 * Q_BLOCK
    k_loc = jnp.arange(2 * Q_BLOCK)
    tables = []
    for b in step_bias:
        g = jnp.concatenate([jnp.full((N_HEADS, Q_BLOCK - 1), NEG, F32), b.T,
                             jnp.full((N_HEADS, width - Q_BLOCK - SEG_KEYS), NEG, F32)], axis=1)
        flat = jnp.broadcast_to(g[:, None, :], (N_HEADS, Q_BLOCK + 1, width)).reshape(N_HEADS, -1)
        x = flat[:, :Q_BLOCK * (width + 1)].reshape(N_HEADS, Q_BLOCK, width + 1)
        t = x[:, :, :2 * Q_BLOCK][:, :, ::-1]
        t_first = jnp.where(k_loc[None, None, :] >= Q_BLOCK, t, NEG)
        tables.append(jnp.stack([t.reshape(N_SLABS, 2 * Q_BLOCK, 2 * Q_BLOCK),
                                 t_first.reshape(N_SLABS, 2 * Q_BLOCK, 2 * Q_BLOCK)]))
    return jnp.stack(tables)


def _attention_prompt(q, k, v, bias_tab, n_seq, seq_len, sb=2048):
    n = q.shape[1]
    nsb = seq_len // sb
    blk = pl.BlockSpec((None, sb, LANES), lambda b, s, j: (s, b * nsb + j, 0))
    return pl.pallas_call(
        functools.partial(_attn_kernel, sb=sb),
        grid=(n_seq, N_SLABS, nsb),
        in_specs=[blk, blk, blk,
                  pl.BlockSpec((len(DILATIONS), 2, None, 2 * Q_BLOCK, 2 * Q_BLOCK),
                               lambda b, s, j: (0, 0, s, 0, 0))],
        out_specs=blk,
        out_shape=jax.ShapeDtypeStruct((N_SLABS, n, LANES), F32),
        scratch_shapes=[pltpu.VMEM((2 * sb, LANES), F32), pltpu.VMEM((2 * sb, LANES), F32),
                        pltpu.VMEM((len(DILATIONS), sb, LANES), F32),
                        pltpu.VMEM((len(DILATIONS), sb, LANES), F32)],
        compiler_params=_params("arbitrary", "arbitrary", "arbitrary"),
        name="attention_prompt",
    )(q, k, v, bias_tab)


def _route(lg):
    lane = lax.broadcasted_iota(jnp.int32, lg.shape, 1)
    big = jnp.int32(4 * LANES)
    is_g = (lane >= N_EXPERTS) & (lane < N_EXPERTS + N_GROUPS)
    glog = jnp.where(is_g, lg, -jnp.inf)
    gmax = jnp.max(glog, axis=-1, keepdims=True)
    gsum = jnp.sum(jnp.exp(glog - gmax), axis=-1, keepdims=True)
    gidx = jnp.min(jnp.where(glog == gmax, lane, big), axis=-1, keepdims=True) - N_EXPERTS
    p_grp = 1.0 / gsum
    in_grp = (lane >= gidx * EXPERTS_PER_GROUP) & (lane < (gidx + 1) * EXPERTS_PER_GROUP)
    elog = jnp.where(in_grp, lg, -jnp.inf)
    emax = jnp.max(elog, axis=-1, keepdims=True)
    eexp = jnp.exp(elog - emax)
    eprob = eexp / jnp.sum(eexp, axis=-1, keepdims=True)
    eprob = jnp.where(in_grp, eprob, -1.0)
    p1 = jnp.max(eprob, axis=-1, keepdims=True)
    i1 = jnp.min(jnp.where(eprob == p1, lane, big), axis=-1, keepdims=True)
    rest = jnp.where(lane == i1, -1.0, eprob)
    p2 = jnp.max(rest, axis=-1, keepdims=True)
    i2 = jnp.min(jnp.where(rest == p2, lane, big), axis=-1, keepdims=True)
    tot = p1 + p2
    g1 = p_grp * (p1 / tot)
    g2 = p_grp * (p2 / tot)
    return jnp.where(lane == i1, g1, 0.0) + jnp.where(lane == i2, g2, 0.0)


def _split_bf16(x):
    hi = x.astype(BF16)
    return hi, (x - hi.astype(F32)).astype(BF16)


def _out_proj_kernel(attn_ref, conv_ref, h_ref, wo_ref, g_ref, wrh_ref, wrl_ref,
                     h1_ref, xn_ref, comb_ref):
    mix = jnp.concatenate([attn_ref[s] for s in range(N_SLABS)] + [conv_ref[...]], axis=1)
    h1 = h_ref[...] + jnp.dot(mix.astype(BF16), wo_ref[...], preferred_element_type=F32)
    h1_ref[...] = h1
    xn = _rms(h1, g_ref[...])
    xh, xl = _split_bf16(xn)
    xn_ref[...] = xh
    lg = (jnp.dot(xh, wrh_ref[...], preferred_element_type=F32)
          + jnp.dot(xl, wrh_ref[...], preferred_element_type=F32)
          + jnp.dot(xh, wrl_ref[...], preferred_element_type=F32))
    comb_ref[...] = _route(lg)


def _out_proj(attn, conv, h, wo_bf16, g, w_router, tm):
    n = h.shape[0]
    wr_hi, wr_lo = _split_bf16(w_router)
    return pl.pallas_call(
        _out_proj_kernel,
        grid=(n // tm,),
        in_specs=[pl.BlockSpec((N_SLABS, tm, LANES), lambda i: (0, i, 0)),
                  pl.BlockSpec((tm, CONV_DIM), lambda i: (i, 0)),
                  pl.BlockSpec((tm, D_MODEL), lambda i: (i, 0)),
                  pl.BlockSpec((D_MODEL, D_MODEL), lambda i: (0, 0)),
                  pl.BlockSpec((1, D_MODEL), lambda i: (0, 0)),
                  pl.BlockSpec((D_MODEL, LANES), lambda i: (0, 0)),
                  pl.BlockSpec((D_MODEL, LANES), lambda i: (0, 0))],
        out_specs=[pl.BlockSpec((tm, D_MODEL), lambda i: (i, 0)),
                   pl.BlockSpec((tm, D_MODEL), lambda i: (i, 0)),
                   pl.BlockSpec((tm, LANES), lambda i: (i, 0))],
        out_shape=[jax.ShapeDtypeStruct((n, D_MODEL), F32),
                   jax.ShapeDtypeStruct((n, D_MODEL), BF16),
                   jax.ShapeDtypeStruct((n, LANES), F32)],
        compiler_params=_params("arbitrary"),
        name="out_proj",
    )(attn, conv, h, wo_bf16, g, wr_hi, wr_lo)


def _moe_kernel(x_ref, comb_ref, wgu_ref, wd_ref, h_ref, gf_ref, o_ref, *, final_norm):
    e = pl.program_id(1)

    @pl.when(e == 0)
    def _():
        o_ref[...] = h_ref[...]

    x = x_ref[...]
    gu = jnp.dot(x, wgu_ref[...], preferred_element_type=F32)
    gate = gu[:, :D_EXPERT]
    hdn = gate * jax.nn.sigmoid(gate) * gu[:, D_EXPERT:]
    comb = comb_ref[...]
    lane = lax.broadcasted_iota(jnp.int32, comb.shape, 1)
    w = jnp.sum(jnp.where(lane == e, comb, 0.0), axis=-1, keepdims=True)
    o_ref[...] += jnp.dot((w * hdn).astype(BF16), wd_ref[...], preferred_element_type=F32)

    if final_norm:
        @pl.when(e == N_EXPERTS - 1)
        def _():
            o_ref[...] = _rms(o_ref[...], gf_ref[...])


def _moe(xn, comb, wgu, wd, h, g_final, tm, final_norm):
    n = h.shape[0]
    return pl.pallas_call(
        functools.partial(_moe_kernel, final_norm=final_norm),
        grid=(n // tm, N_EXPERTS),
        in_specs=[pl.BlockSpec((tm, D_MODEL), lambda i, e: (i, 0)),
                  pl.BlockSpec((tm, LANES), lambda i, e: (i, 0)),
                  pl.BlockSpec((None, D_MODEL, 2 * D_EXPERT), lambda i, e: (e, 0, 0)),
                  pl.BlockSpec((None, D_EXPERT, D_MODEL), lambda i, e: (e, 0, 0)),
                  pl.BlockSpec((tm, D_MODEL), lambda i, e: (i, 0)),
                  pl.BlockSpec((1, D_MODEL), lambda i, e: (0, 0))],
        out_specs=pl.BlockSpec((tm, D_MODEL), lambda i, e: (i, 0)),
        out_shape=jax.ShapeDtypeStruct((n, D_MODEL), F32),
        compiler_params=_params("arbitrary", "arbitrary"),
        name="moe_experts",
    )(xn, comb, wgu, wd, h, g_final)


def _in_proj_sample_kernel(x_ref, g_ref, w_ref, o_ref):
    xn = _rms(x_ref[...], g_ref[...])
    o_ref[...] = jnp.dot(xn.astype(BF16), w_ref[...], preferred_element_type=F32)


def _in_proj_sample(x, g, w_bf16):
    n = x.shape[0]
    return pl.pallas_call(
        _in_proj_sample_kernel,
        out_shape=jax.ShapeDtypeStruct((n, PROJ_WIDTH), F32),
        compiler_params=pltpu.CompilerParams(vmem_limit_bytes=VMEM_LIMIT),
        name="in_proj_sample",
    )(x, g, w_bf16)


def _sample_mix_kernel(q_ref, kn_ref, vn_ref, kt_ref, vt_ref,
                       bc_ref, bn_ref, gb_ref, gc_ref, hc_ref, st_ref, cw_ref,
                       o_ref, conv_ref, st_out_ref, *, t_new):
    contract_last = (((1,), (1,)), ((), ()))
    for h in range(N_HEADS):
        q = (q_ref[0, h] * SCALE).astype(BF16)
        s_c = jnp.dot(q, kt_ref[0, 0, h].astype(BF16), preferred_element_type=F32) + bc_ref[h]
        s_n = lax.dot_general(q, kn_ref[0, h].astype(BF16), contract_last,
                              preferred_element_type=F32) + bn_ref[h]
        m = jnp.maximum(jnp.max(s_c, axis=-1, keepdims=True), jnp.max(s_n, axis=-1, keepdims=True))
        p_c = jnp.exp(s_c - m)
        p_n = jnp.exp(s_n - m)
        den = jnp.sum(p_c, axis=-1, keepdims=True) + jnp.sum(p_n, axis=-1, keepdims=True)
        num = (lax.dot_general(p_c.astype(BF16), vt_ref[0, 0, h].astype(BF16), contract_last,
                               preferred_element_type=F32)
               + jnp.dot(p_n.astype(BF16), vn_ref[0, h].astype(BF16), preferred_element_type=F32))
        out = num / den
        lse = m + jnp.log(den)
        ls = [lse[i * t_new:(i + 1) * t_new] for i in range(len(DILATIONS))]
        os_ = [out[i * t_new:(i + 1) * t_new] for i in range(len(DILATIONS))]
        mm = jnp.maximum(jnp.maximum(ls[0], ls[1]), ls[2])
        w = [jnp.exp(l - mm) for l in ls]
        o_ref[0, h] = (w[0] * os_[0] + w[1] * os_[1] + w[2] * os_[2]) / (w[0] + w[1] + w[2])
    cw = cw_ref[...]
    u = gc_ref[0] * hc_ref[0]
    st = st_ref[0]
    rows = [st[j:j + 1] for j in range(CONV_K - 1)] + [u[t:t + 1] for t in range(t_new)]
    gb = gb_ref[0]
    conv_ref[0] = jnp.concatenate(
        [gb[t:t + 1] * sum(cw[j:j + 1] * rows[t + j] for j in range(CONV_K)) for t in range(t_new)],
        axis=0)
    st_out_ref[0] = jnp.concatenate(rows[t_new:], axis=0)


SAMPLE_ROWS = 16


def _sample_bias_tables(step_bias, t_new, w_buf):
    cache_rows, new_rows = [], []
    for b, d in zip(step_bias, DILATIONS):
        back = b[1:][::-1].T
        for t in range(t_new):
            span = jnp.full((N_HEADS, SEG_KEYS, d), NEG, F32).at[:, :, t % d].set(back)
            span = span.reshape(N_HEADS, SEG_KEYS * d)
            shift = t - t % d
            span = jnp.pad(span[:, :SEG_KEYS * d - shift], ((0, 0), (shift, 0)), constant_values=NEG)
            cache_rows.append(jnp.pad(span, ((0, 0), (w_buf - SEG_KEYS * d, 0)), constant_values=NEG))
            new = jnp.full((N_HEADS, LANES), NEG, F32)
            for t2 in range(t + 1):
                if (t - t2) % d == 0:
                    new = new.at[:, t2].set(b[(t - t2) // d])
            new_rows.append(new)
    pad = SAMPLE_ROWS - len(cache_rows)
    bias_c = jnp.pad(jnp.stack(cache_rows, axis=1), ((0, 0), (0, pad), (0, 0)))
    bias_n = jnp.pad(jnp.stack(new_rows, axis=1), ((0, 0), (0, pad), (0, 0)))
    return bias_c, bias_n


def _sample_mix(q, kn, vn, cache_k, cache_v, layer, bias_c, bias_n, gb, gc, hc, state, conv_w):
    db, t_new = q.shape[0], q.shape[1]
    w_buf = cache_k.shape[2]
    kt = cache_k.transpose(0, 1, 3, 4, 2)
    vt = cache_v.transpose(0, 1, 3, 4, 2)
    n_rows = len(DILATIONS) * t_new
    q_rows = jnp.pad(jnp.tile(q.transpose(0, 2, 1, 3), (1, 1, len(DILATIONS), 1)),
                     ((0, 0), (0, 0), (0, SAMPLE_ROWS - n_rows), (0, 0)))
    kn_pad = jnp.pad(kn.transpose(0, 2, 1, 3), ((0, 0), (0, 0), (0, LANES - t_new), (0, 0)))
    vn_pad = jnp.pad(vn.transpose(0, 2, 1, 3), ((0, 0), (0, 0), (0, LANES - t_new), (0, 0)))
    row = pl.BlockSpec((1, t_new, CONV_DIM), lambda b: (b, 0, 0))
    new_spec = pl.BlockSpec((1, N_HEADS, LANES, HEAD_DIM), lambda b: (b, 0, 0, 0))
    win_spec = pl.BlockSpec((1, 1, N_HEADS, HEAD_DIM, w_buf), lambda b: (layer, b, 0, 0, 0))
    attn, conv, st = pl.pallas_call(
        functools.partial(_sample_mix_kernel, t_new=t_new),
        grid=(db,),
        in_specs=[pl.BlockSpec((1, N_HEADS, SAMPLE_ROWS, HEAD_DIM), lambda b: (b, 0, 0, 0)),
                  new_spec, new_spec, win_spec, win_spec,
                  pl.BlockSpec(bias_c.shape, lambda b: (0, 0, 0)),
                  pl.BlockSpec(bias_n.shape, lambda b: (0, 0, 0)),
                  row, row, row,
                  pl.BlockSpec((1, CONV_K - 1, CONV_DIM), lambda b: (b, 0, 0)),
                  pl.BlockSpec((CONV_K, CONV_DIM), lambda b: (0, 0))],
        out_specs=[pl.BlockSpec((1, N_HEADS, t_new, HEAD_DIM), lambda b: (b, 0, 0, 0)), row,
                   pl.BlockSpec((1, CONV_K - 1, CONV_DIM), lambda b: (b, 0, 0))],
        out_shape=[jax.ShapeDtypeStruct((db, N_HEADS, t_new, HEAD_DIM), F32),
                   jax.ShapeDtypeStruct((db, t_new, CONV_DIM), F32),
                   jax.ShapeDtypeStruct((db, CONV_K - 1, CONV_DIM), F32)],
        compiler_params=_params("arbitrary"),
        name="sample_mix",
    )(q_rows, kn_pad, vn_pad, kt, vt, bias_c, bias_n, gb, gc, hc, state, conv_w)
    return attn.transpose(0, 2, 1, 3), conv, st


def _to_slabs(x):
    n = x.shape[0]
    return x.reshape(n, N_SLABS, LANES).transpose(1, 0, 2)


def _window_from_slabs(x, batch, seq, w_keep):
    win = x.reshape(N_SLABS, batch, seq, LANES)[:, :, seq - w_keep:]
    return win.transpose(1, 2, 0, 3).reshape(batch, w_keep, N_HEADS, HEAD_DIM)


def kernel(x_prompt, x_sample, cache_k, cache_v, state_conv, rel_bias, norm_mix, norm_ffn,
           norm_final, w_in, conv_w, w_out, w_router_group, w_router_expert, w_gate, w_up,
           w_down):
    batch, seq, _ = x_prompt.shape
    db, t_new, _ = x_sample.shape
    depth = w_in.shape[0]
    w_keep = min(MAX_WINDOW, seq)

    hp = x_prompt.reshape(batch * seq, D_MODEL)
    hs = x_sample.reshape(db * t_new, D_MODEL)
    step_bias = _step_bias(rel_bias)
    bias_prompt = _prompt_bias_tables(step_bias)
    bias_c, bias_n = _sample_bias_tables(step_bias, t_new, cache_k.shape[2])
    g_final = norm_final.reshape(1, D_MODEL)

    pk, pv, pc, sk, sv, sc = [], [], [], [], [], []
    for l in range(depth):
        w_in_b = w_in[l].astype(BF16)
        w_out_b = w_out[l].astype(BF16)
        w_router = jnp.pad(jnp.concatenate([w_router_expert[l], w_router_group[l]], axis=1),
                           ((0, 0), (0, LANES - N_EXPERTS - N_GROUPS)))
        wgu = jnp.concatenate([w_gate[l], w_up[l]], axis=-1).astype(BF16)
        wd = w_down[l].astype(BF16)
        g_mix = norm_mix[l].reshape(1, D_MODEL)
        g_ffn = norm_ffn[l].reshape(1, D_MODEL)
        last = l == depth - 1

        q, k, v, conv, u_last = _in_proj(hp, g_mix, w_in_b, conv_w[l], seq)
        attn = _attention_prompt(q, k, v, bias_prompt, batch, seq)
        h1, xn, comb = _out_proj(attn, conv, hp, w_out_b, g_ffn, w_router, tm=256)
        hp = _moe(xn, comb, wgu, wd, h1, g_final, tm=1024, final_norm=last)
        pk.append(_window_from_slabs(k, batch, seq, w_keep))
        pv.append(_window_from_slabs(v, batch, seq, w_keep))
        pc.append(u_last)

        proj = _in_proj_sample(hs, g_mix, w_in_b)
        qs = proj[:, :ATTN_WIDTH].reshape(db, t_new, N_HEADS, HEAD_DIM)
        ks = proj[:, ATTN_WIDTH:2 * ATTN_WIDTH].reshape(db, t_new, N_HEADS, HEAD_DIM)
        vs = proj[:, 2 * ATTN_WIDTH:3 * ATTN_WIDTH].reshape(db, t_new, N_HEADS, HEAD_DIM)
        c0 = 3 * ATTN_WIDTH
        gb = proj[:, c0:c0 + CONV_DIM].reshape(db, t_new, CONV_DIM)
        gc = proj[:, c0 + CONV_DIM:c0 + 2 * CONV_DIM].reshape(db, t_new, CONV_DIM)
        hc = proj[:, c0 + 2 * CONV_DIM:].reshape(db, t_new, CONV_DIM)
        attn_s, conv_s, state_s = _sample_mix(qs, ks, vs, cache_k, cache_v, l, bias_c, bias_n,
                                              gb, gc, hc, state_conv[l], conv_w[l])
        h1s, xns, combs = _out_proj(_to_slabs(attn_s.reshape(db * t_new, ATTN_WIDTH)),
                                    conv_s.reshape(db * t_new, CONV_DIM), hs, w_out_b, g_ffn,
                                    w_router, tm=db * t_new)
        hs = _moe(xns, combs, wgu, wd, h1s, g_final, tm=db * t_new, final_norm=last)
        sk.append(ks)
        sv.append(vs)
        sc.append(state_s)

    return (hp.reshape(batch, seq, D_MODEL), hs.reshape(db, t_new, D_MODEL),
            jnp.stack(pk), jnp.stack(pv), jnp.stack(pc),
            jnp.stack(sk), jnp.stack(sv), jnp.stack(sc))
```

```python
import functools
import math

import jax
import jax.numpy as jnp
from jax import lax
from jax.experimental import pallas as pl
from jax.experimental.pallas import tpu as pltpu

F32 = jnp.float32
BF16 = jnp.bfloat16

D_MODEL = 1024
N_HEADS = 8
HEAD_DIM = 64
ATTN_WIDTH = N_HEADS * HEAD_DIM
CONV_DIM = D_MODEL - ATTN_WIDTH
CONV_K = 3
PROJ_WIDTH = 3 * ATTN_WIDTH + 3 * CONV_DIM
DILATIONS = (1, 4, 16)
SEG_KEYS = 128
Q_BLOCK = 128
MAX_WINDOW = 2048
N_BUCKETS = 32
MAX_DISTANCE = 2048
N_GROUPS = 4
EXPERTS_PER_GROUP = 8
N_EXPERTS = N_GROUPS * EXPERTS_PER_GROUP
D_EXPERT = D_MODEL // 4
EPS = 1e-6
NEG = -1e30
SCALE = 1.0 / math.sqrt(HEAD_DIM)

LANES = 128
SUBLANES = 8
N_SLABS = ATTN_WIDTH // LANES
VMEM_LIMIT = 48 * 1024 * 1024
UNITS_PER_TRIP = 4


def _params(*sem):
    return pltpu.CompilerParams(dimension_semantics=sem, vmem_limit_bytes=VMEM_LIMIT)


def _rms(x, g):
    return x * lax.rsqrt(jnp.mean(x * x, axis=-1, keepdims=True) + EPS) * g


def _rel_bucket(dist):
    max_exact = N_BUCKETS // 2
    d_f = jnp.maximum(dist, 1).astype(F32)
    large = max_exact + (jnp.log(d_f / max_exact) / math.log(MAX_DISTANCE / max_exact)
                         * (N_BUCKETS - max_exact)).astype(jnp.int32)
    large = jnp.minimum(large, N_BUCKETS - 1)
    return jnp.where(dist < max_exact, dist, large)


def _in_proj_kernel(x_ref, g_ref, w_ref, cw_ref, q_ref, k_ref, v_ref, conv_ref, ulast_ref,
                    ext_ref, *, tm, tiles_per_seq):
    i = pl.program_id(0)
    xn = _rms(x_ref[...], g_ref[...])
    proj = jnp.dot(xn.astype(BF16), w_ref[...], preferred_element_type=F32)
    for s in range(N_SLABS):
        q_ref[s] = proj[:, s * LANES:(s + 1) * LANES]
        k_ref[s] = proj[:, ATTN_WIDTH + s * LANES:ATTN_WIDTH + (s + 1) * LANES]
        v_ref[s] = proj[:, 2 * ATTN_WIDTH + s * LANES:2 * ATTN_WIDTH + (s + 1) * LANES]
    c0 = 3 * ATTN_WIDTH
    gb = proj[:, c0:c0 + CONV_DIM]
    u = proj[:, c0 + CONV_DIM:c0 + 2 * CONV_DIM] * proj[:, c0 + 2 * CONV_DIM:c0 + 3 * CONV_DIM]

    @pl.when(i % tiles_per_seq == 0)
    def _():
        ext_ref[0:SUBLANES, :] = jnp.zeros((SUBLANES, CONV_DIM), F32)

    ext_ref[SUBLANES:SUBLANES + tm, :] = u
    u1 = ext_ref[SUBLANES - 1:SUBLANES - 1 + tm, :]
    u2 = ext_ref[SUBLANES - 2:SUBLANES - 2 + tm, :]
    cw = cw_ref[...]
    conv_ref[...] = gb * (cw[0:1] * u2 + cw[1:2] * u1 + cw[2:3] * u)
    ulast_ref[0] = ext_ref[tm + SUBLANES - 2:tm + SUBLANES, :]
    ext_ref[0:SUBLANES, :] = ext_ref[tm:tm + SUBLANES, :]


def _in_proj(x, g, w_bf16, conv_w, seq_len, tm=256):
    n = x.shape[0]
    n_seq = n // seq_len
    slab = jax.ShapeDtypeStruct((N_SLABS, n, LANES), F32)
    slab_spec = pl.BlockSpec((N_SLABS, tm, LANES), lambda i: (0, i, 0))
    tiles_per_seq = seq_len // tm
    return pl.pallas_call(
        functools.partial(_in_proj_kernel, tm=tm, tiles_per_seq=tiles_per_seq),
        grid=(n // tm,),
        in_specs=[pl.BlockSpec((tm, D_MODEL), lambda i: (i, 0)),
                  pl.BlockSpec((1, D_MODEL), lambda i: (0, 0)),
                  pl.BlockSpec((D_MODEL, PROJ_WIDTH), lambda i: (0, 0)),
                  pl.BlockSpec((CONV_K, CONV_DIM), lambda i: (0, 0))],
        out_specs=[slab_spec, slab_spec, slab_spec,
                   pl.BlockSpec((tm, CONV_DIM), lambda i: (i, 0)),
                   pl.BlockSpec((1, CONV_K - 1, CONV_DIM), lambda i: (i // tiles_per_seq, 0, 0))],
        out_shape=[slab, slab, slab,
                   jax.ShapeDtypeStruct((n, CONV_DIM), F32),
                   jax.ShapeDtypeStruct((n_seq, CONV_K - 1, CONV_DIM), F32)],
        scratch_shapes=[pltpu.VMEM((tm + SUBLANES, CONV_DIM), F32)],
        compiler_params=_params("arbitrary"),
        name="in_proj_prompt",
    )(x, g, w_bf16, conv_w)


def _attn_kernel(q_ref, k_ref, v_ref, bias_ref, o_ref, kcat, vcat, oacc, lacc, *, sb):
    j = pl.program_id(2)

    @pl.when(j == 0)
    def _():
        kcat[0:sb, :] = jnp.zeros((sb, LANES), F32)
        vcat[0:sb, :] = jnp.zeros((sb, LANES), F32)

    @pl.when(j > 0)
    def _():
        kcat[0:sb, :] = kcat[sb:2 * sb, :]
        vcat[0:sb, :] = vcat[sb:2 * sb, :]

    kcat[sb:2 * sb, :] = k_ref[...]
    vcat[sb:2 * sb, :] = v_ref[...]

    lane = lax.broadcasted_iota(jnp.int32, (Q_BLOCK, LANES), 1)
    low = lane < HEAD_DIM
    ones = jnp.ones((2 * Q_BLOCK, LANES), BF16)

    for di, d in enumerate(DILATIONS):
        nq = sb // (Q_BLOCK * d)

        def unit(u, di=di, d=d, nq=nq):
            r = u // nq
            n = u % nq
            qstart = r + d * Q_BLOCK * n
            kstart = sb + qstart - d * Q_BLOCK
            q = q_ref[pl.ds(qstart, Q_BLOCK, stride=d), :] * SCALE
            kk = kcat[pl.ds(kstart, 2 * Q_BLOCK, stride=d), :].astype(BF16)
            vv = vcat[pl.ds(kstart, 2 * Q_BLOCK, stride=d), :].astype(BF16)
            qm = jnp.concatenate([jnp.where(low, q, 0.0), jnp.where(low, 0.0, q)],
                                 axis=0).astype(BF16)
            s = lax.dot_general(qm, kk, (((1,), (1,)), ((), ())),
                                preferred_element_type=F32)
            first = jnp.logical_and(j == 0, n == 0).astype(jnp.int32)
            s = s + bias_ref[di, first]
            m = jnp.max(s, axis=-1, keepdims=True)
            p = jnp.exp(s - m).astype(BF16)
            pv = jnp.dot(p, jnp.concatenate([vv, ones], axis=1),
                         preferred_element_type=F32)
            o_sel = jnp.where(low, pv[0:Q_BLOCK, 0:LANES], pv[Q_BLOCK:, 0:LANES])
            l_sel = jnp.where(low, pv[0:Q_BLOCK, LANES:], pv[Q_BLOCK:, LANES:])
            m_sel = jnp.where(low, jnp.broadcast_to(m[0:Q_BLOCK], (Q_BLOCK, LANES)),
                              jnp.broadcast_to(m[Q_BLOCK:], (Q_BLOCK, LANES)))
            oacc[di, pl.ds(qstart, Q_BLOCK, stride=d), :] = o_sel / l_sel
            lacc[di, pl.ds(qstart, Q_BLOCK, stride=d), :] = m_sel + jnp.log(l_sel)

        def group(g, carry, unit=unit):
            for uu in range(UNITS_PER_TRIP):
                unit(g * UNITS_PER_TRIP + uu)
            return carry

        lax.fori_loop(0, sb // Q_BLOCK // UNITS_PER_TRIP, group, 0)

    def merge(c, carry):
        rows = pl.ds(pl.multiple_of(c * 256, 256), 256)
        l0, l1, l2 = lacc[0, rows, :], lacc[1, rows, :], lacc[2, rows, :]
        mm = jnp.maximum(jnp.maximum(l0, l1), l2)
        w0, w1, w2 = jnp.exp(l0 - mm), jnp.exp(l1 - mm), jnp.exp(l2 - mm)
        num = w0 * oacc[0, rows, :] + w1 * oacc[1, rows, :] + w2 * oacc[2, rows, :]
        o_ref[rows, :] = num / (w0 + w1 + w2)
        return carry

    lax.fori_loop(0, sb // 256, merge, 0)


def _step_bias(rel_bias):
    steps = jnp.arange(SEG_KEYS + 1)
    return [rel_bias[_rel_bucket(steps * d)].astype(F32) for d in DILATIONS]


def _prompt_bias_tables(step_bias):
    width = 3 * Q_BLOCK
    k_loc = jnp.arange(2 * Q_BLOCK)
    tables = []
    for b in step_bias:
        g = jnp.concatenate([jnp.full((N_HEADS, Q_BLOCK - 1), NEG, F32), b.T,
                             jnp.full((N_HEADS, width - Q_BLOCK - SEG_KEYS), NEG, F32)], axis=1)
        flat = jnp.broadcast_to(g[:, None, :], (N_HEADS, Q_BLOCK + 1, width)).reshape(N_HEADS, -1)
        x = flat[:, :Q_BLOCK * (width + 1)].reshape(N_HEADS, Q_BLOCK, width + 1)
        t = x[:, :, :2 * Q_BLOCK][:, :, ::-1]
        t_first = jnp.where(k_loc[None, None, :] >= Q_BLOCK, t, NEG)
        tables.append(jnp.stack([t.reshape(N_SLABS, 2 * Q_BLOCK, 2 * Q_BLOCK),
                                 t_first.reshape(N_SLABS, 2 * Q_BLOCK, 2 * Q_BLOCK)]))
    return jnp.stack(tables)


def _attention_prompt(q, k, v, bias_tab, n_seq, seq_len, sb=2048):
    n = q.shape[1]
    nsb = seq_len // sb
    blk = pl.BlockSpec((None, sb, LANES), lambda b, s, j: (s, b * nsb + j, 0))
    return pl.pallas_call(
        functools.partial(_attn_kernel, sb=sb),
        grid=(n_seq, N_SLABS, nsb),
        in_specs=[blk, blk, blk,
                  pl.BlockSpec((len(DILATIONS), 2, None, 2 * Q_BLOCK, 2 * Q_BLOCK),
                               lambda b, s, j: (0, 0, s, 0, 0))],
        out_specs=blk,
        out_shape=jax.ShapeDtypeStruct((N_SLABS, n, LANES), F32),
        scratch_shapes=[pltpu.VMEM((2 * sb, LANES), F32), pltpu.VMEM((2 * sb, LANES), F32),
                        pltpu.VMEM((len(DILATIONS), sb, LANES), F32),
                        pltpu.VMEM((len(DILATIONS), sb, LANES), F32)],
        compiler_params=_params("arbitrary", "arbitrary", "arbitrary"),
        name="attention_prompt",
    )(q, k, v, bias_tab)


ROUTER_ROWS = 48
MOE_TILE = 256
RUN_ALIGN = SUBLANES
TABLE_LANES = 128


def _local_rows(tm):
    return -(-(2 * tm + N_EXPERTS * (RUN_ALIGN - 1)) // LANES) * LANES


def _route(lg):
    tm = lg.shape[1]
    gl = lg[N_EXPERTS:N_EXPERTS + N_GROUPS]
    grow = lax.broadcasted_iota(jnp.int32, (N_GROUPS, tm), 0)
    gmax = jnp.max(gl, axis=0, keepdims=True)
    gsum = jnp.sum(jnp.exp(gl - gmax), axis=0, keepdims=True)
    gidx = jnp.min(jnp.where(gl == gmax, grow, N_GROUPS), axis=0, keepdims=True)
    p_grp = 1.0 / gsum
    el = lg[0:N_EXPERTS]
    e = lax.broadcasted_iota(jnp.int32, (N_EXPERTS, tm), 0)
    in_grp = (e // EXPERTS_PER_GROUP) == gidx
    elog = jnp.where(in_grp, el, -jnp.inf)
    emax = jnp.max(elog, axis=0, keepdims=True)
    eexp = jnp.exp(elog - emax)
    eprob = jnp.where(in_grp, eexp / jnp.sum(eexp, axis=0, keepdims=True), -1.0)
    p1 = jnp.max(eprob, axis=0, keepdims=True)
    i1 = jnp.min(jnp.where(eprob == p1, e, N_EXPERTS), axis=0, keepdims=True)
    rest = jnp.where(e == i1, -1.0, eprob)
    p2 = jnp.max(rest, axis=0, keepdims=True)
    i2 = jnp.min(jnp.where(rest == p2, e, N_EXPERTS), axis=0, keepdims=True)
    tot = p1 + p2
    return e == i1, e == i2, p_grp * (p1 / tot), p_grp * (p2 / tot)


def _split_bf16(x):
    hi = x.astype(BF16)
    return hi, (x - hi.astype(F32)).astype(BF16)


def _out_proj_kernel(attn_ref, conv_ref, h_ref, wo_ref, g_ref, wrh_ref, wrl_ref, upper_ref, lower_ref,
                     h1_ref, xn_ref, route_ref, routet_ref, cnt_ref, *, tm):
    mix = jnp.concatenate([attn_ref[s] for s in range(N_SLABS)] + [conv_ref[...]], axis=1)
    h1 = h_ref[...] + jnp.dot(mix.astype(BF16), wo_ref[...], preferred_element_type=F32)
    h1_ref[...] = h1
    xn = _rms(h1, g_ref[...])
    xh, xl = _split_bf16(xn)
    xn_ref[...] = xh
    contract_last = (((1,), (1,)), ((), ()))
    lg = (lax.dot_general(wrh_ref[...], xh, contract_last, preferred_element_type=F32)
          + lax.dot_general(wrh_ref[...], xl, contract_last, preferred_element_type=F32)
          + lax.dot_general(wrl_ref[...], xh, contract_last, preferred_element_type=F32))
    sel1, sel2, g1, g2 = _route(lg)
    chosen = jnp.where(sel1 | sel2, 1.0, 0.0)
    rank = jnp.dot(chosen.astype(BF16), upper_ref[...], preferred_element_type=F32)
    cnt = jnp.sum(chosen, axis=1, keepdims=True)
    cnt_pad = jnp.floor((cnt + (RUN_ALIGN - 1)) * (1.0 / RUN_ALIGN)) * RUN_ALIGN
    cnt_b = jnp.broadcast_to(cnt_pad, (N_EXPERTS, LANES))
    start = jnp.dot(lower_ref[...], cnt_b.astype(BF16), preferred_element_type=F32)
    pos = start[:, 0:1] + rank
    pos1 = jnp.sum(jnp.where(sel1, pos, 0.0), axis=0, keepdims=True)
    pos2 = jnp.sum(jnp.where(sel2, pos, 0.0), axis=0, keepdims=True)
    rows = jnp.concatenate([pos1, pos2, g1, g2, jnp.zeros((SUBLANES - 4, tm), F32)], axis=0)
    route_ref[0] = rows
    routet_ref[...] = jnp.concatenate([rows, jnp.zeros((LANES - SUBLANES, tm), F32)], axis=0).T
    cnt_ref[0] = cnt_b


def _out_proj(attn, conv, h, wo_bf16, g, w_router_t, tm):
    n = h.shape[0]
    wr_hi, wr_lo = _split_bf16(w_router_t)
    idx = jnp.arange(tm)
    upper = (idx[:, None] < idx[None, :]).astype(BF16)
    eidx = jnp.arange(N_EXPERTS)
    lower = (eidx[None, :] < eidx[:, None]).astype(BF16)
    const = lambda shape: pl.BlockSpec(shape, lambda i: (0,) * len(shape))
    nt = n // tm
    return pl.pallas_call(
        functools.partial(_out_proj_kernel, tm=tm),
        grid=(nt,),
        in_specs=[pl.BlockSpec((N_SLABS, tm, LANES), lambda i: (0, i, 0)),
                  pl.BlockSpec((tm, CONV_DIM), lambda i: (i, 0)),
                  pl.BlockSpec((tm, D_MODEL), lambda i: (i, 0)),
                  const((D_MODEL, D_MODEL)), const((1, D_MODEL)),
                  const((ROUTER_ROWS, D_MODEL)), const((ROUTER_ROWS, D_MODEL)),
                  const((tm, tm)), const((N_EXPERTS, N_EXPERTS))],
        out_specs=[pl.BlockSpec((tm, D_MODEL), lambda i: (i, 0)),
                   pl.BlockSpec((tm, D_MODEL), lambda i: (i, 0)),
                   pl.BlockSpec((1, SUBLANES, tm), lambda i: (i, 0, 0)),
                   pl.BlockSpec((tm, LANES), lambda i: (i, 0)),
                   pl.BlockSpec((1, N_EXPERTS, LANES), lambda i: (i, 0, 0))],
        out_shape=[jax.ShapeDtypeStruct((n, D_MODEL), F32),
                   jax.ShapeDtypeStruct((n, D_MODEL), BF16),
                   jax.ShapeDtypeStruct((nt, SUBLANES, tm), F32),
                   jax.ShapeDtypeStruct((n, LANES), F32),
                   jax.ShapeDtypeStruct((nt, N_EXPERTS, LANES), F32)],
        compiler_params=_params("arbitrary"),
        name="out_proj",
    )(attn, conv, h, wo_bf16, g, wr_hi, wr_lo, upper, lower)


def _moe_plan(cnt_tiles, max_tiles):
    c = cnt_tiles.astype(jnp.int32)
    rows_e = jnp.sum(c, axis=0)
    tiles_e = (rows_e + MOE_TILE - 1) // MOE_TILE
    cum = jnp.cumsum(tiles_e)
    total = cum[-1]
    first_tile = cum - tiles_e

    run_start = jnp.cumsum(c, axis=0) - c
    local_start = jnp.cumsum(c, axis=1) - c
    local_end = local_start + c
    row = jnp.arange(TABLE_LANES, dtype=jnp.int32) * RUN_ALIGN
    owner = jnp.sum(local_end[:, None, :] <= row[None, :, None], axis=-1)
    owner = jnp.minimum(owner, N_EXPERTS - 1)
    base = first_tile[None] * MOE_TILE + run_start - local_start
    n_chunks = jnp.sum(c, axis=1) // RUN_ALIGN
    table = jnp.take_along_axis(base, owner, axis=1) + row[None]
    table = jnp.where(jnp.arange(TABLE_LANES)[None] < n_chunks[:, None], table, 0)

    g = jnp.minimum(jnp.arange(max_tiles, dtype=jnp.int32), total - 1)
    eid = jnp.minimum(jnp.sum(cum[None, :] <= g[:, None], axis=-1), N_EXPERTS - 1).astype(jnp.int32)
    n_valid = jnp.clip(rows_e[eid] - (g - first_tile[eid]) * MOE_TILE, 0, MOE_TILE)
    return (table.astype(jnp.int32), n_chunks.astype(jnp.int32), eid,
            n_valid.astype(jnp.int32), total.astype(jnp.int32).reshape(1))


def _dispatch_kernel(tbl_ref, nch_ref, xn_ref, route_ref, *rest, tm, rl, chained):
    xs_hbm, lsort, sem = rest[1:] if chained else rest
    i = pl.program_id(0)
    slot = i % 2
    rows = route_ref[0]
    pos1 = rows[0:1].astype(jnp.int32)
    pos2 = rows[1:2].astype(jnp.int32)
    p = lax.broadcasted_iota(jnp.int32, (rl, tm), 0)
    one_hot = jnp.where((p == pos1) | (p == pos2), 1.0, 0.0).astype(BF16)
    lsort[slot] = jnp.dot(one_hot, xn_ref[...], preferred_element_type=F32)

    def chunk_copy(s, q, row):
        return pltpu.make_async_copy(
            lsort.at[s, pl.ds(pl.multiple_of(q * RUN_ALIGN, RUN_ALIGN), RUN_ALIGN), :],
            xs_hbm.at[pl.ds(pl.multiple_of(row, RUN_ALIGN), RUN_ALIGN), :], sem.at[s])

    def issue(q, carry):
        chunk_copy(slot, q, tbl_ref[i * TABLE_LANES + q]).start()
        return carry

    lax.fori_loop(0, nch_ref[i], issue, 0)

    def wait_tile(tile, s):
        def body(q, carry):
            chunk_copy(s, 0, 0).wait()
            return carry
        lax.fori_loop(0, nch_ref[tile], body, 0)

    @pl.when(i > 0)
    def _():
        wait_tile(i - 1, 1 - slot)

    @pl.when(i == pl.num_programs(0) - 1)
    def _():
        wait_tile(i, slot)


def _dispatch(table, n_chunks, xn, route, tm, total_rows, xs_prev=None):
    nt = xn.shape[0] // tm
    rl = _local_rows(tm)
    chained = xs_prev is not None
    in_specs = [pl.BlockSpec((tm, D_MODEL), lambda i, t, c: (i, 0)),
                pl.BlockSpec((1, SUBLANES, tm), lambda i, t, c: (i, 0, 0))]
    args = [table.reshape(-1), n_chunks, xn, route]
    if chained:
        in_specs.append(pl.BlockSpec(memory_space=pl.ANY))
        args.append(xs_prev)
    return pl.pallas_call(
        functools.partial(_dispatch_kernel, tm=tm, rl=rl, chained=chained),
        grid_spec=pltpu.PrefetchScalarGridSpec(
            num_scalar_prefetch=2, grid=(nt,), in_specs=in_specs,
            out_specs=pl.BlockSpec(memory_space=pl.ANY),
            scratch_shapes=[pltpu.VMEM((2, rl, D_MODEL), F32), pltpu.SemaphoreType.DMA((2,))]),
        out_shape=jax.ShapeDtypeStruct((total_rows, D_MODEL), F32),
        input_output_aliases={4: 0} if chained else {},
        compiler_params=_params("arbitrary"),
        name="moe_dispatch",
    )(*args)


def _expert_kernel(eid_ref, nval_ref, tot_ref, x_ref, wg_ref, wu_ref, wd_ref, o_ref):
    g = pl.program_id(0)

    @pl.when(g < tot_ref[0])
    def _():
        row = lax.broadcasted_iota(jnp.int32, (MOE_TILE, 1), 0)
        x = jnp.where(row < nval_ref[g], x_ref[...], 0.0).astype(BF16)
        gate = jnp.dot(x, wg_ref[0].astype(BF16), preferred_element_type=F32)
        up = jnp.dot(x, wu_ref[0].astype(BF16), preferred_element_type=F32)
        hdn = gate * jax.nn.sigmoid(gate) * up
        o_ref[...] = jnp.dot(hdn.astype(BF16), wd_ref[0].astype(BF16), preferred_element_type=F32)


def _experts(eid, n_valid, total, xs, w_gate, w_up, w_down):
    max_tiles = eid.shape[0]
    rows = lambda g, e, v, t: (jnp.minimum(g, t[0] - 1), 0)
    weight = lambda g, e, v, t: (e[g], 0, 0)
    return pl.pallas_call(
        _expert_kernel,
        grid_spec=pltpu.PrefetchScalarGridSpec(
            num_scalar_prefetch=3, grid=(max_tiles,),
            in_specs=[pl.BlockSpec((MOE_TILE, D_MODEL), rows),
                      pl.BlockSpec((1, D_MODEL, D_EXPERT), weight),
                      pl.BlockSpec((1, D_MODEL, D_EXPERT), weight),
                      pl.BlockSpec((1, D_EXPERT, D_MODEL), weight)],
            out_specs=pl.BlockSpec((MOE_TILE, D_MODEL), rows)),
        out_shape=jax.ShapeDtypeStruct(xs.shape, F32),
        compiler_params=_params("arbitrary"),
        name="moe_experts",
    )(eid, n_valid, total, xs, w_gate, w_up, w_down)


def _combine_kernel(tbl_ref, nch_ref, h1_ref, routet_ref, gf_ref, eo_hbm, o_ref, leo, sem,
                    *, tm, rl, final_norm):
    i = pl.program_id(0)
    n = pl.num_programs(0)
    slot = i % 2

    def chunk_copy(s, q, row):
        return pltpu.make_async_copy(
            eo_hbm.at[pl.ds(pl.multiple_of(row, RUN_ALIGN), RUN_ALIGN), :],
            leo.at[s, pl.ds(pl.multiple_of(q * RUN_ALIGN, RUN_ALIGN), RUN_ALIGN), :], sem.at[s])

    def gather(tile, s):
        def body(q, carry):
            chunk_copy(s, q, tbl_ref[tile * TABLE_LANES + q]).start()
            return carry
        lax.fori_loop(0, nch_ref[tile], body, 0)

    @pl.when(i == 0)
    def _():
        leo[...] = jnp.zeros(leo.shape, F32)
        gather(0, 0)

    @pl.when(i + 1 < n)
    def _():
        gather(i + 1, 1 - slot)

    def wait_body(q, carry):
        chunk_copy(slot, 0, 0).wait()
        return carry

    lax.fori_loop(0, nch_ref[i], wait_body, 0)

    rt = routet_ref[...]
    pos1 = rt[:, 0:1].astype(jnp.int32)
    pos2 = rt[:, 1:2].astype(jnp.int32)
    lane = lax.broadcasted_iota(jnp.int32, (tm, rl), 1)
    weights = jnp.where(lane == pos1, rt[:, 2:3], 0.0) + jnp.where(lane == pos2, rt[:, 3:4], 0.0)
    y = jnp.dot(weights.astype(BF16), leo[slot].astype(BF16), preferred_element_type=F32)
    out = h1_ref[...] + y
    if final_norm:
        out = _rms(out, gf_ref[...])
    o_ref[...] = out


def _combine(table, n_chunks, h1, routet, g_final, eo, tm, final_norm):
    n = h1.shape[0]
    rl = _local_rows(tm)
    return pl.pallas_call(
        functools.partial(_combine_kernel, tm=tm, rl=rl, final_norm=final_norm),
        grid_spec=pltpu.PrefetchScalarGridSpec(
            num_scalar_prefetch=2, grid=(n // tm,),
            in_specs=[pl.BlockSpec((tm, D_MODEL), lambda i, t, c: (i, 0)),
                      pl.BlockSpec((tm, LANES), lambda i, t, c: (i, 0)),
                      pl.BlockSpec((1, D_MODEL), lambda i, t, c: (0, 0)),
                      pl.BlockSpec(memory_space=pl.ANY)],
            out_specs=pl.BlockSpec((tm, D_MODEL), lambda i, t, c: (i, 0)),
            scratch_shapes=[pltpu.VMEM((2, rl, D_MODEL), F32), pltpu.SemaphoreType.DMA((2,))]),
        out_shape=jax.ShapeDtypeStruct((n, D_MODEL), F32),
        compiler_params=_params("arbitrary"),
        name="moe_combine",
    )(table.reshape(-1), n_chunks, h1, routet, g_final, eo)


def _in_proj_sample_kernel(x_ref, g_ref, w_ref, o_ref):
    xn = _rms(x_ref[...], g_ref[...])
    o_ref[...] = jnp.dot(xn.astype(BF16), w_ref[...], preferred_element_type=F32)


def _in_proj_sample(x, g, w_bf16):
    n = x.shape[0]
    return pl.pallas_call(
        _in_proj_sample_kernel,
        out_shape=jax.ShapeDtypeStruct((n, PROJ_WIDTH), F32),
        compiler_params=pltpu.CompilerParams(vmem_limit_bytes=VMEM_LIMIT),
        name="in_proj_sample",
    )(x, g, w_bf16)


def _sample_mix_kernel(q_ref, kn_ref, vn_ref, kt_ref, vt_ref,
                       bc_ref, bn_ref, gb_ref, gc_ref, hc_ref, st_ref, cw_ref,
                       o_ref, conv_ref, st_out_ref, *, t_new):
    contract_last = (((1,), (1,)), ((), ()))
    for h in range(N_HEADS):
        q = (q_ref[0, h] * SCALE).astype(BF16)
        s_c = jnp.dot(q, kt_ref[0, 0, h].astype(BF16), preferred_element_type=F32) + bc_ref[h]
        s_n = lax.dot_general(q, kn_ref[0, h].astype(BF16), contract_last,
                              preferred_element_type=F32) + bn_ref[h]
        m = jnp.maximum(jnp.max(s_c, axis=-1, keepdims=True), jnp.max(s_n, axis=-1, keepdims=True))
        p_c = jnp.exp(s_c - m)
        p_n = jnp.exp(s_n - m)
        den = jnp.sum(p_c, axis=-1, keepdims=True) + jnp.sum(p_n, axis=-1, keepdims=True)
        num = (lax.dot_general(p_c.astype(BF16), vt_ref[0, 0, h].astype(BF16), contract_last,
                               preferred_element_type=F32)
               + jnp.dot(p_n.astype(BF16), vn_ref[0, h].astype(BF16), preferred_element_type=F32))
        out = num / den
        lse = m + jnp.log(den)
        ls = [lse[i * t_new:(i + 1) * t_new] for i in range(len(DILATIONS))]
        os_ = [out[i * t_new:(i + 1) * t_new] for i in range(len(DILATIONS))]
        mm = jnp.maximum(jnp.maximum(ls[0], ls[1]), ls[2])
        w = [jnp.exp(l - mm) for l in ls]
        o_ref[0, h] = (w[0] * os_[0] + w[1] * os_[1] + w[2] * os_[2]) / (w[0] + w[1] + w[2])
    cw = cw_ref[...]
    u = gc_ref[0] * hc_ref[0]
    st = st_ref[0]
    rows = [st[j:j + 1] for j in range(CONV_K - 1)] + [u[t:t + 1] for t in range(t_new)]
    gb = gb_ref[0]
    conv_ref[0] = jnp.concatenate(
        [gb[t:t + 1] * sum(cw[j:j + 1] * rows[t + j] for j in range(CONV_K)) for t in range(t_new)],
        axis=0)
    st_out_ref[0] = jnp.concatenate(rows[t_new:], axis=0)


SAMPLE_ROWS = 16


def _sample_bias_tables(step_bias, t_new, w_buf):
    cache_rows, new_rows = [], []
    for b, d in zip(step_bias, DILATIONS):
        back = b[1:][::-1].T
        for t in range(t_new):
            span = jnp.full((N_HEADS, SEG_KEYS, d), NEG, F32).at[:, :, t % d].set(back)
            span = span.reshape(N_HEADS, SEG_KEYS * d)
            shift = t - t % d
            span = jnp.pad(span[:, :SEG_KEYS * d - shift], ((0, 0), (shift, 0)), constant_values=NEG)
            cache_rows.append(jnp.pad(span, ((0, 0), (w_buf - SEG_KEYS * d, 0)), constant_values=NEG))
            new = jnp.full((N_HEADS, LANES), NEG, F32)
            for t2 in range(t + 1):
                if (t - t2) % d == 0:
                    new = new.at[:, t2].set(b[(t - t2) // d])
            new_rows.append(new)
    pad = SAMPLE_ROWS - len(cache_rows)
    bias_c = jnp.pad(jnp.stack(cache_rows, axis=1), ((0, 0), (0, pad), (0, 0)))
    bias_n = jnp.pad(jnp.stack(new_rows, axis=1), ((0, 0), (0, pad), (0, 0)))
    return bias_c, bias_n


def _sample_mix(q, kn, vn, cache_k, cache_v, layer, bias_c, bias_n, gb, gc, hc, state, conv_w):
    db, t_new = q.shape[0], q.shape[1]
    w_buf = cache_k.shape[2]
    kt = cache_k.transpose(0, 1, 3, 4, 2)
    vt = cache_v.transpose(0, 1, 3, 4, 2)
    n_rows = len(DILATIONS) * t_new
    q_rows = jnp.pad(jnp.tile(q.transpose(0, 2, 1, 3), (1, 1, len(DILATIONS), 1)),
                     ((0, 0), (0, 0), (0, SAMPLE_ROWS - n_rows), (0, 0)))
    kn_pad = jnp.pad(kn.transpose(0, 2, 1, 3), ((0, 0), (0, 0), (0, LANES - t_new), (0, 0)))
    vn_pad = jnp.pad(vn.transpose(0, 2, 1, 3), ((0, 0), (0, 0), (0, LANES - t_new), (0, 0)))
    row = pl.BlockSpec((1, t_new, CONV_DIM), lambda b: (b, 0, 0))
    new_spec = pl.BlockSpec((1, N_HEADS, LANES, HEAD_DIM), lambda b: (b, 0, 0, 0))
    win_spec = pl.BlockSpec((1, 1, N_HEADS, HEAD_DIM, w_buf), lambda b: (layer, b, 0, 0, 0))
    attn, conv, st = pl.pallas_call(
        functools.partial(_sample_mix_kernel, t_new=t_new),
        grid=(db,),
        in_specs=[pl.BlockSpec((1, N_HEADS, SAMPLE_ROWS, HEAD_DIM), lambda b: (b, 0, 0, 0)),
                  new_spec, new_spec, win_spec, win_spec,
                  pl.BlockSpec(bias_c.shape, lambda b: (0, 0, 0)),
                  pl.BlockSpec(bias_n.shape, lambda b: (0, 0, 0)),
                  row, row, row,
                  pl.BlockSpec((1, CONV_K - 1, CONV_DIM), lambda b: (b, 0, 0)),
                  pl.BlockSpec((CONV_K, CONV_DIM), lambda b: (0, 0))],
        out_specs=[pl.BlockSpec((1, N_HEADS, t_new, HEAD_DIM), lambda b: (b, 0, 0, 0)), row,
                   pl.BlockSpec((1, CONV_K - 1, CONV_DIM), lambda b: (b, 0, 0))],
        out_shape=[jax.ShapeDtypeStruct((db, N_HEADS, t_new, HEAD_DIM), F32),
                   jax.ShapeDtypeStruct((db, t_new, CONV_DIM), F32),
                   jax.ShapeDtypeStruct((db, CONV_K - 1, CONV_DIM), F32)],
        compiler_params=_params("arbitrary"),
        name="sample_mix",
    )(q_rows, kn_pad, vn_pad, kt, vt, bias_c, bias_n, gb, gc, hc, state, conv_w)
    return attn.transpose(0, 2, 1, 3), conv, st


def _to_slabs(x):
    n = x.shape[0]
    return x.reshape(n, N_SLABS, LANES).transpose(1, 0, 2)


def _window_from_slabs(x, batch, seq, w_keep):
    win = x.reshape(N_SLABS, batch, seq, LANES)[:, :, seq - w_keep:]
    return win.transpose(1, 2, 0, 3).reshape(batch, w_keep, N_HEADS, HEAD_DIM)


def kernel(x_prompt, x_sample, cache_k, cache_v, state_conv, rel_bias, norm_mix, norm_ffn,
           norm_final, w_in, conv_w, w_out, w_router_group, w_router_expert, w_gate, w_up,
           w_down):
    batch, seq, _ = x_prompt.shape
    db, t_new, _ = x_sample.shape
    depth = w_in.shape[0]
    w_keep = min(MAX_WINDOW, seq)

    hp = x_prompt.reshape(batch * seq, D_MODEL)
    hs = x_sample.reshape(db * t_new, D_MODEL)
    step_bias = _step_bias(rel_bias)
    bias_prompt = _prompt_bias_tables(step_bias)
    bias_c, bias_n = _sample_bias_tables(step_bias, t_new, cache_k.shape[2])
    g_final = norm_final.reshape(1, D_MODEL)

    tm_p = 256
    n_s = db * t_new
    nt_p = batch * seq // tm_p
    pad_per_tile = N_EXPERTS * (RUN_ALIGN - 1)
    max_tiles = -(-(2 * (batch * seq + n_s) + (nt_p + 1) * pad_per_tile) // MOE_TILE) + N_EXPERTS

    pk, pv, pc, sk, sv, sc = [], [], [], [], [], []
    for l in range(depth):
        w_in_b = w_in[l].astype(BF16)
        w_out_b = w_out[l].astype(BF16)
        w_router_t = jnp.pad(jnp.concatenate([w_router_expert[l], w_router_group[l]], axis=1).T,
                             ((0, ROUTER_ROWS - N_EXPERTS - N_GROUPS), (0, 0)))
        g_mix = norm_mix[l].reshape(1, D_MODEL)
        g_ffn = norm_ffn[l].reshape(1, D_MODEL)
        last = l == depth - 1

        q, k, v, conv, u_last = _in_proj(hp, g_mix, w_in_b, conv_w[l], seq)
        attn = _attention_prompt(q, k, v, bias_prompt, batch, seq)
        h1, xn, route, routet, cnt = _out_proj(attn, conv, hp, w_out_b, g_ffn, w_router_t, tm=tm_p)
        pk.append(_window_from_slabs(k, batch, seq, w_keep))
        pv.append(_window_from_slabs(v, batch, seq, w_keep))
        pc.append(u_last)

        proj = _in_proj_sample(hs, g_mix, w_in_b)
        qs = proj[:, :ATTN_WIDTH].reshape(db, t_new, N_HEADS, HEAD_DIM)
        ks = proj[:, ATTN_WIDTH:2 * ATTN_WIDTH].reshape(db, t_new, N_HEADS, HEAD_DIM)
        vs = proj[:, 2 * ATTN_WIDTH:3 * ATTN_WIDTH].reshape(db, t_new, N_HEADS, HEAD_DIM)
        c0 = 3 * ATTN_WIDTH
        gb = proj[:, c0:c0 + CONV_DIM].reshape(db, t_new, CONV_DIM)
        gc = proj[:, c0 + CONV_DIM:c0 + 2 * CONV_DIM].reshape(db, t_new, CONV_DIM)
        hc = proj[:, c0 + 2 * CONV_DIM:].reshape(db, t_new, CONV_DIM)
        attn_s, conv_s, state_s = _sample_mix(qs, ks, vs, cache_k, cache_v, l, bias_c, bias_n,
                                              gb, gc, hc, state_conv[l], conv_w[l])
        h1s, xns, route_s, routet_s, cnt_s = _out_proj(
            _to_slabs(attn_s.reshape(n_s, ATTN_WIDTH)), conv_s.reshape(n_s, CONV_DIM), hs,
            w_out_b, g_ffn, w_router_t, tm=n_s)

        table, n_chunks, eid, n_valid, total = _moe_plan(
            jnp.concatenate([cnt[:, :, 0], cnt_s[:, :, 0]], axis=0), max_tiles)
        xs = _dispatch(table[:nt_p], n_chunks[:nt_p], xn, route, tm_p, max_tiles * MOE_TILE)
        xs = _dispatch(table[nt_p:], n_chunks[nt_p:], xns, route_s, n_s, max_tiles * MOE_TILE,
                       xs_prev=xs)
        eo = _experts(eid, n_valid, total, xs, w_gate[l], w_up[l], w_down[l])
        hp = _combine(table[:nt_p], n_chunks[:nt_p], h1, routet, g_final, eo, tm_p, last)
        hs = _combine(table[nt_p:], n_chunks[nt_p:], h1s, routet_s, g_final, eo, n_s, last)
        sk.append(ks)
        sv.append(vs)
        sc.append(state_s)

    return (hp.reshape(batch, seq, D_MODEL), hs.reshape(db, t_new, D_MODEL),
            jnp.stack(pk), jnp.stack(pv), jnp.stack(pc),
            jnp.stack(sk), jnp.stack(sv), jnp.stack(sc))
```

```python
import functools
import math

import jax
import jax.numpy as jnp
from jax import lax
from jax.experimental import pallas as pl
from jax.experimental.pallas import tpu as pltpu

F32 = jnp.float32
BF16 = jnp.bfloat16

D_MODEL = 1024
N_HEADS = 8
HEAD_DIM = 64
ATTN_WIDTH = N_HEADS * HEAD_DIM
CONV_DIM = D_MODEL - ATTN_WIDTH
CONV_K = 3
PROJ_WIDTH = 3 * ATTN_WIDTH + 3 * CONV_DIM
DILATIONS = (1, 4, 16)
SEG_KEYS = 128
Q_BLOCK = 128
MAX_WINDOW = 2048
N_BUCKETS = 32
MAX_DISTANCE = 2048
N_GROUPS = 4
EXPERTS_PER_GROUP = 8
N_EXPERTS = N_GROUPS * EXPERTS_PER_GROUP
D_EXPERT = D_MODEL // 4
EPS = 1e-6
NEG = -1e30
SCALE = 1.0 / math.sqrt(HEAD_DIM)

LANES = 128
SUBLANES = 8
N_SLABS = ATTN_WIDTH // LANES
VMEM_LIMIT = 48 * 1024 * 1024
UNITS_PER_TRIP = 8


def _params(*sem):
    return pltpu.CompilerParams(dimension_semantics=sem, vmem_limit_bytes=VMEM_LIMIT)


def _rms(x, g):
    return x * lax.rsqrt(jnp.mean(x * x, axis=-1, keepdims=True) + EPS) * g


def _rel_bucket(dist):
    max_exact = N_BUCKETS // 2
    d_f = jnp.maximum(dist, 1).astype(F32)
    large = max_exact + (jnp.log(d_f / max_exact) / math.log(MAX_DISTANCE / max_exact)
                         * (N_BUCKETS - max_exact)).astype(jnp.int32)
    large = jnp.minimum(large, N_BUCKETS - 1)
    return jnp.where(dist < max_exact, dist, large)


def _in_proj_kernel(x_ref, g_ref, w_ref, cw_ref, q_ref, k_ref, v_ref, conv_ref, ulast_ref,
                    ext_ref, *, tm, tiles_per_seq):
    i = pl.program_id(0)
    xn = _rms(x_ref[...], g_ref[...])
    proj = jnp.dot(xn.astype(BF16), w_ref[...], preferred_element_type=F32)
    for s in range(N_SLABS):
        q_ref[s] = proj[:, s * LANES:(s + 1) * LANES]
        k_ref[s] = proj[:, ATTN_WIDTH + s * LANES:ATTN_WIDTH + (s + 1) * LANES]
        v_ref[s] = proj[:, 2 * ATTN_WIDTH + s * LANES:2 * ATTN_WIDTH + (s + 1) * LANES]
    c0 = 3 * ATTN_WIDTH
    gb = proj[:, c0:c0 + CONV_DIM]
    u = proj[:, c0 + CONV_DIM:c0 + 2 * CONV_DIM] * proj[:, c0 + 2 * CONV_DIM:c0 + 3 * CONV_DIM]

    @pl.when(i % tiles_per_seq == 0)
    def _():
        ext_ref[0:SUBLANES, :] = jnp.zeros((SUBLANES, CONV_DIM), F32)

    ext_ref[SUBLANES:SUBLANES + tm, :] = u
    u1 = ext_ref[SUBLANES - 1:SUBLANES - 1 + tm, :]
    u2 = ext_ref[SUBLANES - 2:SUBLANES - 2 + tm, :]
    cw = cw_ref[...]
    conv_ref[...] = gb * (cw[0:1] * u2 + cw[1:2] * u1 + cw[2:3] * u)
    ulast_ref[0] = ext_ref[tm + SUBLANES - 2:tm + SUBLANES, :]
    ext_ref[0:SUBLANES, :] = ext_ref[tm:tm + SUBLANES, :]


def _in_proj(x, g, w_bf16, conv_w, seq_len, tm=256):
    n = x.shape[0]
    n_seq = n // seq_len
    slab = jax.ShapeDtypeStruct((N_SLABS, n, LANES), F32)
    slab_spec = pl.BlockSpec((N_SLABS, tm, LANES), lambda i: (0, i, 0))
    tiles_per_seq = seq_len // tm
    return pl.pallas_call(
        functools.partial(_in_proj_kernel, tm=tm, tiles_per_seq=tiles_per_seq),
        grid=(n // tm,),
        in_specs=[pl.BlockSpec((tm, D_MODEL), lambda i: (i, 0)),
                  pl.BlockSpec((1, D_MODEL), lambda i: (0, 0)),
                  pl.BlockSpec((D_MODEL, PROJ_WIDTH), lambda i: (0, 0)),
                  pl.BlockSpec((CONV_K, CONV_DIM), lambda i: (0, 0))],
        out_specs=[slab_spec, slab_spec, slab_spec,
                   pl.BlockSpec((tm, CONV_DIM), lambda i: (i, 0)),
                   pl.BlockSpec((1, CONV_K - 1, CONV_DIM), lambda i: (i // tiles_per_seq, 0, 0))],
        out_shape=[slab, slab, slab,
                   jax.ShapeDtypeStruct((n, CONV_DIM), F32),
                   jax.ShapeDtypeStruct((n_seq, CONV_K - 1, CONV_DIM), F32)],
        scratch_shapes=[pltpu.VMEM((tm + SUBLANES, CONV_DIM), F32)],
        compiler_params=_params("arbitrary"),
        name="in_proj_prompt",
    )(x, g, w_bf16, conv_w)


def _attn_kernel(q_ref, k_ref, v_ref, bias_ref, o_ref, kcat, vcat, oacc, lacc, *, sb):
    j = pl.program_id(2)

    @pl.when(j == 0)
    def _():
        kcat[0:sb, :] = jnp.zeros((sb, LANES), F32)
        vcat[0:sb, :] = jnp.zeros((sb, LANES), F32)

    @pl.when(j > 0)
    def _():
        kcat[0:sb, :] = kcat[sb:2 * sb, :]
        vcat[0:sb, :] = vcat[sb:2 * sb, :]

    kcat[sb:2 * sb, :] = k_ref[...]
    vcat[sb:2 * sb, :] = v_ref[...]

    lane = lax.broadcasted_iota(jnp.int32, (Q_BLOCK, LANES), 1)
    low = lane < HEAD_DIM
    ones = jnp.ones((2 * Q_BLOCK, LANES), BF16)

    for di, d in enumerate(DILATIONS):
        nq = sb // (Q_BLOCK * d)

        def unit(u, di=di, d=d, nq=nq):
            r = u // nq
            n = u % nq
            qstart = r + d * Q_BLOCK * n
            kstart = sb + qstart - d * Q_BLOCK
            q = q_ref[pl.ds(qstart, Q_BLOCK, stride=d), :] * SCALE
            kk = kcat[pl.ds(kstart, 2 * Q_BLOCK, stride=d), :].astype(BF16)
            vv = vcat[pl.ds(kstart, 2 * Q_BLOCK, stride=d), :].astype(BF16)
            qm = jnp.concatenate([jnp.where(low, q, 0.0), jnp.where(low, 0.0, q)],
                                 axis=0).astype(BF16)
            s = lax.dot_general(qm, kk, (((1,), (1,)), ((), ())),
                                preferred_element_type=F32)
            first = jnp.logical_and(j == 0, n == 0).astype(jnp.int32)
            s = s + bias_ref[di, first]
            m = jnp.max(s, axis=-1, keepdims=True)
            p = jnp.exp(s - m).astype(BF16)
            pv = jnp.dot(p, jnp.concatenate([vv, ones], axis=1),
                         preferred_element_type=F32)
            o_sel = jnp.where(low, pv[0:Q_BLOCK, 0:LANES], pv[Q_BLOCK:, 0:LANES])
            l_sel = jnp.where(low, pv[0:Q_BLOCK, LANES:], pv[Q_BLOCK:, LANES:])
            m_sel = jnp.where(low, jnp.broadcast_to(m[0:Q_BLOCK], (Q_BLOCK, LANES)),
                              jnp.broadcast_to(m[Q_BLOCK:], (Q_BLOCK, LANES)))
            oacc[di, pl.ds(qstart, Q_BLOCK, stride=d), :] = o_sel / l_sel
            lacc[di, pl.ds(qstart, Q_BLOCK, stride=d), :] = m_sel + jnp.log(l_sel)

        def group(g, carry, unit=unit):
            for uu in range(UNITS_PER_TRIP):
                unit(g * UNITS_PER_TRIP + uu)
            return carry

        lax.fori_loop(0, sb // Q_BLOCK // UNITS_PER_TRIP, group, 0)

    def merge(c, carry):
        rows = pl.ds(pl.multiple_of(c * 256, 256), 256)
        l0, l1, l2 = lacc[0, rows, :], lacc[1, rows, :], lacc[2, rows, :]
        mm = jnp.maximum(jnp.maximum(l0, l1), l2)
        w0, w1, w2 = jnp.exp(l0 - mm), jnp.exp(l1 - mm), jnp.exp(l2 - mm)
        num = w0 * oacc[0, rows, :] + w1 * oacc[1, rows, :] + w2 * oacc[2, rows, :]
        o_ref[rows, :] = num / (w0 + w1 + w2)
        return carry

    lax.fori_loop(0, sb // 256, merge, 0)


def _step_bias(rel_bias):
    steps = jnp.arange(SEG_KEYS + 1)
    return [rel_bias[_rel_bucket(steps * d)].astype(F32) for d in DILATIONS]


def _prompt_bias_tables(step_bias):
    width = 3 * Q_BLOCK
    k_loc = jnp.arange(2 * Q_BLOCK)
    tables = []
    for b in step_bias:
        g = jnp.concatenate([jnp.full((N_HEADS, Q_BLOCK - 1), NEG, F32), b.T,
                             jnp.full((N_HEADS, width - Q_BLOCK - SEG_KEYS), NEG, F32)], axis=1)
        flat = jnp.broadcast_to(g[:, None, :], (N_HEADS, Q_BLOCK + 1, width)).reshape(N_HEADS, -1)
        x = flat[:, :Q_BLOCK * (width + 1)].reshape(N_HEADS, Q_BLOCK, width + 1)
        t = x[:, :, :2 * Q_BLOCK][:, :, ::-1]
        t_first = jnp.where(k_loc[None, None, :] >= Q_BLOCK, t, NEG)
        tables.append(jnp.stack([t.reshape(N_SLABS, 2 * Q_BLOCK, 2 * Q_BLOCK),
                                 t_first.reshape(N_SLABS, 2 * Q_BLOCK, 2 * Q_BLOCK)]))
    return jnp.stack(tables)


def _attention_prompt(q, k, v, bias_tab, n_seq, seq_len, sb=2048):
    n = q.shape[1]
    nsb = seq_len // sb
    blk = pl.BlockSpec((None, sb, LANES), lambda b, s, j: (s, b * nsb + j, 0))
    return pl.pallas_call(
        functools.partial(_attn_kernel, sb=sb),
        grid=(n_seq, N_SLABS, nsb),
        in_specs=[blk, blk, blk,
                  pl.BlockSpec((len(DILATIONS), 2, None, 2 * Q_BLOCK, 2 * Q_BLOCK),
                               lambda b, s, j: (0, 0, s, 0, 0))],
        out_specs=blk,
        out_shape=jax.ShapeDtypeStruct((N_SLABS, n, LANES), F32),
        scratch_shapes=[pltpu.VMEM((2 * sb, LANES), F32), pltpu.VMEM((2 * sb, LANES), F32),
                        pltpu.VMEM((len(DILATIONS), sb, LANES), F32),
                        pltpu.VMEM((len(DILATIONS), sb, LANES), F32)],
        compiler_params=_params("arbitrary", "arbitrary", "arbitrary"),
        name="attention_prompt",
    )(q, k, v, bias_tab)


ROUTER_ROWS = 48
MOE_TILE = 256
RUN_ALIGN = 2 * SUBLANES
TABLE_LANES = 128


def _local_rows(tm):
    return -(-(2 * tm + N_EXPERTS * (RUN_ALIGN - 1)) // LANES) * LANES


def _route(lg):
    tm = lg.shape[1]
    gl = lg[N_EXPERTS:N_EXPERTS + N_GROUPS]
    grow = lax.broadcasted_iota(jnp.int32, (N_GROUPS, tm), 0)
    gmax = jnp.max(gl, axis=0, keepdims=True)
    gsum = jnp.sum(jnp.exp(gl - gmax), axis=0, keepdims=True)
    gidx = jnp.min(jnp.where(gl == gmax, grow, N_GROUPS), axis=0, keepdims=True)
    p_grp = 1.0 / gsum
    el = lg[0:N_EXPERTS]
    e = lax.broadcasted_iota(jnp.int32, (N_EXPERTS, tm), 0)
    in_grp = (e // EXPERTS_PER_GROUP) == gidx
    elog = jnp.where(in_grp, el, -jnp.inf)
    emax = jnp.max(elog, axis=0, keepdims=True)
    eexp = jnp.exp(elog - emax)
    eprob = jnp.where(in_grp, eexp / jnp.sum(eexp, axis=0, keepdims=True), -1.0)
    p1 = jnp.max(eprob, axis=0, keepdims=True)
    i1 = jnp.min(jnp.where(eprob == p1, e, N_EXPERTS), axis=0, keepdims=True)
    rest = jnp.where(e == i1, -1.0, eprob)
    p2 = jnp.max(rest, axis=0, keepdims=True)
    i2 = jnp.min(jnp.where(rest == p2, e, N_EXPERTS), axis=0, keepdims=True)
    tot = p1 + p2
    return e == i1, e == i2, p_grp * (p1 / tot), p_grp * (p2 / tot)


def _split_bf16(x):
    hi = x.astype(BF16)
    return hi, (x - hi.astype(F32)).astype(BF16)


def _out_proj_kernel(attn_ref, conv_ref, h_ref, wo_ref, g_ref, wrh_ref, wrl_ref, upper_ref, lower_ref,
                     h1_ref, xn_ref, route_ref, routet_ref, cnt_ref, *, tm, sub):
    contract_last = (((1,), (1,)), ((), ()))
    for t in range(tm // sub):
        rs = slice(t * sub, (t + 1) * sub)
        mix = jnp.concatenate([attn_ref[s, rs, :] for s in range(N_SLABS)] + [conv_ref[rs, :]], axis=1)
        h1 = h_ref[rs, :] + jnp.dot(mix.astype(BF16), wo_ref[...], preferred_element_type=F32)
        h1_ref[rs, :] = h1
        xn = _rms(h1, g_ref[...])
        xh, xl = _split_bf16(xn)
        xn_ref[rs, :] = xh
        lg = (lax.dot_general(wrh_ref[...], xh, contract_last, preferred_element_type=F32)
              + lax.dot_general(wrh_ref[...], xl, contract_last, preferred_element_type=F32)
              + lax.dot_general(wrl_ref[...], xh, contract_last, preferred_element_type=F32))
        sel1, sel2, g1, g2 = _route(lg)
        chosen = jnp.where(sel1 | sel2, 1.0, 0.0)
        rank = jnp.dot(chosen.astype(BF16), upper_ref[...], preferred_element_type=F32)
        cnt = jnp.sum(chosen, axis=1, keepdims=True)
        cnt_pad = jnp.floor((cnt + (RUN_ALIGN - 1)) * (1.0 / RUN_ALIGN)) * RUN_ALIGN
        cnt_b = jnp.broadcast_to(cnt_pad, (N_EXPERTS, LANES))
        start = jnp.dot(lower_ref[...], cnt_b.astype(BF16), preferred_element_type=F32)
        pos = start[:, 0:1] + rank
        pos1 = jnp.sum(jnp.where(sel1, pos, 0.0), axis=0, keepdims=True)
        pos2 = jnp.sum(jnp.where(sel2, pos, 0.0), axis=0, keepdims=True)
        rows = jnp.concatenate([pos1, pos2, g1, g2, jnp.zeros((SUBLANES - 4, sub), F32)], axis=0)
        route_ref[t] = rows
        routet_ref[rs, :] = jnp.concatenate([rows, jnp.zeros((LANES - SUBLANES, sub), F32)], axis=0).T
        cnt_ref[t] = cnt_b


def _out_proj(attn, conv, h, wo_bf16, g, w_router_t, tm, sub):
    n = h.shape[0]
    wr_hi, wr_lo = _split_bf16(w_router_t)
    idx = jnp.arange(sub)
    upper = (idx[:, None] < idx[None, :]).astype(BF16)
    eidx = jnp.arange(N_EXPERTS)
    lower = (eidx[None, :] < eidx[:, None]).astype(BF16)
    const = lambda shape: pl.BlockSpec(shape, lambda i: (0,) * len(shape))
    per_step = tm // sub
    return pl.pallas_call(
        functools.partial(_out_proj_kernel, tm=tm, sub=sub),
        grid=(n // tm,),
        in_specs=[pl.BlockSpec((N_SLABS, tm, LANES), lambda i: (0, i, 0)),
                  pl.BlockSpec((tm, CONV_DIM), lambda i: (i, 0)),
                  pl.BlockSpec((tm, D_MODEL), lambda i: (i, 0)),
                  const((D_MODEL, D_MODEL)), const((1, D_MODEL)),
                  const((ROUTER_ROWS, D_MODEL)), const((ROUTER_ROWS, D_MODEL)),
                  const((sub, sub)), const((N_EXPERTS, N_EXPERTS))],
        out_specs=[pl.BlockSpec((tm, D_MODEL), lambda i: (i, 0)),
                   pl.BlockSpec((tm, D_MODEL), lambda i: (i, 0)),
                   pl.BlockSpec((per_step, SUBLANES, sub), lambda i: (i, 0, 0)),
                   pl.BlockSpec((tm, LANES), lambda i: (i, 0)),
                   pl.BlockSpec((per_step, N_EXPERTS, LANES), lambda i: (i, 0, 0))],
        out_shape=[jax.ShapeDtypeStruct((n, D_MODEL), F32),
                   jax.ShapeDtypeStruct((n, D_MODEL), BF16),
                   jax.ShapeDtypeStruct((n // sub, SUBLANES, sub), F32),
                   jax.ShapeDtypeStruct((n, LANES), F32),
                   jax.ShapeDtypeStruct((n // sub, N_EXPERTS, LANES), F32)],
        compiler_params=_params("arbitrary"),
        name="out_proj",
    )(attn, conv, h, wo_bf16, g, wr_hi, wr_lo, upper, lower)


def _moe_plan(cnt_tiles, max_tiles):
    c = cnt_tiles.astype(jnp.int32)
    rows_e = jnp.sum(c, axis=0)
    tiles_e = (rows_e + MOE_TILE - 1) // MOE_TILE
    cum = jnp.cumsum(tiles_e)
    total = cum[-1]
    first_tile = cum - tiles_e

    run_start = jnp.cumsum(c, axis=0) - c
    local_start = jnp.cumsum(c, axis=1) - c
    local_end = local_start + c
    row = jnp.arange(TABLE_LANES, dtype=jnp.int32) * RUN_ALIGN
    inside = (local_start[:, None, :] <= row[None, :, None]) & (row[None, :, None] < local_end[:, None, :])
    base = first_tile[None] * MOE_TILE + run_start - local_start
    n_chunks = jnp.sum(c, axis=1) // RUN_ALIGN
    table = jnp.sum(jnp.where(inside, base[:, None, :] + row[None, :, None], 0), axis=-1)

    g = jnp.minimum(jnp.arange(max_tiles, dtype=jnp.int32), total - 1)
    mine = (first_tile[None, :] <= g[:, None]) & (g[:, None] < cum[None, :])
    eid = jnp.sum(jnp.where(mine, jnp.arange(N_EXPERTS, dtype=jnp.int32)[None], 0), axis=-1)
    left = jnp.sum(jnp.where(mine, rows_e[None] - (g[:, None] - first_tile[None]) * MOE_TILE, 0), axis=-1)
    n_valid = jnp.clip(left, 0, MOE_TILE)
    return (table.astype(jnp.int32), n_chunks.astype(jnp.int32), eid,
            n_valid.astype(jnp.int32), total.astype(jnp.int32).reshape(1))


def _dispatch_kernel(tbl_ref, nch_ref, xn_ref, route_ref, *rest, tm, rl, chained):
    xs_hbm, lsort, sem = rest[1:] if chained else rest
    i = pl.program_id(0)
    slot = i % 2
    rows = route_ref[0]
    pos1 = rows[0:1].astype(jnp.int32)
    pos2 = rows[1:2].astype(jnp.int32)
    p = lax.broadcasted_iota(jnp.int32, (rl, tm), 0)
    one_hot = jnp.where((p == pos1) | (p == pos2), 1.0, 0.0).astype(BF16)
    lsort[slot] = jnp.dot(one_hot, xn_ref[...], preferred_element_type=F32).astype(BF16)

    def chunk_copy(s, q, row):
        return pltpu.make_async_copy(
            lsort.at[s, pl.ds(pl.multiple_of(q * RUN_ALIGN, RUN_ALIGN), RUN_ALIGN), :],
            xs_hbm.at[pl.ds(pl.multiple_of(row, RUN_ALIGN), RUN_ALIGN), :], sem.at[s])

    def issue(q, carry):
        chunk_copy(slot, q, tbl_ref[i * TABLE_LANES + q]).start()
        return carry

    lax.fori_loop(0, nch_ref[i], issue, 0)

    def wait_tile(tile, s):
        def body(q, carry):
            chunk_copy(s, 0, 0).wait()
            return carry
        lax.fori_loop(0, nch_ref[tile], body, 0)

    @pl.when(i > 0)
    def _():
        wait_tile(i - 1, 1 - slot)

    @pl.when(i == pl.num_programs(0) - 1)
    def _():
        wait_tile(i, slot)


def _dispatch(table, n_chunks, xn, route, tm, total_rows, xs_prev=None):
    nt = xn.shape[0] // tm
    rl = _local_rows(tm)
    chained = xs_prev is not None
    in_specs = [pl.BlockSpec((tm, D_MODEL), lambda i, t, c: (i, 0)),
                pl.BlockSpec((1, SUBLANES, tm), lambda i, t, c: (i, 0, 0))]
    args = [table.reshape(-1), n_chunks, xn, route]
    if chained:
        in_specs.append(pl.BlockSpec(memory_space=pl.ANY))
        args.append(xs_prev)
    return pl.pallas_call(
        functools.partial(_dispatch_kernel, tm=tm, rl=rl, chained=chained),
        grid_spec=pltpu.PrefetchScalarGridSpec(
            num_scalar_prefetch=2, grid=(nt,), in_specs=in_specs,
            out_specs=pl.BlockSpec(memory_space=pl.ANY),
            scratch_shapes=[pltpu.VMEM((2, rl, D_MODEL), BF16), pltpu.SemaphoreType.DMA((2,))]),
        out_shape=jax.ShapeDtypeStruct((total_rows, D_MODEL), BF16),
        input_output_aliases={4: 0} if chained else {},
        compiler_params=_params("arbitrary"),
        name="moe_dispatch",
    )(*args)


def _expert_kernel(eid_ref, nval_ref, tot_ref, x_ref, wg_ref, wu_ref, wd_ref, o_ref):
    g = pl.program_id(0)

    @pl.when(g < tot_ref[0])
    def _():
        row = lax.broadcasted_iota(jnp.int32, (MOE_TILE, 1), 0)
        x = x_ref[...]
        x = jnp.where(row < nval_ref[g], x, jnp.zeros_like(x))
        gate = jnp.dot(x, wg_ref[0, 0].astype(BF16), preferred_element_type=F32)
        up = jnp.dot(x, wu_ref[0, 0].astype(BF16), preferred_element_type=F32)
        hdn = gate * jax.nn.sigmoid(gate) * up
        o_ref[...] = jnp.dot(hdn.astype(BF16), wd_ref[0, 0].astype(BF16),
                             preferred_element_type=F32).astype(BF16)


def _experts(eid, n_valid, total, xs, w_gate, w_up, w_down, layer):
    max_tiles = eid.shape[0]
    rows = lambda g, e, v, t: (jnp.minimum(g, t[0] - 1), 0)
    weight = lambda g, e, v, t: (layer, e[g], 0, 0)
    return pl.pallas_call(
        _expert_kernel,
        grid_spec=pltpu.PrefetchScalarGridSpec(
            num_scalar_prefetch=3, grid=(max_tiles,),
            in_specs=[pl.BlockSpec((MOE_TILE, D_MODEL), rows),
                      pl.BlockSpec((1, 1, D_MODEL, D_EXPERT), weight),
                      pl.BlockSpec((1, 1, D_MODEL, D_EXPERT), weight),
                      pl.BlockSpec((1, 1, D_EXPERT, D_MODEL), weight)],
            out_specs=pl.BlockSpec((MOE_TILE, D_MODEL), rows)),
        out_shape=jax.ShapeDtypeStruct(xs.shape, BF16),
        compiler_params=_params("arbitrary"),
        name="moe_experts",
    )(eid, n_valid, total, xs, w_gate, w_up, w_down)


def _combine_kernel(tbl_ref, nch_ref, h1_ref, routet_ref, gf_ref, eo_hbm, o_ref, leo, sem,
                    *, tm, rl, final_norm):
    i = pl.program_id(0)
    n = pl.num_programs(0)
    slot = i % 2

    def chunk_copy(s, q, row):
        return pltpu.make_async_copy(
            eo_hbm.at[pl.ds(pl.multiple_of(row, RUN_ALIGN), RUN_ALIGN), :],
            leo.at[s, pl.ds(pl.multiple_of(q * RUN_ALIGN, RUN_ALIGN), RUN_ALIGN), :], sem.at[s])

    def gather(tile, s):
        def body(q, carry):
            chunk_copy(s, q, tbl_ref[tile * TABLE_LANES + q]).start()
            return carry
        lax.fori_loop(0, nch_ref[tile], body, 0)

    @pl.when(i == 0)
    def _():
        leo[...] = jnp.zeros(leo.shape, BF16)
        gather(0, 0)

    @pl.when(i + 1 < n)
    def _():
        gather(i + 1, 1 - slot)

    def wait_body(q, carry):
        chunk_copy(slot, 0, 0).wait()
        return carry

    lax.fori_loop(0, nch_ref[i], wait_body, 0)

    rt = routet_ref[...]
    pos1 = rt[:, 0:1].astype(jnp.int32)
    pos2 = rt[:, 1:2].astype(jnp.int32)
    lane = lax.broadcasted_iota(jnp.int32, (tm, rl), 1)
    weights = jnp.where(lane == pos1, rt[:, 2:3], 0.0) + jnp.where(lane == pos2, rt[:, 3:4], 0.0)
    y = jnp.dot(weights.astype(BF16), leo[slot], preferred_element_type=F32)
    out = h1_ref[...] + y
    if final_norm:
        out = _rms(out, gf_ref[...])
    o_ref[...] = out


def _combine(table, n_chunks, h1, routet, g_final, eo, tm, final_norm):
    n = h1.shape[0]
    rl = _local_rows(tm)
    return pl.pallas_call(
        functools.partial(_combine_kernel, tm=tm, rl=rl, final_norm=final_norm),
        grid_spec=pltpu.PrefetchScalarGridSpec(
            num_scalar_prefetch=2, grid=(n // tm,),
            in_specs=[pl.BlockSpec((tm, D_MODEL), lambda i, t, c: (i, 0)),
                      pl.BlockSpec((tm, LANES), lambda i, t, c: (i, 0)),
                      pl.BlockSpec((1, D_MODEL), lambda i, t, c: (0, 0)),
                      pl.BlockSpec(memory_space=pl.ANY)],
            out_specs=pl.BlockSpec((tm, D_MODEL), lambda i, t, c: (i, 0)),
            scratch_shapes=[pltpu.VMEM((2, rl, D_MODEL), BF16), pltpu.SemaphoreType.DMA((2,))]),
        out_shape=jax.ShapeDtypeStruct((n, D_MODEL), F32),
        compiler_params=_params("arbitrary"),
        name="moe_combine",
    )(table.reshape(-1), n_chunks, h1, routet, g_final, eo)


def _in_proj_sample_kernel(x_ref, g_ref, w_ref, o_ref):
    xn = _rms(x_ref[...], g_ref[...])
    o_ref[...] = jnp.dot(xn.astype(BF16), w_ref[...], preferred_element_type=F32)


def _in_proj_sample(x, g, w_bf16):
    n = x.shape[0]
    return pl.pallas_call(
        _in_proj_sample_kernel,
        out_shape=jax.ShapeDtypeStruct((n, PROJ_WIDTH), F32),
        compiler_params=pltpu.CompilerParams(vmem_limit_bytes=VMEM_LIMIT),
        name="in_proj_sample",
    )(x, g, w_bf16)


def _sample_mix_kernel(q_ref, kn_ref, vn_ref, kt_ref, vt_ref,
                       bc_ref, bn_ref, gb_ref, gc_ref, hc_ref, st_ref, cw_ref,
                       o_ref, conv_ref, st_out_ref, *, t_new):
    contract_last = (((1,), (1,)), ((), ()))
    for h in range(N_HEADS):
        q = (q_ref[0, h] * SCALE).astype(BF16)
        s_c = jnp.dot(q, kt_ref[0, 0, h].astype(BF16), preferred_element_type=F32) + bc_ref[h]
        s_n = lax.dot_general(q, kn_ref[0, h].astype(BF16), contract_last,
                              preferred_element_type=F32) + bn_ref[h]
        m = jnp.maximum(jnp.max(s_c, axis=-1, keepdims=True), jnp.max(s_n, axis=-1, keepdims=True))
        p_c = jnp.exp(s_c - m)
        p_n = jnp.exp(s_n - m)
        den = jnp.sum(p_c, axis=-1, keepdims=True) + jnp.sum(p_n, axis=-1, keepdims=True)
        num = (lax.dot_general(p_c.astype(BF16), vt_ref[0, 0, h].astype(BF16), contract_last,
                               preferred_element_type=F32)
               + jnp.dot(p_n.astype(BF16), vn_ref[0, h].astype(BF16), preferred_element_type=F32))
        out = num / den
        lse = m + jnp.log(den)
        ls = [lse[i * t_new:(i + 1) * t_new] for i in range(len(DILATIONS))]
        os_ = [out[i * t_new:(i + 1) * t_new] for i in range(len(DILATIONS))]
        mm = jnp.maximum(jnp.maximum(ls[0], ls[1]), ls[2])
        w = [jnp.exp(l - mm) for l in ls]
        o_ref[0, h] = (w[0] * os_[0] + w[1] * os_[1] + w[2] * os_[2]) / (w[0] + w[1] + w[2])
    cw = cw_ref[...]
    u = gc_ref[0] * hc_ref[0]
    st = st_ref[0]
    rows = [st[j:j + 1] for j in range(CONV_K - 1)] + [u[t:t + 1] for t in range(t_new)]
    gb = gb_ref[0]
    conv_ref[0] = jnp.concatenate(
        [gb[t:t + 1] * sum(cw[j:j + 1] * rows[t + j] for j in range(CONV_K)) for t in range(t_new)],
        axis=0)
    st_out_ref[0] = jnp.concatenate(rows[t_new:], axis=0)


SAMPLE_ROWS = 16


def _sample_bias_tables(step_bias, t_new, w_buf):
    cache_rows, new_rows = [], []
    for b, d in zip(step_bias, DILATIONS):
        back = b[1:][::-1].T
        for t in range(t_new):
            span = jnp.full((N_HEADS, SEG_KEYS, d), NEG, F32).at[:, :, t % d].set(back)
            span = span.reshape(N_HEADS, SEG_KEYS * d)
            shift = t - t % d
            span = jnp.pad(span[:, :SEG_KEYS * d - shift], ((0, 0), (shift, 0)), constant_values=NEG)
            cache_rows.append(jnp.pad(span, ((0, 0), (w_buf - SEG_KEYS * d, 0)), constant_values=NEG))
            new = jnp.full((N_HEADS, LANES), NEG, F32)
            for t2 in range(t + 1):
                if (t - t2) % d == 0:
                    new = new.at[:, t2].set(b[(t - t2) // d])
            new_rows.append(new)
    pad = SAMPLE_ROWS - len(cache_rows)
    bias_c = jnp.pad(jnp.stack(cache_rows, axis=1), ((0, 0), (0, pad), (0, 0)))
    bias_n = jnp.pad(jnp.stack(new_rows, axis=1), ((0, 0), (0, pad), (0, 0)))
    return bias_c, bias_n


def _sample_mix(q, kn, vn, cache_k, cache_v, layer, bias_c, bias_n, gb, gc, hc, state, conv_w):
    db, t_new = q.shape[0], q.shape[1]
    w_buf = cache_k.shape[2]
    kt = cache_k.transpose(0, 1, 3, 4, 2)
    vt = cache_v.transpose(0, 1, 3, 4, 2)
    n_rows = len(DILATIONS) * t_new
    q_rows = jnp.pad(jnp.tile(q.transpose(0, 2, 1, 3), (1, 1, len(DILATIONS), 1)),
                     ((0, 0), (0, 0), (0, SAMPLE_ROWS - n_rows), (0, 0)))
    kn_pad = jnp.pad(kn.transpose(0, 2, 1, 3), ((0, 0), (0, 0), (0, LANES - t_new), (0, 0)))
    vn_pad = jnp.pad(vn.transpose(0, 2, 1, 3), ((0, 0), (0, 0), (0, LANES - t_new), (0, 0)))
    row = pl.BlockSpec((1, t_new, CONV_DIM), lambda b: (b, 0, 0))
    new_spec = pl.BlockSpec((1, N_HEADS, LANES, HEAD_DIM), lambda b: (b, 0, 0, 0))
    win_spec = pl.BlockSpec((1, 1, N_HEADS, HEAD_DIM, w_buf), lambda b: (layer, b, 0, 0, 0))
    attn, conv, st = pl.pallas_call(
        functools.partial(_sample_mix_kernel, t_new=t_new),
        grid=(db,),
        in_specs=[pl.BlockSpec((1, N_HEADS, SAMPLE_ROWS, HEAD_DIM), lambda b: (b, 0, 0, 0)),
                  new_spec, new_spec, win_spec, win_spec,
                  pl.BlockSpec(bias_c.shape, lambda b: (0, 0, 0)),
                  pl.BlockSpec(bias_n.shape, lambda b: (0, 0, 0)),
                  row, row, row,
                  pl.BlockSpec((1, CONV_K - 1, CONV_DIM), lambda b: (b, 0, 0)),
                  pl.BlockSpec((CONV_K, CONV_DIM), lambda b: (0, 0))],
        out_specs=[pl.BlockSpec((1, N_HEADS, t_new, HEAD_DIM), lambda b: (b, 0, 0, 0)), row,
                   pl.BlockSpec((1, CONV_K - 1, CONV_DIM), lambda b: (b, 0, 0))],
        out_shape=[jax.ShapeDtypeStruct((db, N_HEADS, t_new, HEAD_DIM), F32),
                   jax.ShapeDtypeStruct((db, t_new, CONV_DIM), F32),
                   jax.ShapeDtypeStruct((db, CONV_K - 1, CONV_DIM), F32)],
        compiler_params=_params("arbitrary"),
        name="sample_mix",
    )(q_rows, kn_pad, vn_pad, kt, vt, bias_c, bias_n, gb, gc, hc, state, conv_w)
    return attn.transpose(0, 2, 1, 3), conv, st


def _to_slabs(x):
    n = x.shape[0]
    return x.reshape(n, N_SLABS, LANES).transpose(1, 0, 2)


def _window_from_slabs(x, batch, seq, w_keep):
    win = x.reshape(N_SLABS, batch, seq, LANES)[:, :, seq - w_keep:]
    return win.transpose(1, 2, 0, 3).reshape(batch, w_keep, N_HEADS, HEAD_DIM)


def kernel(x_prompt, x_sample, cache_k, cache_v, state_conv, rel_bias, norm_mix, norm_ffn,
           norm_final, w_in, conv_w, w_out, w_router_group, w_router_expert, w_gate, w_up,
           w_down):
    batch, seq, _ = x_prompt.shape
    db, t_new, _ = x_sample.shape
    depth = w_in.shape[0]
    w_keep = min(MAX_WINDOW, seq)

    hp = x_prompt.reshape(batch * seq, D_MODEL)
    hs = x_sample.reshape(db * t_new, D_MODEL)
    step_bias = _step_bias(rel_bias)
    bias_prompt = _prompt_bias_tables(step_bias)
    bias_c, bias_n = _sample_bias_tables(step_bias, t_new, cache_k.shape[2])
    g_final = norm_final.reshape(1, D_MODEL)

    tm_p = 256
    n_s = db * t_new
    nt_p = batch * seq // tm_p
    pad_per_tile = N_EXPERTS * (RUN_ALIGN - 1)
    max_tiles = -(-(2 * (batch * seq + n_s) + (nt_p + 1) * pad_per_tile) // MOE_TILE) + N_EXPERTS

    pk, pv, pc, sk, sv, sc = [], [], [], [], [], []
    for l in range(depth):
        w_in_b = w_in[l].astype(BF16)
        w_out_b = w_out[l].astype(BF16)
        w_router_t = jnp.pad(jnp.concatenate([w_router_expert[l], w_router_group[l]], axis=1).T,
                             ((0, ROUTER_ROWS - N_EXPERTS - N_GROUPS), (0, 0)))
        g_mix = norm_mix[l].reshape(1, D_MODEL)
        g_ffn = norm_ffn[l].reshape(1, D_MODEL)
        last = l == depth - 1

        q, k, v, conv, u_last = _in_proj(hp, g_mix, w_in_b, conv_w[l], seq)
        attn = _attention_prompt(q, k, v, bias_prompt, batch, seq)
        h1, xn, route, routet, cnt = _out_proj(attn, conv, hp, w_out_b, g_ffn, w_router_t,
                                               tm=2 * tm_p, sub=tm_p)
        pk.append(_window_from_slabs(k, batch, seq, w_keep))
        pv.append(_window_from_slabs(v, batch, seq, w_keep))
        pc.append(u_last)

        proj = _in_proj_sample(hs, g_mix, w_in_b)
        qs = proj[:, :ATTN_WIDTH].reshape(db, t_new, N_HEADS, HEAD_DIM)
        ks = proj[:, ATTN_WIDTH:2 * ATTN_WIDTH].reshape(db, t_new, N_HEADS, HEAD_DIM)
        vs = proj[:, 2 * ATTN_WIDTH:3 * ATTN_WIDTH].reshape(db, t_new, N_HEADS, HEAD_DIM)
        c0 = 3 * ATTN_WIDTH
        gb = proj[:, c0:c0 + CONV_DIM].reshape(db, t_new, CONV_DIM)
        gc = proj[:, c0 + CONV_DIM:c0 + 2 * CONV_DIM].reshape(db, t_new, CONV_DIM)
        hc = proj[:, c0 + 2 * CONV_DIM:].reshape(db, t_new, CONV_DIM)
        attn_s, conv_s, state_s = _sample_mix(qs, ks, vs, cache_k, cache_v, l, bias_c, bias_n,
                                              gb, gc, hc, state_conv[l], conv_w[l])
        h1s, xns, route_s, routet_s, cnt_s = _out_proj(
            _to_slabs(attn_s.reshape(n_s, ATTN_WIDTH)), conv_s.reshape(n_s, CONV_DIM), hs,
            w_out_b, g_ffn, w_router_t, tm=n_s, sub=n_s)

        table, n_chunks, eid, n_valid, total = _moe_plan(
            jnp.concatenate([cnt[:, :, 0], cnt_s[:, :, 0]], axis=0), max_tiles)
        xs = _dispatch(table[:nt_p], n_chunks[:nt_p], xn, route, tm_p, max_tiles * MOE_TILE)
        xs = _dispatch(table[nt_p:], n_chunks[nt_p:], xns, route_s, n_s, max_tiles * MOE_TILE,
                       xs_prev=xs)
        eo = _experts(eid, n_valid, total, xs, w_gate, w_up, w_down, l)
        hp = _combine(table[:nt_p], n_chunks[:nt_p], h1, routet, g_final, eo, tm_p, last)
        hs = _combine(table[nt_p:], n_chunks[nt_p:], h1s, routet_s, g_final, eo, n_s, last)
        sk.append(ks)
        sv.append(vs)
        sc.append(state_s)

    return (hp.reshape(batch, seq, D_MODEL), hs.reshape(db, t_new, D_MODEL),
            jnp.stack(pk), jnp.stack(pv), jnp.stack(pc),
            jnp.stack(sk), jnp.stack(sv), jnp.stack(sc))
```

```python
import functools
import math

import jax
import jax.numpy as jnp
from jax import lax
from jax.experimental import pallas as pl
from jax.experimental.pallas import tpu as pltpu

F32 = jnp.float32
BF16 = jnp.bfloat16

D_MODEL = 1024
N_HEADS = 8
HEAD_DIM = 64
ATTN_WIDTH = N_HEADS * HEAD_DIM
CONV_DIM = D_MODEL - ATTN_WIDTH
CONV_K = 3
PROJ_WIDTH = 3 * ATTN_WIDTH + 3 * CONV_DIM
DILATIONS = (1, 4, 16)
SEG_KEYS = 128
Q_BLOCK = 128
MAX_WINDOW = 2048
N_BUCKETS = 32
MAX_DISTANCE = 2048
N_GROUPS = 4
EXPERTS_PER_GROUP = 8
N_EXPERTS = N_GROUPS * EXPERTS_PER_GROUP
D_EXPERT = D_MODEL // 4
EPS = 1e-6
NEG = -1e30
SCALE = 1.0 / math.sqrt(HEAD_DIM)

LANES = 128
SUBLANES = 8
N_SLABS = ATTN_WIDTH // LANES
VMEM_LIMIT = 48 * 1024 * 1024
UNITS_PER_TRIP = 8


def _params(*sem):
    return pltpu.CompilerParams(dimension_semantics=sem, vmem_limit_bytes=VMEM_LIMIT)


def _rms(x, g):
    return x * lax.rsqrt(jnp.mean(x * x, axis=-1, keepdims=True) + EPS) * g


def _rel_bucket(dist):
    max_exact = N_BUCKETS // 2
    d_f = jnp.maximum(dist, 1).astype(F32)
    large = max_exact + (jnp.log(d_f / max_exact) / math.log(MAX_DISTANCE / max_exact)
                         * (N_BUCKETS - max_exact)).astype(jnp.int32)
    large = jnp.minimum(large, N_BUCKETS - 1)
    return jnp.where(dist < max_exact, dist, large)


def _in_proj_kernel(x_ref, g_ref, w_ref, cw_ref, q_ref, k_ref, v_ref, conv_ref, ulast_ref,
                    ext_ref, *, tm, tiles_per_seq):
    i = pl.program_id(0)
    xn = _rms(x_ref[...], g_ref[...])
    proj = jnp.dot(xn.astype(BF16), w_ref[...], preferred_element_type=F32)
    for s in range(N_SLABS):
        q_ref[s] = proj[:, s * LANES:(s + 1) * LANES]
        k_ref[s] = proj[:, ATTN_WIDTH + s * LANES:ATTN_WIDTH + (s + 1) * LANES]
        v_ref[s] = proj[:, 2 * ATTN_WIDTH + s * LANES:2 * ATTN_WIDTH + (s + 1) * LANES]
    c0 = 3 * ATTN_WIDTH
    gb = proj[:, c0:c0 + CONV_DIM]
    u = proj[:, c0 + CONV_DIM:c0 + 2 * CONV_DIM] * proj[:, c0 + 2 * CONV_DIM:c0 + 3 * CONV_DIM]

    @pl.when(i % tiles_per_seq == 0)
    def _():
        ext_ref[0:SUBLANES, :] = jnp.zeros((SUBLANES, CONV_DIM), F32)

    ext_ref[SUBLANES:SUBLANES + tm, :] = u
    u1 = ext_ref[SUBLANES - 1:SUBLANES - 1 + tm, :]
    u2 = ext_ref[SUBLANES - 2:SUBLANES - 2 + tm, :]
    cw = cw_ref[...]
    conv_ref[...] = gb * (cw[0:1] * u2 + cw[1:2] * u1 + cw[2:3] * u)
    ulast_ref[0] = ext_ref[tm + SUBLANES - 2:tm + SUBLANES, :]
    ext_ref[0:SUBLANES, :] = ext_ref[tm:tm + SUBLANES, :]


def _in_proj(x, g, w_bf16, conv_w, seq_len, tm=256):
    n = x.shape[0]
    n_seq = n // seq_len
    slab = jax.ShapeDtypeStruct((N_SLABS, n, LANES), F32)
    slab_spec = pl.BlockSpec((N_SLABS, tm, LANES), lambda i: (0, i, 0))
    tiles_per_seq = seq_len // tm
    return pl.pallas_call(
        functools.partial(_in_proj_kernel, tm=tm, tiles_per_seq=tiles_per_seq),
        grid=(n // tm,),
        in_specs=[pl.BlockSpec((tm, D_MODEL), lambda i: (i, 0)),
                  pl.BlockSpec((1, D_MODEL), lambda i: (0, 0)),
                  pl.BlockSpec((D_MODEL, PROJ_WIDTH), lambda i: (0, 0)),
                  pl.BlockSpec((CONV_K, CONV_DIM), lambda i: (0, 0))],
        out_specs=[slab_spec, slab_spec, slab_spec,
                   pl.BlockSpec((tm, CONV_DIM), lambda i: (i, 0)),
                   pl.BlockSpec((1, CONV_K - 1, CONV_DIM), lambda i: (i // tiles_per_seq, 0, 0))],
        out_shape=[slab, slab, slab,
                   jax.ShapeDtypeStruct((n, CONV_DIM), F32),
                   jax.ShapeDtypeStruct((n_seq, CONV_K - 1, CONV_DIM), F32)],
        scratch_shapes=[pltpu.VMEM((tm + SUBLANES, CONV_DIM), F32)],
        compiler_params=_params("arbitrary"),
        name="in_proj_prompt",
    )(x, g, w_bf16, conv_w)


def _attn_kernel(q_ref, k_ref, v_ref, bias_ref, o_ref, kcat, vcat, oacc, lacc, *, sb):
    j = pl.program_id(2)

    @pl.when(j == 0)
    def _():
        kcat[0:sb, :] = jnp.zeros((sb, LANES), F32)
        vcat[0:sb, :] = jnp.zeros((sb, LANES), F32)

    @pl.when(j > 0)
    def _():
        kcat[0:sb, :] = kcat[sb:2 * sb, :]
        vcat[0:sb, :] = vcat[sb:2 * sb, :]

    kcat[sb:2 * sb, :] = k_ref[...]
    vcat[sb:2 * sb, :] = v_ref[...]

    lane = lax.broadcasted_iota(jnp.int32, (Q_BLOCK, LANES), 1)
    low = lane < HEAD_DIM
    ones = jnp.ones((2 * Q_BLOCK, LANES), BF16)

    for di, d in enumerate(DILATIONS):
        nq = sb // (Q_BLOCK * d)

        def unit(u, di=di, d=d, nq=nq):
            r = u // nq
            n = u % nq
            qstart = r + d * Q_BLOCK * n
            kstart = sb + qstart - d * Q_BLOCK
            q = q_ref[pl.ds(qstart, Q_BLOCK, stride=d), :] * SCALE
            kk = kcat[pl.ds(kstart, 2 * Q_BLOCK, stride=d), :].astype(BF16)
            vv = vcat[pl.ds(kstart, 2 * Q_BLOCK, stride=d), :].astype(BF16)
            qm = jnp.concatenate([jnp.where(low, q, 0.0), jnp.where(low, 0.0, q)],
                                 axis=0).astype(BF16)
            s = lax.dot_general(qm, kk, (((1,), (1,)), ((), ())),
                                preferred_element_type=F32)
            first = jnp.logical_and(j == 0, n == 0).astype(jnp.int32)
            s = s + bias_ref[di, first]
            m = jnp.max(s, axis=-1, keepdims=True)
            p = jnp.exp(s - m).astype(BF16)
            pv = jnp.dot(p, jnp.concatenate([vv, ones], axis=1),
                         preferred_element_type=F32)
            o_sel = jnp.where(low, pv[0:Q_BLOCK, 0:LANES], pv[Q_BLOCK:, 0:LANES])
            l_sel = jnp.where(low, pv[0:Q_BLOCK, LANES:], pv[Q_BLOCK:, LANES:])
            m_sel = jnp.where(low, jnp.broadcast_to(m[0:Q_BLOCK], (Q_BLOCK, LANES)),
                              jnp.broadcast_to(m[Q_BLOCK:], (Q_BLOCK, LANES)))
            oacc[di, pl.ds(qstart, Q_BLOCK, stride=d), :] = o_sel / l_sel
            lacc[di, pl.ds(qstart, Q_BLOCK, stride=d), :] = m_sel + jnp.log(l_sel)

        def group(g, carry, unit=unit):
            for uu in range(UNITS_PER_TRIP):
                unit(g * UNITS_PER_TRIP + uu)
            return carry

        lax.fori_loop(0, sb // Q_BLOCK // UNITS_PER_TRIP, group, 0)

    def merge(c, carry):
        rows = pl.ds(pl.multiple_of(c * 256, 256), 256)
        l0, l1, l2 = lacc[0, rows, :], lacc[1, rows, :], lacc[2, rows, :]
        mm = jnp.maximum(jnp.maximum(l0, l1), l2)
        w0, w1, w2 = jnp.exp(l0 - mm), jnp.exp(l1 - mm), jnp.exp(l2 - mm)
        num = w0 * oacc[0, rows, :] + w1 * oacc[1, rows, :] + w2 * oacc[2, rows, :]
        o_ref[rows, :] = num / (w0 + w1 + w2)
        return carry

    lax.fori_loop(0, sb // 256, merge, 0)


def _step_bias(rel_bias):
    steps = jnp.arange(SEG_KEYS + 1)
    return [rel_bias[_rel_bucket(steps * d)].astype(F32) for d in DILATIONS]


def _prompt_bias_tables(step_bias):
    width = 3 * Q_BLOCK
    k_loc = jnp.arange(2 * Q_BLOCK)
    tables = []
    for b in step_bias:
        g = jnp.concatenate([jnp.full((N_HEADS, Q_BLOCK - 1), NEG, F32), b.T,
                             jnp.full((N_HEADS, width - Q_BLOCK - SEG_KEYS), NEG, F32)], axis=1)
        flat = jnp.broadcast_to(g[:, None, :], (N_HEADS, Q_BLOCK + 1, width)).reshape(N_HEADS, -1)
        x = flat[:, :Q_BLOCK * (width + 1)].reshape(N_HEADS, Q_BLOCK, width + 1)
        t = x[:, :, :2 * Q_BLOCK][:, :, ::-1]
        t_first = jnp.where(k_loc[None, None, :] >= Q_BLOCK, t, NEG)
        tables.append(jnp.stack([t.reshape(N_SLABS, 2 * Q_BLOCK, 2 * Q_BLOCK),
                                 t_first.reshape(N_SLABS, 2 * Q_BLOCK, 2 * Q_BLOCK)]))
    return jnp.stack(tables)


def _attention_prompt(q, k, v, bias_tab, n_seq, seq_len, sb=2048):
    n = q.shape[1]
    nsb = seq_len // sb
    blk = pl.BlockSpec((None, sb, LANES), lambda b, s, j: (s, b * nsb + j, 0))
    return pl.pallas_call(
        functools.partial(_attn_kernel, sb=sb),
        grid=(n_seq, N_SLABS, nsb),
        in_specs=[blk, blk, blk,
                  pl.BlockSpec((len(DILATIONS), 2, None, 2 * Q_BLOCK, 2 * Q_BLOCK),
                               lambda b, s, j: (0, 0, s, 0, 0))],
        out_specs=blk,
        out_shape=jax.ShapeDtypeStruct((N_SLABS, n, LANES), F32),
        scratch_shapes=[pltpu.VMEM((2 * sb, LANES), F32), pltpu.VMEM((2 * sb, LANES), F32),
                        pltpu.VMEM((len(DILATIONS), sb, LANES), F32),
                        pltpu.VMEM((len(DILATIONS), sb, LANES), F32)],
        compiler_params=_params("arbitrary", "arbitrary", "arbitrary"),
        name="attention_prompt",
    )(q, k, v, bias_tab)


ROUTER_ROWS = 48
MOE_TILE = 256
RUN_ALIGN = 2 * SUBLANES
TABLE_LANES = 128


def _local_rows(tm):
    return -(-(2 * tm + N_EXPERTS * (RUN_ALIGN - 1)) // LANES) * LANES


def _route(lg):
    tm = lg.shape[1]
    gl = lg[N_EXPERTS:N_EXPERTS + N_GROUPS]
    grow = lax.broadcasted_iota(jnp.int32, (N_GROUPS, tm), 0)
    gmax = jnp.max(gl, axis=0, keepdims=True)
    gsum = jnp.sum(jnp.exp(gl - gmax), axis=0, keepdims=True)
    gidx = jnp.min(jnp.where(gl == gmax, grow, N_GROUPS), axis=0, keepdims=True)
    p_grp = 1.0 / gsum
    el = lg[0:N_EXPERTS]
    e = lax.broadcasted_iota(jnp.int32, (N_EXPERTS, tm), 0)
    in_grp = (e // EXPERTS_PER_GROUP) == gidx
    elog = jnp.where(in_grp, el, -jnp.inf)
    emax = jnp.max(elog, axis=0, keepdims=True)
    eexp = jnp.exp(elog - emax)
    eprob = jnp.where(in_grp, eexp / jnp.sum(eexp, axis=0, keepdims=True), -1.0)
    p1 = jnp.max(eprob, axis=0, keepdims=True)
    i1 = jnp.min(jnp.where(eprob == p1, e, N_EXPERTS), axis=0, keepdims=True)
    rest = jnp.where(e == i1, -1.0, eprob)
    p2 = jnp.max(rest, axis=0, keepdims=True)
    i2 = jnp.min(jnp.where(rest == p2, e, N_EXPERTS), axis=0, keepdims=True)
    tot = p1 + p2
    return e == i1, e == i2, p_grp * (p1 / tot), p_grp * (p2 / tot)


def _split_bf16(x):
    hi = x.astype(BF16)
    return hi, (x - hi.astype(F32)).astype(BF16)


def _out_proj_kernel(attn_ref, conv_ref, h_ref, wo_ref, g_ref, wrh_ref, wrl_ref, upper_ref, lower_ref,
                     h1_ref, xn_ref, route_ref, routet_ref, cnt_ref, *, tm, sub):
    contract_last = (((1,), (1,)), ((), ()))
    for t in range(tm // sub):
        rs = slice(t * sub, (t + 1) * sub)
        mix = jnp.concatenate([attn_ref[s, rs, :] for s in range(N_SLABS)] + [conv_ref[rs, :]], axis=1)
        h1 = h_ref[rs, :] + jnp.dot(mix.astype(BF16), wo_ref[...], preferred_element_type=F32)
        h1_ref[rs, :] = h1
        xn = _rms(h1, g_ref[...])
        xh, xl = _split_bf16(xn)
        xn_ref[rs, :] = xh
        lg = (lax.dot_general(wrh_ref[...], xh, contract_last, preferred_element_type=F32)
              + lax.dot_general(wrh_ref[...], xl, contract_last, preferred_element_type=F32)
              + lax.dot_general(wrl_ref[...], xh, contract_last, preferred_element_type=F32))
        sel1, sel2, g1, g2 = _route(lg)
        chosen = jnp.where(sel1 | sel2, 1.0, 0.0)
        rank = jnp.dot(chosen.astype(BF16), upper_ref[...], preferred_element_type=F32)
        cnt = jnp.sum(chosen, axis=1, keepdims=True)
        cnt_pad = jnp.floor((cnt + (RUN_ALIGN - 1)) * (1.0 / RUN_ALIGN)) * RUN_ALIGN
        cnt_b = jnp.broadcast_to(cnt_pad, (N_EXPERTS, LANES))
        start = jnp.dot(lower_ref[...], cnt_b.astype(BF16), preferred_element_type=F32)
        pos = start[:, 0:1] + rank
        pos1 = jnp.sum(jnp.where(sel1, pos, 0.0), axis=0, keepdims=True)
        pos2 = jnp.sum(jnp.where(sel2, pos, 0.0), axis=0, keepdims=True)
        rows = jnp.concatenate([pos1, pos2, g1, g2, jnp.zeros((SUBLANES - 4, sub), F32)], axis=0)
        route_ref[t] = rows
        routet_ref[rs, :] = jnp.concatenate([rows, jnp.zeros((LANES - SUBLANES, sub), F32)], axis=0).T
        cnt_ref[t] = cnt_b


def _out_proj(attn, conv, h, wo_bf16, g, w_router_t, tm, sub):
    n = h.shape[0]
    wr_hi, wr_lo = _split_bf16(w_router_t)
    idx = jnp.arange(sub)
    upper = (idx[:, None] < idx[None, :]).astype(BF16)
    eidx = jnp.arange(N_EXPERTS)
    lower = (eidx[None, :] < eidx[:, None]).astype(BF16)
    const = lambda shape: pl.BlockSpec(shape, lambda i: (0,) * len(shape))
    per_step = tm // sub
    return pl.pallas_call(
        functools.partial(_out_proj_kernel, tm=tm, sub=sub),
        grid=(n // tm,),
        in_specs=[pl.BlockSpec((N_SLABS, tm, LANES), lambda i: (0, i, 0)),
                  pl.BlockSpec((tm, CONV_DIM), lambda i: (i, 0)),
                  pl.BlockSpec((tm, D_MODEL), lambda i: (i, 0)),
                  const((D_MODEL, D_MODEL)), const((1, D_MODEL)),
                  const((ROUTER_ROWS, D_MODEL)), const((ROUTER_ROWS, D_MODEL)),
                  const((sub, sub)), const((N_EXPERTS, N_EXPERTS))],
        out_specs=[pl.BlockSpec((tm, D_MODEL), lambda i: (i, 0)),
                   pl.BlockSpec((tm, D_MODEL), lambda i: (i, 0)),
                   pl.BlockSpec((per_step, SUBLANES, sub), lambda i: (i, 0, 0)),
                   pl.BlockSpec((tm, LANES), lambda i: (i, 0)),
                   pl.BlockSpec((per_step, N_EXPERTS, LANES), lambda i: (i, 0, 0))],
        out_shape=[jax.ShapeDtypeStruct((n, D_MODEL), F32),
                   jax.ShapeDtypeStruct((n, D_MODEL), BF16),
                   jax.ShapeDtypeStruct((n // sub, SUBLANES, sub), F32),
                   jax.ShapeDtypeStruct((n, LANES), F32),
                   jax.ShapeDtypeStruct((n // sub, N_EXPERTS, LANES), F32)],
        compiler_params=_params("arbitrary"),
        name="out_proj",
    )(attn, conv, h, wo_bf16, g, wr_hi, wr_lo, upper, lower)


def _moe_plan(cnt_tiles, max_tiles):
    c = cnt_tiles.astype(jnp.int32)
    rows_e = jnp.sum(c, axis=0)
    tiles_e = (rows_e + MOE_TILE - 1) // MOE_TILE
    cum = jnp.cumsum(tiles_e)
    total = cum[-1]
    first_tile = cum - tiles_e

    run_start = jnp.cumsum(c, axis=0) - c
    local_start = jnp.cumsum(c, axis=1) - c
    local_end = local_start + c
    row = jnp.arange(TABLE_LANES, dtype=jnp.int32) * RUN_ALIGN
    inside = (local_start[:, None, :] <= row[None, :, None]) & (row[None, :, None] < local_end[:, None, :])
    base = first_tile[None] * MOE_TILE + run_start - local_start
    n_chunks = jnp.sum(c, axis=1) // RUN_ALIGN
    table = jnp.sum(jnp.where(inside, base[:, None, :] + row[None, :, None], 0), axis=-1)
    live = jnp.arange(TABLE_LANES)[None] < n_chunks[:, None]
    table = jnp.where(live, table, max_tiles * MOE_TILE + row[None])
    i32 = lambda a: a.astype(jnp.int32)
    return (i32(table), i32(n_chunks), i32(first_tile), i32(tiles_e), i32(rows_e),
            i32(total).reshape(1))


def _dispatch_kernel(tbl_ref, xn_ref, route_ref, *rest, tm, rl, chained):
    xs_hbm, lsort, sem = rest[1:] if chained else rest
    i = pl.program_id(0)
    slot = i % 2
    rows = route_ref[0]
    pos1 = rows[0:1].astype(jnp.int32)
    pos2 = rows[1:2].astype(jnp.int32)
    p = lax.broadcasted_iota(jnp.int32, (rl, tm), 0)
    one_hot = jnp.where((p == pos1) | (p == pos2), 1.0, 0.0).astype(BF16)
    lsort[slot] = jnp.dot(one_hot, xn_ref[...], preferred_element_type=F32).astype(BF16)

    def tile_copies_done(s):
        pltpu.make_async_copy(lsort.at[s], xs_hbm.at[pl.ds(0, rl), :], sem.at[s]).wait()

    @pl.when(i > 0)
    def _():
        tile_copies_done(1 - slot)

    for q in range(rl // RUN_ALIGN):
        row = pl.multiple_of(tbl_ref[i * TABLE_LANES + q], RUN_ALIGN)
        pltpu.make_async_copy(lsort.at[slot, pl.ds(q * RUN_ALIGN, RUN_ALIGN), :],
                              xs_hbm.at[pl.ds(row, RUN_ALIGN), :], sem.at[slot]).start()

    @pl.when(i == pl.num_programs(0) - 1)
    def _():
        tile_copies_done(slot)


def _dispatch(table, xn, route, tm, total_rows, xs_prev=None):
    nt = xn.shape[0] // tm
    rl = _local_rows(tm)
    chained = xs_prev is not None
    in_specs = [pl.BlockSpec((tm, D_MODEL), lambda i, t: (i, 0)),
                pl.BlockSpec((1, SUBLANES, tm), lambda i, t: (i, 0, 0))]
    args = [table.reshape(-1), xn, route]
    if chained:
        in_specs.append(pl.BlockSpec(memory_space=pl.ANY))
        args.append(xs_prev)
    return pl.pallas_call(
        functools.partial(_dispatch_kernel, tm=tm, rl=rl, chained=chained),
        grid_spec=pltpu.PrefetchScalarGridSpec(
            num_scalar_prefetch=1, grid=(nt,), in_specs=in_specs,
            out_specs=pl.BlockSpec(memory_space=pl.ANY),
            scratch_shapes=[pltpu.VMEM((2, rl, D_MODEL), BF16), pltpu.SemaphoreType.DMA((2,))]),
        out_shape=jax.ShapeDtypeStruct((total_rows, D_MODEL), BF16),
        input_output_aliases={3: 0} if chained else {},
        compiler_params=_params("arbitrary"),
        name="moe_dispatch",
    )(*args)


def _expert_kernel(first_ref, tiles_ref, rows_ref, tot_ref, xs_hbm, wg_ref, wu_ref, wd_ref, eo_hbm,
                   xbuf, obuf, wg_b, wu_b, wd_b, sem_in, sem_out):
    e = pl.program_id(0)
    first_tile = first_ref[e]
    n_tiles = tiles_ref[e]
    total = tot_ref[0]

    def in_copy(g):
        return pltpu.make_async_copy(
            xs_hbm.at[pl.ds(pl.multiple_of(g * MOE_TILE, MOE_TILE), MOE_TILE), :],
            xbuf.at[g % 2], sem_in.at[g % 2])

    def out_copy(g):
        return pltpu.make_async_copy(
            obuf.at[g % 2],
            eo_hbm.at[pl.ds(pl.multiple_of(g * MOE_TILE, MOE_TILE), MOE_TILE), :], sem_out.at[g % 2])

    @pl.when(e == 0)
    def _():
        in_copy(0).start()

    @pl.when(n_tiles > 0)
    def _():
        wg_b[...] = wg_ref[0, 0].astype(BF16)
        wu_b[...] = wu_ref[0, 0].astype(BF16)
        wd_b[...] = wd_ref[0, 0].astype(BF16)

    def tile(j, carry):
        g = first_tile + j

        @pl.when(g + 1 < total)
        def _():
            in_copy(g + 1).start()

        in_copy(g).wait()

        @pl.when(g >= 2)
        def _():
            out_copy(g - 2).wait()

        row = lax.broadcasted_iota(jnp.int32, (MOE_TILE, 1), 0)
        x = xbuf[g % 2]
        x = jnp.where(row < rows_ref[e] - j * MOE_TILE, x, jnp.zeros_like(x))
        gate = jnp.dot(x, wg_b[...], preferred_element_type=F32)
        up = jnp.dot(x, wu_b[...], preferred_element_type=F32)
        hdn = gate * jax.nn.sigmoid(gate) * up
        obuf[g % 2] = jnp.dot(hdn.astype(BF16), wd_b[...], preferred_element_type=F32).astype(BF16)
        out_copy(g).start()
        return carry

    lax.fori_loop(0, n_tiles, tile, 0)

    @pl.when(e == pl.num_programs(0) - 1)
    def _():
        @pl.when(total >= 2)
        def _():
            out_copy(total - 2).wait()

        out_copy(total - 1).wait()


def _experts(first_tile, tiles_e, rows_e, total, xs, w_gate, w_up, w_down, layer):
    weight = lambda e, f, t, r, n: (layer, e, 0, 0)
    return pl.pallas_call(
        _expert_kernel,
        grid_spec=pltpu.PrefetchScalarGridSpec(
            num_scalar_prefetch=4, grid=(N_EXPERTS,),
            in_specs=[pl.BlockSpec(memory_space=pl.ANY),
                      pl.BlockSpec((1, 1, D_MODEL, D_EXPERT), weight),
                      pl.BlockSpec((1, 1, D_MODEL, D_EXPERT), weight),
                      pl.BlockSpec((1, 1, D_EXPERT, D_MODEL), weight)],
            out_specs=pl.BlockSpec(memory_space=pl.ANY),
            scratch_shapes=[pltpu.VMEM((2, MOE_TILE, D_MODEL), BF16),
                            pltpu.VMEM((2, MOE_TILE, D_MODEL), BF16),
                            pltpu.VMEM((D_MODEL, D_EXPERT), BF16),
                            pltpu.VMEM((D_MODEL, D_EXPERT), BF16),
                            pltpu.VMEM((D_EXPERT, D_MODEL), BF16),
                            pltpu.SemaphoreType.DMA((2,)), pltpu.SemaphoreType.DMA((2,))]),
        out_shape=jax.ShapeDtypeStruct(xs.shape, BF16),
        compiler_params=_params("arbitrary"),
        name="moe_experts",
    )(first_tile, tiles_e, rows_e, total, xs, w_gate, w_up, w_down)


def _combine_kernel(tbl_ref, nch_ref, h1_ref, routet_ref, gf_ref, eo_hbm, o_ref, leo, sem,
                    *, tm, rl, final_norm):
    i = pl.program_id(0)
    n = pl.num_programs(0)
    slot = i % 2

    def gather(tile, s):
        for q in range(rl // RUN_ALIGN):
            row = jnp.where(q < nch_ref[tile], tbl_ref[tile * TABLE_LANES + q], 0)
            pltpu.make_async_copy(eo_hbm.at[pl.ds(pl.multiple_of(row, RUN_ALIGN), RUN_ALIGN), :],
                                  leo.at[s, pl.ds(q * RUN_ALIGN, RUN_ALIGN), :], sem.at[s]).start()

    @pl.when(i == 0)
    def _():
        gather(0, 0)

    @pl.when(i + 1 < n)
    def _():
        gather(i + 1, 1 - slot)

    pltpu.make_async_copy(eo_hbm.at[pl.ds(0, rl), :], leo.at[slot], sem.at[slot]).wait()

    rt = routet_ref[...]
    pos1 = rt[:, 0:1].astype(jnp.int32)
    pos2 = rt[:, 1:2].astype(jnp.int32)
    lane = lax.broadcasted_iota(jnp.int32, (tm, rl), 1)
    weights = jnp.where(lane == pos1, rt[:, 2:3], 0.0) + jnp.where(lane == pos2, rt[:, 3:4], 0.0)
    y = jnp.dot(weights.astype(BF16), leo[slot], preferred_element_type=F32)
    out = h1_ref[...] + y
    if final_norm:
        out = _rms(out, gf_ref[...])
    o_ref[...] = out


def _combine(table, n_chunks, h1, routet, g_final, eo, tm, final_norm):
    n = h1.shape[0]
    rl = _local_rows(tm)
    return pl.pallas_call(
        functools.partial(_combine_kernel, tm=tm, rl=rl, final_norm=final_norm),
        grid_spec=pltpu.PrefetchScalarGridSpec(
            num_scalar_prefetch=2, grid=(n // tm,),
            in_specs=[pl.BlockSpec((tm, D_MODEL), lambda i, t, c: (i, 0)),
                      pl.BlockSpec((tm, LANES), lambda i, t, c: (i, 0)),
                      pl.BlockSpec((1, D_MODEL), lambda i, t, c: (0, 0)),
                      pl.BlockSpec(memory_space=pl.ANY)],
            out_specs=pl.BlockSpec((tm, D_MODEL), lambda i, t, c: (i, 0)),
            scratch_shapes=[pltpu.VMEM((2, rl, D_MODEL), BF16), pltpu.SemaphoreType.DMA((2,))]),
        out_shape=jax.ShapeDtypeStruct((n, D_MODEL), F32),
        compiler_params=_params("arbitrary"),
        name="moe_combine",
    )(table.reshape(-1), n_chunks, h1, routet, g_final, eo)


def _in_proj_sample_kernel(x_ref, g_ref, w_ref, o_ref):
    xn = _rms(x_ref[...], g_ref[...])
    o_ref[...] = jnp.dot(xn.astype(BF16), w_ref[...], preferred_element_type=F32)


def _in_proj_sample(x, g, w_bf16):
    n = x.shape[0]
    return pl.pallas_call(
        _in_proj_sample_kernel,
        out_shape=jax.ShapeDtypeStruct((n, PROJ_WIDTH), F32),
        compiler_params=pltpu.CompilerParams(vmem_limit_bytes=VMEM_LIMIT),
        name="in_proj_sample",
    )(x, g, w_bf16)


def _sample_mix_kernel(q_ref, kn_ref, vn_ref, kt_ref, vt_ref,
                       bc_ref, bn_ref, gb_ref, gc_ref, hc_ref, st_ref, cw_ref,
                       o_ref, conv_ref, st_out_ref, *, t_new):
    contract_last = (((1,), (1,)), ((), ()))
    for h in range(N_HEADS):
        q = (q_ref[0, h] * SCALE).astype(BF16)
        s_c = jnp.dot(q, kt_ref[0, 0, h].astype(BF16), preferred_element_type=F32) + bc_ref[h]
        s_n = lax.dot_general(q, kn_ref[0, h].astype(BF16), contract_last,
                              preferred_element_type=F32) + bn_ref[h]
        m = jnp.maximum(jnp.max(s_c, axis=-1, keepdims=True), jnp.max(s_n, axis=-1, keepdims=True))
        p_c = jnp.exp(s_c - m)
        p_n = jnp.exp(s_n - m)
        den = jnp.sum(p_c, axis=-1, keepdims=True) + jnp.sum(p_n, axis=-1, keepdims=True)
        num = (lax.dot_general(p_c.astype(BF16), vt_ref[0, 0, h].astype(BF16), contract_last,
                               preferred_element_type=F32)
               + jnp.dot(p_n.astype(BF16), vn_ref[0, h].astype(BF16), preferred_element_type=F32))
        out = num / den
        lse = m + jnp.log(den)
        ls = [lse[i * t_new:(i + 1) * t_new] for i in range(len(DILATIONS))]
        os_ = [out[i * t_new:(i + 1) * t_new] for i in range(len(DILATIONS))]
        mm = jnp.maximum(jnp.maximum(ls[0], ls[1]), ls[2])
        w = [jnp.exp(l - mm) for l in ls]
        o_ref[0, h] = (w[0] * os_[0] + w[1] * os_[1] + w[2] * os_[2]) / (w[0] + w[1] + w[2])
    cw = cw_ref[...]
    u = gc_ref[0] * hc_ref[0]
    st = st_ref[0]
    rows = [st[j:j + 1] for j in range(CONV_K - 1)] + [u[t:t + 1] for t in range(t_new)]
    gb = gb_ref[0]
    conv_ref[0] = jnp.concatenate(
        [gb[t:t + 1] * sum(cw[j:j + 1] * rows[t + j] for j in range(CONV_K)) for t in range(t_new)],
        axis=0)
    st_out_ref[0] = jnp.concatenate(rows[t_new:], axis=0)


SAMPLE_ROWS = 16


def _sample_bias_tables(step_bias, t_new, w_buf):
    cache_rows, new_rows = [], []
    for b, d in zip(step_bias, DILATIONS):
        back = b[1:][::-1].T
        for t in range(t_new):
            span = jnp.full((N_HEADS, SEG_KEYS, d), NEG, F32).at[:, :, t % d].set(back)
            span = span.reshape(N_HEADS, SEG_KEYS * d)
            shift = t - t % d
            span = jnp.pad(span[:, :SEG_KEYS * d - shift], ((0, 0), (shift, 0)), constant_values=NEG)
            cache_rows.append(jnp.pad(span, ((0, 0), (w_buf - SEG_KEYS * d, 0)), constant_values=NEG))
            new = jnp.full((N_HEADS, LANES), NEG, F32)
            for t2 in range(t + 1):
                if (t - t2) % d == 0:
                    new = new.at[:, t2].set(b[(t - t2) // d])
            new_rows.append(new)
    pad = SAMPLE_ROWS - len(cache_rows)
    bias_c = jnp.pad(jnp.stack(cache_rows, axis=1), ((0, 0), (0, pad), (0, 0)))
    bias_n = jnp.pad(jnp.stack(new_rows, axis=1), ((0, 0), (0, pad), (0, 0)))
    return bias_c, bias_n


def _sample_mix(q, kn, vn, cache_k, cache_v, layer, bias_c, bias_n, gb, gc, hc, state, conv_w):
    db, t_new = q.shape[0], q.shape[1]
    w_buf = cache_k.shape[2]
    kt = cache_k.transpose(0, 1, 3, 4, 2)
    vt = cache_v.transpose(0, 1, 3, 4, 2)
    n_rows = len(DILATIONS) * t_new
    q_rows = jnp.pad(jnp.tile(q.transpose(0, 2, 1, 3), (1, 1, len(DILATIONS), 1)),
                     ((0, 0), (0, 0), (0, SAMPLE_ROWS - n_rows), (0, 0)))
    kn_pad = jnp.pad(kn.transpose(0, 2, 1, 3), ((0, 0), (0, 0), (0, LANES - t_new), (0, 0)))
    vn_pad = jnp.pad(vn.transpose(0, 2, 1, 3), ((0, 0), (0, 0), (0, LANES - t_new), (0, 0)))
    row = pl.BlockSpec((1, t_new, CONV_DIM), lambda b: (b, 0, 0))
    new_spec = pl.BlockSpec((1, N_HEADS, LANES, HEAD_DIM), lambda b: (b, 0, 0, 0))
    win_spec = pl.BlockSpec((1, 1, N_HEADS, HEAD_DIM, w_buf), lambda b: (layer, b, 0, 0, 0))
    attn, conv, st = pl.pallas_call(
        functools.partial(_sample_mix_kernel, t_new=t_new),
        grid=(db,),
        in_specs=[pl.BlockSpec((1, N_HEADS, SAMPLE_ROWS, HEAD_DIM), lambda b: (b, 0, 0, 0)),
                  new_spec, new_spec, win_spec, win_spec,
                  pl.BlockSpec(bias_c.shape, lambda b: (0, 0, 0)),
                  pl.BlockSpec(bias_n.shape, lambda b: (0, 0, 0)),
                  row, row, row,
                  pl.BlockSpec((1, CONV_K - 1, CONV_DIM), lambda b: (b, 0, 0)),
                  pl.BlockSpec((CONV_K, CONV_DIM), lambda b: (0, 0))],
        out_specs=[pl.BlockSpec((1, N_HEADS, t_new, HEAD_DIM), lambda b: (b, 0, 0, 0)), row,
                   pl.BlockSpec((1, CONV_K - 1, CONV_DIM), lambda b: (b, 0, 0))],
        out_shape=[jax.ShapeDtypeStruct((db, N_HEADS, t_new, HEAD_DIM), F32),
                   jax.ShapeDtypeStruct((db, t_new, CONV_DIM), F32),
                   jax.ShapeDtypeStruct((db, CONV_K - 1, CONV_DIM), F32)],
        compiler_params=_params("arbitrary"),
        name="sample_mix",
    )(q_rows, kn_pad, vn_pad, kt, vt, bias_c, bias_n, gb, gc, hc, state, conv_w)
    return attn.transpose(0, 2, 1, 3), conv, st


def _to_slabs(x):
    n = x.shape[0]
    return x.reshape(n, N_SLABS, LANES).transpose(1, 0, 2)


def _window_from_slabs(x, batch, seq, w_keep):
    win = x.reshape(N_SLABS, batch, seq, LANES)[:, :, seq - w_keep:]
    return win.transpose(1, 2, 0, 3).reshape(batch, w_keep, N_HEADS, HEAD_DIM)


def kernel(x_prompt, x_sample, cache_k, cache_v, state_conv, rel_bias, norm_mix, norm_ffn,
           norm_final, w_in, conv_w, w_out, w_router_group, w_router_expert, w_gate, w_up,
           w_down):
    batch, seq, _ = x_prompt.shape
    db, t_new, _ = x_sample.shape
    depth = w_in.shape[0]
    w_keep = min(MAX_WINDOW, seq)

    hp = x_prompt.reshape(batch * seq, D_MODEL)
    hs = x_sample.reshape(db * t_new, D_MODEL)
    step_bias = _step_bias(rel_bias)
    bias_prompt = _prompt_bias_tables(step_bias)
    bias_c, bias_n = _sample_bias_tables(step_bias, t_new, cache_k.shape[2])
    g_final = norm_final.reshape(1, D_MODEL)

    tm_p = 256
    n_s = db * t_new
    nt_p = batch * seq // tm_p
    pad_per_tile = N_EXPERTS * (RUN_ALIGN - 1)
    max_tiles = -(-(2 * (batch * seq + n_s) + (nt_p + 1) * pad_per_tile) // MOE_TILE) + N_EXPERTS
    buf_rows = max_tiles * MOE_TILE + _local_rows(tm_p)

    pk, pv, pc, sk, sv, sc = [], [], [], [], [], []
    for l in range(depth):
        w_in_b = w_in[l].astype(BF16)
        w_out_b = w_out[l].astype(BF16)
        w_router_t = jnp.pad(jnp.concatenate([w_router_expert[l], w_router_group[l]], axis=1).T,
                             ((0, ROUTER_ROWS - N_EXPERTS - N_GROUPS), (0, 0)))
        g_mix = norm_mix[l].reshape(1, D_MODEL)
        g_ffn = norm_ffn[l].reshape(1, D_MODEL)
        last = l == depth - 1

        q, k, v, conv, u_last = _in_proj(hp, g_mix, w_in_b, conv_w[l], seq)
        attn = _attention_prompt(q, k, v, bias_prompt, batch, seq)
        h1, xn, route, routet, cnt = _out_proj(attn, conv, hp, w_out_b, g_ffn, w_router_t,
                                               tm=2 * tm_p, sub=tm_p)
        pk.append(_window_from_slabs(k, batch, seq, w_keep))
        pv.append(_window_from_slabs(v, batch, seq, w_keep))
        pc.append(u_last)

        proj = _in_proj_sample(hs, g_mix, w_in_b)
        qs = proj[:, :ATTN_WIDTH].reshape(db, t_new, N_HEADS, HEAD_DIM)
        ks = proj[:, ATTN_WIDTH:2 * ATTN_WIDTH].reshape(db, t_new, N_HEADS, HEAD_DIM)
        vs = proj[:, 2 * ATTN_WIDTH:3 * ATTN_WIDTH].reshape(db, t_new, N_HEADS, HEAD_DIM)
        c0 = 3 * ATTN_WIDTH
        gb = proj[:, c0:c0 + CONV_DIM].reshape(db, t_new, CONV_DIM)
        gc = proj[:, c0 + CONV_DIM:c0 + 2 * CONV_DIM].reshape(db, t_new, CONV_DIM)
        hc = proj[:, c0 + 2 * CONV_DIM:].reshape(db, t_new, CONV_DIM)
        attn_s, conv_s, state_s = _sample_mix(qs, ks, vs, cache_k, cache_v, l, bias_c, bias_n,
                                              gb, gc, hc, state_conv[l], conv_w[l])
        h1s, xns, route_s, routet_s, cnt_s = _out_proj(
            _to_slabs(attn_s.reshape(n_s, ATTN_WIDTH)), conv_s.reshape(n_s, CONV_DIM), hs,
            w_out_b, g_ffn, w_router_t, tm=n_s, sub=n_s)

        table, n_chunks, first_tile, tiles_e, rows_e, total = _moe_plan(
            jnp.concatenate([cnt[:, :, 0], cnt_s[:, :, 0]], axis=0), max_tiles)
        xs = _dispatch(table[:nt_p], xn, route, tm_p, buf_rows)
        xs = _dispatch(table[nt_p:], xns, route_s, n_s, buf_rows, xs_prev=xs)
        eo = _experts(first_tile, tiles_e, rows_e, total, xs, w_gate, w_up, w_down, l)
        hp = _combine(table[:nt_p], n_chunks[:nt_p], h1, routet, g_final, eo, tm_p, last)
        hs = _combine(table[nt_p:], n_chunks[nt_p:], h1s, routet_s, g_final, eo, n_s, last)
        sk.append(ks)
        sv.append(vs)
        sc.append(state_s)

    return (hp.reshape(batch, seq, D_MODEL), hs.reshape(db, t_new, D_MODEL),
            jnp.stack(pk), jnp.stack(pv), jnp.stack(pc),
            jnp.stack(sk), jnp.stack(sv), jnp.stack(sc))
```

```python
import functools
import math

import jax
import jax.numpy as jnp
from jax import lax
from jax.experimental import pallas as pl
from jax.experimental.pallas import tpu as pltpu

F32 = jnp.float32
BF16 = jnp.bfloat16

D_MODEL = 1024
N_HEADS = 8
HEAD_DIM = 64
ATTN_WIDTH = N_HEADS * HEAD_DIM
CONV_DIM = D_MODEL - ATTN_WIDTH
CONV_K = 3
PROJ_WIDTH = 3 * ATTN_WIDTH + 3 * CONV_DIM
DILATIONS = (1, 4, 16)
SEG_KEYS = 128
Q_BLOCK = 128
MAX_WINDOW = 2048
N_BUCKETS = 32
MAX_DISTANCE = 2048
N_GROUPS = 4
EXPERTS_PER_GROUP = 8
N_EXPERTS = N_GROUPS * EXPERTS_PER_GROUP
D_EXPERT = D_MODEL // 4
EPS = 1e-6
NEG = -1e30
SCALE = 1.0 / math.sqrt(HEAD_DIM)

LANES = 128
SUBLANES = 8
N_SLABS = ATTN_WIDTH // LANES
VMEM_LIMIT = 48 * 1024 * 1024
UNITS_PER_TRIP = 8


def _params(*sem):
    return pltpu.CompilerParams(dimension_semantics=sem, vmem_limit_bytes=VMEM_LIMIT)


def _rms(x, g):
    return x * lax.rsqrt(jnp.mean(x * x, axis=-1, keepdims=True) + EPS) * g


def _rel_bucket(dist):
    max_exact = N_BUCKETS // 2
    d_f = jnp.maximum(dist, 1).astype(F32)
    large = max_exact + (jnp.log(d_f / max_exact) / math.log(MAX_DISTANCE / max_exact)
                         * (N_BUCKETS - max_exact)).astype(jnp.int32)
    large = jnp.minimum(large, N_BUCKETS - 1)
    return jnp.where(dist < max_exact, dist, large)


def _in_proj_kernel(x_ref, g_ref, w_ref, cw_ref, q_ref, k_ref, v_ref, conv_ref, ulast_ref,
                    ext_ref, *, tm, tiles_per_seq):
    i = pl.program_id(0)
    xn = _rms(x_ref[...], g_ref[...])
    proj = jnp.dot(xn.astype(BF16), w_ref[...], preferred_element_type=F32)
    for s in range(N_SLABS):
        q_ref[s] = proj[:, s * LANES:(s + 1) * LANES]
        k_ref[s] = proj[:, ATTN_WIDTH + s * LANES:ATTN_WIDTH + (s + 1) * LANES]
        v_ref[s] = proj[:, 2 * ATTN_WIDTH + s * LANES:2 * ATTN_WIDTH + (s + 1) * LANES]
    c0 = 3 * ATTN_WIDTH
    gb = proj[:, c0:c0 + CONV_DIM]
    u = proj[:, c0 + CONV_DIM:c0 + 2 * CONV_DIM] * proj[:, c0 + 2 * CONV_DIM:c0 + 3 * CONV_DIM]

    @pl.when(i % tiles_per_seq == 0)
    def _():
        ext_ref[0:SUBLANES, :] = jnp.zeros((SUBLANES, CONV_DIM), F32)

    ext_ref[SUBLANES:SUBLANES + tm, :] = u
    u1 = ext_ref[SUBLANES - 1:SUBLANES - 1 + tm, :]
    u2 = ext_ref[SUBLANES - 2:SUBLANES - 2 + tm, :]
    cw = cw_ref[...]
    conv_ref[...] = gb * (cw[0:1] * u2 + cw[1:2] * u1 + cw[2:3] * u)
    ulast_ref[0] = ext_ref[tm + SUBLANES - 2:tm + SUBLANES, :]
    ext_ref[0:SUBLANES, :] = ext_ref[tm:tm + SUBLANES, :]


def _in_proj(x, g, w_bf16, conv_w, seq_len, tm=256):
    n = x.shape[0]
    n_seq = n // seq_len
    slab = jax.ShapeDtypeStruct((N_SLABS, n, LANES), F32)
    slab_spec = pl.BlockSpec((N_SLABS, tm, LANES), lambda i: (0, i, 0))
    tiles_per_seq = seq_len // tm
    return pl.pallas_call(
        functools.partial(_in_proj_kernel, tm=tm, tiles_per_seq=tiles_per_seq),
        grid=(n // tm,),
        in_specs=[pl.BlockSpec((tm, D_MODEL), lambda i: (i, 0)),
                  pl.BlockSpec((1, D_MODEL), lambda i: (0, 0)),
                  pl.BlockSpec((D_MODEL, PROJ_WIDTH), lambda i: (0, 0)),
                  pl.BlockSpec((CONV_K, CONV_DIM), lambda i: (0, 0))],
        out_specs=[slab_spec, slab_spec, slab_spec,
                   pl.BlockSpec((tm, CONV_DIM), lambda i: (i, 0)),
                   pl.BlockSpec((1, CONV_K - 1, CONV_DIM), lambda i: (i // tiles_per_seq, 0, 0))],
        out_shape=[slab, slab, slab,
                   jax.ShapeDtypeStruct((n, CONV_DIM), F32),
                   jax.ShapeDtypeStruct((n_seq, CONV_K - 1, CONV_DIM), F32)],
        scratch_shapes=[pltpu.VMEM((tm + SUBLANES, CONV_DIM), F32)],
        compiler_params=_params("arbitrary"),
        name="in_proj_prompt",
    )(x, g, w_bf16, conv_w)


def _attn_kernel(q_ref, k_ref, v_ref, bias_ref, o_ref, kcat, vcat, oacc, lacc, *, sb):
    j = pl.program_id(2)

    @pl.when(j == 0)
    def _():
        kcat[0:sb, :] = jnp.zeros((sb, LANES), F32)
        vcat[0:sb, :] = jnp.zeros((sb, LANES), F32)

    @pl.when(j > 0)
    def _():
        kcat[0:sb, :] = kcat[sb:2 * sb, :]
        vcat[0:sb, :] = vcat[sb:2 * sb, :]

    kcat[sb:2 * sb, :] = k_ref[...]
    vcat[sb:2 * sb, :] = v_ref[...]

    lane = lax.broadcasted_iota(jnp.int32, (Q_BLOCK, LANES), 1)
    low = lane < HEAD_DIM
    ones = jnp.ones((2 * Q_BLOCK, LANES), BF16)

    for di, d in enumerate(DILATIONS):
        nq = sb // (Q_BLOCK * d)

        def unit(u, di=di, d=d, nq=nq):
            r = u // nq
            n = u % nq
            qstart = r + d * Q_BLOCK * n
            kstart = sb + qstart - d * Q_BLOCK
            q = q_ref[pl.ds(qstart, Q_BLOCK, stride=d), :] * SCALE
            kk = kcat[pl.ds(kstart, 2 * Q_BLOCK, stride=d), :].astype(BF16)
            vv = vcat[pl.ds(kstart, 2 * Q_BLOCK, stride=d), :].astype(BF16)
            qm = jnp.concatenate([jnp.where(low, q, 0.0), jnp.where(low, 0.0, q)],
                                 axis=0).astype(BF16)
            s = lax.dot_general(qm, kk, (((1,), (1,)), ((), ())),
                                preferred_element_type=F32)
            first = jnp.logical_and(j == 0, n == 0).astype(jnp.int32)
            s = s + bias_ref[di, first]
            m = jnp.max(s, axis=-1, keepdims=True)
            p = jnp.exp(s - m).astype(BF16)
            pv = jnp.dot(p, jnp.concatenate([vv, ones], axis=1),
                         preferred_element_type=F32)
            o_sel = jnp.where(low, pv[0:Q_BLOCK, 0:LANES], pv[Q_BLOCK:, 0:LANES])
            l_sel = jnp.where(low, pv[0:Q_BLOCK, LANES:], pv[Q_BLOCK:, LANES:])
            m_sel = jnp.where(low, jnp.broadcast_to(m[0:Q_BLOCK], (Q_BLOCK, LANES)),
                              jnp.broadcast_to(m[Q_BLOCK:], (Q_BLOCK, LANES)))
            oacc[di, pl.ds(qstart, Q_BLOCK, stride=d), :] = o_sel / l_sel
            lacc[di, pl.ds(qstart, Q_BLOCK, stride=d), :] = m_sel + jnp.log(l_sel)

        def group(g, carry, unit=unit):
            for uu in range(UNITS_PER_TRIP):
                unit(g * UNITS_PER_TRIP + uu)
            return carry

        lax.fori_loop(0, sb // Q_BLOCK // UNITS_PER_TRIP, group, 0)

    def merge(c, carry):
        rows = pl.ds(pl.multiple_of(c * 256, 256), 256)
        l0, l1, l2 = lacc[0, rows, :], lacc[1, rows, :], lacc[2, rows, :]
        mm = jnp.maximum(jnp.maximum(l0, l1), l2)
        w0, w1, w2 = jnp.exp(l0 - mm), jnp.exp(l1 - mm), jnp.exp(l2 - mm)
        num = w0 * oacc[0, rows, :] + w1 * oacc[1, rows, :] + w2 * oacc[2, rows, :]
        o_ref[rows, :] = num / (w0 + w1 + w2)
        return carry

    lax.fori_loop(0, sb // 256, merge, 0)


def _step_bias(rel_bias):
    steps = jnp.arange(SEG_KEYS + 1)
    return [rel_bias[_rel_bucket(steps * d)].astype(F32) for d in DILATIONS]


def _prompt_bias_tables(step_bias):
    width = 3 * Q_BLOCK
    k_loc = jnp.arange(2 * Q_BLOCK)
    tables = []
    for b in step_bias:
        g = jnp.concatenate([jnp.full((N_HEADS, Q_BLOCK - 1), NEG, F32), b.T,
                             jnp.full((N_HEADS, width - Q_BLOCK - SEG_KEYS), NEG, F32)], axis=1)
        flat = jnp.broadcast_to(g[:, None, :], (N_HEADS, Q_BLOCK + 1, width)).reshape(N_HEADS, -1)
        x = flat[:, :Q_BLOCK * (width + 1)].reshape(N_HEADS, Q_BLOCK, width + 1)
        t = x[:, :, :2 * Q_BLOCK][:, :, ::-1]
        t_first = jnp.where(k_loc[None, None, :] >= Q_BLOCK, t, NEG)
        tables.append(jnp.stack([t.reshape(N_SLABS, 2 * Q_BLOCK, 2 * Q_BLOCK),
                                 t_first.reshape(N_SLABS, 2 * Q_BLOCK, 2 * Q_BLOCK)]))
    return jnp.stack(tables)


def _attention_prompt(q, k, v, bias_tab, n_seq, seq_len, sb=2048):
    n = q.shape[1]
    nsb = seq_len // sb
    blk = pl.BlockSpec((None, sb, LANES), lambda b, s, j: (s, b * nsb + j, 0))
    return pl.pallas_call(
        functools.partial(_attn_kernel, sb=sb),
        grid=(n_seq, N_SLABS, nsb),
        in_specs=[blk, blk, blk,
                  pl.BlockSpec((len(DILATIONS), 2, None, 2 * Q_BLOCK, 2 * Q_BLOCK),
                               lambda b, s, j: (0, 0, s, 0, 0))],
        out_specs=blk,
        out_shape=jax.ShapeDtypeStruct((N_SLABS, n, LANES), F32),
        scratch_shapes=[pltpu.VMEM((2 * sb, LANES), F32), pltpu.VMEM((2 * sb, LANES), F32),
                        pltpu.VMEM((len(DILATIONS), sb, LANES), F32),
                        pltpu.VMEM((len(DILATIONS), sb, LANES), F32)],
        compiler_params=_params("arbitrary", "arbitrary", "arbitrary"),
        name="attention_prompt",
    )(q, k, v, bias_tab)


ROUTER_ROWS = 48
MOE_TILE = 256
RUN_ALIGN = 2 * SUBLANES
TABLE_LANES = 128
EXPERT_BUFFERS = 4


def _local_rows(tm):
    return -(-(2 * tm + N_EXPERTS * (RUN_ALIGN - 1)) // LANES) * LANES


def _route(lg):
    tm = lg.shape[1]
    gl = lg[N_EXPERTS:N_EXPERTS + N_GROUPS]
    grow = lax.broadcasted_iota(jnp.int32, (N_GROUPS, tm), 0)
    gmax = jnp.max(gl, axis=0, keepdims=True)
    gsum = jnp.sum(jnp.exp(gl - gmax), axis=0, keepdims=True)
    gidx = jnp.min(jnp.where(gl == gmax, grow, N_GROUPS), axis=0, keepdims=True)
    p_grp = 1.0 / gsum
    el = lg[0:N_EXPERTS]
    e = lax.broadcasted_iota(jnp.int32, (N_EXPERTS, tm), 0)
    in_grp = (e // EXPERTS_PER_GROUP) == gidx
    elog = jnp.where(in_grp, el, -jnp.inf)
    emax = jnp.max(elog, axis=0, keepdims=True)
    eexp = jnp.exp(elog - emax)
    eprob = jnp.where(in_grp, eexp / jnp.sum(eexp, axis=0, keepdims=True), -1.0)
    p1 = jnp.max(eprob, axis=0, keepdims=True)
    i1 = jnp.min(jnp.where(eprob == p1, e, N_EXPERTS), axis=0, keepdims=True)
    rest = jnp.where(e == i1, -1.0, eprob)
    p2 = jnp.max(rest, axis=0, keepdims=True)
    i2 = jnp.min(jnp.where(rest == p2, e, N_EXPERTS), axis=0, keepdims=True)
    tot = p1 + p2
    return e == i1, e == i2, p_grp * (p1 / tot), p_grp * (p2 / tot)


def _split_bf16(x):
    hi = x.astype(BF16)
    return hi, (x - hi.astype(F32)).astype(BF16)


def _out_proj_kernel(attn_ref, conv_ref, h_ref, wo_ref, g_ref, wrh_ref, wrl_ref, upper_ref, lower_ref,
                     h1_ref, xn_ref, route_ref, routet_ref, cnt_ref, *, tm, sub):
    contract_last = (((1,), (1,)), ((), ()))
    for t in range(tm // sub):
        rs = slice(t * sub, (t + 1) * sub)
        mix = jnp.concatenate([attn_ref[s, rs, :] for s in range(N_SLABS)] + [conv_ref[rs, :]], axis=1)
        h1 = h_ref[rs, :] + jnp.dot(mix.astype(BF16), wo_ref[...], preferred_element_type=F32)
        h1_ref[rs, :] = h1
        xn = _rms(h1, g_ref[...])
        xh, xl = _split_bf16(xn)
        xn_ref[rs, :] = xh
        lg = (lax.dot_general(wrh_ref[...], xh, contract_last, preferred_element_type=F32)
              + lax.dot_general(wrh_ref[...], xl, contract_last, preferred_element_type=F32)
              + lax.dot_general(wrl_ref[...], xh, contract_last, preferred_element_type=F32))
        sel1, sel2, g1, g2 = _route(lg)
        chosen = jnp.where(sel1 | sel2, 1.0, 0.0)
        rank = jnp.dot(chosen.astype(BF16), upper_ref[...], preferred_element_type=F32)
        cnt = jnp.sum(chosen, axis=1, keepdims=True)
        cnt_pad = jnp.floor((cnt + (RUN_ALIGN - 1)) * (1.0 / RUN_ALIGN)) * RUN_ALIGN
        cnt_b = jnp.broadcast_to(cnt_pad, (N_EXPERTS, LANES))
        start = jnp.dot(lower_ref[...], cnt_b.astype(BF16), preferred_element_type=F32)
        pos = start[:, 0:1] + rank
        pos1 = jnp.sum(jnp.where(sel1, pos, 0.0), axis=0, keepdims=True)
        pos2 = jnp.sum(jnp.where(sel2, pos, 0.0), axis=0, keepdims=True)
        rows = jnp.concatenate([pos1, pos2, g1, g2, jnp.zeros((SUBLANES - 4, sub), F32)], axis=0)
        route_ref[t] = rows
        routet_ref[rs, :] = jnp.concatenate([rows, jnp.zeros((LANES - SUBLANES, sub), F32)], axis=0).T
        cnt_ref[t] = cnt_b


def _out_proj(attn, conv, h, wo_bf16, g, w_router_t, tm, sub):
    n = h.shape[0]
    wr_hi, wr_lo = _split_bf16(w_router_t)
    idx = jnp.arange(sub)
    upper = (idx[:, None] < idx[None, :]).astype(BF16)
    eidx = jnp.arange(N_EXPERTS)
    lower = (eidx[None, :] < eidx[:, None]).astype(BF16)
    const = lambda shape: pl.BlockSpec(shape, lambda i: (0,) * len(shape))
    per_step = tm // sub
    return pl.pallas_call(
        functools.partial(_out_proj_kernel, tm=tm, sub=sub),
        grid=(n // tm,),
        in_specs=[pl.BlockSpec((N_SLABS, tm, LANES), lambda i: (0, i, 0)),
                  pl.BlockSpec((tm, CONV_DIM), lambda i: (i, 0)),
                  pl.BlockSpec((tm, D_MODEL), lambda i: (i, 0)),
                  const((D_MODEL, D_MODEL)), const((1, D_MODEL)),
                  const((ROUTER_ROWS, D_MODEL)), const((ROUTER_ROWS, D_MODEL)),
                  const((sub, sub)), const((N_EXPERTS, N_EXPERTS))],
        out_specs=[pl.BlockSpec((tm, D_MODEL), lambda i: (i, 0)),
                   pl.BlockSpec((tm, D_MODEL), lambda i: (i, 0)),
                   pl.BlockSpec((per_step, SUBLANES, sub), lambda i: (i, 0, 0)),
                   pl.BlockSpec((tm, LANES), lambda i: (i, 0)),
                   pl.BlockSpec((per_step, N_EXPERTS, LANES), lambda i: (i, 0, 0))],
        out_shape=[jax.ShapeDtypeStruct((n, D_MODEL), F32),
                   jax.ShapeDtypeStruct((n, D_MODEL), BF16),
                   jax.ShapeDtypeStruct((n // sub, SUBLANES, sub), F32),
                   jax.ShapeDtypeStruct((n, LANES), F32),
                   jax.ShapeDtypeStruct((n // sub, N_EXPERTS, LANES), F32)],
        compiler_params=_params("arbitrary"),
        name="out_proj",
    )(attn, conv, h, wo_bf16, g, wr_hi, wr_lo, upper, lower)


def _moe_plan(cnt_tiles, max_tiles):
    c = cnt_tiles.astype(jnp.int32)
    rows_e = jnp.sum(c, axis=0)
    tiles_e = (rows_e + MOE_TILE - 1) // MOE_TILE
    cum = jnp.cumsum(tiles_e)
    total = cum[-1]
    first_tile = cum - tiles_e

    run_start = jnp.cumsum(c, axis=0) - c
    local_start = jnp.cumsum(c, axis=1) - c
    local_end = local_start + c
    row = jnp.arange(TABLE_LANES, dtype=jnp.int32) * RUN_ALIGN
    inside = (local_start[:, None, :] <= row[None, :, None]) & (row[None, :, None] < local_end[:, None, :])
    base = first_tile[None] * MOE_TILE + run_start - local_start
    n_chunks = jnp.sum(c, axis=1) // RUN_ALIGN
    table = jnp.sum(jnp.where(inside, base[:, None, :] + row[None, :, None], 0), axis=-1)
    live = jnp.arange(TABLE_LANES)[None] < n_chunks[:, None]
    table = jnp.where(live, table, max_tiles * MOE_TILE + row[None])
    i32 = lambda a: a.astype(jnp.int32)
    return (i32(table), i32(n_chunks), i32(first_tile), i32(tiles_e), i32(rows_e),
            i32(total).reshape(1))


def _dispatch_kernel(tbl_ref, xn_ref, route_ref, *rest, tm, rl, chained):
    xs_hbm, lsort, sem = rest[1:] if chained else rest
    i = pl.program_id(0)
    slot = i % 2
    rows = route_ref[0]
    pos1 = rows[0:1].astype(jnp.int32)
    pos2 = rows[1:2].astype(jnp.int32)
    p = lax.broadcasted_iota(jnp.int32, (rl, tm), 0)
    one_hot = jnp.where((p == pos1) | (p == pos2), 1.0, 0.0).astype(BF16)
    lsort[slot] = jnp.dot(one_hot, xn_ref[...], preferred_element_type=F32).astype(BF16)

    def tile_copies_done(s):
        pltpu.make_async_copy(lsort.at[s], xs_hbm.at[pl.ds(0, rl), :], sem.at[s]).wait()

    @pl.when(i > 0)
    def _():
        tile_copies_done(1 - slot)

    for q in range(rl // RUN_ALIGN):
        row = pl.multiple_of(tbl_ref[i * TABLE_LANES + q], RUN_ALIGN)
        pltpu.make_async_copy(lsort.at[slot, pl.ds(q * RUN_ALIGN, RUN_ALIGN), :],
                              xs_hbm.at[pl.ds(row, RUN_ALIGN), :], sem.at[slot]).start()

    @pl.when(i == pl.num_programs(0) - 1)
    def _():
        tile_copies_done(slot)


def _dispatch(table, xn, route, tm, total_rows, xs_prev=None):
    nt = xn.shape[0] // tm
    rl = _local_rows(tm)
    chained = xs_prev is not None
    in_specs = [pl.BlockSpec((tm, D_MODEL), lambda i, t: (i, 0)),
                pl.BlockSpec((1, SUBLANES, tm), lambda i, t: (i, 0, 0))]
    args = [table.reshape(-1), xn, route]
    if chained:
        in_specs.append(pl.BlockSpec(memory_space=pl.ANY))
        args.append(xs_prev)
    return pl.pallas_call(
        functools.partial(_dispatch_kernel, tm=tm, rl=rl, chained=chained),
        grid_spec=pltpu.PrefetchScalarGridSpec(
            num_scalar_prefetch=1, grid=(nt,), in_specs=in_specs,
            out_specs=pl.BlockSpec(memory_space=pl.ANY),
            scratch_shapes=[pltpu.VMEM((2, rl, D_MODEL), BF16), pltpu.SemaphoreType.DMA((2,))]),
        out_shape=jax.ShapeDtypeStruct((total_rows, D_MODEL), BF16),
        input_output_aliases={3: 0} if chained else {},
        compiler_params=_params("arbitrary"),
        name="moe_dispatch",
    )(*args)


def _expert_kernel(first_ref, tiles_ref, rows_ref, tot_ref, xs_hbm, wg_ref, wu_ref, wd_ref, eo_hbm,
                   xbuf, obuf, wg_b, wu_b, wd_b, sem_in, sem_out):
    e = pl.program_id(0)
    first_tile = first_ref[e]
    n_tiles = tiles_ref[e]
    total = tot_ref[0]

    def in_copy(g):
        return pltpu.make_async_copy(
            xs_hbm.at[pl.ds(pl.multiple_of(g * MOE_TILE, MOE_TILE), MOE_TILE), :],
            xbuf.at[g % EXPERT_BUFFERS], sem_in.at[g % EXPERT_BUFFERS])

    def out_copy(g):
        return pltpu.make_async_copy(
            obuf.at[g % EXPERT_BUFFERS],
            eo_hbm.at[pl.ds(pl.multiple_of(g * MOE_TILE, MOE_TILE), MOE_TILE), :],
            sem_out.at[g % EXPERT_BUFFERS])

    @pl.when(e == 0)
    def _():
        for g0 in range(EXPERT_BUFFERS - 1):
            @pl.when(g0 < total)
            def _():
                in_copy(g0).start()

    @pl.when(n_tiles > 0)
    def _():
        wg_b[...] = wg_ref[0, 0].astype(BF16)
        wu_b[...] = wu_ref[0, 0].astype(BF16)
        wd_b[...] = wd_ref[0, 0].astype(BF16)

    def tile(j, carry):
        g = first_tile + j

        @pl.when(g + EXPERT_BUFFERS - 1 < total)
        def _():
            in_copy(g + EXPERT_BUFFERS - 1).start()

        in_copy(g).wait()

        @pl.when(g >= EXPERT_BUFFERS)
        def _():
            out_copy(g - EXPERT_BUFFERS).wait()

        row = lax.broadcasted_iota(jnp.int32, (MOE_TILE, 1), 0)
        x = xbuf[g % EXPERT_BUFFERS]
        x = jnp.where(row < rows_ref[e] - j * MOE_TILE, x, jnp.zeros_like(x))
        gate = jnp.dot(x, wg_b[...], preferred_element_type=F32)
        up = jnp.dot(x, wu_b[...], preferred_element_type=F32)
        hdn = gate * jax.nn.sigmoid(gate) * up
        obuf[g % EXPERT_BUFFERS] = jnp.dot(hdn.astype(BF16), wd_b[...],
                                           preferred_element_type=F32).astype(BF16)
        out_copy(g).start()
        return carry

    lax.fori_loop(0, n_tiles, tile, 0)

    @pl.when(e == pl.num_programs(0) - 1)
    def _():
        for back in range(EXPERT_BUFFERS, 0, -1):
            @pl.when(total >= back)
            def _():
                out_copy(total - back).wait()


def _experts(first_tile, tiles_e, rows_e, total, xs, w_gate, w_up, w_down, layer):
    weight = lambda e, f, t, r, n: (layer, e, 0, 0)
    return pl.pallas_call(
        _expert_kernel,
        grid_spec=pltpu.PrefetchScalarGridSpec(
            num_scalar_prefetch=4, grid=(N_EXPERTS,),
            in_specs=[pl.BlockSpec(memory_space=pl.ANY),
                      pl.BlockSpec((1, 1, D_MODEL, D_EXPERT), weight),
                      pl.BlockSpec((1, 1, D_MODEL, D_EXPERT), weight),
                      pl.BlockSpec((1, 1, D_EXPERT, D_MODEL), weight)],
            out_specs=pl.BlockSpec(memory_space=pl.ANY),
            scratch_shapes=[pltpu.VMEM((EXPERT_BUFFERS, MOE_TILE, D_MODEL), BF16),
                            pltpu.VMEM((EXPERT_BUFFERS, MOE_TILE, D_MODEL), BF16),
                            pltpu.VMEM((D_MODEL, D_EXPERT), BF16),
                            pltpu.VMEM((D_MODEL, D_EXPERT), BF16),
                            pltpu.VMEM((D_EXPERT, D_MODEL), BF16),
                            pltpu.SemaphoreType.DMA((EXPERT_BUFFERS,)),
                            pltpu.SemaphoreType.DMA((EXPERT_BUFFERS,))]),
        out_shape=jax.ShapeDtypeStruct(xs.shape, BF16),
        compiler_params=_params("arbitrary"),
        name="moe_experts",
    )(first_tile, tiles_e, rows_e, total, xs, w_gate, w_up, w_down)


def _combine_kernel(tbl_ref, nch_ref, h1_ref, routet_ref, gf_ref, eo_hbm, o_ref, leo, sem,
                    *, tm, rl, final_norm):
    i = pl.program_id(0)
    n = pl.num_programs(0)
    slot = i % 2

    def gather(tile, s):
        for q in range(rl // RUN_ALIGN):
            row = jnp.where(q < nch_ref[tile], tbl_ref[tile * TABLE_LANES + q], q * RUN_ALIGN)
            pltpu.make_async_copy(eo_hbm.at[pl.ds(pl.multiple_of(row, RUN_ALIGN), RUN_ALIGN), :],
                                  leo.at[s, pl.ds(q * RUN_ALIGN, RUN_ALIGN), :], sem.at[s]).start()

    @pl.when(i == 0)
    def _():
        gather(0, 0)

    @pl.when(i + 1 < n)
    def _():
        gather(i + 1, 1 - slot)

    pltpu.make_async_copy(eo_hbm.at[pl.ds(0, rl), :], leo.at[slot], sem.at[slot]).wait()

    rt = routet_ref[...]
    pos1 = rt[:, 0:1].astype(jnp.int32)
    pos2 = rt[:, 1:2].astype(jnp.int32)
    lane = lax.broadcasted_iota(jnp.int32, (tm, rl), 1)
    weights = jnp.where(lane == pos1, rt[:, 2:3], 0.0) + jnp.where(lane == pos2, rt[:, 3:4], 0.0)
    y = jnp.dot(weights.astype(BF16), leo[slot], preferred_element_type=F32)
    out = h1_ref[...] + y
    if final_norm:
        out = _rms(out, gf_ref[...])
    o_ref[...] = out


def _combine(table, n_chunks, h1, routet, g_final, eo, tm, final_norm):
    n = h1.shape[0]
    rl = _local_rows(tm)
    return pl.pallas_call(
        functools.partial(_combine_kernel, tm=tm, rl=rl, final_norm=final_norm),
        grid_spec=pltpu.PrefetchScalarGridSpec(
            num_scalar_prefetch=2, grid=(n // tm,),
            in_specs=[pl.BlockSpec((tm, D_MODEL), lambda i, t, c: (i, 0)),
                      pl.BlockSpec((tm, LANES), lambda i, t, c: (i, 0)),
                      pl.BlockSpec((1, D_MODEL), lambda i, t, c: (0, 0)),
                      pl.BlockSpec(memory_space=pl.ANY)],
            out_specs=pl.BlockSpec((tm, D_MODEL), lambda i, t, c: (i, 0)),
            scratch_shapes=[pltpu.VMEM((2, rl, D_MODEL), BF16), pltpu.SemaphoreType.DMA((2,))]),
        out_shape=jax.ShapeDtypeStruct((n, D_MODEL), F32),
        compiler_params=_params("arbitrary"),
        name="moe_combine",
    )(table.reshape(-1), n_chunks, h1, routet, g_final, eo)


def _in_proj_sample_kernel(x_ref, g_ref, w_ref, o_ref):
    xn = _rms(x_ref[...], g_ref[...])
    o_ref[...] = jnp.dot(xn.astype(BF16), w_ref[...], preferred_element_type=F32)


def _in_proj_sample(x, g, w_bf16):
    n = x.shape[0]
    return pl.pallas_call(
        _in_proj_sample_kernel,
        out_shape=jax.ShapeDtypeStruct((n, PROJ_WIDTH), F32),
        compiler_params=pltpu.CompilerParams(vmem_limit_bytes=VMEM_LIMIT),
        name="in_proj_sample",
    )(x, g, w_bf16)


def _sample_mix_kernel(q_ref, kn_ref, vn_ref, kt_ref, vt_ref,
                       bc_ref, bn_ref, gb_ref, gc_ref, hc_ref, st_ref, cw_ref,
                       o_ref, conv_ref, st_out_ref, *, t_new):
    contract_last = (((1,), (1,)), ((), ()))
    for h in range(N_HEADS):
        q = (q_ref[0, h] * SCALE).astype(BF16)
        s_c = jnp.dot(q, kt_ref[0, 0, h].astype(BF16), preferred_element_type=F32) + bc_ref[h]
        s_n = lax.dot_general(q, kn_ref[0, h].astype(BF16), contract_last,
                              preferred_element_type=F32) + bn_ref[h]
        m = jnp.maximum(jnp.max(s_c, axis=-1, keepdims=True), jnp.max(s_n, axis=-1, keepdims=True))
        p_c = jnp.exp(s_c - m)
        p_n = jnp.exp(s_n - m)
        den = jnp.sum(p_c, axis=-1, keepdims=True) + jnp.sum(p_n, axis=-1, keepdims=True)
        num = (lax.dot_general(p_c.astype(BF16), vt_ref[0, 0, h].astype(BF16), contract_last,
                               preferred_element_type=F32)
               + jnp.dot(p_n.astype(BF16), vn_ref[0, h].astype(BF16), preferred_element_type=F32))
        out = num / den
        lse = m + jnp.log(den)
        ls = [lse[i * t_new:(i + 1) * t_new] for i in range(len(DILATIONS))]
        os_ = [out[i * t_new:(i + 1) * t_new] for i in range(len(DILATIONS))]
        mm = jnp.maximum(jnp.maximum(ls[0], ls[1]), ls[2])
        w = [jnp.exp(l - mm) for l in ls]
        o_ref[0, h] = (w[0] * os_[0] + w[1] * os_[1] + w[2] * os_[2]) / (w[0] + w[1] + w[2])
    cw = cw_ref[...]
    u = gc_ref[0] * hc_ref[0]
    st = st_ref[0]
    rows = [st[j:j + 1] for j in range(CONV_K - 1)] + [u[t:t + 1] for t in range(t_new)]
    gb = gb_ref[0]
    conv_ref[0] = jnp.concatenate(
        [gb[t:t + 1] * sum(cw[j:j + 1] * rows[t + j] for j in range(CONV_K)) for t in range(t_new)],
        axis=0)
    st_out_ref[0] = jnp.concatenate(rows[t_new:], axis=0)


SAMPLE_ROWS = 16


def _sample_bias_tables(step_bias, t_new, w_buf):
    cache_rows, new_rows = [], []
    for b, d in zip(step_bias, DILATIONS):
        back = b[1:][::-1].T
        for t in range(t_new):
            span = jnp.full((N_HEADS, SEG_KEYS, d), NEG, F32).at[:, :, t % d].set(back)
            span = span.reshape(N_HEADS, SEG_KEYS * d)
            shift = t - t % d
            span = jnp.pad(span[:, :SEG_KEYS * d - shift], ((0, 0), (shift, 0)), constant_values=NEG)
            cache_rows.append(jnp.pad(span, ((0, 0), (w_buf - SEG_KEYS * d, 0)), constant_values=NEG))
            new = jnp.full((N_HEADS, LANES), NEG, F32)
            for t2 in range(t + 1):
                if (t - t2) % d == 0:
                    new = new.at[:, t2].set(b[(t - t2) // d])
            new_rows.append(new)
    pad = SAMPLE_ROWS - len(cache_rows)
    bias_c = jnp.pad(jnp.stack(cache_rows, axis=1), ((0, 0), (0, pad), (0, 0)))
    bias_n = jnp.pad(jnp.stack(new_rows, axis=1), ((0, 0), (0, pad), (0, 0)))
    return bias_c, bias_n


def _sample_mix(q, kn, vn, cache_k, cache_v, layer, bias_c, bias_n, gb, gc, hc, state, conv_w):
    db, t_new = q.shape[0], q.shape[1]
    w_buf = cache_k.shape[2]
    kt = cache_k.transpose(0, 1, 3, 4, 2)
    vt = cache_v.transpose(0, 1, 3, 4, 2)
    n_rows = len(DILATIONS) * t_new
    q_rows = jnp.pad(jnp.tile(q.transpose(0, 2, 1, 3), (1, 1, len(DILATIONS), 1)),
                     ((0, 0), (0, 0), (0, SAMPLE_ROWS - n_rows), (0, 0)))
    kn_pad = jnp.pad(kn.transpose(0, 2, 1, 3), ((0, 0), (0, 0), (0, LANES - t_new), (0, 0)))
    vn_pad = jnp.pad(vn.transpose(0, 2, 1, 3), ((0, 0), (0, 0), (0, LANES - t_new), (0, 0)))
    row = pl.BlockSpec((1, t_new, CONV_DIM), lambda b: (b, 0, 0))
    new_spec = pl.BlockSpec((1, N_HEADS, LANES, HEAD_DIM), lambda b: (b, 0, 0, 0))
    win_spec = pl.BlockSpec((1, 1, N_HEADS, HEAD_DIM, w_buf), lambda b: (layer, b, 0, 0, 0))
    attn, conv, st = pl.pallas_call(
        functools.partial(_sample_mix_kernel, t_new=t_new),
        grid=(db,),
        in_specs=[pl.BlockSpec((1, N_HEADS, SAMPLE_ROWS, HEAD_DIM), lambda b: (b, 0, 0, 0)),
                  new_spec, new_spec, win_spec, win_spec,
                  pl.BlockSpec(bias_c.shape, lambda b: (0, 0, 0)),
                  pl.BlockSpec(bias_n.shape, lambda b: (0, 0, 0)),
                  row, row, row,
                  pl.BlockSpec((1, CONV_K - 1, CONV_DIM), lambda b: (b, 0, 0)),
                  pl.BlockSpec((CONV_K, CONV_DIM), lambda b: (0, 0))],
        out_specs=[pl.BlockSpec((1, N_HEADS, t_new, HEAD_DIM), lambda b: (b, 0, 0, 0)), row,
                   pl.BlockSpec((1, CONV_K - 1, CONV_DIM), lambda b: (b, 0, 0))],
        out_shape=[jax.ShapeDtypeStruct((db, N_HEADS, t_new, HEAD_DIM), F32),
                   jax.ShapeDtypeStruct((db, t_new, CONV_DIM), F32),
                   jax.ShapeDtypeStruct((db, CONV_K - 1, CONV_DIM), F32)],
        compiler_params=_params("arbitrary"),
        name="sample_mix",
    )(q_rows, kn_pad, vn_pad, kt, vt, bias_c, bias_n, gb, gc, hc, state, conv_w)
    return attn.transpose(0, 2, 1, 3), conv, st


def _to_slabs(x):
    n = x.shape[0]
    return x.reshape(n, N_SLABS, LANES).transpose(1, 0, 2)


def _window_from_slabs(x, batch, seq, w_keep):
    win = x.reshape(N_SLABS, batch, seq, LANES)[:, :, seq - w_keep:]
    return win.transpose(1, 2, 0, 3).reshape(batch, w_keep, N_HEADS, HEAD_DIM)


def kernel(x_prompt, x_sample, cache_k, cache_v, state_conv, rel_bias, norm_mix, norm_ffn,
           norm_final, w_in, conv_w, w_out, w_router_group, w_router_expert, w_gate, w_up,
           w_down):
    batch, seq, _ = x_prompt.shape
    db, t_new, _ = x_sample.shape
    depth = w_in.shape[0]
    w_keep = min(MAX_WINDOW, seq)

    hp = x_prompt.reshape(batch * seq, D_MODEL)
    hs = x_sample.reshape(db * t_new, D_MODEL)
    step_bias = _step_bias(rel_bias)
    bias_prompt = _prompt_bias_tables(step_bias)
    bias_c, bias_n = _sample_bias_tables(step_bias, t_new, cache_k.shape[2])
    g_final = norm_final.reshape(1, D_MODEL)

    tm_p = 256
    n_s = db * t_new
    nt_p = batch * seq // tm_p
    pad_per_tile = N_EXPERTS * (RUN_ALIGN - 1)
    max_tiles = -(-(2 * (batch * seq + n_s) + (nt_p + 1) * pad_per_tile) // MOE_TILE) + N_EXPERTS
    buf_rows = max_tiles * MOE_TILE + _local_rows(tm_p)

    pk, pv, pc, sk, sv, sc = [], [], [], [], [], []
    for l in range(depth):
        w_in_b = w_in[l].astype(BF16)
        w_out_b = w_out[l].astype(BF16)
        w_router_t = jnp.pad(jnp.concatenate([w_router_expert[l], w_router_group[l]], axis=1).T,
                             ((0, ROUTER_ROWS - N_EXPERTS - N_GROUPS), (0, 0)))
        g_mix = norm_mix[l].reshape(1, D_MODEL)
        g_ffn = norm_ffn[l].reshape(1, D_MODEL)
        last = l == depth - 1

        q, k, v, conv, u_last = _in_proj(hp, g_mix, w_in_b, conv_w[l], seq)
        attn = _attention_prompt(q, k, v, bias_prompt, batch, seq)
        h1, xn, route, routet, cnt = _out_proj(attn, conv, hp, w_out_b, g_ffn, w_router_t,
                                               tm=2 * tm_p, sub=tm_p)
        pk.append(_window_from_slabs(k, batch, seq, w_keep))
        pv.append(_window_from_slabs(v, batch, seq, w_keep))
        pc.append(u_last)

        proj = _in_proj_sample(hs, g_mix, w_in_b)
        qs = proj[:, :ATTN_WIDTH].reshape(db, t_new, N_HEADS, HEAD_DIM)
        ks = proj[:, ATTN_WIDTH:2 * ATTN_WIDTH].reshape(db, t_new, N_HEADS, HEAD_DIM)
        vs = proj[:, 2 * ATTN_WIDTH:3 * ATTN_WIDTH].reshape(db, t_new, N_HEADS, HEAD_DIM)
        c0 = 3 * ATTN_WIDTH
        gb = proj[:, c0:c0 + CONV_DIM].reshape(db, t_new, CONV_DIM)
        gc = proj[:, c0 + CONV_DIM:c0 + 2 * CONV_DIM].reshape(db, t_new, CONV_DIM)
        hc = proj[:, c0 + 2 * CONV_DIM:].reshape(db, t_new, CONV_DIM)
        attn_s, conv_s, state_s = _sample_mix(qs, ks, vs, cache_k, cache_v, l, bias_c, bias_n,
                                              gb, gc, hc, state_conv[l], conv_w[l])
        h1s, xns, route_s, routet_s, cnt_s = _out_proj(
            _to_slabs(attn_s.reshape(n_s, ATTN_WIDTH)), conv_s.reshape(n_s, CONV_DIM), hs,
            w_out_b, g_ffn, w_router_t, tm=n_s, sub=n_s)

        table, n_chunks, first_tile, tiles_e, rows_e, total = _moe_plan(
            jnp.concatenate([cnt[:, :, 0], cnt_s[:, :, 0]], axis=0), max_tiles)
        xs = _dispatch(table[:nt_p], xn, route, tm_p, buf_rows)
        xs = _dispatch(table[nt_p:], xns, route_s, n_s, buf_rows, xs_prev=xs)
        eo = _experts(first_tile, tiles_e, rows_e, total, xs, w_gate, w_up, w_down, l)
        hp = _combine(table[:nt_p], n_chunks[:nt_p], h1, routet, g_final, eo, tm_p, last)
        hs = _combine(table[nt_p:], n_chunks[nt_p:], h1s, routet_s, g_final, eo, n_s, last)
        sk.append(ks)
        sv.append(vs)
        sc.append(state_s)

    return (hp.reshape(batch, seq, D_MODEL), hs.reshape(db, t_new, D_MODEL),
            jnp.stack(pk), jnp.stack(pv), jnp.stack(pc),
            jnp.stack(sk), jnp.stack(sv), jnp.stack(sc))
```

```python
import functools
import math

import jax
import jax.numpy as jnp
import numpy as np
from jax import lax
from jax.experimental import pallas as pl
from jax.experimental.pallas import tpu as pltpu

F32 = jnp.float32
BF16 = jnp.bfloat16

D_MODEL = 1024
N_HEADS = 8
HEAD_DIM = 64
ATTN_WIDTH = N_HEADS * HEAD_DIM
CONV_DIM = D_MODEL - ATTN_WIDTH
CONV_K = 3
PROJ_WIDTH = 3 * ATTN_WIDTH + 3 * CONV_DIM
DILATIONS = (1, 4, 16)
SEG_KEYS = 128
Q_BLOCK = 128
MAX_WINDOW = 2048
N_BUCKETS = 32
MAX_DISTANCE = 2048
N_GROUPS = 4
EXPERTS_PER_GROUP = 8
N_EXPERTS = N_GROUPS * EXPERTS_PER_GROUP
D_EXPERT = D_MODEL // 4
EPS = 1e-6
NEG = -1e30
SCALE = 1.0 / math.sqrt(HEAD_DIM)

LANES = 128
SUBLANES = 8
N_SLABS = ATTN_WIDTH // LANES
VMEM_LIMIT = 48 * 1024 * 1024
UNITS_PER_TRIP = 8


def _params(*sem):
    return pltpu.CompilerParams(dimension_semantics=sem, vmem_limit_bytes=VMEM_LIMIT)


def _rms(x, g):
    return x * lax.rsqrt(jnp.mean(x * x, axis=-1, keepdims=True) + EPS) * g


def _rel_bucket(dist):
    max_exact = N_BUCKETS // 2
    d_f = jnp.maximum(dist, 1).astype(F32)
    large = max_exact + (jnp.log(d_f / max_exact) / math.log(MAX_DISTANCE / max_exact)
                         * (N_BUCKETS - max_exact)).astype(jnp.int32)
    large = jnp.minimum(large, N_BUCKETS - 1)
    return jnp.where(dist < max_exact, dist, large)


def _in_proj_kernel(x_ref, g_ref, w_ref, cw_ref, q_ref, k_ref, v_ref, conv_ref, ulast_ref,
                    ext_ref, *, tm, tiles_per_seq):
    i = pl.program_id(0)
    xn = _rms(x_ref[...], g_ref[...])
    proj = jnp.dot(xn.astype(BF16), w_ref[...], preferred_element_type=F32)
    for s in range(N_SLABS):
        q_ref[s] = proj[:, s * LANES:(s + 1) * LANES]
        k_ref[s] = proj[:, ATTN_WIDTH + s * LANES:ATTN_WIDTH + (s + 1) * LANES]
        v_ref[s] = proj[:, 2 * ATTN_WIDTH + s * LANES:2 * ATTN_WIDTH + (s + 1) * LANES]
    c0 = 3 * ATTN_WIDTH
    gb = proj[:, c0:c0 + CONV_DIM]
    u = proj[:, c0 + CONV_DIM:c0 + 2 * CONV_DIM] * proj[:, c0 + 2 * CONV_DIM:c0 + 3 * CONV_DIM]

    @pl.when(i % tiles_per_seq == 0)
    def _():
        ext_ref[0:SUBLANES, :] = jnp.zeros((SUBLANES, CONV_DIM), F32)

    ext_ref[SUBLANES:SUBLANES + tm, :] = u
    u1 = ext_ref[SUBLANES - 1:SUBLANES - 1 + tm, :]
    u2 = ext_ref[SUBLANES - 2:SUBLANES - 2 + tm, :]
    cw = cw_ref[...]
    conv_ref[...] = gb * (cw[0:1] * u2 + cw[1:2] * u1 + cw[2:3] * u)
    ulast_ref[0] = ext_ref[tm + SUBLANES - 2:tm + SUBLANES, :]
    ext_ref[0:SUBLANES, :] = ext_ref[tm:tm + SUBLANES, :]


def _in_proj(x, g, w_bf16, conv_w, seq_len, tm=1024):
    n = x.shape[0]
    n_seq = n // seq_len
    slab = jax.ShapeDtypeStruct((N_SLABS, n, LANES), F32)
    slab_spec = pl.BlockSpec((N_SLABS, tm, LANES), lambda i: (0, i, 0))
    tiles_per_seq = seq_len // tm
    return pl.pallas_call(
        functools.partial(_in_proj_kernel, tm=tm, tiles_per_seq=tiles_per_seq),
        grid=(n // tm,),
        in_specs=[pl.BlockSpec((tm, D_MODEL), lambda i: (i, 0)),
                  pl.BlockSpec((1, D_MODEL), lambda i: (0, 0)),
                  pl.BlockSpec((D_MODEL, PROJ_WIDTH), lambda i: (0, 0)),
                  pl.BlockSpec((CONV_K, CONV_DIM), lambda i: (0, 0))],
        out_specs=[slab_spec, slab_spec, slab_spec,
                   pl.BlockSpec((tm, CONV_DIM), lambda i: (i, 0)),
                   pl.BlockSpec((1, CONV_K - 1, CONV_DIM), lambda i: (i // tiles_per_seq, 0, 0))],
        out_shape=[slab, slab, slab,
                   jax.ShapeDtypeStruct((n, CONV_DIM), F32),
                   jax.ShapeDtypeStruct((n_seq, CONV_K - 1, CONV_DIM), F32)],
        scratch_shapes=[pltpu.VMEM((tm + SUBLANES, CONV_DIM), F32)],
        compiler_params=_params("arbitrary"),
        name="in_proj_prompt",
    )(x, g, w_bf16, conv_w)


def _attn_kernel(q_ref, k_ref, v_ref, bias_ref, o_ref, kcat, vcat, oacc, lacc, *, sb):
    j = pl.program_id(2)

    @pl.when(j == 0)
    def _():
        kcat[0:sb, :] = jnp.zeros((sb, LANES), F32)
        vcat[0:sb, :] = jnp.zeros((sb, LANES), F32)

    @pl.when(j > 0)
    def _():
        kcat[0:sb, :] = kcat[sb:2 * sb, :]
        vcat[0:sb, :] = vcat[sb:2 * sb, :]

    kcat[sb:2 * sb, :] = k_ref[...]
    vcat[sb:2 * sb, :] = v_ref[...]

    lane = lax.broadcasted_iota(jnp.int32, (Q_BLOCK, LANES), 1)
    low = lane < HEAD_DIM
    ones = jnp.ones((2 * Q_BLOCK, LANES), BF16)

    for di, d in enumerate(DILATIONS):
        nq = sb // (Q_BLOCK * d)

        def unit(u, di=di, d=d, nq=nq):
            r = u // nq
            n = u % nq
            qstart = r + d * Q_BLOCK * n
            kstart = sb + qstart - d * Q_BLOCK
            q = q_ref[pl.ds(qstart, Q_BLOCK, stride=d), :] * SCALE
            kk = kcat[pl.ds(kstart, 2 * Q_BLOCK, stride=d), :].astype(BF16)
            vv = vcat[pl.ds(kstart, 2 * Q_BLOCK, stride=d), :].astype(BF16)
            qm = jnp.concatenate([jnp.where(low, q, 0.0), jnp.where(low, 0.0, q)],
                                 axis=0).astype(BF16)
            s = lax.dot_general(qm, kk, (((1,), (1,)), ((), ())),
                                preferred_element_type=F32)
            first = jnp.logical_and(j == 0, n == 0).astype(jnp.int32)
            s = s + bias_ref[di, first]
            m = jnp.max(s, axis=-1, keepdims=True)
            p = jnp.exp(s - m).astype(BF16)
            pv = jnp.dot(p, jnp.concatenate([vv, ones], axis=1),
                         preferred_element_type=F32)
            o_sel = jnp.where(low, pv[0:Q_BLOCK, 0:LANES], pv[Q_BLOCK:, 0:LANES])
            l_sel = jnp.where(low, pv[0:Q_BLOCK, LANES:], pv[Q_BLOCK:, LANES:])
            m_sel = jnp.where(low, jnp.broadcast_to(m[0:Q_BLOCK], (Q_BLOCK, LANES)),
                              jnp.broadcast_to(m[Q_BLOCK:], (Q_BLOCK, LANES)))
            oacc[di, pl.ds(qstart, Q_BLOCK, stride=d), :] = o_sel / l_sel
            lacc[di, pl.ds(qstart, Q_BLOCK, stride=d), :] = m_sel + jnp.log(l_sel)

        def group(g, carry, unit=unit):
            for uu in range(UNITS_PER_TRIP):
                unit(g * UNITS_PER_TRIP + uu)
            return carry

        lax.fori_loop(0, sb // Q_BLOCK // UNITS_PER_TRIP, group, 0)

    def merge(c, carry):
        rows = pl.ds(pl.multiple_of(c * 256, 256), 256)
        l0, l1, l2 = lacc[0, rows, :], lacc[1, rows, :], lacc[2, rows, :]
        mm = jnp.maximum(jnp.maximum(l0, l1), l2)
        w0, w1, w2 = jnp.exp(l0 - mm), jnp.exp(l1 - mm), jnp.exp(l2 - mm)
        num = w0 * oacc[0, rows, :] + w1 * oacc[1, rows, :] + w2 * oacc[2, rows, :]
        o_ref[rows, :] = num / (w0 + w1 + w2)
        return carry

    lax.fori_loop(0, sb // 256, merge, 0)


def _step_bias(rel_bias):
    steps = jnp.arange(SEG_KEYS + 1)
    return [rel_bias[_rel_bucket(steps * d)].astype(F32) for d in DILATIONS]


def _prompt_bias_tables(step_bias):
    width = 3 * Q_BLOCK
    k_loc = jnp.arange(2 * Q_BLOCK)
    tables = []
    for b in step_bias:
        g = jnp.concatenate([jnp.full((N_HEADS, Q_BLOCK - 1), NEG, F32), b.T,
                             jnp.full((N_HEADS, width - Q_BLOCK - SEG_KEYS), NEG, F32)], axis=1)
        flat = jnp.broadcast_to(g[:, None, :], (N_HEADS, Q_BLOCK + 1, width)).reshape(N_HEADS, -1)
        x = flat[:, :Q_BLOCK * (width + 1)].reshape(N_HEADS, Q_BLOCK, width + 1)
        t = x[:, :, :2 * Q_BLOCK][:, :, ::-1]
        t_first = jnp.where(k_loc[None, None, :] >= Q_BLOCK, t, NEG)
        tables.append(jnp.stack([t.reshape(N_SLABS, 2 * Q_BLOCK, 2 * Q_BLOCK),
                                 t_first.reshape(N_SLABS, 2 * Q_BLOCK, 2 * Q_BLOCK)]))
    return jnp.stack(tables)


def _attention_prompt(q, k, v, bias_tab, n_seq, seq_len, sb=2048):
    n = q.shape[1]
    nsb = seq_len // sb
    blk = pl.BlockSpec((None, sb, LANES), lambda b, s, j: (s, b * nsb + j, 0))
    return pl.pallas_call(
        functools.partial(_attn_kernel, sb=sb),
        grid=(n_seq, N_SLABS, nsb),
        in_specs=[blk, blk, blk,
                  pl.BlockSpec((len(DILATIONS), 2, None, 2 * Q_BLOCK, 2 * Q_BLOCK),
                               lambda b, s, j: (0, 0, s, 0, 0))],
        out_specs=blk,
        out_shape=jax.ShapeDtypeStruct((N_SLABS, n, LANES), F32),
        scratch_shapes=[pltpu.VMEM((2 * sb, LANES), F32), pltpu.VMEM((2 * sb, LANES), F32),
                        pltpu.VMEM((len(DILATIONS), sb, LANES), F32),
                        pltpu.VMEM((len(DILATIONS), sb, LANES), F32)],
        compiler_params=_params("arbitrary", "arbitrary", "arbitrary"),
        name="attention_prompt",
    )(q, k, v, bias_tab)


ROUTER_ROWS = 48
MOE_TILE = 256
RUN_ALIGN = 2 * SUBLANES
TABLE_LANES = 128
EXPERT_BUFFERS = 4


def _local_rows(tm):
    return -(-(2 * tm + N_EXPERTS * (RUN_ALIGN - 1)) // LANES) * LANES


def _route(lg):
    tm = lg.shape[1]
    gl = lg[N_EXPERTS:N_EXPERTS + N_GROUPS]
    grow = lax.broadcasted_iota(jnp.int32, (N_GROUPS, tm), 0)
    gmax = jnp.max(gl, axis=0, keepdims=True)
    gsum = jnp.sum(jnp.exp(gl - gmax), axis=0, keepdims=True)
    gidx = jnp.min(jnp.where(gl == gmax, grow, N_GROUPS), axis=0, keepdims=True)
    p_grp = 1.0 / gsum
    el = lg[0:N_EXPERTS]
    e = lax.broadcasted_iota(jnp.int32, (N_EXPERTS, tm), 0)
    in_grp = (e // EXPERTS_PER_GROUP) == gidx
    elog = jnp.where(in_grp, el, -jnp.inf)
    emax = jnp.max(elog, axis=0, keepdims=True)
    eexp = jnp.exp(elog - emax)
    eprob = jnp.where(in_grp, eexp / jnp.sum(eexp, axis=0, keepdims=True), -1.0)
    p1 = jnp.max(eprob, axis=0, keepdims=True)
    i1 = jnp.min(jnp.where(eprob == p1, e, N_EXPERTS), axis=0, keepdims=True)
    rest = jnp.where(e == i1, -1.0, eprob)
    p2 = jnp.max(rest, axis=0, keepdims=True)
    i2 = jnp.min(jnp.where(rest == p2, e, N_EXPERTS), axis=0, keepdims=True)
    tot = p1 + p2
    return e == i1, e == i2, p_grp * (p1 / tot), p_grp * (p2 / tot)


def _split_bf16(x):
    hi = x.astype(BF16)
    return hi, (x - hi.astype(F32)).astype(BF16)


def _out_proj_kernel(attn_ref, conv_ref, h_ref, wo_ref, g_ref, wr_ref, upper_ref, lower_ref,
                     h1_ref, xn_ref, route_ref, routet_ref, cnt_ref, *, tm, sub):
    contract_last = (((1,), (1,)), ((), ()))
    mix = jnp.concatenate([attn_ref[s] for s in range(N_SLABS)] + [conv_ref[...]], axis=1)
    h1_ref[...] = h_ref[...] + jnp.dot(mix.astype(BF16), wo_ref[...], preferred_element_type=F32)
    for t in range(tm // sub):
        rs = slice(t * sub, (t + 1) * sub)
        xn = _rms(h1_ref[rs, :], g_ref[...])
        xh, xl = _split_bf16(xn)
        xn_ref[rs, :] = xh
        both = lax.dot_general(wr_ref[...], xh, contract_last, preferred_element_type=F32)
        lg = (both[:ROUTER_ROWS] + both[ROUTER_ROWS:]
              + lax.dot_general(wr_ref[:ROUTER_ROWS, :], xl, contract_last, preferred_element_type=F32))
        sel1, sel2, g1, g2 = _route(lg)
        chosen = jnp.where(sel1 | sel2, 1.0, 0.0)
        rank = jnp.dot(chosen.astype(BF16), upper_ref[...], preferred_element_type=F32)
        cnt = jnp.sum(chosen, axis=1, keepdims=True)
        cnt_pad = jnp.floor((cnt + (RUN_ALIGN - 1)) * (1.0 / RUN_ALIGN)) * RUN_ALIGN
        cnt_b = jnp.broadcast_to(cnt_pad, (N_EXPERTS, LANES))
        start = jnp.dot(lower_ref[...], cnt_b.astype(BF16), preferred_element_type=F32)
        pos = start[:, 0:1] + rank
        pos1 = jnp.sum(jnp.where(sel1, pos, 0.0), axis=0, keepdims=True)
        pos2 = jnp.sum(jnp.where(sel2, pos, 0.0), axis=0, keepdims=True)
        rows = jnp.concatenate([pos1, pos2, g1, g2, jnp.zeros((SUBLANES - 4, sub), F32)], axis=0)
        route_ref[t] = rows
        routet_ref[rs, :] = jnp.concatenate([rows, jnp.zeros((LANES - SUBLANES, sub), F32)], axis=0).T
        cnt_ref[t] = cnt_b


def _out_proj(attn, conv, h, wo_bf16, g, w_router_t, tm, sub):
    n = h.shape[0]
    w_router = jnp.concatenate(_split_bf16(w_router_t), axis=0)
    idx = jnp.arange(sub)
    upper = (idx[:, None] < idx[None, :]).astype(BF16)
    eidx = jnp.arange(N_EXPERTS)
    lower = (eidx[None, :] < eidx[:, None]).astype(BF16)
    const = lambda shape: pl.BlockSpec(shape, lambda i: (0,) * len(shape))
    per_step = tm // sub
    return pl.pallas_call(
        functools.partial(_out_proj_kernel, tm=tm, sub=sub),
        grid=(n // tm,),
        in_specs=[pl.BlockSpec((N_SLABS, tm, LANES), lambda i: (0, i, 0)),
                  pl.BlockSpec((tm, CONV_DIM), lambda i: (i, 0)),
                  pl.BlockSpec((tm, D_MODEL), lambda i: (i, 0)),
                  const((D_MODEL, D_MODEL)), const((1, D_MODEL)),
                  const((2 * ROUTER_ROWS, D_MODEL)),
                  const((sub, sub)), const((N_EXPERTS, N_EXPERTS))],
        out_specs=[pl.BlockSpec((tm, D_MODEL), lambda i: (i, 0)),
                   pl.BlockSpec((tm, D_MODEL), lambda i: (i, 0)),
                   pl.BlockSpec((per_step, SUBLANES, sub), lambda i: (i, 0, 0)),
                   pl.BlockSpec((tm, LANES), lambda i: (i, 0)),
                   pl.BlockSpec((per_step, N_EXPERTS, LANES), lambda i: (i, 0, 0))],
        out_shape=[jax.ShapeDtypeStruct((n, D_MODEL), F32),
                   jax.ShapeDtypeStruct((n, D_MODEL), BF16),
                   jax.ShapeDtypeStruct((n // sub, SUBLANES, sub), F32),
                   jax.ShapeDtypeStruct((n, LANES), F32),
                   jax.ShapeDtypeStruct((n // sub, N_EXPERTS, LANES), F32)],
        compiler_params=_params("arbitrary"),
        name="out_proj",
    )(attn, conv, h, wo_bf16, g, w_router, upper, lower)


def _moe_plan(cnt_tiles, max_tiles):
    c = cnt_tiles.astype(jnp.int32)
    rows_e = jnp.sum(c, axis=0)
    tiles_e = (rows_e + MOE_TILE - 1) // MOE_TILE
    cum = jnp.cumsum(tiles_e)
    total = cum[-1]
    first_tile = cum - tiles_e

    run_start = jnp.cumsum(c, axis=0) - c
    local_start = jnp.cumsum(c, axis=1) - c
    local_end = local_start + c
    row = jnp.arange(TABLE_LANES, dtype=jnp.int32) * RUN_ALIGN
    inside = (local_start[:, None, :] <= row[None, :, None]) & (row[None, :, None] < local_end[:, None, :])
    base = first_tile[None] * MOE_TILE + run_start - local_start
    n_chunks = jnp.sum(c, axis=1) // RUN_ALIGN
    table = jnp.sum(jnp.where(inside, base[:, None, :] + row[None, :, None], 0), axis=-1)
    live = jnp.arange(TABLE_LANES)[None] < n_chunks[:, None]
    table = jnp.where(live, table, max_tiles * MOE_TILE + row[None])
    i32 = lambda a: a.astype(jnp.int32)
    return (i32(table), i32(n_chunks), i32(first_tile), i32(tiles_e), i32(rows_e),
            i32(total).reshape(1))


def _dispatch_kernel(tbl_ref, xn_ref, route_ref, *rest, tm, rl, chained):
    xs_hbm, lsort, sem = rest[1:] if chained else rest
    i = pl.program_id(0)
    slot = i % 2
    rows = route_ref[0]
    pos1 = rows[0:1].astype(jnp.int32)
    pos2 = rows[1:2].astype(jnp.int32)
    p = lax.broadcasted_iota(jnp.int32, (rl, tm), 0)
    one_hot = jnp.where((p == pos1) | (p == pos2), 1.0, 0.0).astype(BF16)
    lsort[slot] = jnp.dot(one_hot, xn_ref[...], preferred_element_type=F32).astype(BF16)

    def tile_copies_done(s):
        pltpu.make_async_copy(lsort.at[s], xs_hbm.at[pl.ds(0, rl), :], sem.at[s]).wait()

    @pl.when(i > 0)
    def _():
        tile_copies_done(1 - slot)

    for q in range(rl // RUN_ALIGN):
        row = pl.multiple_of(tbl_ref[i * TABLE_LANES + q], RUN_ALIGN)
        pltpu.make_async_copy(lsort.at[slot, pl.ds(q * RUN_ALIGN, RUN_ALIGN), :],
                              xs_hbm.at[pl.ds(row, RUN_ALIGN), :], sem.at[slot]).start()

    @pl.when(i == pl.num_programs(0) - 1)
    def _():
        tile_copies_done(slot)


def _dispatch(table, xn, route, tm, total_rows, xs_prev=None):
    nt = xn.shape[0] // tm
    rl = _local_rows(tm)
    chained = xs_prev is not None
    in_specs = [pl.BlockSpec((tm, D_MODEL), lambda i, t: (i, 0)),
                pl.BlockSpec((1, SUBLANES, tm), lambda i, t: (i, 0, 0))]
    args = [table.reshape(-1), xn, route]
    if chained:
        in_specs.append(pl.BlockSpec(memory_space=pl.ANY))
        args.append(xs_prev)
    return pl.pallas_call(
        functools.partial(_dispatch_kernel, tm=tm, rl=rl, chained=chained),
        grid_spec=pltpu.PrefetchScalarGridSpec(
            num_scalar_prefetch=1, grid=(nt,), in_specs=in_specs,
            out_specs=pl.BlockSpec(memory_space=pl.ANY),
            scratch_shapes=[pltpu.VMEM((2, rl, D_MODEL), BF16), pltpu.SemaphoreType.DMA((2,))]),
        out_shape=jax.ShapeDtypeStruct((total_rows, D_MODEL), BF16),
        input_output_aliases={3: 0} if chained else {},
        compiler_params=_params("arbitrary"),
        name="moe_dispatch",
    )(*args)


def _expert_kernel(first_ref, tiles_ref, rows_ref, tot_ref, xs_hbm, wg_ref, wu_ref, wd_ref, eo_hbm,
                   xbuf, obuf, wg_b, wu_b, wd_b, sem_in, sem_out):
    e = pl.program_id(0)
    first_tile = first_ref[e]
    n_tiles = tiles_ref[e]
    total = tot_ref[0]

    def in_copy(g):
        return pltpu.make_async_copy(
            xs_hbm.at[pl.ds(pl.multiple_of(g * MOE_TILE, MOE_TILE), MOE_TILE), :],
            xbuf.at[g % EXPERT_BUFFERS], sem_in.at[g % EXPERT_BUFFERS])

    def out_copy(g):
        return pltpu.make_async_copy(
            obuf.at[g % EXPERT_BUFFERS],
            eo_hbm.at[pl.ds(pl.multiple_of(g * MOE_TILE, MOE_TILE), MOE_TILE), :],
            sem_out.at[g % EXPERT_BUFFERS])

    @pl.when(e == 0)
    def _():
        for g0 in range(EXPERT_BUFFERS - 1):
            @pl.when(g0 < total)
            def _():
                in_copy(g0).start()

    @pl.when(n_tiles > 0)
    def _():
        wg_b[...] = wg_ref[0, 0].astype(BF16)
        wu_b[...] = wu_ref[0, 0].astype(BF16)
        wd_b[...] = wd_ref[0, 0].astype(BF16)

    def tile(j, carry):
        g = first_tile + j

        @pl.when(g + EXPERT_BUFFERS - 1 < total)
        def _():
            in_copy(g + EXPERT_BUFFERS - 1).start()

        in_copy(g).wait()

        @pl.when(g >= EXPERT_BUFFERS)
        def _():
            out_copy(g - EXPERT_BUFFERS).wait()

        row = lax.broadcasted_iota(jnp.int32, (MOE_TILE, 1), 0)
        x = xbuf[g % EXPERT_BUFFERS]
        x = jnp.where(row < rows_ref[e] - j * MOE_TILE, x, jnp.zeros_like(x))
        gate = jnp.dot(x, wg_b[...], preferred_element_type=F32)
        up = jnp.dot(x, wu_b[...], preferred_element_type=F32)
        hdn = gate * jax.nn.sigmoid(gate) * up
        obuf[g % EXPERT_BUFFERS] = jnp.dot(hdn.astype(BF16), wd_b[...],
                                           preferred_element_type=F32).astype(BF16)
        out_copy(g).start()
        return carry

    lax.fori_loop(0, n_tiles, tile, 0)

    @pl.when(e == pl.num_programs(0) - 1)
    def _():
        for back in range(EXPERT_BUFFERS, 0, -1):
            @pl.when(total >= back)
            def _():
                out_copy(total - back).wait()


def _experts(first_tile, tiles_e, rows_e, total, xs, w_gate, w_up, w_down, layer):
    weight = lambda e, f, t, r, n: (layer, e, 0, 0)
    return pl.pallas_call(
        _expert_kernel,
        grid_spec=pltpu.PrefetchScalarGridSpec(
            num_scalar_prefetch=4, grid=(N_EXPERTS,),
            in_specs=[pl.BlockSpec(memory_space=pl.ANY),
                      pl.BlockSpec((1, 1, D_MODEL, D_EXPERT), weight),
                      pl.BlockSpec((1, 1, D_MODEL, D_EXPERT), weight),
                      pl.BlockSpec((1, 1, D_EXPERT, D_MODEL), weight)],
            out_specs=pl.BlockSpec(memory_space=pl.ANY),
            scratch_shapes=[pltpu.VMEM((EXPERT_BUFFERS, MOE_TILE, D_MODEL), BF16),
                            pltpu.VMEM((EXPERT_BUFFERS, MOE_TILE, D_MODEL), BF16),
                            pltpu.VMEM((D_MODEL, D_EXPERT), BF16),
                            pltpu.VMEM((D_MODEL, D_EXPERT), BF16),
                            pltpu.VMEM((D_EXPERT, D_MODEL), BF16),
                            pltpu.SemaphoreType.DMA((EXPERT_BUFFERS,)),
                            pltpu.SemaphoreType.DMA((EXPERT_BUFFERS,))]),
        out_shape=jax.ShapeDtypeStruct(xs.shape, BF16),
        compiler_params=_params("arbitrary"),
        name="moe_experts",
    )(first_tile, tiles_e, rows_e, total, xs, w_gate, w_up, w_down)


def _combine_kernel(tbl_ref, nch_ref, h1_ref, routet_ref, gf_ref, eo_hbm, o_ref, leo, sem,
                    *, tm, rl, final_norm):
    i = pl.program_id(0)
    n = pl.num_programs(0)
    slot = i % 2

    def gather(tile, s):
        for q in range(rl // RUN_ALIGN):
            row = jnp.where(q < nch_ref[tile], tbl_ref[tile * TABLE_LANES + q], q * RUN_ALIGN)
            pltpu.make_async_copy(eo_hbm.at[pl.ds(pl.multiple_of(row, RUN_ALIGN), RUN_ALIGN), :],
                                  leo.at[s, pl.ds(q * RUN_ALIGN, RUN_ALIGN), :], sem.at[s]).start()

    @pl.when(i == 0)
    def _():
        gather(0, 0)

    @pl.when(i + 1 < n)
    def _():
        gather(i + 1, 1 - slot)

    pltpu.make_async_copy(eo_hbm.at[pl.ds(0, rl), :], leo.at[slot], sem.at[slot]).wait()

    rt = routet_ref[...]
    pos1 = rt[:, 0:1].astype(jnp.int32)
    pos2 = rt[:, 1:2].astype(jnp.int32)
    lane = lax.broadcasted_iota(jnp.int32, (tm, rl), 1)
    weights = jnp.where(lane == pos1, rt[:, 2:3], 0.0) + jnp.where(lane == pos2, rt[:, 3:4], 0.0)
    y = jnp.dot(weights.astype(BF16), leo[slot], preferred_element_type=F32)
    out = h1_ref[...] + y
    if final_norm:
        out = _rms(out, gf_ref[...])
    o_ref[...] = out


def _combine(table, n_chunks, h1, routet, g_final, eo, tm, final_norm):
    n = h1.shape[0]
    rl = _local_rows(tm)
    return pl.pallas_call(
        functools.partial(_combine_kernel, tm=tm, rl=rl, final_norm=final_norm),
        grid_spec=pltpu.PrefetchScalarGridSpec(
            num_scalar_prefetch=2, grid=(n // tm,),
            in_specs=[pl.BlockSpec((tm, D_MODEL), lambda i, t, c: (i, 0)),
                      pl.BlockSpec((tm, LANES), lambda i, t, c: (i, 0)),
                      pl.BlockSpec((1, D_MODEL), lambda i, t, c: (0, 0)),
                      pl.BlockSpec(memory_space=pl.ANY)],
            out_specs=pl.BlockSpec((tm, D_MODEL), lambda i, t, c: (i, 0)),
            scratch_shapes=[pltpu.VMEM((2, rl, D_MODEL), BF16), pltpu.SemaphoreType.DMA((2,))]),
        out_shape=jax.ShapeDtypeStruct((n, D_MODEL), F32),
        compiler_params=_params("arbitrary"),
        name="moe_combine",
    )(table.reshape(-1), n_chunks, h1, routet, g_final, eo)


def _in_proj_sample_kernel(x_ref, g_ref, w_ref, o_ref):
    xn = _rms(x_ref[...], g_ref[...])
    o_ref[...] = jnp.dot(xn.astype(BF16), w_ref[...], preferred_element_type=F32)


def _in_proj_sample(x, g, w_bf16):
    n = x.shape[0]
    return pl.pallas_call(
        _in_proj_sample_kernel,
        out_shape=jax.ShapeDtypeStruct((n, PROJ_WIDTH), F32),
        compiler_params=pltpu.CompilerParams(vmem_limit_bytes=VMEM_LIMIT),
        name="in_proj_sample",
    )(x, g, w_bf16)


def _sample_mix_kernel(q_ref, kn_ref, vn_ref, kt_ref, vt_ref,
                       bc_ref, bn_ref, gb_ref, gc_ref, hc_ref, st_ref, cw_ref,
                       o_ref, conv_ref, st_out_ref, *, t_new):
    contract_last = (((1,), (1,)), ((), ()))
    for h in range(N_HEADS):
        q = (q_ref[0, h] * SCALE).astype(BF16)
        s_c = jnp.dot(q, kt_ref[0, 0, h].astype(BF16), preferred_element_type=F32) + bc_ref[h]
        s_n = lax.dot_general(q, kn_ref[0, h].astype(BF16), contract_last,
                              preferred_element_type=F32) + bn_ref[h]
        m = jnp.maximum(jnp.max(s_c, axis=-1, keepdims=True), jnp.max(s_n, axis=-1, keepdims=True))
        p_c = jnp.exp(s_c - m)
        p_n = jnp.exp(s_n - m)
        den = jnp.sum(p_c, axis=-1, keepdims=True) + jnp.sum(p_n, axis=-1, keepdims=True)
        num = (lax.dot_general(p_c.astype(BF16), vt_ref[0, 0, h].astype(BF16), contract_last,
                               preferred_element_type=F32)
               + jnp.dot(p_n.astype(BF16), vn_ref[0, h].astype(BF16), preferred_element_type=F32))
        out = num / den
        lse = m + jnp.log(den)
        ls = [lse[i * t_new:(i + 1) * t_new] for i in range(len(DILATIONS))]
        os_ = [out[i * t_new:(i + 1) * t_new] for i in range(len(DILATIONS))]
        mm = jnp.maximum(jnp.maximum(ls[0], ls[1]), ls[2])
        w = [jnp.exp(l - mm) for l in ls]
        o_ref[0, h] = (w[0] * os_[0] + w[1] * os_[1] + w[2] * os_[2]) / (w[0] + w[1] + w[2])
    cw = cw_ref[...]
    u = gc_ref[0] * hc_ref[0]
    st = st_ref[0]
    rows = [st[j:j + 1] for j in range(CONV_K - 1)] + [u[t:t + 1] for t in range(t_new)]
    gb = gb_ref[0]
    conv_ref[0] = jnp.concatenate(
        [gb[t:t + 1] * sum(cw[j:j + 1] * rows[t + j] for j in range(CONV_K)) for t in range(t_new)],
        axis=0)
    st_out_ref[0] = jnp.concatenate(rows[t_new:], axis=0)


SAMPLE_ROWS = 16


def _sample_bias_tables(step_bias, t_new, w_buf):
    cache_rows = []
    for b, d in zip(step_bias, DILATIONS):
        back = b[1:][::-1].T
        if d >= t_new:
            residue = np.eye(d, dtype=bool)[:t_new]
            span = jnp.where(residue[None, :, None, :], back[:, None, :, None], NEG)
            span = span.reshape(N_HEADS, t_new, SEG_KEYS * d)
        else:
            assert d == 1, "dilations between 1 and the number of new tokens are not supported"
            span = jnp.stack([jnp.pad(back[:, :SEG_KEYS - t], ((0, 0), (t, 0)), constant_values=NEG)
                              for t in range(t_new)], axis=1)
        cache_rows.append(jnp.pad(span, ((0, 0), (0, 0), (w_buf - SEG_KEYS * d, 0)),
                                  constant_values=NEG))
    n_rows = len(DILATIONS) * t_new
    bias_c = jnp.pad(jnp.concatenate(cache_rows, axis=1), ((0, 0), (0, SAMPLE_ROWS - n_rows), (0, 0)))

    rows, cols, which, steps = [], [], [], []
    for di, d in enumerate(DILATIONS):
        for t in range(t_new):
            for t2 in range(t + 1):
                if (t - t2) % d == 0:
                    rows.append(di * t_new + t)
                    cols.append(t2)
                    which.append(di)
                    steps.append((t - t2) // d)
    vals = jnp.stack(step_bias)[np.array(which), np.array(steps)]
    bias_n = jnp.full((N_HEADS, n_rows, LANES), NEG, F32).at[:, np.array(rows), np.array(cols)].set(vals.T)
    bias_n = jnp.pad(bias_n, ((0, 0), (0, SAMPLE_ROWS - n_rows), (0, 0)))
    return bias_c, bias_n


def _sample_mix(q, kn, vn, cache_k, cache_v, layer, bias_c, bias_n, gb, gc, hc, state, conv_w):
    db, t_new = q.shape[0], q.shape[1]
    w_buf = cache_k.shape[2]
    kt = cache_k.transpose(0, 1, 3, 4, 2)
    vt = cache_v.transpose(0, 1, 3, 4, 2)
    n_rows = len(DILATIONS) * t_new
    q_rows = jnp.pad(jnp.tile(q.transpose(0, 2, 1, 3), (1, 1, len(DILATIONS), 1)),
                     ((0, 0), (0, 0), (0, SAMPLE_ROWS - n_rows), (0, 0)))
    kn_pad = jnp.pad(kn.transpose(0, 2, 1, 3), ((0, 0), (0, 0), (0, LANES - t_new), (0, 0)))
    vn_pad = jnp.pad(vn.transpose(0, 2, 1, 3), ((0, 0), (0, 0), (0, LANES - t_new), (0, 0)))
    row = pl.BlockSpec((1, t_new, CONV_DIM), lambda b: (b, 0, 0))
    new_spec = pl.BlockSpec((1, N_HEADS, LANES, HEAD_DIM), lambda b: (b, 0, 0, 0))
    win_spec = pl.BlockSpec((1, 1, N_HEADS, HEAD_DIM, w_buf), lambda b: (layer, b, 0, 0, 0))
    attn, conv, st = pl.pallas_call(
        functools.partial(_sample_mix_kernel, t_new=t_new),
        grid=(db,),
        in_specs=[pl.BlockSpec((1, N_HEADS, SAMPLE_ROWS, HEAD_DIM), lambda b: (b, 0, 0, 0)),
                  new_spec, new_spec, win_spec, win_spec,
                  pl.BlockSpec(bias_c.shape, lambda b: (0, 0, 0)),
                  pl.BlockSpec(bias_n.shape, lambda b: (0, 0, 0)),
                  row, row, row,
                  pl.BlockSpec((1, CONV_K - 1, CONV_DIM), lambda b: (b, 0, 0)),
                  pl.BlockSpec((CONV_K, CONV_DIM), lambda b: (0, 0))],
        out_specs=[pl.BlockSpec((1, N_HEADS, t_new, HEAD_DIM), lambda b: (b, 0, 0, 0)), row,
                   pl.BlockSpec((1, CONV_K - 1, CONV_DIM), lambda b: (b, 0, 0))],
        out_shape=[jax.ShapeDtypeStruct((db, N_HEADS, t_new, HEAD_DIM), F32),
                   jax.ShapeDtypeStruct((db, t_new, CONV_DIM), F32),
                   jax.ShapeDtypeStruct((db, CONV_K - 1, CONV_DIM), F32)],
        compiler_params=_params("arbitrary"),
        name="sample_mix",
    )(q_rows, kn_pad, vn_pad, kt, vt, bias_c, bias_n, gb, gc, hc, state, conv_w)
    return attn.transpose(0, 2, 1, 3), conv, st


def _to_slabs(x):
    n = x.shape[0]
    return x.reshape(n, N_SLABS, LANES).transpose(1, 0, 2)


def _window_from_slabs(x, batch, seq, w_keep):
    win = x.reshape(N_SLABS, batch, seq, LANES)[:, :, seq - w_keep:]
    return win.transpose(1, 2, 0, 3).reshape(batch, w_keep, N_HEADS, HEAD_DIM)


def kernel(x_prompt, x_sample, cache_k, cache_v, state_conv, rel_bias, norm_mix, norm_ffn,
           norm_final, w_in, conv_w, w_out, w_router_group, w_router_expert, w_gate, w_up,
           w_down):
    batch, seq, _ = x_prompt.shape
    db, t_new, _ = x_sample.shape
    depth = w_in.shape[0]
    w_keep = min(MAX_WINDOW, seq)

    hp = x_prompt.reshape(batch * seq, D_MODEL)
    hs = x_sample.reshape(db * t_new, D_MODEL)
    step_bias = _step_bias(rel_bias)
    bias_prompt = _prompt_bias_tables(step_bias)
    bias_c, bias_n = _sample_bias_tables(step_bias, t_new, cache_k.shape[2])
    g_final = norm_final.reshape(1, D_MODEL)

    tm_p = 256
    n_s = db * t_new
    nt_p = batch * seq // tm_p
    pad_per_tile = N_EXPERTS * (RUN_ALIGN - 1)
    max_tiles = -(-(2 * (batch * seq + n_s) + (nt_p + 1) * pad_per_tile) // MOE_TILE) + N_EXPERTS
    buf_rows = max_tiles * MOE_TILE + _local_rows(tm_p)

    pk, pv, pc, sk, sv, sc = [], [], [], [], [], []
    for l in range(depth):
        w_in_b = w_in[l].astype(BF16)
        w_out_b = w_out[l].astype(BF16)
        w_router_t = jnp.pad(jnp.concatenate([w_router_expert[l], w_router_group[l]], axis=1).T,
                             ((0, ROUTER_ROWS - N_EXPERTS - N_GROUPS), (0, 0)))
        g_mix = norm_mix[l].reshape(1, D_MODEL)
        g_ffn = norm_ffn[l].reshape(1, D_MODEL)
        last = l == depth - 1

        q, k, v, conv, u_last = _in_proj(hp, g_mix, w_in_b, conv_w[l], seq)
        attn = _attention_prompt(q, k, v, bias_prompt, batch, seq)
        h1, xn, route, routet, cnt = _out_proj(attn, conv, hp, w_out_b, g_ffn, w_router_t,
                                               tm=4 * tm_p, sub=tm_p)
        pk.append(_window_from_slabs(k, batch, seq, w_keep))
        pv.append(_window_from_slabs(v, batch, seq, w_keep))
        pc.append(u_last)

        proj = _in_proj_sample(hs, g_mix, w_in_b)
        qs = proj[:, :ATTN_WIDTH].reshape(db, t_new, N_HEADS, HEAD_DIM)
        ks = proj[:, ATTN_WIDTH:2 * ATTN_WIDTH].reshape(db, t_new, N_HEADS, HEAD_DIM)
        vs = proj[:, 2 * ATTN_WIDTH:3 * ATTN_WIDTH].reshape(db, t_new, N_HEADS, HEAD_DIM)
        c0 = 3 * ATTN_WIDTH
        gb = proj[:, c0:c0 + CONV_DIM].reshape(db, t_new, CONV_DIM)
        gc = proj[:, c0 + CONV_DIM:c0 + 2 * CONV_DIM].reshape(db, t_new, CONV_DIM)
        hc = proj[:, c0 + 2 * CONV_DIM:].reshape(db, t_new, CONV_DIM)
        attn_s, conv_s, state_s = _sample_mix(qs, ks, vs, cache_k, cache_v, l, bias_c, bias_n,
                                              gb, gc, hc, state_conv[l], conv_w[l])
        h1s, xns, route_s, routet_s, cnt_s = _out_proj(
            _to_slabs(attn_s.reshape(n_s, ATTN_WIDTH)), conv_s.reshape(n_s, CONV_DIM), hs,
            w_out_b, g_ffn, w_router_t, tm=n_s, sub=n_s)

        table, n_chunks, first_tile, tiles_e, rows_e, total = _moe_plan(
            jnp.concatenate([cnt[:, :, 0], cnt_s[:, :, 0]], axis=0), max_tiles)
        xs = _dispatch(table[:nt_p], xn, route, tm_p, buf_rows)
        xs = _dispatch(table[nt_p:], xns, route_s, n_s, buf_rows, xs_prev=xs)
        eo = _experts(first_tile, tiles_e, rows_e, total, xs, w_gate, w_up, w_down, l)
        hp = _combine(table[:nt_p], n_chunks[:nt_p], h1, routet, g_final, eo, tm_p, last)
        hs = _combine(table[nt_p:], n_chunks[nt_p:], h1s, routet_s, g_final, eo, n_s, last)
        sk.append(ks)
        sv.append(vs)
        sc.append(state_s)

    return (hp.reshape(batch, seq, D_MODEL), hs.reshape(db, t_new, D_MODEL),
            jnp.stack(pk), jnp.stack(pv), jnp.stack(pc),
            jnp.stack(sk), jnp.stack(sv), jnp.stack(sc))
```

```python
import functools
import math

import jax
import jax.numpy as jnp
import numpy as np
from jax import lax
from jax.experimental import pallas as pl
from jax.experimental.pallas import tpu as pltpu

F32 = jnp.float32
BF16 = jnp.bfloat16

D_MODEL = 1024
N_HEADS = 8
HEAD_DIM = 64
ATTN_WIDTH = N_HEADS * HEAD_DIM
CONV_DIM = D_MODEL - ATTN_WIDTH
CONV_K = 3
PROJ_WIDTH = 3 * ATTN_WIDTH + 3 * CONV_DIM
DILATIONS = (1, 4, 16)
SEG_KEYS = 128
Q_BLOCK = 128
MAX_WINDOW = 2048
N_BUCKETS = 32
MAX_DISTANCE = 2048
N_GROUPS = 4
EXPERTS_PER_GROUP = 8
N_EXPERTS = N_GROUPS * EXPERTS_PER_GROUP
D_EXPERT = D_MODEL // 4
EPS = 1e-6
NEG = -1e30
SCALE = 1.0 / math.sqrt(HEAD_DIM)

LANES = 128
SUBLANES = 8
N_SLABS = ATTN_WIDTH // LANES
VMEM_LIMIT = 48 * 1024 * 1024
UNITS_PER_TRIP = 8


def _params(*sem):
    return pltpu.CompilerParams(dimension_semantics=sem, vmem_limit_bytes=VMEM_LIMIT)


def _rms(x, g):
    return x * lax.rsqrt(jnp.mean(x * x, axis=-1, keepdims=True) + EPS) * g


def _rel_bucket(dist):
    max_exact = N_BUCKETS // 2
    d_f = jnp.maximum(dist, 1).astype(F32)
    large = max_exact + (jnp.log(d_f / max_exact) / math.log(MAX_DISTANCE / max_exact)
                         * (N_BUCKETS - max_exact)).astype(jnp.int32)
    large = jnp.minimum(large, N_BUCKETS - 1)
    return jnp.where(dist < max_exact, dist, large)


def _in_proj_kernel(x_ref, g_ref, w_ref, cw_ref, q_ref, k_ref, v_ref, conv_ref, ulast_ref,
                    ext_ref, *, tm, tiles_per_seq):
    i = pl.program_id(0)
    xn = _rms(x_ref[...], g_ref[...])
    proj = jnp.dot(xn.astype(BF16), w_ref[...], preferred_element_type=F32)
    for s in range(N_SLABS):
        q_ref[s] = proj[:, s * LANES:(s + 1) * LANES]
        k_ref[s] = proj[:, ATTN_WIDTH + s * LANES:ATTN_WIDTH + (s + 1) * LANES]
        v_ref[s] = proj[:, 2 * ATTN_WIDTH + s * LANES:2 * ATTN_WIDTH + (s + 1) * LANES]
    c0 = 3 * ATTN_WIDTH
    gb = proj[:, c0:c0 + CONV_DIM]
    u = proj[:, c0 + CONV_DIM:c0 + 2 * CONV_DIM] * proj[:, c0 + 2 * CONV_DIM:c0 + 3 * CONV_DIM]

    @pl.when(i % tiles_per_seq == 0)
    def _():
        ext_ref[0:SUBLANES, :] = jnp.zeros((SUBLANES, CONV_DIM), F32)

    ext_ref[SUBLANES:SUBLANES + tm, :] = u
    u1 = ext_ref[SUBLANES - 1:SUBLANES - 1 + tm, :]
    u2 = ext_ref[SUBLANES - 2:SUBLANES - 2 + tm, :]
    cw = cw_ref[...]
    conv_ref[...] = gb * (cw[0:1] * u2 + cw[1:2] * u1 + cw[2:3] * u)
    ulast_ref[0] = ext_ref[tm + SUBLANES - 2:tm + SUBLANES, :]
    ext_ref[0:SUBLANES, :] = ext_ref[tm:tm + SUBLANES, :]


def _in_proj(x, g, w_bf16, conv_w, seq_len, tm=1024):
    n = x.shape[0]
    n_seq = n // seq_len
    slab = jax.ShapeDtypeStruct((N_SLABS, n, LANES), F32)
    slab_spec = pl.BlockSpec((N_SLABS, tm, LANES), lambda i: (0, i, 0))
    tiles_per_seq = seq_len // tm
    return pl.pallas_call(
        functools.partial(_in_proj_kernel, tm=tm, tiles_per_seq=tiles_per_seq),
        grid=(n // tm,),
        in_specs=[pl.BlockSpec((tm, D_MODEL), lambda i: (i, 0)),
                  pl.BlockSpec((1, D_MODEL), lambda i: (0, 0)),
                  pl.BlockSpec((D_MODEL, PROJ_WIDTH), lambda i: (0, 0)),
                  pl.BlockSpec((CONV_K, CONV_DIM), lambda i: (0, 0))],
        out_specs=[slab_spec, slab_spec, slab_spec,
                   pl.BlockSpec((tm, CONV_DIM), lambda i: (i, 0)),
                   pl.BlockSpec((1, CONV_K - 1, CONV_DIM), lambda i: (i // tiles_per_seq, 0, 0))],
        out_shape=[slab, slab, slab,
                   jax.ShapeDtypeStruct((n, CONV_DIM), F32),
                   jax.ShapeDtypeStruct((n_seq, CONV_K - 1, CONV_DIM), F32)],
        scratch_shapes=[pltpu.VMEM((tm + SUBLANES, CONV_DIM), F32)],
        compiler_params=_params("arbitrary"),
        name="in_proj_prompt",
    )(x, g, w_bf16, conv_w)


def _attn_kernel(q_ref, k_ref, v_ref, bias_ref, o_ref, kcat, vcat, oacc, lacc, *, sb):
    j = pl.program_id(2)

    @pl.when(j == 0)
    def _():
        kcat[0:sb, :] = jnp.zeros((sb, LANES), F32)
        vcat[0:sb, :] = jnp.zeros((sb, LANES), F32)

    @pl.when(j > 0)
    def _():
        kcat[0:sb, :] = kcat[sb:2 * sb, :]
        vcat[0:sb, :] = vcat[sb:2 * sb, :]

    kcat[sb:2 * sb, :] = k_ref[...]
    vcat[sb:2 * sb, :] = v_ref[...]

    lane = lax.broadcasted_iota(jnp.int32, (Q_BLOCK, LANES), 1)
    low = lane < HEAD_DIM
    ones = jnp.ones((2 * Q_BLOCK, LANES), BF16)

    for di, d in enumerate(DILATIONS):
        nq = sb // (Q_BLOCK * d)

        def unit(u, di=di, d=d, nq=nq):
            r = u // nq
            n = u % nq
            qstart = r + d * Q_BLOCK * n
            kstart = sb + qstart - d * Q_BLOCK
            q = q_ref[pl.ds(qstart, Q_BLOCK, stride=d), :] * SCALE
            kk = kcat[pl.ds(kstart, 2 * Q_BLOCK, stride=d), :].astype(BF16)
            vv = vcat[pl.ds(kstart, 2 * Q_BLOCK, stride=d), :].astype(BF16)
            qm = jnp.concatenate([jnp.where(low, q, 0.0), jnp.where(low, 0.0, q)],
                                 axis=0).astype(BF16)
            s = lax.dot_general(qm, kk, (((1,), (1,)), ((), ())),
                                preferred_element_type=F32)
            first = jnp.logical_and(j == 0, n == 0).astype(jnp.int32)
            s = s + bias_ref[di, first]
            m = jnp.max(s, axis=-1, keepdims=True)
            p = jnp.exp(s - m).astype(BF16)
            pv = jnp.dot(p, jnp.concatenate([vv, ones], axis=1),
                         preferred_element_type=F32)
            o_sel = jnp.where(low, pv[0:Q_BLOCK, 0:LANES], pv[Q_BLOCK:, 0:LANES])
            l_sel = jnp.where(low, pv[0:Q_BLOCK, LANES:], pv[Q_BLOCK:, LANES:])
            m_sel = jnp.where(low, jnp.broadcast_to(m[0:Q_BLOCK], (Q_BLOCK, LANES)),
                              jnp.broadcast_to(m[Q_BLOCK:], (Q_BLOCK, LANES)))
            oacc[di, pl.ds(qstart, Q_BLOCK, stride=d), :] = o_sel / l_sel
            lacc[di, pl.ds(qstart, Q_BLOCK, stride=d), :] = m_sel + jnp.log(l_sel)

        def group(g, carry, unit=unit):
            for uu in range(UNITS_PER_TRIP):
                unit(g * UNITS_PER_TRIP + uu)
            return carry

        lax.fori_loop(0, sb // Q_BLOCK // UNITS_PER_TRIP, group, 0)

    def merge(c, carry):
        rows = pl.ds(pl.multiple_of(c * 256, 256), 256)
        l0, l1, l2 = lacc[0, rows, :], lacc[1, rows, :], lacc[2, rows, :]
        mm = jnp.maximum(jnp.maximum(l0, l1), l2)
        w0, w1, w2 = jnp.exp(l0 - mm), jnp.exp(l1 - mm), jnp.exp(l2 - mm)
        num = w0 * oacc[0, rows, :] + w1 * oacc[1, rows, :] + w2 * oacc[2, rows, :]
        o_ref[rows, :] = num / (w0 + w1 + w2)
        return carry

    lax.fori_loop(0, sb // 256, merge, 0)


def _step_bias(rel_bias):
    steps = jnp.arange(SEG_KEYS + 1)
    return [rel_bias[_rel_bucket(steps * d)].astype(F32) for d in DILATIONS]


def _prompt_bias_tables(step_bias):
    width = 3 * Q_BLOCK
    k_loc = jnp.arange(2 * Q_BLOCK)
    tables = []
    for b in step_bias:
        g = jnp.concatenate([b[::-1].T, jnp.full((N_HEADS, width - SEG_KEYS), NEG, F32)], axis=1)
        flat = jnp.broadcast_to(g[:, None, :], (N_HEADS, Q_BLOCK, width + 1)).reshape(N_HEADS, -1)
        x = flat[:, :Q_BLOCK * width].reshape(N_HEADS, Q_BLOCK, width)
        t = x[:, :, :2 * Q_BLOCK]
        t_first = jnp.where(k_loc[None, None, :] >= Q_BLOCK, t, NEG)
        tables.append(jnp.stack([t.reshape(N_SLABS, 2 * Q_BLOCK, 2 * Q_BLOCK),
                                 t_first.reshape(N_SLABS, 2 * Q_BLOCK, 2 * Q_BLOCK)]))
    return jnp.stack(tables)


def _attention_prompt(q, k, v, bias_tab, n_seq, seq_len, sb=2048):
    n = q.shape[1]
    nsb = seq_len // sb
    blk = pl.BlockSpec((None, sb, LANES), lambda b, s, j: (s, b * nsb + j, 0))
    return pl.pallas_call(
        functools.partial(_attn_kernel, sb=sb),
        grid=(n_seq, N_SLABS, nsb),
        in_specs=[blk, blk, blk,
                  pl.BlockSpec((len(DILATIONS), 2, None, 2 * Q_BLOCK, 2 * Q_BLOCK),
                               lambda b, s, j: (0, 0, s, 0, 0))],
        out_specs=blk,
        out_shape=jax.ShapeDtypeStruct((N_SLABS, n, LANES), F32),
        scratch_shapes=[pltpu.VMEM((2 * sb, LANES), F32), pltpu.VMEM((2 * sb, LANES), F32),
                        pltpu.VMEM((len(DILATIONS), sb, LANES), F32),
                        pltpu.VMEM((len(DILATIONS), sb, LANES), F32)],
        compiler_params=_params("arbitrary", "arbitrary", "arbitrary"),
        name="attention_prompt",
    )(q, k, v, bias_tab)


ROUTER_ROWS = 48
MOE_TILE = 256
RUN_ALIGN = 2 * SUBLANES
TABLE_LANES = 128
EXPERT_BUFFERS = 4


def _local_rows(tm):
    return -(-(2 * tm + N_EXPERTS * (RUN_ALIGN - 1)) // LANES) * LANES


def _route(lg):
    tm = lg.shape[1]
    gl = lg[N_EXPERTS:N_EXPERTS + N_GROUPS]
    grow = lax.broadcasted_iota(jnp.int32, (N_GROUPS, tm), 0)
    gmax = jnp.max(gl, axis=0, keepdims=True)
    gsum = jnp.sum(jnp.exp(gl - gmax), axis=0, keepdims=True)
    gidx = jnp.min(jnp.where(gl == gmax, grow, N_GROUPS), axis=0, keepdims=True)
    p_grp = 1.0 / gsum
    el = lg[0:N_EXPERTS]
    e = lax.broadcasted_iota(jnp.int32, (N_EXPERTS, tm), 0)
    in_grp = (e // EXPERTS_PER_GROUP) == gidx
    elog = jnp.where(in_grp, el, -jnp.inf)
    emax = jnp.max(elog, axis=0, keepdims=True)
    eexp = jnp.exp(elog - emax)
    eprob = jnp.where(in_grp, eexp / jnp.sum(eexp, axis=0, keepdims=True), -1.0)
    p1 = jnp.max(eprob, axis=0, keepdims=True)
    i1 = jnp.min(jnp.where(eprob == p1, e, N_EXPERTS), axis=0, keepdims=True)
    rest = jnp.where(e == i1, -1.0, eprob)
    p2 = jnp.max(rest, axis=0, keepdims=True)
    i2 = jnp.min(jnp.where(rest == p2, e, N_EXPERTS), axis=0, keepdims=True)
    tot = p1 + p2
    return e == i1, e == i2, p_grp * (p1 / tot), p_grp * (p2 / tot)


def _split_bf16(x):
    hi = x.astype(BF16)
    return hi, (x - hi.astype(F32)).astype(BF16)


def _out_proj_kernel(attn_ref, conv_ref, h_ref, wo_ref, g_ref, wr_ref, upper_ref, lower_ref,
                     h1_ref, xn_ref, route_ref, routet_ref, cnt_ref, *, tm, sub):
    contract_last = (((1,), (1,)), ((), ()))
    mix = jnp.concatenate([attn_ref[s] for s in range(N_SLABS)] + [conv_ref[...]], axis=1)
    h1_ref[...] = h_ref[...] + jnp.dot(mix.astype(BF16), wo_ref[...], preferred_element_type=F32)
    for t in range(tm // sub):
        rs = slice(t * sub, (t + 1) * sub)
        xn = _rms(h1_ref[rs, :], g_ref[...])
        xh, xl = _split_bf16(xn)
        xn_ref[rs, :] = xh
        both = lax.dot_general(wr_ref[...], xh, contract_last, preferred_element_type=F32)
        lg = (both[:ROUTER_ROWS] + both[ROUTER_ROWS:]
              + lax.dot_general(wr_ref[:ROUTER_ROWS, :], xl, contract_last, preferred_element_type=F32))
        sel1, sel2, g1, g2 = _route(lg)
        chosen = jnp.where(sel1 | sel2, 1.0, 0.0)
        rank = jnp.dot(chosen.astype(BF16), upper_ref[...], preferred_element_type=F32)
        cnt = jnp.sum(chosen, axis=1, keepdims=True)
        cnt_pad = jnp.floor((cnt + (RUN_ALIGN - 1)) * (1.0 / RUN_ALIGN)) * RUN_ALIGN
        cnt_b = jnp.broadcast_to(cnt_pad, (N_EXPERTS, LANES))
        start = jnp.dot(lower_ref[...], cnt_b.astype(BF16), preferred_element_type=F32)
        pos = start[:, 0:1] + rank
        pos1 = jnp.sum(jnp.where(sel1, pos, 0.0), axis=0, keepdims=True)
        pos2 = jnp.sum(jnp.where(sel2, pos, 0.0), axis=0, keepdims=True)
        rows = jnp.concatenate([pos1, pos2, g1, g2, jnp.zeros((SUBLANES - 4, sub), F32)], axis=0)
        route_ref[t] = rows
        routet_ref[rs, :] = jnp.concatenate([rows, jnp.zeros((LANES - SUBLANES, sub), F32)], axis=0).T
        cnt_ref[t] = cnt_b


def _out_proj(attn, conv, h, wo_bf16, g, w_router_t, tm, sub):
    n = h.shape[0]
    w_router = jnp.concatenate(_split_bf16(w_router_t), axis=0)
    idx = jnp.arange(sub)
    upper = (idx[:, None] < idx[None, :]).astype(BF16)
    eidx = jnp.arange(N_EXPERTS)
    lower = (eidx[None, :] < eidx[:, None]).astype(BF16)
    const = lambda shape: pl.BlockSpec(shape, lambda i: (0,) * len(shape))
    per_step = tm // sub
    return pl.pallas_call(
        functools.partial(_out_proj_kernel, tm=tm, sub=sub),
        grid=(n // tm,),
        in_specs=[pl.BlockSpec((N_SLABS, tm, LANES), lambda i: (0, i, 0)),
                  pl.BlockSpec((tm, CONV_DIM), lambda i: (i, 0)),
                  pl.BlockSpec((tm, D_MODEL), lambda i: (i, 0)),
                  const((D_MODEL, D_MODEL)), const((1, D_MODEL)),
                  const((2 * ROUTER_ROWS, D_MODEL)),
                  const((sub, sub)), const((N_EXPERTS, N_EXPERTS))],
        out_specs=[pl.BlockSpec((tm, D_MODEL), lambda i: (i, 0)),
                   pl.BlockSpec((tm, D_MODEL), lambda i: (i, 0)),
                   pl.BlockSpec((per_step, SUBLANES, sub), lambda i: (i, 0, 0)),
                   pl.BlockSpec((tm, LANES), lambda i: (i, 0)),
                   pl.BlockSpec((per_step, N_EXPERTS, LANES), lambda i: (i, 0, 0))],
        out_shape=[jax.ShapeDtypeStruct((n, D_MODEL), F32),
                   jax.ShapeDtypeStruct((n, D_MODEL), BF16),
                   jax.ShapeDtypeStruct((n // sub, SUBLANES, sub), F32),
                   jax.ShapeDtypeStruct((n, LANES), F32),
                   jax.ShapeDtypeStruct((n // sub, N_EXPERTS, LANES), F32)],
        compiler_params=_params("arbitrary"),
        name="out_proj",
    )(attn, conv, h, wo_bf16, g, w_router, upper, lower)


def _moe_plan(cnt_tiles, max_tiles):
    c = cnt_tiles.astype(jnp.int32)
    rows_e = jnp.sum(c, axis=0)
    tiles_e = (rows_e + MOE_TILE - 1) // MOE_TILE
    cum = jnp.cumsum(tiles_e)
    total = cum[-1]
    first_tile = cum - tiles_e

    run_start = jnp.cumsum(c, axis=0) - c
    local_start = jnp.cumsum(c, axis=1) - c
    local_end = local_start + c
    row = jnp.arange(TABLE_LANES, dtype=jnp.int32) * RUN_ALIGN
    inside = (local_start[:, None, :] <= row[None, :, None]) & (row[None, :, None] < local_end[:, None, :])
    base = first_tile[None] * MOE_TILE + run_start - local_start
    n_chunks = jnp.sum(c, axis=1) // RUN_ALIGN
    table = jnp.sum(jnp.where(inside, base[:, None, :] + row[None, :, None], 0), axis=-1)
    live = jnp.arange(TABLE_LANES)[None] < n_chunks[:, None]
    table = jnp.where(live, table, max_tiles * MOE_TILE + row[None])
    i32 = lambda a: a.astype(jnp.int32)
    return (i32(table), i32(n_chunks), i32(first_tile), i32(tiles_e), i32(rows_e),
            i32(total).reshape(1))


def _dispatch_kernel(tbl_ref, xn_ref, route_ref, *rest, tm, rl, chained):
    xs_hbm, lsort, sem = rest[1:] if chained else rest
    i = pl.program_id(0)
    slot = i % 2
    rows = route_ref[0]
    pos1 = rows[0:1].astype(jnp.int32)
    pos2 = rows[1:2].astype(jnp.int32)
    p = lax.broadcasted_iota(jnp.int32, (rl, tm), 0)
    one_hot = jnp.where((p == pos1) | (p == pos2), 1.0, 0.0).astype(BF16)
    lsort[slot] = jnp.dot(one_hot, xn_ref[...], preferred_element_type=F32).astype(BF16)

    def tile_copies_done(s):
        pltpu.make_async_copy(lsort.at[s], xs_hbm.at[pl.ds(0, rl), :], sem.at[s]).wait()

    @pl.when(i > 0)
    def _():
        tile_copies_done(1 - slot)

    for q in range(rl // RUN_ALIGN):
        row = pl.multiple_of(tbl_ref[i * TABLE_LANES + q], RUN_ALIGN)
        pltpu.make_async_copy(lsort.at[slot, pl.ds(q * RUN_ALIGN, RUN_ALIGN), :],
                              xs_hbm.at[pl.ds(row, RUN_ALIGN), :], sem.at[slot]).start()

    @pl.when(i == pl.num_programs(0) - 1)
    def _():
        tile_copies_done(slot)


def _dispatch(table, xn, route, tm, total_rows, xs_prev=None):
    nt = xn.shape[0] // tm
    rl = _local_rows(tm)
    chained = xs_prev is not None
    in_specs = [pl.BlockSpec((tm, D_MODEL), lambda i, t: (i, 0)),
                pl.BlockSpec((1, SUBLANES, tm), lambda i, t: (i, 0, 0))]
    args = [table.reshape(-1), xn, route]
    if chained:
        in_specs.append(pl.BlockSpec(memory_space=pl.ANY))
        args.append(xs_prev)
    return pl.pallas_call(
        functools.partial(_dispatch_kernel, tm=tm, rl=rl, chained=chained),
        grid_spec=pltpu.PrefetchScalarGridSpec(
            num_scalar_prefetch=1, grid=(nt,), in_specs=in_specs,
            out_specs=pl.BlockSpec(memory_space=pl.ANY),
            scratch_shapes=[pltpu.VMEM((2, rl, D_MODEL), BF16), pltpu.SemaphoreType.DMA((2,))]),
        out_shape=jax.ShapeDtypeStruct((total_rows, D_MODEL), BF16),
        input_output_aliases={3: 0} if chained else {},
        compiler_params=_params("arbitrary"),
        name="moe_dispatch",
    )(*args)


def _expert_kernel(first_ref, tiles_ref, rows_ref, tot_ref, xs_hbm, wg_ref, wu_ref, wd_ref, eo_hbm,
                   xbuf, obuf, wg_b, wu_b, wd_b, sem_in, sem_out):
    e = pl.program_id(0)
    first_tile = first_ref[e]
    n_tiles = tiles_ref[e]
    total = tot_ref[0]

    def in_copy(g):
        return pltpu.make_async_copy(
            xs_hbm.at[pl.ds(pl.multiple_of(g * MOE_TILE, MOE_TILE), MOE_TILE), :],
            xbuf.at[g % EXPERT_BUFFERS], sem_in.at[g % EXPERT_BUFFERS])

    def out_copy(g):
        return pltpu.make_async_copy(
            obuf.at[g % EXPERT_BUFFERS],
            eo_hbm.at[pl.ds(pl.multiple_of(g * MOE_TILE, MOE_TILE), MOE_TILE), :],
            sem_out.at[g % EXPERT_BUFFERS])

    @pl.when(e == 0)
    def _():
        for g0 in range(EXPERT_BUFFERS - 1):
            @pl.when(g0 < total)
            def _():
                in_copy(g0).start()

    @pl.when(n_tiles > 0)
    def _():
        wg_b[...] = wg_ref[0, 0].astype(BF16)
        wu_b[...] = wu_ref[0, 0].astype(BF16)
        wd_b[...] = wd_ref[0, 0].astype(BF16)

    def tile(j, carry):
        g = first_tile + j

        @pl.when(g + EXPERT_BUFFERS - 1 < total)
        def _():
            in_copy(g + EXPERT_BUFFERS - 1).start()

        in_copy(g).wait()

        @pl.when(g >= EXPERT_BUFFERS)
        def _():
            out_copy(g - EXPERT_BUFFERS).wait()

        row = lax.broadcasted_iota(jnp.int32, (MOE_TILE, 1), 0)
        x = xbuf[g % EXPERT_BUFFERS]
        x = jnp.where(row < rows_ref[e] - j * MOE_TILE, x, jnp.zeros_like(x))
        gate = jnp.dot(x, wg_b[...], preferred_element_type=F32)
        up = jnp.dot(x, wu_b[...], preferred_element_type=F32)
        hdn = gate * jax.nn.sigmoid(gate) * up
        obuf[g % EXPERT_BUFFERS] = jnp.dot(hdn.astype(BF16), wd_b[...],
                                           preferred_element_type=F32).astype(BF16)
        out_copy(g).start()
        return carry

    lax.fori_loop(0, n_tiles, tile, 0)

    @pl.when(e == pl.num_programs(0) - 1)
    def _():
        for back in range(EXPERT_BUFFERS, 0, -1):
            @pl.when(total >= back)
            def _():
                out_copy(total - back).wait()


def _experts(first_tile, tiles_e, rows_e, total, xs, w_gate, w_up, w_down, layer):
    weight = lambda e, f, t, r, n: (layer, e, 0, 0)
    return pl.pallas_call(
        _expert_kernel,
        grid_spec=pltpu.PrefetchScalarGridSpec(
            num_scalar_prefetch=4, grid=(N_EXPERTS,),
            in_specs=[pl.BlockSpec(memory_space=pl.ANY),
                      pl.BlockSpec((1, 1, D_MODEL, D_EXPERT), weight),
                      pl.BlockSpec((1, 1, D_MODEL, D_EXPERT), weight),
                      pl.BlockSpec((1, 1, D_EXPERT, D_MODEL), weight)],
            out_specs=pl.BlockSpec(memory_space=pl.ANY),
            scratch_shapes=[pltpu.VMEM((EXPERT_BUFFERS, MOE_TILE, D_MODEL), BF16),
                            pltpu.VMEM((EXPERT_BUFFERS, MOE_TILE, D_MODEL), BF16),
                            pltpu.VMEM((D_MODEL, D_EXPERT), BF16),
                            pltpu.VMEM((D_MODEL, D_EXPERT), BF16),
                            pltpu.VMEM((D_EXPERT, D_MODEL), BF16),
                            pltpu.SemaphoreType.DMA((EXPERT_BUFFERS,)),
                            pltpu.SemaphoreType.DMA((EXPERT_BUFFERS,))]),
        out_shape=jax.ShapeDtypeStruct(xs.shape, BF16),
        compiler_params=_params("arbitrary"),
        name="moe_experts",
    )(first_tile, tiles_e, rows_e, total, xs, w_gate, w_up, w_down)


def _combine_kernel(tbl_ref, nch_ref, h1_ref, routet_ref, gf_ref, eo_hbm, o_ref, leo, sem,
                    *, tm, rl, final_norm):
    i = pl.program_id(0)
    n = pl.num_programs(0)
    slot = i % 2

    def gather(tile, s):
        for q in range(rl // RUN_ALIGN):
            row = jnp.where(q < nch_ref[tile], tbl_ref[tile * TABLE_LANES + q], q * RUN_ALIGN)
            pltpu.make_async_copy(eo_hbm.at[pl.ds(pl.multiple_of(row, RUN_ALIGN), RUN_ALIGN), :],
                                  leo.at[s, pl.ds(q * RUN_ALIGN, RUN_ALIGN), :], sem.at[s]).start()

    @pl.when(i == 0)
    def _():
        gather(0, 0)

    @pl.when(i + 1 < n)
    def _():
        gather(i + 1, 1 - slot)

    pltpu.make_async_copy(eo_hbm.at[pl.ds(0, rl), :], leo.at[slot], sem.at[slot]).wait()

    rt = routet_ref[...]
    pos1 = rt[:, 0:1].astype(jnp.int32)
    pos2 = rt[:, 1:2].astype(jnp.int32)
    lane = lax.broadcasted_iota(jnp.int32, (tm, rl), 1)
    weights = jnp.where(lane == pos1, rt[:, 2:3], 0.0) + jnp.where(lane == pos2, rt[:, 3:4], 0.0)
    y = jnp.dot(weights.astype(BF16), leo[slot], preferred_element_type=F32)
    out = h1_ref[...] + y
    if final_norm:
        out = _rms(out, gf_ref[...])
    o_ref[...] = out


def _combine(table, n_chunks, h1, routet, g_final, eo, tm, final_norm):
    n = h1.shape[0]
    rl = _local_rows(tm)
    return pl.pallas_call(
        functools.partial(_combine_kernel, tm=tm, rl=rl, final_norm=final_norm),
        grid_spec=pltpu.PrefetchScalarGridSpec(
            num_scalar_prefetch=2, grid=(n // tm,),
            in_specs=[pl.BlockSpec((tm, D_MODEL), lambda i, t, c: (i, 0)),
                      pl.BlockSpec((tm, LANES), lambda i, t, c: (i, 0)),
                      pl.BlockSpec((1, D_MODEL), lambda i, t, c: (0, 0)),
                      pl.BlockSpec(memory_space=pl.ANY)],
            out_specs=pl.BlockSpec((tm, D_MODEL), lambda i, t, c: (i, 0)),
            scratch_shapes=[pltpu.VMEM((2, rl, D_MODEL), BF16), pltpu.SemaphoreType.DMA((2,))]),
        out_shape=jax.ShapeDtypeStruct((n, D_MODEL), F32),
        compiler_params=_params("arbitrary"),
        name="moe_combine",
    )(table.reshape(-1), n_chunks, h1, routet, g_final, eo)


def _in_proj_sample_kernel(x_ref, g_ref, w_ref, o_ref):
    xn = _rms(x_ref[...], g_ref[...])
    o_ref[...] = jnp.dot(xn.astype(BF16), w_ref[...], preferred_element_type=F32)


def _in_proj_sample(x, g, w_bf16):
    n = x.shape[0]
    return pl.pallas_call(
        _in_proj_sample_kernel,
        out_shape=jax.ShapeDtypeStruct((n, PROJ_WIDTH), F32),
        compiler_params=pltpu.CompilerParams(vmem_limit_bytes=VMEM_LIMIT),
        name="in_proj_sample",
    )(x, g, w_bf16)


def _sample_mix_kernel(q_ref, kn_ref, vn_ref, kt_ref, vt_ref,
                       bc_ref, bn_ref, gb_ref, gc_ref, hc_ref, st_ref, cw_ref,
                       o_ref, conv_ref, st_out_ref, *, t_new):
    contract_last = (((1,), (1,)), ((), ()))
    no_rows = jnp.zeros((LANES - SUBLANES, HEAD_DIM), F32)
    for h in range(N_HEADS):
        q = (q_ref[0, h] * SCALE).astype(BF16)
        s_c = jnp.dot(q, kt_ref[0, 0, h].astype(BF16), preferred_element_type=F32) + bc_ref[h]
        kn = jnp.concatenate([kn_ref[0, h], no_rows], axis=0).astype(BF16)
        vn = jnp.concatenate([vn_ref[0, h], no_rows], axis=0).astype(BF16)
        s_n = lax.dot_general(q, kn, contract_last, preferred_element_type=F32) + bn_ref[h]
        m = jnp.maximum(jnp.max(s_c, axis=-1, keepdims=True), jnp.max(s_n, axis=-1, keepdims=True))
        p_c = jnp.exp(s_c - m)
        p_n = jnp.exp(s_n - m)
        den = jnp.sum(p_c, axis=-1, keepdims=True) + jnp.sum(p_n, axis=-1, keepdims=True)
        num = (lax.dot_general(p_c.astype(BF16), vt_ref[0, 0, h].astype(BF16), contract_last,
                               preferred_element_type=F32)
               + jnp.dot(p_n.astype(BF16), vn, preferred_element_type=F32))
        out = num / den
        lse = m + jnp.log(den)
        ls = [lse[i * t_new:(i + 1) * t_new] for i in range(len(DILATIONS))]
        os_ = [out[i * t_new:(i + 1) * t_new] for i in range(len(DILATIONS))]
        mm = jnp.maximum(jnp.maximum(ls[0], ls[1]), ls[2])
        w = [jnp.exp(l - mm) for l in ls]
        o_ref[0, h] = (w[0] * os_[0] + w[1] * os_[1] + w[2] * os_[2]) / (w[0] + w[1] + w[2])
    cw = cw_ref[...]
    u = gc_ref[0] * hc_ref[0]
    st = st_ref[0]
    rows = [st[j:j + 1] for j in range(CONV_K - 1)] + [u[t:t + 1] for t in range(t_new)]
    gb = gb_ref[0]
    conv_ref[0] = jnp.concatenate(
        [gb[t:t + 1] * sum(cw[j:j + 1] * rows[t + j] for j in range(CONV_K)) for t in range(t_new)],
        axis=0)
    st_out_ref[0] = jnp.concatenate(rows[t_new:], axis=0)


SAMPLE_ROWS = 16


def _sample_bias_tables(step_bias, t_new, w_buf):
    cache_rows = []
    for b, d in zip(step_bias, DILATIONS):
        back = b[1:][::-1].T
        if d >= t_new:
            residue = np.eye(d, dtype=bool)[:t_new]
            span = jnp.where(residue[None, :, None, :], back[:, None, :, None], NEG)
            span = span.reshape(N_HEADS, t_new, SEG_KEYS * d)
        else:
            assert d == 1, "dilations between 1 and the number of new tokens are not supported"
            span = jnp.stack([jnp.pad(back[:, :SEG_KEYS - t], ((0, 0), (t, 0)), constant_values=NEG)
                              for t in range(t_new)], axis=1)
        cache_rows.append(jnp.pad(span, ((0, 0), (0, 0), (w_buf - SEG_KEYS * d, 0)),
                                  constant_values=NEG))
    n_rows = len(DILATIONS) * t_new
    bias_c = jnp.pad(jnp.concatenate(cache_rows, axis=1), ((0, 0), (0, SAMPLE_ROWS - n_rows), (0, 0)))

    rows, cols, which, steps = [], [], [], []
    for di, d in enumerate(DILATIONS):
        for t in range(t_new):
            for t2 in range(t + 1):
                if (t - t2) % d == 0:
                    rows.append(di * t_new + t)
                    cols.append(t2)
                    which.append(di)
                    steps.append((t - t2) // d)
    vals = jnp.stack(step_bias)[np.array(which), np.array(steps)]
    bias_n = jnp.full((N_HEADS, n_rows, LANES), NEG, F32).at[:, np.array(rows), np.array(cols)].set(vals.T)
    bias_n = jnp.pad(bias_n, ((0, 0), (0, SAMPLE_ROWS - n_rows), (0, 0)))
    return bias_c, bias_n


def _sample_mix(q, kn, vn, cache_k, cache_v, layer, bias_c, bias_n, gb, gc, hc, state, conv_w):
    db, t_new = q.shape[0], q.shape[1]
    w_buf = cache_k.shape[2]
    kt = cache_k.transpose(0, 1, 3, 4, 2)
    vt = cache_v.transpose(0, 1, 3, 4, 2)
    n_rows = len(DILATIONS) * t_new
    q_rows = jnp.pad(jnp.tile(q.transpose(0, 2, 1, 3), (1, 1, len(DILATIONS), 1)),
                     ((0, 0), (0, 0), (0, SAMPLE_ROWS - n_rows), (0, 0)))
    kn_pad = jnp.pad(kn.transpose(0, 2, 1, 3), ((0, 0), (0, 0), (0, SUBLANES - t_new), (0, 0)))
    vn_pad = jnp.pad(vn.transpose(0, 2, 1, 3), ((0, 0), (0, 0), (0, SUBLANES - t_new), (0, 0)))
    row = pl.BlockSpec((1, t_new, CONV_DIM), lambda b: (b, 0, 0))
    new_spec = pl.BlockSpec((1, N_HEADS, SUBLANES, HEAD_DIM), lambda b: (b, 0, 0, 0))
    win_spec = pl.BlockSpec((1, 1, N_HEADS, HEAD_DIM, w_buf), lambda b: (layer, b, 0, 0, 0))
    attn, conv, st = pl.pallas_call(
        functools.partial(_sample_mix_kernel, t_new=t_new),
        grid=(db,),
        in_specs=[pl.BlockSpec((1, N_HEADS, SAMPLE_ROWS, HEAD_DIM), lambda b: (b, 0, 0, 0)),
                  new_spec, new_spec, win_spec, win_spec,
                  pl.BlockSpec(bias_c.shape, lambda b: (0, 0, 0)),
                  pl.BlockSpec(bias_n.shape, lambda b: (0, 0, 0)),
                  row, row, row,
                  pl.BlockSpec((1, CONV_K - 1, CONV_DIM), lambda b: (b, 0, 0)),
                  pl.BlockSpec((CONV_K, CONV_DIM), lambda b: (0, 0))],
        out_specs=[pl.BlockSpec((1, N_HEADS, t_new, HEAD_DIM), lambda b: (b, 0, 0, 0)), row,
                   pl.BlockSpec((1, CONV_K - 1, CONV_DIM), lambda b: (b, 0, 0))],
        out_shape=[jax.ShapeDtypeStruct((db, N_HEADS, t_new, HEAD_DIM), F32),
                   jax.ShapeDtypeStruct((db, t_new, CONV_DIM), F32),
                   jax.ShapeDtypeStruct((db, CONV_K - 1, CONV_DIM), F32)],
        compiler_params=_params("arbitrary"),
        name="sample_mix",
    )(q_rows, kn_pad, vn_pad, kt, vt, bias_c, bias_n, gb, gc, hc, state, conv_w)
    return attn.transpose(0, 2, 1, 3), conv, st


def _to_slabs(x):
    n = x.shape[0]
    return x.reshape(n, N_SLABS, LANES).transpose(1, 0, 2)


def _window_from_slabs(x, batch, seq, w_keep):
    win = x.reshape(N_SLABS, batch, seq, LANES)[:, :, seq - w_keep:]
    return win.transpose(1, 2, 0, 3).reshape(batch, w_keep, N_HEADS, HEAD_DIM)


def kernel(x_prompt, x_sample, cache_k, cache_v, state_conv, rel_bias, norm_mix, norm_ffn,
           norm_final, w_in, conv_w, w_out, w_router_group, w_router_expert, w_gate, w_up,
           w_down):
    batch, seq, _ = x_prompt.shape
    db, t_new, _ = x_sample.shape
    depth = w_in.shape[0]
    w_keep = min(MAX_WINDOW, seq)

    hp = x_prompt.reshape(batch * seq, D_MODEL)
    hs = x_sample.reshape(db * t_new, D_MODEL)
    step_bias = _step_bias(rel_bias)
    bias_prompt = _prompt_bias_tables(step_bias)
    bias_c, bias_n = _sample_bias_tables(step_bias, t_new, cache_k.shape[2])
    g_final = norm_final.reshape(1, D_MODEL)

    tm_p = 256
    n_s = db * t_new
    nt_p = batch * seq // tm_p
    pad_per_tile = N_EXPERTS * (RUN_ALIGN - 1)
    max_tiles = -(-(2 * (batch * seq + n_s) + (nt_p + 1) * pad_per_tile) // MOE_TILE) + N_EXPERTS
    buf_rows = max_tiles * MOE_TILE + _local_rows(tm_p)

    pk, pv, pc, sk, sv, sc = [], [], [], [], [], []
    for l in range(depth):
        w_in_b = w_in[l].astype(BF16)
        w_out_b = w_out[l].astype(BF16)
        w_router_t = jnp.pad(jnp.concatenate([w_router_expert[l], w_router_group[l]], axis=1).T,
                             ((0, ROUTER_ROWS - N_EXPERTS - N_GROUPS), (0, 0)))
        g_mix = norm_mix[l].reshape(1, D_MODEL)
        g_ffn = norm_ffn[l].reshape(1, D_MODEL)
        last = l == depth - 1

        q, k, v, conv, u_last = _in_proj(hp, g_mix, w_in_b, conv_w[l], seq)
        attn = _attention_prompt(q, k, v, bias_prompt, batch, seq)
        h1, xn, route, routet, cnt = _out_proj(attn, conv, hp, w_out_b, g_ffn, w_router_t,
                                               tm=4 * tm_p, sub=tm_p)
        pk.append(_window_from_slabs(k, batch, seq, w_keep))
        pv.append(_window_from_slabs(v, batch, seq, w_keep))
        pc.append(u_last)

        proj = _in_proj_sample(hs, g_mix, w_in_b)
        qs = proj[:, :ATTN_WIDTH].reshape(db, t_new, N_HEADS, HEAD_DIM)
        ks = proj[:, ATTN_WIDTH:2 * ATTN_WIDTH].reshape(db, t_new, N_HEADS, HEAD_DIM)
        vs = proj[:, 2 * ATTN_WIDTH:3 * ATTN_WIDTH].reshape(db, t_new, N_HEADS, HEAD_DIM)
        c0 = 3 * ATTN_WIDTH
        gb = proj[:, c0:c0 + CONV_DIM].reshape(db, t_new, CONV_DIM)
        gc = proj[:, c0 + CONV_DIM:c0 + 2 * CONV_DIM].reshape(db, t_new, CONV_DIM)
        hc = proj[:, c0 + 2 * CONV_DIM:].reshape(db, t_new, CONV_DIM)
        attn_s, conv_s, state_s = _sample_mix(qs, ks, vs, cache_k, cache_v, l, bias_c, bias_n,
                                              gb, gc, hc, state_conv[l], conv_w[l])
        h1s, xns, route_s, routet_s, cnt_s = _out_proj(
            _to_slabs(attn_s.reshape(n_s, ATTN_WIDTH)), conv_s.reshape(n_s, CONV_DIM), hs,
            w_out_b, g_ffn, w_router_t, tm=n_s, sub=n_s)

        table, n_chunks, first_tile, tiles_e, rows_e, total = _moe_plan(
            jnp.concatenate([cnt[:, :, 0], cnt_s[:, :, 0]], axis=0), max_tiles)
        xs = _dispatch(table[:nt_p], xn, route, tm_p, buf_rows)
        xs = _dispatch(table[nt_p:], xns, route_s, n_s, buf_rows, xs_prev=xs)
        eo = _experts(first_tile, tiles_e, rows_e, total, xs, w_gate, w_up, w_down, l)
        hp = _combine(table[:nt_p], n_chunks[:nt_p], h1, routet, g_final, eo, tm_p, last)
        hs = _combine(table[nt_p:], n_chunks[nt_p:], h1s, routet_s, g_final, eo, n_s, last)
        sk.append(ks)
        sv.append(vs)
        sc.append(state_s)

    return (hp.reshape(batch, seq, D_MODEL), hs.reshape(db, t_new, D_MODEL),
            jnp.stack(pk), jnp.stack(pv), jnp.stack(pc),
            jnp.stack(sk), jnp.stack(sv), jnp.stack(sc))
```

```python
import functools
import math

import jax
import jax.numpy as jnp
import numpy as np
from jax import lax
from jax.experimental import pallas as pl
from jax.experimental.pallas import tpu as pltpu

F32 = jnp.float32
BF16 = jnp.bfloat16

D_MODEL = 1024
N_HEADS = 8
HEAD_DIM = 64
ATTN_WIDTH = N_HEADS * HEAD_DIM
CONV_DIM = D_MODEL - ATTN_WIDTH
CONV_K = 3
PROJ_WIDTH = 3 * ATTN_WIDTH + 3 * CONV_DIM
DILATIONS = (1, 4, 16)
SEG_KEYS = 128
Q_BLOCK = 128
MAX_WINDOW = 2048
N_BUCKETS = 32
MAX_DISTANCE = 2048
N_GROUPS = 4
EXPERTS_PER_GROUP = 8
N_EXPERTS = N_GROUPS * EXPERTS_PER_GROUP
D_EXPERT = D_MODEL // 4
EPS = 1e-6
NEG = -1e30
SCALE = 1.0 / math.sqrt(HEAD_DIM)

LANES = 128
SUBLANES = 8
N_SLABS = ATTN_WIDTH // LANES
VMEM_LIMIT = 48 * 1024 * 1024
REGROUP = 4
UNITS_PER_TRIP = 8


def _params(*sem):
    return pltpu.CompilerParams(dimension_semantics=sem, vmem_limit_bytes=VMEM_LIMIT)


def _rms(x, g):
    return x * lax.rsqrt(jnp.mean(x * x, axis=-1, keepdims=True) + EPS) * g


def _rel_bucket(dist):
    max_exact = N_BUCKETS // 2
    d_f = jnp.maximum(dist, 1).astype(F32)
    large = max_exact + (jnp.log(d_f / max_exact) / math.log(MAX_DISTANCE / max_exact)
                         * (N_BUCKETS - max_exact)).astype(jnp.int32)
    large = jnp.minimum(large, N_BUCKETS - 1)
    return jnp.where(dist < max_exact, dist, large)


def _in_proj_kernel(x_ref, g_ref, w_ref, cw_ref, q_ref, k_ref, v_ref, conv_ref, ulast_ref,
                    ext_ref, *, tm, tiles_per_seq):
    i = pl.program_id(0)
    xn = _rms(x_ref[...], g_ref[...])
    proj = jnp.dot(xn.astype(BF16), w_ref[...], preferred_element_type=F32)
    for s in range(N_SLABS):
        q_ref[s] = proj[:, s * LANES:(s + 1) * LANES]
        k_ref[s] = proj[:, ATTN_WIDTH + s * LANES:ATTN_WIDTH + (s + 1) * LANES]
        v_ref[s] = proj[:, 2 * ATTN_WIDTH + s * LANES:2 * ATTN_WIDTH + (s + 1) * LANES]
    c0 = 3 * ATTN_WIDTH
    gb = proj[:, c0:c0 + CONV_DIM]
    u = proj[:, c0 + CONV_DIM:c0 + 2 * CONV_DIM] * proj[:, c0 + 2 * CONV_DIM:c0 + 3 * CONV_DIM]

    @pl.when(i % tiles_per_seq == 0)
    def _():
        ext_ref[0:SUBLANES, :] = jnp.zeros((SUBLANES, CONV_DIM), F32)

    ext_ref[SUBLANES:SUBLANES + tm, :] = u
    u1 = ext_ref[SUBLANES - 1:SUBLANES - 1 + tm, :]
    u2 = ext_ref[SUBLANES - 2:SUBLANES - 2 + tm, :]
    cw = cw_ref[...]
    conv_ref[...] = gb * (cw[0:1] * u2 + cw[1:2] * u1 + cw[2:3] * u)
    ulast_ref[0] = ext_ref[tm + SUBLANES - 2:tm + SUBLANES, :]
    ext_ref[0:SUBLANES, :] = ext_ref[tm:tm + SUBLANES, :]


def _in_proj(x, g, w_bf16, conv_w, seq_len, tm=1024):
    n = x.shape[0]
    n_seq = n // seq_len
    slab = jax.ShapeDtypeStruct((N_SLABS, n, LANES), F32)
    slab_spec = pl.BlockSpec((N_SLABS, tm, LANES), lambda i: (0, i, 0))
    tiles_per_seq = seq_len // tm
    return pl.pallas_call(
        functools.partial(_in_proj_kernel, tm=tm, tiles_per_seq=tiles_per_seq),
        grid=(n // tm,),
        in_specs=[pl.BlockSpec((tm, D_MODEL), lambda i: (i, 0)),
                  pl.BlockSpec((1, D_MODEL), lambda i: (0, 0)),
                  pl.BlockSpec((D_MODEL, PROJ_WIDTH), lambda i: (0, 0)),
                  pl.BlockSpec((CONV_K, CONV_DIM), lambda i: (0, 0))],
        out_specs=[slab_spec, slab_spec, slab_spec,
                   pl.BlockSpec((tm, CONV_DIM), lambda i: (i, 0)),
                   pl.BlockSpec((1, CONV_K - 1, CONV_DIM), lambda i: (i // tiles_per_seq, 0, 0))],
        out_shape=[slab, slab, slab,
                   jax.ShapeDtypeStruct((n, CONV_DIM), F32),
                   jax.ShapeDtypeStruct((n_seq, CONV_K - 1, CONV_DIM), F32)],
        scratch_shapes=[pltpu.VMEM((tm + SUBLANES, CONV_DIM), F32)],
        compiler_params=_params("arbitrary"),
        name="in_proj_prompt",
    )(x, g, w_bf16, conv_w)


def _attn_kernel(q_ref, k_ref, v_ref, bias_ref, o_ref, q4, k4, v4, ktail, vtail, oacc, lacc, *, sb):
    j = pl.program_id(2)
    slot = j % 2
    prev = 1 - slot
    seg = sb // REGROUP

    @pl.when(j == 0)
    def _():
        k4[1] = jnp.zeros((sb, LANES), F32)
        v4[1] = jnp.zeros((sb, LANES), F32)
        ktail[...] = jnp.zeros((Q_BLOCK, LANES), F32)
        vtail[...] = jnp.zeros((Q_BLOCK, LANES), F32)

    for r in range(REGROUP):
        q4[r * seg:(r + 1) * seg, :] = q_ref[pl.ds(r, seg, stride=REGROUP), :] * SCALE
        k4[slot, r * seg:(r + 1) * seg, :] = k_ref[pl.ds(r, seg, stride=REGROUP), :]
        v4[slot, r * seg:(r + 1) * seg, :] = v_ref[pl.ds(r, seg, stride=REGROUP), :]

    lane = lax.broadcasted_iota(jnp.int32, (Q_BLOCK, LANES), 1)
    low = lane < HEAD_DIM
    ones = jnp.ones((2 * Q_BLOCK, LANES), BF16)

    def block(start):
        return pl.ds(pl.multiple_of(start, Q_BLOCK), Q_BLOCK)

    for di, d in enumerate(DILATIONS):
        nq = sb // (Q_BLOCK * d)

        def unit(u, di=di, d=d, nq=nq):
            r = u // nq
            n = u % nq
            qstart = r + d * Q_BLOCK * n
            if d == 1:
                here = block(n * Q_BLOCK)
                before = block(jnp.maximum(n - 1, 0) * Q_BLOCK)
                q = q_ref[here, :] * SCALE
                k_new, v_new = k_ref[here, :], v_ref[here, :]
                k_old = jnp.where(n == 0, ktail[...], k_ref[before, :])
                v_old = jnp.where(n == 0, vtail[...], v_ref[before, :])
            elif d == REGROUP:
                here = block(r * seg + n * Q_BLOCK)
                before = block(r * seg + jnp.maximum(n - 1, 0) * Q_BLOCK)
                last = block(r * seg + seg - Q_BLOCK)
                q = q4[here, :]
                k_new, v_new = k4[slot, here, :], v4[slot, here, :]
                k_old = jnp.where(n == 0, k4[prev, last, :], k4[slot, before, :])
                v_old = jnp.where(n == 0, v4[prev, last, :], v4[slot, before, :])
            else:
                rows = pl.ds((u % REGROUP) * seg + u // REGROUP, Q_BLOCK, stride=REGROUP)
                q = q4[rows, :]
                k_new, v_new = k4[slot, rows, :], v4[slot, rows, :]
                k_old, v_old = k4[prev, rows, :], v4[prev, rows, :]
            kk = jnp.concatenate([k_old, k_new], axis=0).astype(BF16)
            vv = jnp.concatenate([v_old, v_new], axis=0).astype(BF16)
            qm = jnp.concatenate([jnp.where(low, q, 0.0), jnp.where(low, 0.0, q)],
                                 axis=0).astype(BF16)
            s = lax.dot_general(qm, kk, (((1,), (1,)), ((), ())),
                                preferred_element_type=F32)
            first = jnp.logical_and(j == 0, n == 0).astype(jnp.int32)
            s = s + bias_ref[di, first]
            m = jnp.max(s, axis=-1, keepdims=True)
            p = jnp.exp(s - m).astype(BF16)
            pv = jnp.dot(p, jnp.concatenate([vv, ones], axis=1),
                         preferred_element_type=F32)
            o_sel = jnp.where(low, pv[0:Q_BLOCK, 0:LANES], pv[Q_BLOCK:, 0:LANES])
            l_sel = jnp.where(low, pv[0:Q_BLOCK, LANES:], pv[Q_BLOCK:, LANES:])
            m_sel = jnp.where(low, jnp.broadcast_to(m[0:Q_BLOCK], (Q_BLOCK, LANES)),
                              jnp.broadcast_to(m[Q_BLOCK:], (Q_BLOCK, LANES)))
            oacc[di, pl.ds(qstart, Q_BLOCK, stride=d), :] = o_sel / l_sel
            lacc[di, pl.ds(qstart, Q_BLOCK, stride=d), :] = m_sel + jnp.log(l_sel)

        def group(g, carry, unit=unit):
            for uu in range(UNITS_PER_TRIP):
                unit(g * UNITS_PER_TRIP + uu)
            return carry

        lax.fori_loop(0, sb // Q_BLOCK // UNITS_PER_TRIP, group, 0)

    def merge(c, carry):
        rows = pl.ds(pl.multiple_of(c * 256, 256), 256)
        l0, l1, l2 = lacc[0, rows, :], lacc[1, rows, :], lacc[2, rows, :]
        mm = jnp.maximum(jnp.maximum(l0, l1), l2)
        w0, w1, w2 = jnp.exp(l0 - mm), jnp.exp(l1 - mm), jnp.exp(l2 - mm)
        num = w0 * oacc[0, rows, :] + w1 * oacc[1, rows, :] + w2 * oacc[2, rows, :]
        o_ref[rows, :] = num / (w0 + w1 + w2)
        return carry

    lax.fori_loop(0, sb // 256, merge, 0)
    ktail[...] = k_ref[sb - Q_BLOCK:sb, :]
    vtail[...] = v_ref[sb - Q_BLOCK:sb, :]


def _step_bias(rel_bias):
    steps = jnp.arange(SEG_KEYS + 1)
    return [rel_bias[_rel_bucket(steps * d)].astype(F32) for d in DILATIONS]


def _prompt_bias_tables(step_bias):
    width = 3 * Q_BLOCK
    k_loc = jnp.arange(2 * Q_BLOCK)
    tables = []
    for b in step_bias:
        g = jnp.concatenate([b[::-1].T, jnp.full((N_HEADS, width - SEG_KEYS), NEG, F32)], axis=1)
        flat = jnp.broadcast_to(g[:, None, :], (N_HEADS, Q_BLOCK, width + 1)).reshape(N_HEADS, -1)
        x = flat[:, :Q_BLOCK * width].reshape(N_HEADS, Q_BLOCK, width)
        t = x[:, :, :2 * Q_BLOCK]
        t_first = jnp.where(k_loc[None, None, :] >= Q_BLOCK, t, NEG)
        tables.append(jnp.stack([t.reshape(N_SLABS, 2 * Q_BLOCK, 2 * Q_BLOCK),
                                 t_first.reshape(N_SLABS, 2 * Q_BLOCK, 2 * Q_BLOCK)]))
    return jnp.stack(tables)


def _attention_prompt(q, k, v, bias_tab, n_seq, seq_len, sb=2048):
    assert DILATIONS == (1, REGROUP, REGROUP * REGROUP)
    n = q.shape[1]
    nsb = seq_len // sb
    blk = pl.BlockSpec((None, sb, LANES), lambda b, s, j: (s, b * nsb + j, 0))
    return pl.pallas_call(
        functools.partial(_attn_kernel, sb=sb),
        grid=(n_seq, N_SLABS, nsb),
        in_specs=[blk, blk, blk,
                  pl.BlockSpec((len(DILATIONS), 2, None, 2 * Q_BLOCK, 2 * Q_BLOCK),
                               lambda b, s, j: (0, 0, s, 0, 0))],
        out_specs=blk,
        out_shape=jax.ShapeDtypeStruct((N_SLABS, n, LANES), F32),
        scratch_shapes=[pltpu.VMEM((sb, LANES), F32),
                        pltpu.VMEM((2, sb, LANES), F32), pltpu.VMEM((2, sb, LANES), F32),
                        pltpu.VMEM((Q_BLOCK, LANES), F32), pltpu.VMEM((Q_BLOCK, LANES), F32),
                        pltpu.VMEM((len(DILATIONS), sb, LANES), F32),
                        pltpu.VMEM((len(DILATIONS), sb, LANES), F32)],
        compiler_params=_params("arbitrary", "arbitrary", "arbitrary"),
        name="attention_prompt",
    )(q, k, v, bias_tab)


ROUTER_ROWS = 48
MOE_TILE = 256
RUN_ALIGN = 2 * SUBLANES
TABLE_LANES = 128
EXPERT_BUFFERS = 4


def _local_rows(tm):
    return -(-(2 * tm + N_EXPERTS * (RUN_ALIGN - 1)) // LANES) * LANES


def _route(lg):
    tm = lg.shape[1]
    gl = lg[N_EXPERTS:N_EXPERTS + N_GROUPS]
    grow = lax.broadcasted_iota(jnp.int32, (N_GROUPS, tm), 0)
    gmax = jnp.max(gl, axis=0, keepdims=True)
    gsum = jnp.sum(jnp.exp(gl - gmax), axis=0, keepdims=True)
    gidx = jnp.min(jnp.where(gl == gmax, grow, N_GROUPS), axis=0, keepdims=True)
    p_grp = 1.0 / gsum
    el = lg[0:N_EXPERTS]
    e = lax.broadcasted_iota(jnp.int32, (N_EXPERTS, tm), 0)
    in_grp = (e // EXPERTS_PER_GROUP) == gidx
    elog = jnp.where(in_grp, el, -jnp.inf)
    emax = jnp.max(elog, axis=0, keepdims=True)
    eexp = jnp.exp(elog - emax)
    eprob = jnp.where(in_grp, eexp / jnp.sum(eexp, axis=0, keepdims=True), -1.0)
    p1 = jnp.max(eprob, axis=0, keepdims=True)
    i1 = jnp.min(jnp.where(eprob == p1, e, N_EXPERTS), axis=0, keepdims=True)
    rest = jnp.where(e == i1, -1.0, eprob)
    p2 = jnp.max(rest, axis=0, keepdims=True)
    i2 = jnp.min(jnp.where(rest == p2, e, N_EXPERTS), axis=0, keepdims=True)
    tot = p1 + p2
    return e == i1, e == i2, p_grp * (p1 / tot), p_grp * (p2 / tot)


def _split_bf16(x):
    hi = x.astype(BF16)
    return hi, (x - hi.astype(F32)).astype(BF16)


def _out_proj_kernel(attn_ref, conv_ref, h_ref, wo_ref, g_ref, wr_ref, upper_ref, lower_ref,
                     h1_ref, xn_ref, route_ref, routet_ref, cnt_ref, *, tm, sub):
    contract_last = (((1,), (1,)), ((), ()))
    mix = jnp.concatenate([attn_ref[s] for s in range(N_SLABS)] + [conv_ref[...]], axis=1)
    h1_ref[...] = h_ref[...] + jnp.dot(mix.astype(BF16), wo_ref[...], preferred_element_type=F32)
    for t in range(tm // sub):
        rs = slice(t * sub, (t + 1) * sub)
        xn = _rms(h1_ref[rs, :], g_ref[...])
        xh, xl = _split_bf16(xn)
        xn_ref[rs, :] = xh
        both = lax.dot_general(wr_ref[...], xh, contract_last, preferred_element_type=F32)
        lg = (both[:ROUTER_ROWS] + both[ROUTER_ROWS:]
              + lax.dot_general(wr_ref[:ROUTER_ROWS, :], xl, contract_last, preferred_element_type=F32))
        sel1, sel2, g1, g2 = _route(lg)
        chosen = jnp.where(sel1 | sel2, 1.0, 0.0)
        rank = jnp.dot(chosen.astype(BF16), upper_ref[...], preferred_element_type=F32)
        cnt = jnp.sum(chosen, axis=1, keepdims=True)
        cnt_pad = jnp.floor((cnt + (RUN_ALIGN - 1)) * (1.0 / RUN_ALIGN)) * RUN_ALIGN
        cnt_b = jnp.broadcast_to(cnt_pad, (N_EXPERTS, LANES))
        start = jnp.dot(lower_ref[...], cnt_b.astype(BF16), preferred_element_type=F32)
        pos = start[:, 0:1] + rank
        pos1 = jnp.sum(jnp.where(sel1, pos, 0.0), axis=0, keepdims=True)
        pos2 = jnp.sum(jnp.where(sel2, pos, 0.0), axis=0, keepdims=True)
        rows = jnp.concatenate([pos1, pos2, g1, g2, jnp.zeros((SUBLANES - 4, sub), F32)], axis=0)
        route_ref[t] = rows
        routet_ref[rs, :] = jnp.concatenate([rows, jnp.zeros((LANES - SUBLANES, sub), F32)], axis=0).T
        cnt_ref[t] = cnt_b


def _out_proj(attn, conv, h, wo_bf16, g, w_router_t, tm, sub):
    n = h.shape[0]
    w_router = jnp.concatenate(_split_bf16(w_router_t), axis=0)
    idx = jnp.arange(sub)
    upper = (idx[:, None] < idx[None, :]).astype(BF16)
    eidx = jnp.arange(N_EXPERTS)
    lower = (eidx[None, :] < eidx[:, None]).astype(BF16)
    const = lambda shape: pl.BlockSpec(shape, lambda i: (0,) * len(shape))
    per_step = tm // sub
    return pl.pallas_call(
        functools.partial(_out_proj_kernel, tm=tm, sub=sub),
        grid=(n // tm,),
        in_specs=[pl.BlockSpec((N_SLABS, tm, LANES), lambda i: (0, i, 0)),
                  pl.BlockSpec((tm, CONV_DIM), lambda i: (i, 0)),
                  pl.BlockSpec((tm, D_MODEL), lambda i: (i, 0)),
                  const((D_MODEL, D_MODEL)), const((1, D_MODEL)),
                  const((2 * ROUTER_ROWS, D_MODEL)),
                  const((sub, sub)), const((N_EXPERTS, N_EXPERTS))],
        out_specs=[pl.BlockSpec((tm, D_MODEL), lambda i: (i, 0)),
                   pl.BlockSpec((tm, D_MODEL), lambda i: (i, 0)),
                   pl.BlockSpec((per_step, SUBLANES, sub), lambda i: (i, 0, 0)),
                   pl.BlockSpec((tm, LANES), lambda i: (i, 0)),
                   pl.BlockSpec((per_step, N_EXPERTS, LANES), lambda i: (i, 0, 0))],
        out_shape=[jax.ShapeDtypeStruct((n, D_MODEL), F32),
                   jax.ShapeDtypeStruct((n, D_MODEL), BF16),
                   jax.ShapeDtypeStruct((n // sub, SUBLANES, sub), F32),
                   jax.ShapeDtypeStruct((n, LANES), F32),
                   jax.ShapeDtypeStruct((n // sub, N_EXPERTS, LANES), F32)],
        compiler_params=_params("arbitrary"),
        name="out_proj",
    )(attn, conv, h, wo_bf16, g, w_router, upper, lower)


def _moe_plan(cnt_tiles, max_tiles):
    c = cnt_tiles.astype(jnp.int32)
    rows_e = jnp.sum(c, axis=0)
    tiles_e = (rows_e + MOE_TILE - 1) // MOE_TILE
    cum = jnp.cumsum(tiles_e)
    total = cum[-1]
    first_tile = cum - tiles_e

    run_start = jnp.cumsum(c, axis=0) - c
    local_start = jnp.cumsum(c, axis=1) - c
    local_end = local_start + c
    row = jnp.arange(TABLE_LANES, dtype=jnp.int32) * RUN_ALIGN
    inside = (local_start[:, None, :] <= row[None, :, None]) & (row[None, :, None] < local_end[:, None, :])
    base = first_tile[None] * MOE_TILE + run_start - local_start
    n_chunks = jnp.sum(c, axis=1) // RUN_ALIGN
    table = jnp.sum(jnp.where(inside, base[:, None, :] + row[None, :, None], 0), axis=-1)
    live = jnp.arange(TABLE_LANES)[None] < n_chunks[:, None]
    table = jnp.where(live, table, max_tiles * MOE_TILE + row[None])
    i32 = lambda a: a.astype(jnp.int32)
    return (i32(table), i32(n_chunks), i32(first_tile), i32(tiles_e), i32(rows_e),
            i32(total).reshape(1))


def _dispatch_kernel(tbl_ref, xn_ref, route_ref, *rest, tm, rl, chained):
    xs_hbm, lsort, sem = rest[1:] if chained else rest
    i = pl.program_id(0)
    slot = i % 2
    rows = route_ref[0]
    pos1 = rows[0:1].astype(jnp.int32)
    pos2 = rows[1:2].astype(jnp.int32)
    p = lax.broadcasted_iota(jnp.int32, (rl, tm), 0)
    one_hot = jnp.where((p == pos1) | (p == pos2), 1.0, 0.0).astype(BF16)
    lsort[slot] = jnp.dot(one_hot, xn_ref[...], preferred_element_type=F32).astype(BF16)

    def tile_copies_done(s):
        pltpu.make_async_copy(lsort.at[s], xs_hbm.at[pl.ds(0, rl), :], sem.at[s]).wait()

    @pl.when(i > 0)
    def _():
        tile_copies_done(1 - slot)

    for q in range(rl // RUN_ALIGN):
        row = pl.multiple_of(tbl_ref[i * TABLE_LANES + q], RUN_ALIGN)
        pltpu.make_async_copy(lsort.at[slot, pl.ds(q * RUN_ALIGN, RUN_ALIGN), :],
                              xs_hbm.at[pl.ds(row, RUN_ALIGN), :], sem.at[slot]).start()

    @pl.when(i == pl.num_programs(0) - 1)
    def _():
        tile_copies_done(slot)


def _dispatch(table, xn, route, tm, total_rows, xs_prev=None):
    nt = xn.shape[0] // tm
    rl = _local_rows(tm)
    chained = xs_prev is not None
    in_specs = [pl.BlockSpec((tm, D_MODEL), lambda i, t: (i, 0)),
                pl.BlockSpec((1, SUBLANES, tm), lambda i, t: (i, 0, 0))]
    args = [table.reshape(-1), xn, route]
    if chained:
        in_specs.append(pl.BlockSpec(memory_space=pl.ANY))
        args.append(xs_prev)
    return pl.pallas_call(
        functools.partial(_dispatch_kernel, tm=tm, rl=rl, chained=chained),
        grid_spec=pltpu.PrefetchScalarGridSpec(
            num_scalar_prefetch=1, grid=(nt,), in_specs=in_specs,
            out_specs=pl.BlockSpec(memory_space=pl.ANY),
            scratch_shapes=[pltpu.VMEM((2, rl, D_MODEL), BF16), pltpu.SemaphoreType.DMA((2,))]),
        out_shape=jax.ShapeDtypeStruct((total_rows, D_MODEL), BF16),
        input_output_aliases={3: 0} if chained else {},
        compiler_params=_params("arbitrary"),
        name="moe_dispatch",
    )(*args)


def _expert_kernel(first_ref, tiles_ref, rows_ref, tot_ref, xs_hbm, wg_ref, wu_ref, wd_ref, eo_hbm,
                   xbuf, obuf, wg_b, wu_b, wd_b, sem_in, sem_out):
    e = pl.program_id(0)
    first_tile = first_ref[e]
    n_tiles = tiles_ref[e]
    total = tot_ref[0]

    def in_copy(g):
        return pltpu.make_async_copy(
            xs_hbm.at[pl.ds(pl.multiple_of(g * MOE_TILE, MOE_TILE), MOE_TILE), :],
            xbuf.at[g % EXPERT_BUFFERS], sem_in.at[g % EXPERT_BUFFERS])

    def out_copy(g):
        return pltpu.make_async_copy(
            obuf.at[g % EXPERT_BUFFERS],
            eo_hbm.at[pl.ds(pl.multiple_of(g * MOE_TILE, MOE_TILE), MOE_TILE), :],
            sem_out.at[g % EXPERT_BUFFERS])

    @pl.when(e == 0)
    def _():
        for g0 in range(EXPERT_BUFFERS - 1):
            @pl.when(g0 < total)
            def _():
                in_copy(g0).start()

    @pl.when(n_tiles > 0)
    def _():
        wg_b[...] = wg_ref[0, 0].astype(BF16)
        wu_b[...] = wu_ref[0, 0].astype(BF16)
        wd_b[...] = wd_ref[0, 0].astype(BF16)

    def tile(j, carry):
        g = first_tile + j

        @pl.when(g + EXPERT_BUFFERS - 1 < total)
        def _():
            in_copy(g + EXPERT_BUFFERS - 1).start()

        in_copy(g).wait()

        @pl.when(g >= EXPERT_BUFFERS)
        def _():
            out_copy(g - EXPERT_BUFFERS).wait()

        row = lax.broadcasted_iota(jnp.int32, (MOE_TILE, 1), 0)
        x = xbuf[g % EXPERT_BUFFERS]
        x = jnp.where(row < rows_ref[e] - j * MOE_TILE, x, jnp.zeros_like(x))
        gate = jnp.dot(x, wg_b[...], preferred_element_type=F32)
        up = jnp.dot(x, wu_b[...], preferred_element_type=F32)
        hdn = gate * jax.nn.sigmoid(gate) * up
        obuf[g % EXPERT_BUFFERS] = jnp.dot(hdn.astype(BF16), wd_b[...],
                                           preferred_element_type=F32).astype(BF16)
        out_copy(g).start()
        return carry

    lax.fori_loop(0, n_tiles, tile, 0)

    @pl.when(e == pl.num_programs(0) - 1)
    def _():
        for back in range(EXPERT_BUFFERS, 0, -1):
            @pl.when(total >= back)
            def _():
                out_copy(total - back).wait()


def _experts(first_tile, tiles_e, rows_e, total, xs, w_gate, w_up, w_down, layer):
    weight = lambda e, f, t, r, n: (layer, e, 0, 0)
    return pl.pallas_call(
        _expert_kernel,
        grid_spec=pltpu.PrefetchScalarGridSpec(
            num_scalar_prefetch=4, grid=(N_EXPERTS,),
            in_specs=[pl.BlockSpec(memory_space=pl.ANY),
                      pl.BlockSpec((1, 1, D_MODEL, D_EXPERT), weight),
                      pl.BlockSpec((1, 1, D_MODEL, D_EXPERT), weight),
                      pl.BlockSpec((1, 1, D_EXPERT, D_MODEL), weight)],
            out_specs=pl.BlockSpec(memory_space=pl.ANY),
            scratch_shapes=[pltpu.VMEM((EXPERT_BUFFERS, MOE_TILE, D_MODEL), BF16),
                            pltpu.VMEM((EXPERT_BUFFERS, MOE_TILE, D_MODEL), BF16),
                            pltpu.VMEM((D_MODEL, D_EXPERT), BF16),
                            pltpu.VMEM((D_MODEL, D_EXPERT), BF16),
                            pltpu.VMEM((D_EXPERT, D_MODEL), BF16),
                            pltpu.SemaphoreType.DMA((EXPERT_BUFFERS,)),
                            pltpu.SemaphoreType.DMA((EXPERT_BUFFERS,))]),
        out_shape=jax.ShapeDtypeStruct(xs.shape, BF16),
        compiler_params=_params("arbitrary"),
        name="moe_experts",
    )(first_tile, tiles_e, rows_e, total, xs, w_gate, w_up, w_down)


def _combine_kernel(tbl_ref, nch_ref, h1_ref, routet_ref, gf_ref, eo_hbm, o_ref, leo, sem,
                    *, tm, rl, final_norm):
    i = pl.program_id(0)
    n = pl.num_programs(0)
    slot = i % 2

    def gather(tile, s):
        for q in range(rl // RUN_ALIGN):
            row = jnp.where(q < nch_ref[tile], tbl_ref[tile * TABLE_LANES + q], q * RUN_ALIGN)
            pltpu.make_async_copy(eo_hbm.at[pl.ds(pl.multiple_of(row, RUN_ALIGN), RUN_ALIGN), :],
                                  leo.at[s, pl.ds(q * RUN_ALIGN, RUN_ALIGN), :], sem.at[s]).start()

    @pl.when(i == 0)
    def _():
        gather(0, 0)

    @pl.when(i + 1 < n)
    def _():
        gather(i + 1, 1 - slot)

    pltpu.make_async_copy(eo_hbm.at[pl.ds(0, rl), :], leo.at[slot], sem.at[slot]).wait()

    rt = routet_ref[...]
    pos1 = rt[:, 0:1].astype(jnp.int32)
    pos2 = rt[:, 1:2].astype(jnp.int32)
    lane = lax.broadcasted_iota(jnp.int32, (tm, rl), 1)
    weights = jnp.where(lane == pos1, rt[:, 2:3], 0.0) + jnp.where(lane == pos2, rt[:, 3:4], 0.0)
    y = jnp.dot(weights.astype(BF16), leo[slot], preferred_element_type=F32)
    out = h1_ref[...] + y
    if final_norm:
        out = _rms(out, gf_ref[...])
    o_ref[...] = out


def _combine(table, n_chunks, h1, routet, g_final, eo, tm, final_norm):
    n = h1.shape[0]
    rl = _local_rows(tm)
    return pl.pallas_call(
        functools.partial(_combine_kernel, tm=tm, rl=rl, final_norm=final_norm),
        grid_spec=pltpu.PrefetchScalarGridSpec(
            num_scalar_prefetch=2, grid=(n // tm,),
            in_specs=[pl.BlockSpec((tm, D_MODEL), lambda i, t, c: (i, 0)),
                      pl.BlockSpec((tm, LANES), lambda i, t, c: (i, 0)),
                      pl.BlockSpec((1, D_MODEL), lambda i, t, c: (0, 0)),
                      pl.BlockSpec(memory_space=pl.ANY)],
            out_specs=pl.BlockSpec((tm, D_MODEL), lambda i, t, c: (i, 0)),
            scratch_shapes=[pltpu.VMEM((2, rl, D_MODEL), BF16), pltpu.SemaphoreType.DMA((2,))]),
        out_shape=jax.ShapeDtypeStruct((n, D_MODEL), F32),
        compiler_params=_params("arbitrary"),
        name="moe_combine",
    )(table.reshape(-1), n_chunks, h1, routet, g_final, eo)


def _in_proj_sample_kernel(x_ref, g_ref, w_ref, o_ref):
    xn = _rms(x_ref[...], g_ref[...])
    o_ref[...] = jnp.dot(xn.astype(BF16), w_ref[...], preferred_element_type=F32)


def _in_proj_sample(x, g, w_bf16):
    n = x.shape[0]
    return pl.pallas_call(
        _in_proj_sample_kernel,
        out_shape=jax.ShapeDtypeStruct((n, PROJ_WIDTH), F32),
        compiler_params=pltpu.CompilerParams(vmem_limit_bytes=VMEM_LIMIT),
        name="in_proj_sample",
    )(x, g, w_bf16)


def _sample_mix_kernel(q_ref, kn_ref, vn_ref, kt_ref, vt_ref,
                       bc_ref, bn_ref, gb_ref, gc_ref, hc_ref, st_ref, cw_ref,
                       o_ref, conv_ref, st_out_ref, *, t_new):
    contract_last = (((1,), (1,)), ((), ()))
    no_rows = jnp.zeros((LANES - SUBLANES, HEAD_DIM), F32)
    for h in range(N_HEADS):
        q = (q_ref[0, h] * SCALE).astype(BF16)
        s_c = jnp.dot(q, kt_ref[0, 0, h].astype(BF16), preferred_element_type=F32) + bc_ref[h]
        kn = jnp.concatenate([kn_ref[0, h], no_rows], axis=0).astype(BF16)
        vn = jnp.concatenate([vn_ref[0, h], no_rows], axis=0).astype(BF16)
        s_n = lax.dot_general(q, kn, contract_last, preferred_element_type=F32) + bn_ref[h]
        m = jnp.maximum(jnp.max(s_c, axis=-1, keepdims=True), jnp.max(s_n, axis=-1, keepdims=True))
        p_c = jnp.exp(s_c - m)
        p_n = jnp.exp(s_n - m)
        den = jnp.sum(p_c, axis=-1, keepdims=True) + jnp.sum(p_n, axis=-1, keepdims=True)
        num = (lax.dot_general(p_c.astype(BF16), vt_ref[0, 0, h].astype(BF16), contract_last,
                               preferred_element_type=F32)
               + jnp.dot(p_n.astype(BF16), vn, preferred_element_type=F32))
        out = num / den
        lse = m + jnp.log(den)
        ls = [lse[i * t_new:(i + 1) * t_new] for i in range(len(DILATIONS))]
        os_ = [out[i * t_new:(i + 1) * t_new] for i in range(len(DILATIONS))]
        mm = jnp.maximum(jnp.maximum(ls[0], ls[1]), ls[2])
        w = [jnp.exp(l - mm) for l in ls]
        o_ref[0, h] = (w[0] * os_[0] + w[1] * os_[1] + w[2] * os_[2]) / (w[0] + w[1] + w[2])
    cw = cw_ref[...]
    u = gc_ref[0] * hc_ref[0]
    st = st_ref[0]
    rows = [st[j:j + 1] for j in range(CONV_K - 1)] + [u[t:t + 1] for t in range(t_new)]
    gb = gb_ref[0]
    conv_ref[0] = jnp.concatenate(
        [gb[t:t + 1] * sum(cw[j:j + 1] * rows[t + j] for j in range(CONV_K)) for t in range(t_new)],
        axis=0)
    st_out_ref[0] = jnp.concatenate(rows[t_new:], axis=0)


SAMPLE_ROWS = 16


def _sample_bias_tables(step_bias, t_new, w_buf):
    cache_rows = []
    for b, d in zip(step_bias, DILATIONS):
        back = b[1:][::-1].T
        if d >= t_new:
            residue = np.eye(d, dtype=bool)[:t_new]
            span = jnp.where(residue[None, :, None, :], back[:, None, :, None], NEG)
            span = span.reshape(N_HEADS, t_new, SEG_KEYS * d)
        else:
            assert d == 1, "dilations between 1 and the number of new tokens are not supported"
            span = jnp.stack([jnp.pad(back[:, :SEG_KEYS - t], ((0, 0), (t, 0)), constant_values=NEG)
                              for t in range(t_new)], axis=1)
        cache_rows.append(jnp.pad(span, ((0, 0), (0, 0), (w_buf - SEG_KEYS * d, 0)),
                                  constant_values=NEG))
    n_rows = len(DILATIONS) * t_new
    bias_c = jnp.pad(jnp.concatenate(cache_rows, axis=1), ((0, 0), (0, SAMPLE_ROWS - n_rows), (0, 0)))

    rows, cols, which, steps = [], [], [], []
    for di, d in enumerate(DILATIONS):
        for t in range(t_new):
            for t2 in range(t + 1):
                if (t - t2) % d == 0:
                    rows.append(di * t_new + t)
                    cols.append(t2)
                    which.append(di)
                    steps.append((t - t2) // d)
    vals = jnp.stack(step_bias)[np.array(which), np.array(steps)]
    bias_n = jnp.full((N_HEADS, n_rows, LANES), NEG, F32).at[:, np.array(rows), np.array(cols)].set(vals.T)
    bias_n = jnp.pad(bias_n, ((0, 0), (0, SAMPLE_ROWS - n_rows), (0, 0)))
    return bias_c, bias_n


def _sample_mix(q, kn, vn, cache_k, cache_v, layer, bias_c, bias_n, gb, gc, hc, state, conv_w):
    db, t_new = q.shape[0], q.shape[1]
    w_buf = cache_k.shape[2]
    kt = cache_k.transpose(0, 1, 3, 4, 2)
    vt = cache_v.transpose(0, 1, 3, 4, 2)
    n_rows = len(DILATIONS) * t_new
    q_rows = jnp.pad(jnp.tile(q.transpose(0, 2, 1, 3), (1, 1, len(DILATIONS), 1)),
                     ((0, 0), (0, 0), (0, SAMPLE_ROWS - n_rows), (0, 0)))
    kn_pad = jnp.pad(kn.transpose(0, 2, 1, 3), ((0, 0), (0, 0), (0, SUBLANES - t_new), (0, 0)))
    vn_pad = jnp.pad(vn.transpose(0, 2, 1, 3), ((0, 0), (0, 0), (0, SUBLANES - t_new), (0, 0)))
    row = pl.BlockSpec((1, t_new, CONV_DIM), lambda b: (b, 0, 0))
    new_spec = pl.BlockSpec((1, N_HEADS, SUBLANES, HEAD_DIM), lambda b: (b, 0, 0, 0))
    win_spec = pl.BlockSpec((1, 1, N_HEADS, HEAD_DIM, w_buf), lambda b: (layer, b, 0, 0, 0))
    attn, conv, st = pl.pallas_call(
        functools.partial(_sample_mix_kernel, t_new=t_new),
        grid=(db,),
        in_specs=[pl.BlockSpec((1, N_HEADS, SAMPLE_ROWS, HEAD_DIM), lambda b: (b, 0, 0, 0)),
                  new_spec, new_spec, win_spec, win_spec,
                  pl.BlockSpec(bias_c.shape, lambda b: (0, 0, 0)),
                  pl.BlockSpec(bias_n.shape, lambda b: (0, 0, 0)),
                  row, row, row,
                  pl.BlockSpec((1, CONV_K - 1, CONV_DIM), lambda b: (b, 0, 0)),
                  pl.BlockSpec((CONV_K, CONV_DIM), lambda b: (0, 0))],
        out_specs=[pl.BlockSpec((1, N_HEADS, t_new, HEAD_DIM), lambda b: (b, 0, 0, 0)), row,
                   pl.BlockSpec((1, CONV_K - 1, CONV_DIM), lambda b: (b, 0, 0))],
        out_shape=[jax.ShapeDtypeStruct((db, N_HEADS, t_new, HEAD_DIM), F32),
                   jax.ShapeDtypeStruct((db, t_new, CONV_DIM), F32),
                   jax.ShapeDtypeStruct((db, CONV_K - 1, CONV_DIM), F32)],
        compiler_params=_params("arbitrary"),
        name="sample_mix",
    )(q_rows, kn_pad, vn_pad, kt, vt, bias_c, bias_n, gb, gc, hc, state, conv_w)
    return attn.transpose(0, 2, 1, 3), conv, st


def _to_slabs(x):
    n = x.shape[0]
    return x.reshape(n, N_SLABS, LANES).transpose(1, 0, 2)


def _window_from_slabs(x, batch, seq, w_keep):
    win = x.reshape(N_SLABS, batch, seq, LANES)[:, :, seq - w_keep:]
    return win.transpose(1, 2, 0, 3).reshape(batch, w_keep, N_HEADS, HEAD_DIM)


def kernel(x_prompt, x_sample, cache_k, cache_v, state_conv, rel_bias, norm_mix, norm_ffn,
           norm_final, w_in, conv_w, w_out, w_router_group, w_router_expert, w_gate, w_up,
           w_down):
    batch, seq, _ = x_prompt.shape
    db, t_new, _ = x_sample.shape
    depth = w_in.shape[0]
    w_keep = min(MAX_WINDOW, seq)

    hp = x_prompt.reshape(batch * seq, D_MODEL)
    hs = x_sample.reshape(db * t_new, D_MODEL)
    step_bias = _step_bias(rel_bias)
    bias_prompt = _prompt_bias_tables(step_bias)
    bias_c, bias_n = _sample_bias_tables(step_bias, t_new, cache_k.shape[2])
    g_final = norm_final.reshape(1, D_MODEL)

    tm_p = 256
    n_s = db * t_new
    nt_p = batch * seq // tm_p
    pad_per_tile = N_EXPERTS * (RUN_ALIGN - 1)
    max_tiles = -(-(2 * (batch * seq + n_s) + (nt_p + 1) * pad_per_tile) // MOE_TILE) + N_EXPERTS
    buf_rows = max_tiles * MOE_TILE + _local_rows(tm_p)

    pk, pv, pc, sk, sv, sc = [], [], [], [], [], []
    for l in range(depth):
        w_in_b = w_in[l].astype(BF16)
        w_out_b = w_out[l].astype(BF16)
        w_router_t = jnp.pad(jnp.concatenate([w_router_expert[l], w_router_group[l]], axis=1).T,
                             ((0, ROUTER_ROWS - N_EXPERTS - N_GROUPS), (0, 0)))
        g_mix = norm_mix[l].reshape(1, D_MODEL)
        g_ffn = norm_ffn[l].reshape(1, D_MODEL)
        last = l == depth - 1

        q, k, v, conv, u_last = _in_proj(hp, g_mix, w_in_b, conv_w[l], seq)
        attn = _attention_prompt(q, k, v, bias_prompt, batch, seq)
        h1, xn, route, routet, cnt = _out_proj(attn, conv, hp, w_out_b, g_ffn, w_router_t,
                                               tm=4 * tm_p, sub=tm_p)
        pk.append(_window_from_slabs(k, batch, seq, w_keep))
        pv.append(_window_from_slabs(v, batch, seq, w_keep))
        pc.append(u_last)

        proj = _in_proj_sample(hs, g_mix, w_in_b)
        qs = proj[:, :ATTN_WIDTH].reshape(db, t_new, N_HEADS, HEAD_DIM)
        ks = proj[:, ATTN_WIDTH:2 * ATTN_WIDTH].reshape(db, t_new, N_HEADS, HEAD_DIM)
        vs = proj[:, 2 * ATTN_WIDTH:3 * ATTN_WIDTH].reshape(db, t_new, N_HEADS, HEAD_DIM)
        c0 = 3 * ATTN_WIDTH
        gb = proj[:, c0:c0 + CONV_DIM].reshape(db, t_new, CONV_DIM)
        gc = proj[:, c0 + CONV_DIM:c0 + 2 * CONV_DIM].reshape(db, t_new, CONV_DIM)
        hc = proj[:, c0 + 2 * CONV_DIM:].reshape(db, t_new, CONV_DIM)
        attn_s, conv_s, state_s = _sample_mix(qs, ks, vs, cache_k, cache_v, l, bias_c, bias_n,
                                              gb, gc, hc, state_conv[l], conv_w[l])
        h1s, xns, route_s, routet_s, cnt_s = _out_proj(
            _to_slabs(attn_s.reshape(n_s, ATTN_WIDTH)), conv_s.reshape(n_s, CONV_DIM), hs,
            w_out_b, g_ffn, w_router_t, tm=n_s, sub=n_s)

        table, n_chunks, first_tile, tiles_e, rows_e, total = _moe_plan(
            jnp.concatenate([cnt[:, :, 0], cnt_s[:, :, 0]], axis=0), max_tiles)
        xs = _dispatch(table[:nt_p], xn, route, tm_p, buf_rows)
        xs = _dispatch(table[nt_p:], xns, route_s, n_s, buf_rows, xs_prev=xs)
        eo = _experts(first_tile, tiles_e, rows_e, total, xs, w_gate, w_up, w_down, l)
        hp = _combine(table[:nt_p], n_chunks[:nt_p], h1, routet, g_final, eo, tm_p, last)
        hs = _combine(table[nt_p:], n_chunks[nt_p:], h1s, routet_s, g_final, eo, n_s, last)
        sk.append(ks)
        sv.append(vs)
        sc.append(state_s)

    return (hp.reshape(batch, seq, D_MODEL), hs.reshape(db, t_new, D_MODEL),
            jnp.stack(pk), jnp.stack(pv), jnp.stack(pc),
            jnp.stack(sk), jnp.stack(sv), jnp.stack(sc))
```

```python
import functools
import math

import jax
import jax.numpy as jnp
import numpy as np
from jax import lax
from jax.experimental import pallas as pl
from jax.experimental.pallas import tpu as pltpu

F32 = jnp.float32
BF16 = jnp.bfloat16

D_MODEL = 1024
N_HEADS = 8
HEAD_DIM = 64
ATTN_WIDTH = N_HEADS * HEAD_DIM
CONV_DIM = D_MODEL - ATTN_WIDTH
CONV_K = 3
PROJ_WIDTH = 3 * ATTN_WIDTH + 3 * CONV_DIM
DILATIONS = (1, 4, 16)
SEG_KEYS = 128
Q_BLOCK = 128
MAX_WINDOW = 2048
N_BUCKETS = 32
MAX_DISTANCE = 2048
N_GROUPS = 4
EXPERTS_PER_GROUP = 8
N_EXPERTS = N_GROUPS * EXPERTS_PER_GROUP
D_EXPERT = D_MODEL // 4
EPS = 1e-6
NEG = -1e30
SCALE = 1.0 / math.sqrt(HEAD_DIM)

LANES = 128
SUBLANES = 8
N_SLABS = ATTN_WIDTH // LANES
VMEM_LIMIT = 48 * 1024 * 1024
REGROUP = 4
UNITS_PER_TRIP = 16


def _params(*sem):
    return pltpu.CompilerParams(dimension_semantics=sem, vmem_limit_bytes=VMEM_LIMIT)


def _rms(x, g):
    return x * lax.rsqrt(jnp.mean(x * x, axis=-1, keepdims=True) + EPS) * g


def _rel_bucket(dist):
    max_exact = N_BUCKETS // 2
    d_f = jnp.maximum(dist, 1).astype(F32)
    large = max_exact + (jnp.log(d_f / max_exact) / math.log(MAX_DISTANCE / max_exact)
                         * (N_BUCKETS - max_exact)).astype(jnp.int32)
    large = jnp.minimum(large, N_BUCKETS - 1)
    return jnp.where(dist < max_exact, dist, large)


def _in_proj_kernel(x_ref, g_ref, w_ref, cw_ref, q_ref, k_ref, v_ref, conv_ref, ulast_ref,
                    ext_ref, *, tm, tiles_per_seq):
    i = pl.program_id(0)
    xn = _rms(x_ref[...], g_ref[...])
    proj = jnp.dot(xn.astype(BF16), w_ref[...], preferred_element_type=F32)
    for s in range(N_SLABS):
        q_ref[s] = proj[:, s * LANES:(s + 1) * LANES]
        k_ref[s] = proj[:, ATTN_WIDTH + s * LANES:ATTN_WIDTH + (s + 1) * LANES]
        v_ref[s] = proj[:, 2 * ATTN_WIDTH + s * LANES:2 * ATTN_WIDTH + (s + 1) * LANES]
    c0 = 3 * ATTN_WIDTH
    gb = proj[:, c0:c0 + CONV_DIM]
    u = proj[:, c0 + CONV_DIM:c0 + 2 * CONV_DIM] * proj[:, c0 + 2 * CONV_DIM:c0 + 3 * CONV_DIM]

    @pl.when(i % tiles_per_seq == 0)
    def _():
        ext_ref[0:SUBLANES, :] = jnp.zeros((SUBLANES, CONV_DIM), F32)

    ext_ref[SUBLANES:SUBLANES + tm, :] = u
    u1 = ext_ref[SUBLANES - 1:SUBLANES - 1 + tm, :]
    u2 = ext_ref[SUBLANES - 2:SUBLANES - 2 + tm, :]
    cw = cw_ref[...]
    conv_ref[...] = gb * (cw[0:1] * u2 + cw[1:2] * u1 + cw[2:3] * u)
    ulast_ref[0] = ext_ref[tm + SUBLANES - 2:tm + SUBLANES, :]
    ext_ref[0:SUBLANES, :] = ext_ref[tm:tm + SUBLANES, :]


def _in_proj(x, g, w_bf16, conv_w, seq_len, tm=1024):
    n = x.shape[0]
    n_seq = n // seq_len
    slab = jax.ShapeDtypeStruct((N_SLABS, n, LANES), F32)
    slab_spec = pl.BlockSpec((N_SLABS, tm, LANES), lambda i: (0, i, 0))
    tiles_per_seq = seq_len // tm
    return pl.pallas_call(
        functools.partial(_in_proj_kernel, tm=tm, tiles_per_seq=tiles_per_seq),
        grid=(n // tm,),
        in_specs=[pl.BlockSpec((tm, D_MODEL), lambda i: (i, 0)),
                  pl.BlockSpec((1, D_MODEL), lambda i: (0, 0)),
                  pl.BlockSpec((D_MODEL, PROJ_WIDTH), lambda i: (0, 0)),
                  pl.BlockSpec((CONV_K, CONV_DIM), lambda i: (0, 0))],
        out_specs=[slab_spec, slab_spec, slab_spec,
                   pl.BlockSpec((tm, CONV_DIM), lambda i: (i, 0)),
                   pl.BlockSpec((1, CONV_K - 1, CONV_DIM), lambda i: (i // tiles_per_seq, 0, 0))],
        out_shape=[slab, slab, slab,
                   jax.ShapeDtypeStruct((n, CONV_DIM), F32),
                   jax.ShapeDtypeStruct((n_seq, CONV_K - 1, CONV_DIM), F32)],
        scratch_shapes=[pltpu.VMEM((tm + SUBLANES, CONV_DIM), F32)],
        compiler_params=_params("arbitrary"),
        name="in_proj_prompt",
    )(x, g, w_bf16, conv_w)


def _attn_kernel(q_ref, k_ref, v_ref, bias_ref, o_ref, q4, k4, v4, ktail, vtail, oacc, lacc, *, sb):
    j = pl.program_id(2)
    slot = j % 2
    prev = 1 - slot
    seg = sb // REGROUP

    @pl.when(j == 0)
    def _():
        k4[1] = jnp.zeros((sb, LANES), F32)
        v4[1] = jnp.zeros((sb, LANES), F32)
        ktail[...] = jnp.zeros((Q_BLOCK, LANES), F32)
        vtail[...] = jnp.zeros((Q_BLOCK, LANES), F32)

    for r in range(REGROUP):
        q4[r * seg:(r + 1) * seg, :] = q_ref[pl.ds(r, seg, stride=REGROUP), :] * SCALE
        k4[slot, r * seg:(r + 1) * seg, :] = k_ref[pl.ds(r, seg, stride=REGROUP), :]
        v4[slot, r * seg:(r + 1) * seg, :] = v_ref[pl.ds(r, seg, stride=REGROUP), :]

    lane = lax.broadcasted_iota(jnp.int32, (Q_BLOCK, LANES), 1)
    low = lane < HEAD_DIM
    ones = jnp.ones((2 * Q_BLOCK, LANES), BF16)

    def block(start):
        return pl.ds(pl.multiple_of(start, Q_BLOCK), Q_BLOCK)

    for di, d in enumerate(DILATIONS):
        nq = sb // (Q_BLOCK * d)

        def unit(u, di=di, d=d, nq=nq):
            r = u // nq
            n = u % nq
            qstart = r + d * Q_BLOCK * n
            if d == 1:
                here = block(n * Q_BLOCK)
                before = block(jnp.maximum(n - 1, 0) * Q_BLOCK)
                q = q_ref[here, :] * SCALE
                k_new, v_new = k_ref[here, :], v_ref[here, :]
                k_old = jnp.where(n == 0, ktail[...], k_ref[before, :])
                v_old = jnp.where(n == 0, vtail[...], v_ref[before, :])
            elif d == REGROUP:
                here = block(r * seg + n * Q_BLOCK)
                before = block(r * seg + jnp.maximum(n - 1, 0) * Q_BLOCK)
                last = block(r * seg + seg - Q_BLOCK)
                q = q4[here, :]
                k_new, v_new = k4[slot, here, :], v4[slot, here, :]
                k_old = jnp.where(n == 0, k4[prev, last, :], k4[slot, before, :])
                v_old = jnp.where(n == 0, v4[prev, last, :], v4[slot, before, :])
            else:
                rows = pl.ds((u % REGROUP) * seg + u // REGROUP, Q_BLOCK, stride=REGROUP)
                q = q4[rows, :]
                k_new, v_new = k4[slot, rows, :], v4[slot, rows, :]
                k_old, v_old = k4[prev, rows, :], v4[prev, rows, :]
            kk = jnp.concatenate([k_old, k_new], axis=0).astype(BF16)
            vv = jnp.concatenate([v_old, v_new], axis=0).astype(BF16)
            qm = jnp.concatenate([jnp.where(low, q, 0.0), jnp.where(low, 0.0, q)],
                                 axis=0).astype(BF16)
            s = lax.dot_general(qm, kk, (((1,), (1,)), ((), ())),
                                preferred_element_type=F32)
            first = jnp.logical_and(j == 0, n == 0).astype(jnp.int32)
            s = s + bias_ref[di, first]
            m = jnp.max(s, axis=-1, keepdims=True)
            p = jnp.exp(s - m).astype(BF16)
            pv = jnp.dot(p, jnp.concatenate([vv, ones], axis=1),
                         preferred_element_type=F32)
            o_sel = jnp.where(low, pv[0:Q_BLOCK, 0:LANES], pv[Q_BLOCK:, 0:LANES])
            l_sel = jnp.where(low, pv[0:Q_BLOCK, LANES:], pv[Q_BLOCK:, LANES:])
            m_sel = jnp.where(low, jnp.broadcast_to(m[0:Q_BLOCK], (Q_BLOCK, LANES)),
                              jnp.broadcast_to(m[Q_BLOCK:], (Q_BLOCK, LANES)))
            oacc[di, pl.ds(qstart, Q_BLOCK, stride=d), :] = o_sel / l_sel
            lacc[di, pl.ds(qstart, Q_BLOCK, stride=d), :] = m_sel + jnp.log(l_sel)

        def group(g, carry, unit=unit):
            for uu in range(UNITS_PER_TRIP):
                unit(g * UNITS_PER_TRIP + uu)
            return carry

        lax.fori_loop(0, sb // Q_BLOCK // UNITS_PER_TRIP, group, 0)

    def merge(c, carry):
        rows = pl.ds(pl.multiple_of(c * 256, 256), 256)
        l0, l1, l2 = lacc[0, rows, :], lacc[1, rows, :], lacc[2, rows, :]
        mm = jnp.maximum(jnp.maximum(l0, l1), l2)
        w0, w1, w2 = jnp.exp(l0 - mm), jnp.exp(l1 - mm), jnp.exp(l2 - mm)
        num = w0 * oacc[0, rows, :] + w1 * oacc[1, rows, :] + w2 * oacc[2, rows, :]
        o_ref[rows, :] = num / (w0 + w1 + w2)
        return carry

    lax.fori_loop(0, sb // 256, merge, 0)
    ktail[...] = k_ref[sb - Q_BLOCK:sb, :]
    vtail[...] = v_ref[sb - Q_BLOCK:sb, :]


def _step_bias(rel_bias):
    steps = jnp.arange(SEG_KEYS + 1)
    return [rel_bias[_rel_bucket(steps * d)].astype(F32) for d in DILATIONS]


def _prompt_bias_tables(step_bias):
    width = 3 * Q_BLOCK
    k_loc = jnp.arange(2 * Q_BLOCK)
    tables = []
    for b in step_bias:
        g = jnp.concatenate([b[::-1].T, jnp.full((N_HEADS, width - SEG_KEYS), NEG, F32)], axis=1)
        flat = jnp.broadcast_to(g[:, None, :], (N_HEADS, Q_BLOCK, width + 1)).reshape(N_HEADS, -1)
        x = flat[:, :Q_BLOCK * width].reshape(N_HEADS, Q_BLOCK, width)
        t = x[:, :, :2 * Q_BLOCK]
        t_first = jnp.where(k_loc[None, None, :] >= Q_BLOCK, t, NEG)
        tables.append(jnp.stack([t.reshape(N_SLABS, 2 * Q_BLOCK, 2 * Q_BLOCK),
                                 t_first.reshape(N_SLABS, 2 * Q_BLOCK, 2 * Q_BLOCK)]))
    return jnp.stack(tables)


def _attention_prompt(q, k, v, bias_tab, n_seq, seq_len, sb=2048):
    assert DILATIONS == (1, REGROUP, REGROUP * REGROUP)
    n = q.shape[1]
    nsb = seq_len // sb
    blk = pl.BlockSpec((None, sb, LANES), lambda b, s, j: (s, b * nsb + j, 0))
    return pl.pallas_call(
        functools.partial(_attn_kernel, sb=sb),
        grid=(n_seq, N_SLABS, nsb),
        in_specs=[blk, blk, blk,
                  pl.BlockSpec((len(DILATIONS), 2, None, 2 * Q_BLOCK, 2 * Q_BLOCK),
                               lambda b, s, j: (0, 0, s, 0, 0))],
        out_specs=blk,
        out_shape=jax.ShapeDtypeStruct((N_SLABS, n, LANES), F32),
        scratch_shapes=[pltpu.VMEM((sb, LANES), F32),
                        pltpu.VMEM((2, sb, LANES), F32), pltpu.VMEM((2, sb, LANES), F32),
                        pltpu.VMEM((Q_BLOCK, LANES), F32), pltpu.VMEM((Q_BLOCK, LANES), F32),
                        pltpu.VMEM((len(DILATIONS), sb, LANES), F32),
                        pltpu.VMEM((len(DILATIONS), sb, LANES), F32)],
        compiler_params=_params("arbitrary", "arbitrary", "arbitrary"),
        name="attention_prompt",
    )(q, k, v, bias_tab)


ROUTER_ROWS = 48
MOE_TILE = 256
RUN_ALIGN = 2 * SUBLANES
TABLE_LANES = 128
EXPERT_BUFFERS = 4


def _local_rows(tm):
    return -(-(2 * tm + N_EXPERTS * (RUN_ALIGN - 1)) // LANES) * LANES


def _route(lg):
    tm = lg.shape[1]
    gl = lg[N_EXPERTS:N_EXPERTS + N_GROUPS]
    grow = lax.broadcasted_iota(jnp.int32, (N_GROUPS, tm), 0)
    gmax = jnp.max(gl, axis=0, keepdims=True)
    gsum = jnp.sum(jnp.exp(gl - gmax), axis=0, keepdims=True)
    gidx = jnp.min(jnp.where(gl == gmax, grow, N_GROUPS), axis=0, keepdims=True)
    p_grp = 1.0 / gsum
    el = lg[0:N_EXPERTS]
    e = lax.broadcasted_iota(jnp.int32, (N_EXPERTS, tm), 0)
    in_grp = (e // EXPERTS_PER_GROUP) == gidx
    elog = jnp.where(in_grp, el, -jnp.inf)
    emax = jnp.max(elog, axis=0, keepdims=True)
    eexp = jnp.exp(elog - emax)
    eprob = jnp.where(in_grp, eexp / jnp.sum(eexp, axis=0, keepdims=True), -1.0)
    p1 = jnp.max(eprob, axis=0, keepdims=True)
    i1 = jnp.min(jnp.where(eprob == p1, e, N_EXPERTS), axis=0, keepdims=True)
    rest = jnp.where(e == i1, -1.0, eprob)
    p2 = jnp.max(rest, axis=0, keepdims=True)
    i2 = jnp.min(jnp.where(rest == p2, e, N_EXPERTS), axis=0, keepdims=True)
    tot = p1 + p2
    return e == i1, e == i2, p_grp * (p1 / tot), p_grp * (p2 / tot)


def _split_bf16(x):
    hi = x.astype(BF16)
    return hi, (x - hi.astype(F32)).astype(BF16)


def _out_proj_kernel(attn_ref, conv_ref, h_ref, wo_ref, g_ref, wr_ref, upper_ref, lower_ref,
                     h1_ref, xn_ref, route_ref, routet_ref, cnt_ref, *, tm, sub):
    contract_last = (((1,), (1,)), ((), ()))
    mix = jnp.concatenate([attn_ref[s] for s in range(N_SLABS)] + [conv_ref[...]], axis=1)
    h1_ref[...] = h_ref[...] + jnp.dot(mix.astype(BF16), wo_ref[...], preferred_element_type=F32)
    for t in range(tm // sub):
        rs = slice(t * sub, (t + 1) * sub)
        xn = _rms(h1_ref[rs, :], g_ref[...])
        xh, xl = _split_bf16(xn)
        xn_ref[rs, :] = xh
        both = lax.dot_general(wr_ref[...], xh, contract_last, preferred_element_type=F32)
        lg = (both[:ROUTER_ROWS] + both[ROUTER_ROWS:]
              + lax.dot_general(wr_ref[:ROUTER_ROWS, :], xl, contract_last, preferred_element_type=F32))
        sel1, sel2, g1, g2 = _route(lg)
        chosen = jnp.where(sel1 | sel2, 1.0, 0.0)
        rank = jnp.dot(chosen.astype(BF16), upper_ref[...], preferred_element_type=F32)
        cnt = jnp.sum(chosen, axis=1, keepdims=True)
        cnt_pad = jnp.floor((cnt + (RUN_ALIGN - 1)) * (1.0 / RUN_ALIGN)) * RUN_ALIGN
        cnt_b = jnp.broadcast_to(cnt_pad, (N_EXPERTS, LANES))
        start = jnp.dot(lower_ref[...], cnt_b.astype(BF16), preferred_element_type=F32)
        pos = start[:, 0:1] + rank
        pos1 = jnp.sum(jnp.where(sel1, pos, 0.0), axis=0, keepdims=True)
        pos2 = jnp.sum(jnp.where(sel2, pos, 0.0), axis=0, keepdims=True)
        rows = jnp.concatenate([pos1, pos2, g1, g2, jnp.zeros((SUBLANES - 4, sub), F32)], axis=0)
        route_ref[t] = rows
        routet_ref[rs, :] = jnp.concatenate([rows, jnp.zeros((LANES - SUBLANES, sub), F32)], axis=0).T
        cnt_ref[t] = cnt_b


def _out_proj(attn, conv, h, wo_bf16, g, w_router_t, tm, sub):
    n = h.shape[0]
    w_router = jnp.concatenate(_split_bf16(w_router_t), axis=0)
    idx = jnp.arange(sub)
    upper = (idx[:, None] < idx[None, :]).astype(BF16)
    eidx = jnp.arange(N_EXPERTS)
    lower = (eidx[None, :] < eidx[:, None]).astype(BF16)
    const = lambda shape: pl.BlockSpec(shape, lambda i: (0,) * len(shape))
    per_step = tm // sub
    return pl.pallas_call(
        functools.partial(_out_proj_kernel, tm=tm, sub=sub),
        grid=(n // tm,),
        in_specs=[pl.BlockSpec((N_SLABS, tm, LANES), lambda i: (0, i, 0)),
                  pl.BlockSpec((tm, CONV_DIM), lambda i: (i, 0)),
                  pl.BlockSpec((tm, D_MODEL), lambda i: (i, 0)),
                  const((D_MODEL, D_MODEL)), const((1, D_MODEL)),
                  const((2 * ROUTER_ROWS, D_MODEL)),
                  const((sub, sub)), const((N_EXPERTS, N_EXPERTS))],
        out_specs=[pl.BlockSpec((tm, D_MODEL), lambda i: (i, 0)),
                   pl.BlockSpec((tm, D_MODEL), lambda i: (i, 0)),
                   pl.BlockSpec((per_step, SUBLANES, sub), lambda i: (i, 0, 0)),
                   pl.BlockSpec((tm, LANES), lambda i: (i, 0)),
                   pl.BlockSpec((per_step, N_EXPERTS, LANES), lambda i: (i, 0, 0))],
        out_shape=[jax.ShapeDtypeStruct((n, D_MODEL), F32),
                   jax.ShapeDtypeStruct((n, D_MODEL), BF16),
                   jax.ShapeDtypeStruct((n // sub, SUBLANES, sub), F32),
                   jax.ShapeDtypeStruct((n, LANES), F32),
                   jax.ShapeDtypeStruct((n // sub, N_EXPERTS, LANES), F32)],
        compiler_params=_params("arbitrary"),
        name="out_proj",
    )(attn, conv, h, wo_bf16, g, w_router, upper, lower)


def _moe_plan(cnt_tiles, max_tiles):
    c = cnt_tiles.astype(jnp.int32)
    rows_e = jnp.sum(c, axis=0)
    tiles_e = (rows_e + MOE_TILE - 1) // MOE_TILE
    cum = jnp.cumsum(tiles_e)
    total = cum[-1]
    first_tile = cum - tiles_e

    run_start = jnp.cumsum(c, axis=0) - c
    local_start = jnp.cumsum(c, axis=1) - c
    local_end = local_start + c
    row = jnp.arange(TABLE_LANES, dtype=jnp.int32) * RUN_ALIGN
    inside = (local_start[:, None, :] <= row[None, :, None]) & (row[None, :, None] < local_end[:, None, :])
    base = first_tile[None] * MOE_TILE + run_start - local_start
    n_chunks = jnp.sum(c, axis=1) // RUN_ALIGN
    table = jnp.sum(jnp.where(inside, base[:, None, :] + row[None, :, None], 0), axis=-1)
    live = jnp.arange(TABLE_LANES)[None] < n_chunks[:, None]
    table = jnp.where(live, table, max_tiles * MOE_TILE + row[None])
    i32 = lambda a: a.astype(jnp.int32)
    return (i32(table), i32(n_chunks), i32(first_tile), i32(tiles_e), i32(rows_e),
            i32(total).reshape(1))


def _dispatch_kernel(tbl_ref, xn_ref, route_ref, *rest, tm, rl, chained):
    xs_hbm, lsort, sem = rest[1:] if chained else rest
    i = pl.program_id(0)
    slot = i % 2
    rows = route_ref[0]
    pos1 = rows[0:1].astype(jnp.int32)
    pos2 = rows[1:2].astype(jnp.int32)
    p = lax.broadcasted_iota(jnp.int32, (rl, tm), 0)
    one_hot = jnp.where((p == pos1) | (p == pos2), 1.0, 0.0).astype(BF16)
    lsort[slot] = jnp.dot(one_hot, xn_ref[...], preferred_element_type=F32).astype(BF16)

    def tile_copies_done(s):
        pltpu.make_async_copy(lsort.at[s], xs_hbm.at[pl.ds(0, rl), :], sem.at[s]).wait()

    @pl.when(i > 0)
    def _():
        tile_copies_done(1 - slot)

    for q in range(rl // RUN_ALIGN):
        row = pl.multiple_of(tbl_ref[i * TABLE_LANES + q], RUN_ALIGN)
        pltpu.make_async_copy(lsort.at[slot, pl.ds(q * RUN_ALIGN, RUN_ALIGN), :],
                              xs_hbm.at[pl.ds(row, RUN_ALIGN), :], sem.at[slot]).start()

    @pl.when(i == pl.num_programs(0) - 1)
    def _():
        tile_copies_done(slot)


def _dispatch(table, xn, route, tm, total_rows, xs_prev=None):
    nt = xn.shape[0] // tm
    rl = _local_rows(tm)
    chained = xs_prev is not None
    in_specs = [pl.BlockSpec((tm, D_MODEL), lambda i, t: (i, 0)),
                pl.BlockSpec((1, SUBLANES, tm), lambda i, t: (i, 0, 0))]
    args = [table.reshape(-1), xn, route]
    if chained:
        in_specs.append(pl.BlockSpec(memory_space=pl.ANY))
        args.append(xs_prev)
    return pl.pallas_call(
        functools.partial(_dispatch_kernel, tm=tm, rl=rl, chained=chained),
        grid_spec=pltpu.PrefetchScalarGridSpec(
            num_scalar_prefetch=1, grid=(nt,), in_specs=in_specs,
            out_specs=pl.BlockSpec(memory_space=pl.ANY),
            scratch_shapes=[pltpu.VMEM((2, rl, D_MODEL), BF16), pltpu.SemaphoreType.DMA((2,))]),
        out_shape=jax.ShapeDtypeStruct((total_rows, D_MODEL), BF16),
        input_output_aliases={3: 0} if chained else {},
        compiler_params=_params("arbitrary"),
        name="moe_dispatch",
    )(*args)


def _expert_kernel(first_ref, tiles_ref, rows_ref, tot_ref, xs_hbm, wg_ref, wu_ref, wd_ref, eo_hbm,
                   xbuf, obuf, wg_b, wu_b, wd_b, sem_in, sem_out):
    e = pl.program_id(0)
    first_tile = first_ref[e]
    n_tiles = tiles_ref[e]
    total = tot_ref[0]

    def in_copy(g):
        return pltpu.make_async_copy(
            xs_hbm.at[pl.ds(pl.multiple_of(g * MOE_TILE, MOE_TILE), MOE_TILE), :],
            xbuf.at[g % EXPERT_BUFFERS], sem_in.at[g % EXPERT_BUFFERS])

    def out_copy(g):
        return pltpu.make_async_copy(
            obuf.at[g % EXPERT_BUFFERS],
            eo_hbm.at[pl.ds(pl.multiple_of(g * MOE_TILE, MOE_TILE), MOE_TILE), :],
            sem_out.at[g % EXPERT_BUFFERS])

    @pl.when(e == 0)
    def _():
        for g0 in range(EXPERT_BUFFERS - 1):
            @pl.when(g0 < total)
            def _():
                in_copy(g0).start()

    @pl.when(n_tiles > 0)
    def _():
        wg_b[...] = wg_ref[0, 0].astype(BF16)
        wu_b[...] = wu_ref[0, 0].astype(BF16)
        wd_b[...] = wd_ref[0, 0].astype(BF16)

    def tile(j, carry):
        g = first_tile + j

        @pl.when(g + EXPERT_BUFFERS - 1 < total)
        def _():
            in_copy(g + EXPERT_BUFFERS - 1).start()

        in_copy(g).wait()

        @pl.when(g >= EXPERT_BUFFERS)
        def _():
            out_copy(g - EXPERT_BUFFERS).wait()

        row = lax.broadcasted_iota(jnp.int32, (MOE_TILE, 1), 0)
        x = xbuf[g % EXPERT_BUFFERS]
        x = jnp.where(row < rows_ref[e] - j * MOE_TILE, x, jnp.zeros_like(x))
        gate = jnp.dot(x, wg_b[...], preferred_element_type=F32)
        up = jnp.dot(x, wu_b[...], preferred_element_type=F32)
        hdn = gate * jax.nn.sigmoid(gate) * up
        obuf[g % EXPERT_BUFFERS] = jnp.dot(hdn.astype(BF16), wd_b[...],
                                           preferred_element_type=F32).astype(BF16)
        out_copy(g).start()
        return carry

    lax.fori_loop(0, n_tiles, tile, 0)

    @pl.when(e == pl.num_programs(0) - 1)
    def _():
        for back in range(EXPERT_BUFFERS, 0, -1):
            @pl.when(total >= back)
            def _():
                out_copy(total - back).wait()


def _experts(first_tile, tiles_e, rows_e, total, xs, w_gate, w_up, w_down, layer):
    weight = lambda e, f, t, r, n: (layer, e, 0, 0)
    return pl.pallas_call(
        _expert_kernel,
        grid_spec=pltpu.PrefetchScalarGridSpec(
            num_scalar_prefetch=4, grid=(N_EXPERTS,),
            in_specs=[pl.BlockSpec(memory_space=pl.ANY),
                      pl.BlockSpec((1, 1, D_MODEL, D_EXPERT), weight),
                      pl.BlockSpec((1, 1, D_MODEL, D_EXPERT), weight),
                      pl.BlockSpec((1, 1, D_EXPERT, D_MODEL), weight)],
            out_specs=pl.BlockSpec(memory_space=pl.ANY),
            scratch_shapes=[pltpu.VMEM((EXPERT_BUFFERS, MOE_TILE, D_MODEL), BF16),
                            pltpu.VMEM((EXPERT_BUFFERS, MOE_TILE, D_MODEL), BF16),
                            pltpu.VMEM((D_MODEL, D_EXPERT), BF16),
                            pltpu.VMEM((D_MODEL, D_EXPERT), BF16),
                            pltpu.VMEM((D_EXPERT, D_MODEL), BF16),
                            pltpu.SemaphoreType.DMA((EXPERT_BUFFERS,)),
                            pltpu.SemaphoreType.DMA((EXPERT_BUFFERS,))]),
        out_shape=jax.ShapeDtypeStruct(xs.shape, BF16),
        compiler_params=_params("arbitrary"),
        name="moe_experts",
    )(first_tile, tiles_e, rows_e, total, xs, w_gate, w_up, w_down)


def _combine_kernel(tbl_ref, nch_ref, h1_ref, routet_ref, gf_ref, eo_hbm, o_ref, leo, sem,
                    *, tm, rl, final_norm):
    i = pl.program_id(0)
    n = pl.num_programs(0)
    slot = i % 2

    def gather(tile, s):
        for q in range(rl // RUN_ALIGN):
            row = jnp.where(q < nch_ref[tile], tbl_ref[tile * TABLE_LANES + q], q * RUN_ALIGN)
            pltpu.make_async_copy(eo_hbm.at[pl.ds(pl.multiple_of(row, RUN_ALIGN), RUN_ALIGN), :],
                                  leo.at[s, pl.ds(q * RUN_ALIGN, RUN_ALIGN), :], sem.at[s]).start()

    @pl.when(i == 0)
    def _():
        gather(0, 0)

    @pl.when(i + 1 < n)
    def _():
        gather(i + 1, 1 - slot)

    pltpu.make_async_copy(eo_hbm.at[pl.ds(0, rl), :], leo.at[slot], sem.at[slot]).wait()

    rt = routet_ref[...]
    pos1 = rt[:, 0:1].astype(jnp.int32)
    pos2 = rt[:, 1:2].astype(jnp.int32)
    lane = lax.broadcasted_iota(jnp.int32, (tm, rl), 1)
    weights = jnp.where(lane == pos1, rt[:, 2:3], 0.0) + jnp.where(lane == pos2, rt[:, 3:4], 0.0)
    y = jnp.dot(weights.astype(BF16), leo[slot], preferred_element_type=F32)
    out = h1_ref[...] + y
    if final_norm:
        out = _rms(out, gf_ref[...])
    o_ref[...] = out


def _combine(table, n_chunks, h1, routet, g_final, eo, tm, final_norm):
    n = h1.shape[0]
    rl = _local_rows(tm)
    return pl.pallas_call(
        functools.partial(_combine_kernel, tm=tm, rl=rl, final_norm=final_norm),
        grid_spec=pltpu.PrefetchScalarGridSpec(
            num_scalar_prefetch=2, grid=(n // tm,),
            in_specs=[pl.BlockSpec((tm, D_MODEL), lambda i, t, c: (i, 0)),
                      pl.BlockSpec((tm, LANES), lambda i, t, c: (i, 0)),
                      pl.BlockSpec((1, D_MODEL), lambda i, t, c: (0, 0)),
                      pl.BlockSpec(memory_space=pl.ANY)],
            out_specs=pl.BlockSpec((tm, D_MODEL), lambda i, t, c: (i, 0)),
            scratch_shapes=[pltpu.VMEM((2, rl, D_MODEL), BF16), pltpu.SemaphoreType.DMA((2,))]),
        out_shape=jax.ShapeDtypeStruct((n, D_MODEL), F32),
        compiler_params=_params("arbitrary"),
        name="moe_combine",
    )(table.reshape(-1), n_chunks, h1, routet, g_final, eo)


def _in_proj_sample_kernel(x_ref, g_ref, w_ref, o_ref):
    xn = _rms(x_ref[...], g_ref[...])
    o_ref[...] = jnp.dot(xn.astype(BF16), w_ref[...], preferred_element_type=F32)


def _in_proj_sample(x, g, w_bf16):
    n = x.shape[0]
    return pl.pallas_call(
        _in_proj_sample_kernel,
        out_shape=jax.ShapeDtypeStruct((n, PROJ_WIDTH), F32),
        compiler_params=pltpu.CompilerParams(vmem_limit_bytes=VMEM_LIMIT),
        name="in_proj_sample",
    )(x, g, w_bf16)


def _sample_mix_kernel(q_ref, kn_ref, vn_ref, kt_ref, vt_ref,
                       bc_ref, bn_ref, gb_ref, gc_ref, hc_ref, st_ref, cw_ref,
                       o_ref, conv_ref, st_out_ref, *, t_new):
    contract_last = (((1,), (1,)), ((), ()))
    no_rows = jnp.zeros((LANES - SUBLANES, HEAD_DIM), F32)
    for h in range(N_HEADS):
        q = (q_ref[0, h] * SCALE).astype(BF16)
        s_c = jnp.dot(q, kt_ref[0, 0, h].astype(BF16), preferred_element_type=F32) + bc_ref[h]
        kn = jnp.concatenate([kn_ref[0, h], no_rows], axis=0).astype(BF16)
        vn = jnp.concatenate([vn_ref[0, h], no_rows], axis=0).astype(BF16)
        s_n = lax.dot_general(q, kn, contract_last, preferred_element_type=F32) + bn_ref[h]
        m = jnp.maximum(jnp.max(s_c, axis=-1, keepdims=True), jnp.max(s_n, axis=-1, keepdims=True))
        p_c = jnp.exp(s_c - m)
        p_n = jnp.exp(s_n - m)
        den = jnp.sum(p_c, axis=-1, keepdims=True) + jnp.sum(p_n, axis=-1, keepdims=True)
        num = (lax.dot_general(p_c.astype(BF16), vt_ref[0, 0, h].astype(BF16), contract_last,
                               preferred_element_type=F32)
               + jnp.dot(p_n.astype(BF16), vn, preferred_element_type=F32))
        out = num / den
        lse = m + jnp.log(den)
        ls = [lse[i * t_new:(i + 1) * t_new] for i in range(len(DILATIONS))]
        os_ = [out[i * t_new:(i + 1) * t_new] for i in range(len(DILATIONS))]
        mm = jnp.maximum(jnp.maximum(ls[0], ls[1]), ls[2])
        w = [jnp.exp(l - mm) for l in ls]
        o_ref[0, h] = (w[0] * os_[0] + w[1] * os_[1] + w[2] * os_[2]) / (w[0] + w[1] + w[2])
    cw = cw_ref[...]
    u = gc_ref[0] * hc_ref[0]
    st = st_ref[0]
    rows = [st[j:j + 1] for j in range(CONV_K - 1)] + [u[t:t + 1] for t in range(t_new)]
    gb = gb_ref[0]
    conv_ref[0] = jnp.concatenate(
        [gb[t:t + 1] * sum(cw[j:j + 1] * rows[t + j] for j in range(CONV_K)) for t in range(t_new)],
        axis=0)
    st_out_ref[0] = jnp.concatenate(rows[t_new:], axis=0)


SAMPLE_ROWS = 16


def _sample_bias_tables(step_bias, t_new, w_buf):
    cache_rows = []
    for b, d in zip(step_bias, DILATIONS):
        back = b[1:][::-1].T
        if d >= t_new:
            residue = np.eye(d, dtype=bool)[:t_new]
            span = jnp.where(residue[None, :, None, :], back[:, None, :, None], NEG)
            span = span.reshape(N_HEADS, t_new, SEG_KEYS * d)
        else:
            assert d == 1, "dilations between 1 and the number of new tokens are not supported"
            span = jnp.stack([jnp.pad(back[:, :SEG_KEYS - t], ((0, 0), (t, 0)), constant_values=NEG)
                              for t in range(t_new)], axis=1)
        cache_rows.append(jnp.pad(span, ((0, 0), (0, 0), (w_buf - SEG_KEYS * d, 0)),
                                  constant_values=NEG))
    n_rows = len(DILATIONS) * t_new
    bias_c = jnp.pad(jnp.concatenate(cache_rows, axis=1), ((0, 0), (0, SAMPLE_ROWS - n_rows), (0, 0)))

    rows, cols, which, steps = [], [], [], []
    for di, d in enumerate(DILATIONS):
        for t in range(t_new):
            for t2 in range(t + 1):
                if (t - t2) % d == 0:
                    rows.append(di * t_new + t)
                    cols.append(t2)
                    which.append(di)
                    steps.append((t - t2) // d)
    vals = jnp.stack(step_bias)[np.array(which), np.array(steps)]
    bias_n = jnp.full((N_HEADS, n_rows, LANES), NEG, F32).at[:, np.array(rows), np.array(cols)].set(vals.T)
    bias_n = jnp.pad(bias_n, ((0, 0), (0, SAMPLE_ROWS - n_rows), (0, 0)))
    return bias_c, bias_n


def _sample_mix(q, kn, vn, cache_k, cache_v, layer, bias_c, bias_n, gb, gc, hc, state, conv_w):
    db, t_new = q.shape[0], q.shape[1]
    w_buf = cache_k.shape[2]
    kt = cache_k.transpose(0, 1, 3, 4, 2)
    vt = cache_v.transpose(0, 1, 3, 4, 2)
    n_rows = len(DILATIONS) * t_new
    q_rows = jnp.pad(jnp.tile(q.transpose(0, 2, 1, 3), (1, 1, len(DILATIONS), 1)),
                     ((0, 0), (0, 0), (0, SAMPLE_ROWS - n_rows), (0, 0)))
    kn_pad = jnp.pad(kn.transpose(0, 2, 1, 3), ((0, 0), (0, 0), (0, SUBLANES - t_new), (0, 0)))
    vn_pad = jnp.pad(vn.transpose(0, 2, 1, 3), ((0, 0), (0, 0), (0, SUBLANES - t_new), (0, 0)))
    row = pl.BlockSpec((1, t_new, CONV_DIM), lambda b: (b, 0, 0))
    new_spec = pl.BlockSpec((1, N_HEADS, SUBLANES, HEAD_DIM), lambda b: (b, 0, 0, 0))
    win_spec = pl.BlockSpec((1, 1, N_HEADS, HEAD_DIM, w_buf), lambda b: (layer, b, 0, 0, 0))
    attn, conv, st = pl.pallas_call(
        functools.partial(_sample_mix_kernel, t_new=t_new),
        grid=(db,),
        in_specs=[pl.BlockSpec((1, N_HEADS, SAMPLE_ROWS, HEAD_DIM), lambda b: (b, 0, 0, 0)),
                  new_spec, new_spec, win_spec, win_spec,
                  pl.BlockSpec(bias_c.shape, lambda b: (0, 0, 0)),
                  pl.BlockSpec(bias_n.shape, lambda b: (0, 0, 0)),
                  row, row, row,
                  pl.BlockSpec((1, CONV_K - 1, CONV_DIM), lambda b: (b, 0, 0)),
                  pl.BlockSpec((CONV_K, CONV_DIM), lambda b: (0, 0))],
        out_specs=[pl.BlockSpec((1, N_HEADS, t_new, HEAD_DIM), lambda b: (b, 0, 0, 0)), row,
                   pl.BlockSpec((1, CONV_K - 1, CONV_DIM), lambda b: (b, 0, 0))],
        out_shape=[jax.ShapeDtypeStruct((db, N_HEADS, t_new, HEAD_DIM), F32),
                   jax.ShapeDtypeStruct((db, t_new, CONV_DIM), F32),
                   jax.ShapeDtypeStruct((db, CONV_K - 1, CONV_DIM), F32)],
        compiler_params=_params("arbitrary"),
        name="sample_mix",
    )(q_rows, kn_pad, vn_pad, kt, vt, bias_c, bias_n, gb, gc, hc, state, conv_w)
    return attn.transpose(0, 2, 1, 3), conv, st


def _to_slabs(x):
    n = x.shape[0]
    return x.reshape(n, N_SLABS, LANES).transpose(1, 0, 2)


def _window_from_slabs(x, batch, seq, w_keep):
    win = x.reshape(N_SLABS, batch, seq, LANES)[:, :, seq - w_keep:]
    return win.transpose(1, 2, 0, 3).reshape(batch, w_keep, N_HEADS, HEAD_DIM)


def kernel(x_prompt, x_sample, cache_k, cache_v, state_conv, rel_bias, norm_mix, norm_ffn,
           norm_final, w_in, conv_w, w_out, w_router_group, w_router_expert, w_gate, w_up,
           w_down):
    batch, seq, _ = x_prompt.shape
    db, t_new, _ = x_sample.shape
    depth = w_in.shape[0]
    w_keep = min(MAX_WINDOW, seq)

    hp = x_prompt.reshape(batch * seq, D_MODEL)
    hs = x_sample.reshape(db * t_new, D_MODEL)
    step_bias = _step_bias(rel_bias)
    bias_prompt = _prompt_bias_tables(step_bias)
    bias_c, bias_n = _sample_bias_tables(step_bias, t_new, cache_k.shape[2])
    g_final = norm_final.reshape(1, D_MODEL)

    tm_p = 256
    n_s = db * t_new
    nt_p = batch * seq // tm_p
    pad_per_tile = N_EXPERTS * (RUN_ALIGN - 1)
    max_tiles = -(-(2 * (batch * seq + n_s) + (nt_p + 1) * pad_per_tile) // MOE_TILE) + N_EXPERTS
    buf_rows = max_tiles * MOE_TILE + _local_rows(tm_p)

    pk, pv, pc, sk, sv, sc = [], [], [], [], [], []
    for l in range(depth):
        w_in_b = w_in[l].astype(BF16)
        w_out_b = w_out[l].astype(BF16)
        w_router_t = jnp.pad(jnp.concatenate([w_router_expert[l], w_router_group[l]], axis=1).T,
                             ((0, ROUTER_ROWS - N_EXPERTS - N_GROUPS), (0, 0)))
        g_mix = norm_mix[l].reshape(1, D_MODEL)
        g_ffn = norm_ffn[l].reshape(1, D_MODEL)
        last = l == depth - 1

        q, k, v, conv, u_last = _in_proj(hp, g_mix, w_in_b, conv_w[l], seq)
        attn = _attention_prompt(q, k, v, bias_prompt, batch, seq)
        h1, xn, route, routet, cnt = _out_proj(attn, conv, hp, w_out_b, g_ffn, w_router_t,
                                               tm=4 * tm_p, sub=tm_p)
        pk.append(_window_from_slabs(k, batch, seq, w_keep))
        pv.append(_window_from_slabs(v, batch, seq, w_keep))
        pc.append(u_last)

        proj = _in_proj_sample(hs, g_mix, w_in_b)
        qs = proj[:, :ATTN_WIDTH].reshape(db, t_new, N_HEADS, HEAD_DIM)
        ks = proj[:, ATTN_WIDTH:2 * ATTN_WIDTH].reshape(db, t_new, N_HEADS, HEAD_DIM)
        vs = proj[:, 2 * ATTN_WIDTH:3 * ATTN_WIDTH].reshape(db, t_new, N_HEADS, HEAD_DIM)
        c0 = 3 * ATTN_WIDTH
        gb = proj[:, c0:c0 + CONV_DIM].reshape(db, t_new, CONV_DIM)
        gc = proj[:, c0 + CONV_DIM:c0 + 2 * CONV_DIM].reshape(db, t_new, CONV_DIM)
        hc = proj[:, c0 + 2 * CONV_DIM:].reshape(db, t_new, CONV_DIM)
        attn_s, conv_s, state_s = _sample_mix(qs, ks, vs, cache_k, cache_v, l, bias_c, bias_n,
                                              gb, gc, hc, state_conv[l], conv_w[l])
        h1s, xns, route_s, routet_s, cnt_s = _out_proj(
            _to_slabs(attn_s.reshape(n_s, ATTN_WIDTH)), conv_s.reshape(n_s, CONV_DIM), hs,
            w_out_b, g_ffn, w_router_t, tm=n_s, sub=n_s)

        table, n_chunks, first_tile, tiles_e, rows_e, total = _moe_plan(
            jnp.concatenate([cnt[:, :, 0], cnt_s[:, :, 0]], axis=0), max_tiles)
        xs = _dispatch(table[:nt_p], xn, route, tm_p, buf_rows)
        xs = _dispatch(table[nt_p:], xns, route_s, n_s, buf_rows, xs_prev=xs)
        eo = _experts(first_tile, tiles_e, rows_e, total, xs, w_gate, w_up, w_down, l)
        hp = _combine(table[:nt_p], n_chunks[:nt_p], h1, routet, g_final, eo, tm_p, last)
        hs = _combine(table[nt_p:], n_chunks[nt_p:], h1s, routet_s, g_final, eo, n_s, last)
        sk.append(ks)
        sv.append(vs)
        sc.append(state_s)

    return (hp.reshape(batch, seq, D_MODEL), hs.reshape(db, t_new, D_MODEL),
            jnp.stack(pk), jnp.stack(pv), jnp.stack(pc),
            jnp.stack(sk), jnp.stack(sv), jnp.stack(sc))
```

```python
import functools
import math

import jax
import jax.numpy as jnp
import numpy as np
from jax import lax
from jax.experimental import pallas as pl
from jax.experimental.pallas import tpu as pltpu

F32 = jnp.float32
BF16 = jnp.bfloat16

D_MODEL = 1024
N_HEADS = 8
HEAD_DIM = 64
ATTN_WIDTH = N_HEADS * HEAD_DIM
CONV_DIM = D_MODEL - ATTN_WIDTH
CONV_K = 3
PROJ_WIDTH = 3 * ATTN_WIDTH + 3 * CONV_DIM
DILATIONS = (1, 4, 16)
SEG_KEYS = 128
Q_BLOCK = 128
MAX_WINDOW = 2048
N_BUCKETS = 32
MAX_DISTANCE = 2048
N_GROUPS = 4
EXPERTS_PER_GROUP = 8
N_EXPERTS = N_GROUPS * EXPERTS_PER_GROUP
D_EXPERT = D_MODEL // 4
EPS = 1e-6
NEG = -1e30
SCALE = 1.0 / math.sqrt(HEAD_DIM)

LANES = 128
SUBLANES = 8
N_SLABS = ATTN_WIDTH // LANES
VMEM_LIMIT = 48 * 1024 * 1024
REGROUP = 4
UNITS_PER_TRIP = 16


def _params(*sem):
    return pltpu.CompilerParams(dimension_semantics=sem, vmem_limit_bytes=VMEM_LIMIT)


def _rms(x, g):
    return x * lax.rsqrt(jnp.mean(x * x, axis=-1, keepdims=True) + EPS) * g


def _rel_bucket(dist):
    max_exact = N_BUCKETS // 2
    d_f = jnp.maximum(dist, 1).astype(F32)
    large = max_exact + (jnp.log(d_f / max_exact) / math.log(MAX_DISTANCE / max_exact)
                         * (N_BUCKETS - max_exact)).astype(jnp.int32)
    large = jnp.minimum(large, N_BUCKETS - 1)
    return jnp.where(dist < max_exact, dist, large)


def _in_proj_kernel(x_ref, g_ref, w_ref, cw_ref, q_ref, k_ref, v_ref, conv_ref, ulast_ref,
                    ext_ref, *, tm, tiles_per_seq):
    i = pl.program_id(0)
    xn = _rms(x_ref[...], g_ref[...])
    proj = jnp.dot(xn.astype(BF16), w_ref[...], preferred_element_type=F32)
    for s in range(N_SLABS):
        q_ref[s] = proj[:, s * LANES:(s + 1) * LANES]
        k_ref[s] = proj[:, ATTN_WIDTH + s * LANES:ATTN_WIDTH + (s + 1) * LANES]
        v_ref[s] = proj[:, 2 * ATTN_WIDTH + s * LANES:2 * ATTN_WIDTH + (s + 1) * LANES]
    c0 = 3 * ATTN_WIDTH
    gb = proj[:, c0:c0 + CONV_DIM]
    u = proj[:, c0 + CONV_DIM:c0 + 2 * CONV_DIM] * proj[:, c0 + 2 * CONV_DIM:c0 + 3 * CONV_DIM]

    @pl.when(i % tiles_per_seq == 0)
    def _():
        ext_ref[0:SUBLANES, :] = jnp.zeros((SUBLANES, CONV_DIM), F32)

    ext_ref[SUBLANES:SUBLANES + tm, :] = u
    u1 = ext_ref[SUBLANES - 1:SUBLANES - 1 + tm, :]
    u2 = ext_ref[SUBLANES - 2:SUBLANES - 2 + tm, :]
    cw = cw_ref[...]
    conv_ref[...] = gb * (cw[0:1] * u2 + cw[1:2] * u1 + cw[2:3] * u)
    ulast_ref[0] = ext_ref[tm + SUBLANES - 2:tm + SUBLANES, :]
    ext_ref[0:SUBLANES, :] = ext_ref[tm:tm + SUBLANES, :]


def _in_proj(x, g, w_bf16, conv_w, seq_len, tm=1024):
    n = x.shape[0]
    n_seq = n // seq_len
    slab = jax.ShapeDtypeStruct((N_SLABS, n, LANES), F32)
    slab_spec = pl.BlockSpec((N_SLABS, tm, LANES), lambda i: (0, i, 0))
    tiles_per_seq = seq_len // tm
    return pl.pallas_call(
        functools.partial(_in_proj_kernel, tm=tm, tiles_per_seq=tiles_per_seq),
        grid=(n // tm,),
        in_specs=[pl.BlockSpec((tm, D_MODEL), lambda i: (i, 0)),
                  pl.BlockSpec((1, D_MODEL), lambda i: (0, 0)),
                  pl.BlockSpec((D_MODEL, PROJ_WIDTH), lambda i: (0, 0)),
                  pl.BlockSpec((CONV_K, CONV_DIM), lambda i: (0, 0))],
        out_specs=[slab_spec, slab_spec, slab_spec,
                   pl.BlockSpec((tm, CONV_DIM), lambda i: (i, 0)),
                   pl.BlockSpec((1, CONV_K - 1, CONV_DIM), lambda i: (i // tiles_per_seq, 0, 0))],
        out_shape=[slab, slab, slab,
                   jax.ShapeDtypeStruct((n, CONV_DIM), F32),
                   jax.ShapeDtypeStruct((n_seq, CONV_K - 1, CONV_DIM), F32)],
        scratch_shapes=[pltpu.VMEM((tm + SUBLANES, CONV_DIM), F32)],
        compiler_params=_params("arbitrary"),
        name="in_proj_prompt",
    )(x, g, w_bf16, conv_w)


def _attn_kernel(q_ref, k_ref, v_ref, bias_ref, o_ref, q4, k4, v4, ktail, vtail, oacc, lacc, *, sb):
    j = pl.program_id(2)
    slot = j % 2
    prev = 1 - slot
    seg = sb // REGROUP

    @pl.when(j == 0)
    def _():
        k4[1] = jnp.zeros((sb, LANES), F32)
        v4[1] = jnp.zeros((sb, LANES), F32)
        ktail[...] = jnp.zeros((Q_BLOCK, LANES), F32)
        vtail[...] = jnp.zeros((Q_BLOCK, LANES), F32)

    for r in range(REGROUP):
        q4[r * seg:(r + 1) * seg, :] = q_ref[pl.ds(r, seg, stride=REGROUP), :] * SCALE
        k4[slot, r * seg:(r + 1) * seg, :] = k_ref[pl.ds(r, seg, stride=REGROUP), :]
        v4[slot, r * seg:(r + 1) * seg, :] = v_ref[pl.ds(r, seg, stride=REGROUP), :]

    lane = lax.broadcasted_iota(jnp.int32, (Q_BLOCK, LANES), 1)
    low = lane < HEAD_DIM
    ones = jnp.ones((2 * Q_BLOCK, LANES), BF16)

    def block(start):
        return pl.ds(pl.multiple_of(start, Q_BLOCK), Q_BLOCK)

    for di, d in enumerate(DILATIONS):
        nq = sb // (Q_BLOCK * d)

        def unit(u, di=di, d=d, nq=nq):
            r = u // nq
            n = u % nq
            qstart = r + d * Q_BLOCK * n
            if d == 1:
                here = block(n * Q_BLOCK)
                before = block(jnp.maximum(n - 1, 0) * Q_BLOCK)
                q = q_ref[here, :] * SCALE
                k_new, v_new = k_ref[here, :], v_ref[here, :]
                k_old = jnp.where(n == 0, ktail[...], k_ref[before, :])
                v_old = jnp.where(n == 0, vtail[...], v_ref[before, :])
            elif d == REGROUP:
                here = block(r * seg + n * Q_BLOCK)
                before = block(r * seg + jnp.maximum(n - 1, 0) * Q_BLOCK)
                last = block(r * seg + seg - Q_BLOCK)
                q = q4[here, :]
                k_new, v_new = k4[slot, here, :], v4[slot, here, :]
                k_old = jnp.where(n == 0, k4[prev, last, :], k4[slot, before, :])
                v_old = jnp.where(n == 0, v4[prev, last, :], v4[slot, before, :])
            else:
                rows = pl.ds((u % REGROUP) * seg + u // REGROUP, Q_BLOCK, stride=REGROUP)
                q = q4[rows, :]
                k_new, v_new = k4[slot, rows, :], v4[slot, rows, :]
                k_old, v_old = k4[prev, rows, :], v4[prev, rows, :]
            kk = jnp.concatenate([k_old, k_new], axis=0).astype(BF16)
            vv = jnp.concatenate([v_old, v_new], axis=0).astype(BF16)
            qm = jnp.concatenate([jnp.where(low, q, 0.0), jnp.where(low, 0.0, q)],
                                 axis=0).astype(BF16)
            s = lax.dot_general(qm, kk, (((1,), (1,)), ((), ())),
                                preferred_element_type=F32)
            first = jnp.logical_and(j == 0, n == 0).astype(jnp.int32)
            s = s + bias_ref[di, first]
            m = jnp.max(s, axis=-1, keepdims=True)
            p = jnp.exp(s - m).astype(BF16)
            pv = jnp.dot(p, jnp.concatenate([vv, ones], axis=1),
                         preferred_element_type=F32)
            o_sel = jnp.where(low, pv[0:Q_BLOCK, 0:LANES], pv[Q_BLOCK:, 0:LANES])
            l_sel = jnp.where(low, pv[0:Q_BLOCK, LANES:], pv[Q_BLOCK:, LANES:])
            m_sel = jnp.where(low, jnp.broadcast_to(m[0:Q_BLOCK], (Q_BLOCK, LANES)),
                              jnp.broadcast_to(m[Q_BLOCK:], (Q_BLOCK, LANES)))
            oacc[di, pl.ds(qstart, Q_BLOCK, stride=d), :] = o_sel / l_sel
            lacc[di, pl.ds(qstart, Q_BLOCK, stride=d), :] = m_sel + jnp.log(l_sel)

        def group(g, carry, unit=unit):
            for uu in range(UNITS_PER_TRIP):
                unit(g * UNITS_PER_TRIP + uu)
            return carry

        lax.fori_loop(0, sb // Q_BLOCK // UNITS_PER_TRIP, group, 0)

    for c in range(sb // 256):
        rows = slice(c * 256, (c + 1) * 256)
        l0, l1, l2 = lacc[0, rows, :], lacc[1, rows, :], lacc[2, rows, :]
        mm = jnp.maximum(jnp.maximum(l0, l1), l2)
        w0, w1, w2 = jnp.exp(l0 - mm), jnp.exp(l1 - mm), jnp.exp(l2 - mm)
        num = w0 * oacc[0, rows, :] + w1 * oacc[1, rows, :] + w2 * oacc[2, rows, :]
        o_ref[rows, :] = num / (w0 + w1 + w2)
    ktail[...] = k_ref[sb - Q_BLOCK:sb, :]
    vtail[...] = v_ref[sb - Q_BLOCK:sb, :]


def _step_bias(rel_bias):
    steps = jnp.arange(SEG_KEYS + 1)
    return [rel_bias[_rel_bucket(steps * d)].astype(F32) for d in DILATIONS]


def _prompt_bias_tables(step_bias):
    width = 3 * Q_BLOCK
    k_loc = jnp.arange(2 * Q_BLOCK)
    tables = []
    for b in step_bias:
        g = jnp.concatenate([b[::-1].T, jnp.full((N_HEADS, width - SEG_KEYS), NEG, F32)], axis=1)
        flat = jnp.broadcast_to(g[:, None, :], (N_HEADS, Q_BLOCK, width + 1)).reshape(N_HEADS, -1)
        x = flat[:, :Q_BLOCK * width].reshape(N_HEADS, Q_BLOCK, width)
        t = x[:, :, :2 * Q_BLOCK]
        t_first = jnp.where(k_loc[None, None, :] >= Q_BLOCK, t, NEG)
        tables.append(jnp.stack([t.reshape(N_SLABS, 2 * Q_BLOCK, 2 * Q_BLOCK),
                                 t_first.reshape(N_SLABS, 2 * Q_BLOCK, 2 * Q_BLOCK)]))
    return jnp.stack(tables)


def _attention_prompt(q, k, v, bias_tab, n_seq, seq_len, sb=2048):
    assert DILATIONS == (1, REGROUP, REGROUP * REGROUP)
    n = q.shape[1]
    nsb = seq_len // sb
    blk = pl.BlockSpec((None, sb, LANES), lambda b, s, j: (s, b * nsb + j, 0))
    return pl.pallas_call(
        functools.partial(_attn_kernel, sb=sb),
        grid=(n_seq, N_SLABS, nsb),
        in_specs=[blk, blk, blk,
                  pl.BlockSpec((len(DILATIONS), 2, None, 2 * Q_BLOCK, 2 * Q_BLOCK),
                               lambda b, s, j: (0, 0, s, 0, 0))],
        out_specs=blk,
        out_shape=jax.ShapeDtypeStruct((N_SLABS, n, LANES), F32),
        scratch_shapes=[pltpu.VMEM((sb, LANES), F32),
                        pltpu.VMEM((2, sb, LANES), F32), pltpu.VMEM((2, sb, LANES), F32),
                        pltpu.VMEM((Q_BLOCK, LANES), F32), pltpu.VMEM((Q_BLOCK, LANES), F32),
                        pltpu.VMEM((len(DILATIONS), sb, LANES), F32),
                        pltpu.VMEM((len(DILATIONS), sb, LANES), F32)],
        compiler_params=_params("arbitrary", "arbitrary", "arbitrary"),
        name="attention_prompt",
    )(q, k, v, bias_tab)


ROUTER_ROWS = 48
MOE_TILE = 256
RUN_ALIGN = 2 * SUBLANES
TABLE_LANES = 128
EXPERT_BUFFERS = 6


def _local_rows(tm):
    return -(-(2 * tm + N_EXPERTS * (RUN_ALIGN - 1)) // LANES) * LANES


def _route(lg):
    tm = lg.shape[1]
    gl = lg[N_EXPERTS:N_EXPERTS + N_GROUPS]
    grow = lax.broadcasted_iota(jnp.int32, (N_GROUPS, tm), 0)
    gmax = jnp.max(gl, axis=0, keepdims=True)
    gsum = jnp.sum(jnp.exp(gl - gmax), axis=0, keepdims=True)
    gidx = jnp.min(jnp.where(gl == gmax, grow, N_GROUPS), axis=0, keepdims=True)
    p_grp = 1.0 / gsum
    el = lg[0:N_EXPERTS]
    e = lax.broadcasted_iota(jnp.int32, (N_EXPERTS, tm), 0)
    in_grp = (e // EXPERTS_PER_GROUP) == gidx
    elog = jnp.where(in_grp, el, -jnp.inf)
    emax = jnp.max(elog, axis=0, keepdims=True)
    eexp = jnp.exp(elog - emax)
    eprob = jnp.where(in_grp, eexp / jnp.sum(eexp, axis=0, keepdims=True), -1.0)
    p1 = jnp.max(eprob, axis=0, keepdims=True)
    i1 = jnp.min(jnp.where(eprob == p1, e, N_EXPERTS), axis=0, keepdims=True)
    rest = jnp.where(e == i1, -1.0, eprob)
    p2 = jnp.max(rest, axis=0, keepdims=True)
    i2 = jnp.min(jnp.where(rest == p2, e, N_EXPERTS), axis=0, keepdims=True)
    tot = p1 + p2
    return e == i1, e == i2, p_grp * (p1 / tot), p_grp * (p2 / tot)


def _split_bf16(x):
    hi = x.astype(BF16)
    return hi, (x - hi.astype(F32)).astype(BF16)


def _out_proj_kernel(attn_ref, conv_ref, h_ref, wo_ref, g_ref, wr_ref, upper_ref, lower_ref,
                     h1_ref, xn_ref, route_ref, routet_ref, cnt_ref, *, tm, sub):
    contract_last = (((1,), (1,)), ((), ()))
    mix = jnp.concatenate([attn_ref[s] for s in range(N_SLABS)] + [conv_ref[...]], axis=1)
    h1_ref[...] = h_ref[...] + jnp.dot(mix.astype(BF16), wo_ref[...], preferred_element_type=F32)
    for t in range(tm // sub):
        rs = slice(t * sub, (t + 1) * sub)
        xn = _rms(h1_ref[rs, :], g_ref[...])
        xh, xl = _split_bf16(xn)
        xn_ref[rs, :] = xh
        both = lax.dot_general(wr_ref[...], xh, contract_last, preferred_element_type=F32)
        lg = (both[:ROUTER_ROWS] + both[ROUTER_ROWS:]
              + lax.dot_general(wr_ref[:ROUTER_ROWS, :], xl, contract_last, preferred_element_type=F32))
        sel1, sel2, g1, g2 = _route(lg)
        chosen = jnp.where(sel1 | sel2, 1.0, 0.0)
        rank = jnp.dot(chosen.astype(BF16), upper_ref[...], preferred_element_type=F32)
        cnt = jnp.sum(chosen, axis=1, keepdims=True)
        cnt_pad = jnp.floor((cnt + (RUN_ALIGN - 1)) * (1.0 / RUN_ALIGN)) * RUN_ALIGN
        cnt_b = jnp.broadcast_to(cnt_pad, (N_EXPERTS, LANES))
        start = jnp.dot(lower_ref[...], cnt_b.astype(BF16), preferred_element_type=F32)
        pos = start[:, 0:1] + rank
        pos1 = jnp.sum(jnp.where(sel1, pos, 0.0), axis=0, keepdims=True)
        pos2 = jnp.sum(jnp.where(sel2, pos, 0.0), axis=0, keepdims=True)
        rows = jnp.concatenate([pos1, pos2, g1, g2, jnp.zeros((SUBLANES - 4, sub), F32)], axis=0)
        route_ref[t] = rows
        routet_ref[rs, :] = jnp.concatenate([rows, jnp.zeros((LANES - SUBLANES, sub), F32)], axis=0).T
        cnt_ref[t] = cnt_b


def _out_proj(attn, conv, h, wo_bf16, g, w_router_t, tm, sub):
    n = h.shape[0]
    w_router = jnp.concatenate(_split_bf16(w_router_t), axis=0)
    idx = jnp.arange(sub)
    upper = (idx[:, None] < idx[None, :]).astype(BF16)
    eidx = jnp.arange(N_EXPERTS)
    lower = (eidx[None, :] < eidx[:, None]).astype(BF16)
    const = lambda shape: pl.BlockSpec(shape, lambda i: (0,) * len(shape))
    per_step = tm // sub
    return pl.pallas_call(
        functools.partial(_out_proj_kernel, tm=tm, sub=sub),
        grid=(n // tm,),
        in_specs=[pl.BlockSpec((N_SLABS, tm, LANES), lambda i: (0, i, 0)),
                  pl.BlockSpec((tm, CONV_DIM), lambda i: (i, 0)),
                  pl.BlockSpec((tm, D_MODEL), lambda i: (i, 0)),
                  const((D_MODEL, D_MODEL)), const((1, D_MODEL)),
                  const((2 * ROUTER_ROWS, D_MODEL)),
                  const((sub, sub)), const((N_EXPERTS, N_EXPERTS))],
        out_specs=[pl.BlockSpec((tm, D_MODEL), lambda i: (i, 0)),
                   pl.BlockSpec((tm, D_MODEL), lambda i: (i, 0)),
                   pl.BlockSpec((per_step, SUBLANES, sub), lambda i: (i, 0, 0)),
                   pl.BlockSpec((tm, LANES), lambda i: (i, 0)),
                   pl.BlockSpec((per_step, N_EXPERTS, LANES), lambda i: (i, 0, 0))],
        out_shape=[jax.ShapeDtypeStruct((n, D_MODEL), F32),
                   jax.ShapeDtypeStruct((n, D_MODEL), BF16),
                   jax.ShapeDtypeStruct((n // sub, SUBLANES, sub), F32),
                   jax.ShapeDtypeStruct((n, LANES), F32),
                   jax.ShapeDtypeStruct((n // sub, N_EXPERTS, LANES), F32)],
        compiler_params=_params("arbitrary"),
        name="out_proj",
    )(attn, conv, h, wo_bf16, g, w_router, upper, lower)


def _moe_plan(cnt_tiles, max_tiles):
    c = cnt_tiles.astype(jnp.int32)
    rows_e = jnp.sum(c, axis=0)
    tiles_e = (rows_e + MOE_TILE - 1) // MOE_TILE
    cum = jnp.cumsum(tiles_e)
    total = cum[-1]
    first_tile = cum - tiles_e

    run_start = jnp.cumsum(c, axis=0) - c
    local_start = jnp.cumsum(c, axis=1) - c
    local_end = local_start + c
    row = jnp.arange(TABLE_LANES, dtype=jnp.int32) * RUN_ALIGN
    inside = (local_start[:, None, :] <= row[None, :, None]) & (row[None, :, None] < local_end[:, None, :])
    base = first_tile[None] * MOE_TILE + run_start - local_start
    n_chunks = jnp.sum(c, axis=1) // RUN_ALIGN
    table = jnp.sum(jnp.where(inside, base[:, None, :] + row[None, :, None], 0), axis=-1)
    live = jnp.arange(TABLE_LANES)[None] < n_chunks[:, None]
    table = jnp.where(live, table, max_tiles * MOE_TILE + row[None])
    i32 = lambda a: a.astype(jnp.int32)
    return (i32(table), i32(n_chunks), i32(first_tile), i32(tiles_e), i32(rows_e),
            i32(total).reshape(1))


def _dispatch_kernel(tbl_ref, xn_ref, route_ref, *rest, tm, rl, chained):
    xs_hbm, lsort, sem = rest[1:] if chained else rest
    i = pl.program_id(0)
    slot = i % 2
    rows = route_ref[0]
    pos1 = rows[0:1].astype(jnp.int32)
    pos2 = rows[1:2].astype(jnp.int32)
    p = lax.broadcasted_iota(jnp.int32, (rl, tm), 0)
    one_hot = jnp.where((p == pos1) | (p == pos2), 1.0, 0.0).astype(BF16)
    lsort[slot] = jnp.dot(one_hot, xn_ref[...], preferred_element_type=F32).astype(BF16)

    def tile_copies_done(s):
        pltpu.make_async_copy(lsort.at[s], xs_hbm.at[pl.ds(0, rl), :], sem.at[s]).wait()

    @pl.when(i > 0)
    def _():
        tile_copies_done(1 - slot)

    for q in range(rl // RUN_ALIGN):
        row = pl.multiple_of(tbl_ref[i * TABLE_LANES + q], RUN_ALIGN)
        pltpu.make_async_copy(lsort.at[slot, pl.ds(q * RUN_ALIGN, RUN_ALIGN), :],
                              xs_hbm.at[pl.ds(row, RUN_ALIGN), :], sem.at[slot]).start()

    @pl.when(i == pl.num_programs(0) - 1)
    def _():
        tile_copies_done(slot)


def _dispatch(table, xn, route, tm, total_rows, xs_prev=None):
    nt = xn.shape[0] // tm
    rl = _local_rows(tm)
    chained = xs_prev is not None
    in_specs = [pl.BlockSpec((tm, D_MODEL), lambda i, t: (i, 0)),
                pl.BlockSpec((1, SUBLANES, tm), lambda i, t: (i, 0, 0))]
    args = [table.reshape(-1), xn, route]
    if chained:
        in_specs.append(pl.BlockSpec(memory_space=pl.ANY))
        args.append(xs_prev)
    return pl.pallas_call(
        functools.partial(_dispatch_kernel, tm=tm, rl=rl, chained=chained),
        grid_spec=pltpu.PrefetchScalarGridSpec(
            num_scalar_prefetch=1, grid=(nt,), in_specs=in_specs,
            out_specs=pl.BlockSpec(memory_space=pl.ANY),
            scratch_shapes=[pltpu.VMEM((2, rl, D_MODEL), BF16), pltpu.SemaphoreType.DMA((2,))]),
        out_shape=jax.ShapeDtypeStruct((total_rows, D_MODEL), BF16),
        input_output_aliases={3: 0} if chained else {},
        compiler_params=_params("arbitrary"),
        name="moe_dispatch",
    )(*args)


def _expert_kernel(first_ref, tiles_ref, rows_ref, tot_ref, xs_hbm, wg_ref, wu_ref, wd_ref, eo_hbm,
                   xbuf, obuf, wg_b, wu_b, wd_b, sem_in, sem_out):
    e = pl.program_id(0)
    first_tile = first_ref[e]
    n_tiles = tiles_ref[e]
    total = tot_ref[0]

    def in_copy(g):
        return pltpu.make_async_copy(
            xs_hbm.at[pl.ds(pl.multiple_of(g * MOE_TILE, MOE_TILE), MOE_TILE), :],
            xbuf.at[g % EXPERT_BUFFERS], sem_in.at[g % EXPERT_BUFFERS])

    def out_copy(g):
        return pltpu.make_async_copy(
            obuf.at[g % EXPERT_BUFFERS],
            eo_hbm.at[pl.ds(pl.multiple_of(g * MOE_TILE, MOE_TILE), MOE_TILE), :],
            sem_out.at[g % EXPERT_BUFFERS])

    lookahead = EXPERT_BUFFERS - 2

    @pl.when(e == 0)
    def _():
        for g0 in range(lookahead):
            @pl.when(g0 < total)
            def _():
                in_copy(g0).start()

    @pl.when(n_tiles > 0)
    def _():
        wg_b[...] = wg_ref[0, 0].astype(BF16)
        wu_b[...] = wu_ref[0, 0].astype(BF16)
        wd_b[...] = wd_ref[0, 0].astype(BF16)

    def acquire(g, j):
        @pl.when(g + lookahead < total)
        def _():
            in_copy(g + lookahead).start()

        in_copy(g).wait()

        @pl.when(g >= EXPERT_BUFFERS)
        def _():
            out_copy(g - EXPERT_BUFFERS).wait()

        row = lax.broadcasted_iota(jnp.int32, (MOE_TILE, 1), 0)
        x = xbuf[g % EXPERT_BUFFERS]
        return jnp.where(row < rows_ref[e] - j * MOE_TILE, x, jnp.zeros_like(x))

    def mlp(x):
        gate = jnp.dot(x, wg_b[...], preferred_element_type=F32)
        up = jnp.dot(x, wu_b[...], preferred_element_type=F32)
        hdn = gate * jax.nn.sigmoid(gate) * up
        return jnp.dot(hdn.astype(BF16), wd_b[...], preferred_element_type=F32).astype(BF16)

    def pair(jp, carry):
        j = 2 * jp
        g = first_tile + j
        x0 = acquire(g, j)
        x1 = acquire(g + 1, j + 1)
        y = mlp(jnp.concatenate([x0, x1], axis=0))
        obuf[g % EXPERT_BUFFERS] = y[:MOE_TILE]
        obuf[(g + 1) % EXPERT_BUFFERS] = y[MOE_TILE:]
        out_copy(g).start()
        out_copy(g + 1).start()
        return carry

    lax.fori_loop(0, n_tiles // 2, pair, 0)

    @pl.when(n_tiles % 2 == 1)
    def _():
        j = n_tiles - 1
        g = first_tile + j
        obuf[g % EXPERT_BUFFERS] = mlp(acquire(g, j))
        out_copy(g).start()

    @pl.when(e == pl.num_programs(0) - 1)
    def _():
        for back in range(EXPERT_BUFFERS, 0, -1):
            @pl.when(total >= back)
            def _():
                out_copy(total - back).wait()


def _experts(first_tile, tiles_e, rows_e, total, xs, w_gate, w_up, w_down, layer):
    weight = lambda e, f, t, r, n: (layer, e, 0, 0)
    return pl.pallas_call(
        _expert_kernel,
        grid_spec=pltpu.PrefetchScalarGridSpec(
            num_scalar_prefetch=4, grid=(N_EXPERTS,),
            in_specs=[pl.BlockSpec(memory_space=pl.ANY),
                      pl.BlockSpec((1, 1, D_MODEL, D_EXPERT), weight),
                      pl.BlockSpec((1, 1, D_MODEL, D_EXPERT), weight),
                      pl.BlockSpec((1, 1, D_EXPERT, D_MODEL), weight)],
            out_specs=pl.BlockSpec(memory_space=pl.ANY),
            scratch_shapes=[pltpu.VMEM((EXPERT_BUFFERS, MOE_TILE, D_MODEL), BF16),
                            pltpu.VMEM((EXPERT_BUFFERS, MOE_TILE, D_MODEL), BF16),
                            pltpu.VMEM((D_MODEL, D_EXPERT), BF16),
                            pltpu.VMEM((D_MODEL, D_EXPERT), BF16),
                            pltpu.VMEM((D_EXPERT, D_MODEL), BF16),
                            pltpu.SemaphoreType.DMA((EXPERT_BUFFERS,)),
                            pltpu.SemaphoreType.DMA((EXPERT_BUFFERS,))]),
        out_shape=jax.ShapeDtypeStruct(xs.shape, BF16),
        compiler_params=_params("arbitrary"),
        name="moe_experts",
    )(first_tile, tiles_e, rows_e, total, xs, w_gate, w_up, w_down)


def _combine_kernel(tbl_ref, nch_ref, h1_ref, routet_ref, gf_ref, eo_hbm, o_ref, leo, sem,
                    *, tm, rl, final_norm):
    i = pl.program_id(0)
    n = pl.num_programs(0)
    slot = i % 2

    def gather(tile, s):
        for q in range(rl // RUN_ALIGN):
            row = jnp.where(q < nch_ref[tile], tbl_ref[tile * TABLE_LANES + q], q * RUN_ALIGN)
            pltpu.make_async_copy(eo_hbm.at[pl.ds(pl.multiple_of(row, RUN_ALIGN), RUN_ALIGN), :],
                                  leo.at[s, pl.ds(q * RUN_ALIGN, RUN_ALIGN), :], sem.at[s]).start()

    @pl.when(i == 0)
    def _():
        gather(0, 0)

    @pl.when(i + 1 < n)
    def _():
        gather(i + 1, 1 - slot)

    pltpu.make_async_copy(eo_hbm.at[pl.ds(0, rl), :], leo.at[slot], sem.at[slot]).wait()

    rt = routet_ref[...]
    pos1 = rt[:, 0:1].astype(jnp.int32)
    pos2 = rt[:, 1:2].astype(jnp.int32)
    lane = lax.broadcasted_iota(jnp.int32, (tm, rl), 1)
    weights = jnp.where(lane == pos1, rt[:, 2:3], 0.0) + jnp.where(lane == pos2, rt[:, 3:4], 0.0)
    y = jnp.dot(weights.astype(BF16), leo[slot], preferred_element_type=F32)
    out = h1_ref[...] + y
    if final_norm:
        out = _rms(out, gf_ref[...])
    o_ref[...] = out


def _combine(table, n_chunks, h1, routet, g_final, eo, tm, final_norm):
    n = h1.shape[0]
    rl = _local_rows(tm)
    return pl.pallas_call(
        functools.partial(_combine_kernel, tm=tm, rl=rl, final_norm=final_norm),
        grid_spec=pltpu.PrefetchScalarGridSpec(
            num_scalar_prefetch=2, grid=(n // tm,),
            in_specs=[pl.BlockSpec((tm, D_MODEL), lambda i, t, c: (i, 0)),
                      pl.BlockSpec((tm, LANES), lambda i, t, c: (i, 0)),
                      pl.BlockSpec((1, D_MODEL), lambda i, t, c: (0, 0)),
                      pl.BlockSpec(memory_space=pl.ANY)],
            out_specs=pl.BlockSpec((tm, D_MODEL), lambda i, t, c: (i, 0)),
            scratch_shapes=[pltpu.VMEM((2, rl, D_MODEL), BF16), pltpu.SemaphoreType.DMA((2,))]),
        out_shape=jax.ShapeDtypeStruct((n, D_MODEL), F32),
        compiler_params=_params("arbitrary"),
        name="moe_combine",
    )(table.reshape(-1), n_chunks, h1, routet, g_final, eo)


def _in_proj_sample_kernel(x_ref, g_ref, w_ref, o_ref):
    xn = _rms(x_ref[...], g_ref[...])
    o_ref[...] = jnp.dot(xn.astype(BF16), w_ref[...], preferred_element_type=F32)


def _in_proj_sample(x, g, w_bf16):
    n = x.shape[0]
    return pl.pallas_call(
        _in_proj_sample_kernel,
        out_shape=jax.ShapeDtypeStruct((n, PROJ_WIDTH), F32),
        compiler_params=pltpu.CompilerParams(vmem_limit_bytes=VMEM_LIMIT),
        name="in_proj_sample",
    )(x, g, w_bf16)


def _sample_mix_kernel(q_ref, kn_ref, vn_ref, kt_ref, vt_ref,
                       bc_ref, bn_ref, gb_ref, gc_ref, hc_ref, st_ref, cw_ref,
                       o_ref, conv_ref, st_out_ref, *, t_new):
    contract_last = (((1,), (1,)), ((), ()))
    no_rows = jnp.zeros((LANES - SUBLANES, HEAD_DIM), F32)
    for h in range(N_HEADS):
        q = (q_ref[0, h] * SCALE).astype(BF16)
        s_c = jnp.dot(q, kt_ref[0, 0, h].astype(BF16), preferred_element_type=F32) + bc_ref[h]
        kn = jnp.concatenate([kn_ref[0, h], no_rows], axis=0).astype(BF16)
        vn = jnp.concatenate([vn_ref[0, h], no_rows], axis=0).astype(BF16)
        s_n = lax.dot_general(q, kn, contract_last, preferred_element_type=F32) + bn_ref[h]
        m = jnp.maximum(jnp.max(s_c, axis=-1, keepdims=True), jnp.max(s_n, axis=-1, keepdims=True))
        p_c = jnp.exp(s_c - m)
        p_n = jnp.exp(s_n - m)
        den = jnp.sum(p_c, axis=-1, keepdims=True) + jnp.sum(p_n, axis=-1, keepdims=True)
        num = (lax.dot_general(p_c.astype(BF16), vt_ref[0, 0, h].astype(BF16), contract_last,
                               preferred_element_type=F32)
               + jnp.dot(p_n.astype(BF16), vn, preferred_element_type=F32))
        out = num / den
        lse = m + jnp.log(den)
        ls = [lse[i * t_new:(i + 1) * t_new] for i in range(len(DILATIONS))]
        os_ = [out[i * t_new:(i + 1) * t_new] for i in range(len(DILATIONS))]
        mm = jnp.maximum(jnp.maximum(ls[0], ls[1]), ls[2])
        w = [jnp.exp(l - mm) for l in ls]
        o_ref[0, h] = (w[0] * os_[0] + w[1] * os_[1] + w[2] * os_[2]) / (w[0] + w[1] + w[2])
    cw = cw_ref[...]
    u = gc_ref[0] * hc_ref[0]
    st = st_ref[0]
    rows = [st[j:j + 1] for j in range(CONV_K - 1)] + [u[t:t + 1] for t in range(t_new)]
    gb = gb_ref[0]
    conv_ref[0] = jnp.concatenate(
        [gb[t:t + 1] * sum(cw[j:j + 1] * rows[t + j] for j in range(CONV_K)) for t in range(t_new)],
        axis=0)
    st_out_ref[0] = jnp.concatenate(rows[t_new:], axis=0)


SAMPLE_ROWS = 16


def _sample_bias_tables(step_bias, t_new, w_buf):
    cache_rows = []
    for b, d in zip(step_bias, DILATIONS):
        back = b[1:][::-1].T
        if d >= t_new:
            residue = np.eye(d, dtype=bool)[:t_new]
            span = jnp.where(residue[None, :, None, :], back[:, None, :, None], NEG)
            span = span.reshape(N_HEADS, t_new, SEG_KEYS * d)
        else:
            assert d == 1, "dilations between 1 and the number of new tokens are not supported"
            span = jnp.stack([jnp.pad(back[:, :SEG_KEYS - t], ((0, 0), (t, 0)), constant_values=NEG)
                              for t in range(t_new)], axis=1)
        cache_rows.append(jnp.pad(span, ((0, 0), (0, 0), (w_buf - SEG_KEYS * d, 0)),
                                  constant_values=NEG))
    n_rows = len(DILATIONS) * t_new
    bias_c = jnp.pad(jnp.concatenate(cache_rows, axis=1), ((0, 0), (0, SAMPLE_ROWS - n_rows), (0, 0)))

    rows, cols, which, steps = [], [], [], []
    for di, d in enumerate(DILATIONS):
        for t in range(t_new):
            for t2 in range(t + 1):
                if (t - t2) % d == 0:
                    rows.append(di * t_new + t)
                    cols.append(t2)
                    which.append(di)
                    steps.append((t - t2) // d)
    vals = jnp.stack(step_bias)[np.array(which), np.array(steps)]
    bias_n = jnp.full((N_HEADS, n_rows, LANES), NEG, F32).at[:, np.array(rows), np.array(cols)].set(vals.T)
    bias_n = jnp.pad(bias_n, ((0, 0), (0, SAMPLE_ROWS - n_rows), (0, 0)))
    return bias_c, bias_n


def _sample_mix(q, kn, vn, cache_k, cache_v, layer, bias_c, bias_n, gb, gc, hc, state, conv_w):
    db, t_new = q.shape[0], q.shape[1]
    w_buf = cache_k.shape[2]
    kt = cache_k.transpose(0, 1, 3, 4, 2)
    vt = cache_v.transpose(0, 1, 3, 4, 2)
    n_rows = len(DILATIONS) * t_new
    q_rows = jnp.pad(jnp.tile(q.transpose(0, 2, 1, 3), (1, 1, len(DILATIONS), 1)),
                     ((0, 0), (0, 0), (0, SAMPLE_ROWS - n_rows), (0, 0)))
    kn_pad = jnp.pad(kn.transpose(0, 2, 1, 3), ((0, 0), (0, 0), (0, SUBLANES - t_new), (0, 0)))
    vn_pad = jnp.pad(vn.transpose(0, 2, 1, 3), ((0, 0), (0, 0), (0, SUBLANES - t_new), (0, 0)))
    row = pl.BlockSpec((1, t_new, CONV_DIM), lambda b: (b, 0, 0))
    new_spec = pl.BlockSpec((1, N_HEADS, SUBLANES, HEAD_DIM), lambda b: (b, 0, 0, 0))
    win_spec = pl.BlockSpec((1, 1, N_HEADS, HEAD_DIM, w_buf), lambda b: (layer, b, 0, 0, 0))
    attn, conv, st = pl.pallas_call(
        functools.partial(_sample_mix_kernel, t_new=t_new),
        grid=(db,),
        in_specs=[pl.BlockSpec((1, N_HEADS, SAMPLE_ROWS, HEAD_DIM), lambda b: (b, 0, 0, 0)),
                  new_spec, new_spec, win_spec, win_spec,
                  pl.BlockSpec(bias_c.shape, lambda b: (0, 0, 0)),
                  pl.BlockSpec(bias_n.shape, lambda b: (0, 0, 0)),
                  row, row, row,
                  pl.BlockSpec((1, CONV_K - 1, CONV_DIM), lambda b: (b, 0, 0)),
                  pl.BlockSpec((CONV_K, CONV_DIM), lambda b: (0, 0))],
        out_specs=[pl.BlockSpec((1, N_HEADS, t_new, HEAD_DIM), lambda b: (b, 0, 0, 0)), row,
                   pl.BlockSpec((1, CONV_K - 1, CONV_DIM), lambda b: (b, 0, 0))],
        out_shape=[jax.ShapeDtypeStruct((db, N_HEADS, t_new, HEAD_DIM), F32),
                   jax.ShapeDtypeStruct((db, t_new, CONV_DIM), F32),
                   jax.ShapeDtypeStruct((db, CONV_K - 1, CONV_DIM), F32)],
        compiler_params=_params("arbitrary"),
        name="sample_mix",
    )(q_rows, kn_pad, vn_pad, kt, vt, bias_c, bias_n, gb, gc, hc, state, conv_w)
    return attn.transpose(0, 2, 1, 3), conv, st


def _to_slabs(x):
    n = x.shape[0]
    return x.reshape(n, N_SLABS, LANES).transpose(1, 0, 2)


def _window_from_slabs(x, batch, seq, w_keep):
    win = x.reshape(N_SLABS, batch, seq, LANES)[:, :, seq - w_keep:]
    return win.transpose(1, 2, 0, 3).reshape(batch, w_keep, N_HEADS, HEAD_DIM)


def kernel(x_prompt, x_sample, cache_k, cache_v, state_conv, rel_bias, norm_mix, norm_ffn,
           norm_final, w_in, conv_w, w_out, w_router_group, w_router_expert, w_gate, w_up,
           w_down):
    batch, seq, _ = x_prompt.shape
    db, t_new, _ = x_sample.shape
    depth = w_in.shape[0]
    w_keep = min(MAX_WINDOW, seq)

    hp = x_prompt.reshape(batch * seq, D_MODEL)
    hs = x_sample.reshape(db * t_new, D_MODEL)
    step_bias = _step_bias(rel_bias)
    bias_prompt = _prompt_bias_tables(step_bias)
    bias_c, bias_n = _sample_bias_tables(step_bias, t_new, cache_k.shape[2])
    g_final = norm_final.reshape(1, D_MODEL)

    tm_p = 256
    n_s = db * t_new
    nt_p = batch * seq // tm_p
    pad_per_tile = N_EXPERTS * (RUN_ALIGN - 1)
    max_tiles = -(-(2 * (batch * seq + n_s) + (nt_p + 1) * pad_per_tile) // MOE_TILE) + N_EXPERTS
    buf_rows = max_tiles * MOE_TILE + _local_rows(tm_p)

    pk, pv, pc, sk, sv, sc = [], [], [], [], [], []
    for l in range(depth):
        w_in_b = w_in[l].astype(BF16)
        w_out_b = w_out[l].astype(BF16)
        w_router_t = jnp.pad(jnp.concatenate([w_router_expert[l], w_router_group[l]], axis=1).T,
                             ((0, ROUTER_ROWS - N_EXPERTS - N_GROUPS), (0, 0)))
        g_mix = norm_mix[l].reshape(1, D_MODEL)
        g_ffn = norm_ffn[l].reshape(1, D_MODEL)
        last = l == depth - 1

        q, k, v, conv, u_last = _in_proj(hp, g_mix, w_in_b, conv_w[l], seq)
        attn = _attention_prompt(q, k, v, bias_prompt, batch, seq)
        h1, xn, route, routet, cnt = _out_proj(attn, conv, hp, w_out_b, g_ffn, w_router_t,
                                               tm=4 * tm_p, sub=tm_p)
        pk.append(_window_from_slabs(k, batch, seq, w_keep))
        pv.append(_window_from_slabs(v, batch, seq, w_keep))
        pc.append(u_last)

        proj = _in_proj_sample(hs, g_mix, w_in_b)
        qs = proj[:, :ATTN_WIDTH].reshape(db, t_new, N_HEADS, HEAD_DIM)
        ks = proj[:, ATTN_WIDTH:2 * ATTN_WIDTH].reshape(db, t_new, N_HEADS, HEAD_DIM)
        vs = proj[:, 2 * ATTN_WIDTH:3 * ATTN_WIDTH].reshape(db, t_new, N_HEADS, HEAD_DIM)
        c0 = 3 * ATTN_WIDTH
        gb = proj[:, c0:c0 + CONV_DIM].reshape(db, t_new, CONV_DIM)
        gc = proj[:, c0 + CONV_DIM:c0 + 2 * CONV_DIM].reshape(db, t_new, CONV_DIM)
        hc = proj[:, c0 + 2 * CONV_DIM:].reshape(db, t_new, CONV_DIM)
        attn_s, conv_s, state_s = _sample_mix(qs, ks, vs, cache_k, cache_v, l, bias_c, bias_n,
                                              gb, gc, hc, state_conv[l], conv_w[l])
        h1s, xns, route_s, routet_s, cnt_s = _out_proj(
            _to_slabs(attn_s.reshape(n_s, ATTN_WIDTH)), conv_s.reshape(n_s, CONV_DIM), hs,
            w_out_b, g_ffn, w_router_t, tm=n_s, sub=n_s)

        table, n_chunks, first_tile, tiles_e, rows_e, total = _moe_plan(
            jnp.concatenate([cnt[:, :, 0], cnt_s[:, :, 0]], axis=0), max_tiles)
        xs = _dispatch(table[:nt_p], xn, route, tm_p, buf_rows)
        xs = _dispatch(table[nt_p:], xns, route_s, n_s, buf_rows, xs_prev=xs)
        eo = _experts(first_tile, tiles_e, rows_e, total, xs, w_gate, w_up, w_down, l)
        hp = _combine(table[:nt_p], n_chunks[:nt_p], h1, routet, g_final, eo, tm_p, last)
        hs = _combine(table[nt_p:], n_chunks[nt_p:], h1s, routet_s, g_final, eo, n_s, last)
        sk.append(ks)
        sv.append(vs)
        sc.append(state_s)

    return (hp.reshape(batch, seq, D_MODEL), hs.reshape(db, t_new, D_MODEL),
            jnp.stack(pk), jnp.stack(pv), jnp.stack(pc),
            jnp.stack(sk), jnp.stack(sv), jnp.stack(sc))
```

```python
import functools
import math

import jax
import jax.numpy as jnp
import numpy as np
from jax import lax
from jax.experimental import pallas as pl
from jax.experimental.pallas import tpu as pltpu

F32 = jnp.float32
BF16 = jnp.bfloat16

D_MODEL = 1024
N_HEADS = 8
HEAD_DIM = 64
ATTN_WIDTH = N_HEADS * HEAD_DIM
CONV_DIM = D_MODEL - ATTN_WIDTH
CONV_K = 3
PROJ_WIDTH = 3 * ATTN_WIDTH + 3 * CONV_DIM
DILATIONS = (1, 4, 16)
SEG_KEYS = 128
Q_BLOCK = 128
MAX_WINDOW = 2048
N_BUCKETS = 32
MAX_DISTANCE = 2048
N_GROUPS = 4
EXPERTS_PER_GROUP = 8
N_EXPERTS = N_GROUPS * EXPERTS_PER_GROUP
D_EXPERT = D_MODEL // 4
EPS = 1e-6
NEG = -1e30
SCALE = 1.0 / math.sqrt(HEAD_DIM)

LANES = 128
SUBLANES = 8
N_SLABS = ATTN_WIDTH // LANES
VMEM_LIMIT = 48 * 1024 * 1024
REGROUP = 4
UNITS_PER_TRIP = 16


def _params(*sem):
    return pltpu.CompilerParams(dimension_semantics=sem, vmem_limit_bytes=VMEM_LIMIT)


def _rms(x, g):
    return x * lax.rsqrt(jnp.mean(x * x, axis=-1, keepdims=True) + EPS) * g


def _rel_bucket(dist):
    max_exact = N_BUCKETS // 2
    d_f = jnp.maximum(dist, 1).astype(F32)
    large = max_exact + (jnp.log(d_f / max_exact) / math.log(MAX_DISTANCE / max_exact)
                         * (N_BUCKETS - max_exact)).astype(jnp.int32)
    large = jnp.minimum(large, N_BUCKETS - 1)
    return jnp.where(dist < max_exact, dist, large)


def _in_proj_kernel(x_ref, g_ref, w_ref, cw_ref, *rest, tm, tiles_per_seq, win_tiles, chained):
    (q_ref, k_ref, v_ref, conv_ref, ulast_ref, kwin_ref, vwin_ref, ext_ref) = rest[2:] if chained else rest
    i = pl.program_id(0)
    xn = _rms(x_ref[...], g_ref[...])
    proj = jnp.dot(xn.astype(BF16), w_ref[...], preferred_element_type=F32)
    for s in range(N_SLABS):
        q_ref[s] = proj[:, s * LANES:(s + 1) * LANES]
        k_ref[s] = proj[:, ATTN_WIDTH + s * LANES:ATTN_WIDTH + (s + 1) * LANES]
        v_ref[s] = proj[:, 2 * ATTN_WIDTH + s * LANES:2 * ATTN_WIDTH + (s + 1) * LANES]

    @pl.when(i % tiles_per_seq >= tiles_per_seq - win_tiles)
    def _():
        heads_per_slab = LANES // HEAD_DIM
        for s in range(N_SLABS):
            for src, dst in ((k_ref, kwin_ref), (v_ref, vwin_ref)):
                dst[0, 0, heads_per_slab * s:heads_per_slab * (s + 1)] = (
                    src[s].T.reshape(heads_per_slab, HEAD_DIM, tm))
    c0 = 3 * ATTN_WIDTH
    gb = proj[:, c0:c0 + CONV_DIM]
    u = proj[:, c0 + CONV_DIM:c0 + 2 * CONV_DIM] * proj[:, c0 + 2 * CONV_DIM:c0 + 3 * CONV_DIM]

    @pl.when(i % tiles_per_seq == 0)
    def _():
        ext_ref[0:SUBLANES, :] = jnp.zeros((SUBLANES, CONV_DIM), F32)

    ext_ref[SUBLANES:SUBLANES + tm, :] = u
    u1 = ext_ref[SUBLANES - 1:SUBLANES - 1 + tm, :]
    u2 = ext_ref[SUBLANES - 2:SUBLANES - 2 + tm, :]
    cw = cw_ref[...]
    conv_ref[...] = gb * (cw[0:1] * u2 + cw[1:2] * u1 + cw[2:3] * u)
    ulast_ref[0] = ext_ref[tm + SUBLANES - 2:tm + SUBLANES, :]
    ext_ref[0:SUBLANES, :] = ext_ref[tm:tm + SUBLANES, :]


def _in_proj(x, g, w_bf16, conv_w, seq_len, w_keep, layer, depth, windows=None, tm=1024):
    n = x.shape[0]
    n_seq = n // seq_len
    slab = jax.ShapeDtypeStruct((N_SLABS, n, LANES), F32)
    slab_spec = pl.BlockSpec((N_SLABS, tm, LANES), lambda i: (0, i, 0))
    tiles_per_seq = seq_len // tm
    win_tiles = w_keep // tm
    win = jax.ShapeDtypeStruct((depth, n_seq, N_HEADS, HEAD_DIM, w_keep), F32)
    win_spec = pl.BlockSpec(
        (1, 1, N_HEADS, HEAD_DIM, tm),
        lambda i: (layer, i // tiles_per_seq, 0, 0,
                   jnp.maximum(i % tiles_per_seq - (tiles_per_seq - win_tiles), 0)))
    chained = windows is not None
    in_specs = [pl.BlockSpec((tm, D_MODEL), lambda i: (i, 0)),
                pl.BlockSpec((1, D_MODEL), lambda i: (0, 0)),
                pl.BlockSpec((D_MODEL, PROJ_WIDTH), lambda i: (0, 0)),
                pl.BlockSpec((CONV_K, CONV_DIM), lambda i: (0, 0))]
    args = [x, g, w_bf16, conv_w]
    if chained:
        in_specs += [pl.BlockSpec(memory_space=pl.ANY)] * 2
        args += list(windows)
    return pl.pallas_call(
        functools.partial(_in_proj_kernel, tm=tm, tiles_per_seq=tiles_per_seq, win_tiles=win_tiles,
                          chained=chained),
        grid=(n // tm,),
        in_specs=in_specs,
        out_specs=[slab_spec, slab_spec, slab_spec,
                   pl.BlockSpec((tm, CONV_DIM), lambda i: (i, 0)),
                   pl.BlockSpec((1, CONV_K - 1, CONV_DIM), lambda i: (i // tiles_per_seq, 0, 0)),
                   win_spec, win_spec],
        out_shape=[slab, slab, slab,
                   jax.ShapeDtypeStruct((n, CONV_DIM), F32),
                   jax.ShapeDtypeStruct((n_seq, CONV_K - 1, CONV_DIM), F32), win, win],
        scratch_shapes=[pltpu.VMEM((tm + SUBLANES, CONV_DIM), F32)],
        input_output_aliases={4: 5, 5: 6} if chained else {},
        compiler_params=_params("arbitrary"),
        name="in_proj_prompt",
    )(*args)


def _attn_kernel(q_ref, k_ref, v_ref, bias_ref, o_ref, q4, k4, v4, ktail, vtail, oacc, lacc, *, sb):
    j = pl.program_id(2)
    slot = j % 2
    prev = 1 - slot
    seg = sb // REGROUP

    @pl.when(j == 0)
    def _():
        k4[1] = jnp.zeros((sb, LANES), F32)
        v4[1] = jnp.zeros((sb, LANES), F32)
        ktail[...] = jnp.zeros((Q_BLOCK, LANES), F32)
        vtail[...] = jnp.zeros((Q_BLOCK, LANES), F32)

    for r in range(REGROUP):
        q4[r * seg:(r + 1) * seg, :] = q_ref[pl.ds(r, seg, stride=REGROUP), :] * SCALE
        k4[slot, r * seg:(r + 1) * seg, :] = k_ref[pl.ds(r, seg, stride=REGROUP), :]
        v4[slot, r * seg:(r + 1) * seg, :] = v_ref[pl.ds(r, seg, stride=REGROUP), :]

    lane = lax.broadcasted_iota(jnp.int32, (Q_BLOCK, LANES), 1)
    low = lane < HEAD_DIM
    ones = jnp.ones((2 * Q_BLOCK, LANES), BF16)

    def block(start):
        return pl.ds(pl.multiple_of(start, Q_BLOCK), Q_BLOCK)

    for di, d in enumerate(DILATIONS):
        nq = sb // (Q_BLOCK * d)

        def unit(u, di=di, d=d, nq=nq):
            r = u // nq
            n = u % nq
            qstart = r + d * Q_BLOCK * n
            if d == 1:
                here = block(n * Q_BLOCK)
                before = block(jnp.maximum(n - 1, 0) * Q_BLOCK)
                q = q_ref[here, :] * SCALE
                k_new, v_new = k_ref[here, :], v_ref[here, :]
                k_old = jnp.where(n == 0, ktail[...], k_ref[before, :])
                v_old = jnp.where(n == 0, vtail[...], v_ref[before, :])
            elif d == REGROUP:
                here = block(r * seg + n * Q_BLOCK)
                before = block(r * seg + jnp.maximum(n - 1, 0) * Q_BLOCK)
                last = block(r * seg + seg - Q_BLOCK)
                q = q4[here, :]
                k_new, v_new = k4[slot, here, :], v4[slot, here, :]
                k_old = jnp.where(n == 0, k4[prev, last, :], k4[slot, before, :])
                v_old = jnp.where(n == 0, v4[prev, last, :], v4[slot, before, :])
            else:
                rows = pl.ds((u % REGROUP) * seg + u // REGROUP, Q_BLOCK, stride=REGROUP)
                q = q4[rows, :]
                k_new, v_new = k4[slot, rows, :], v4[slot, rows, :]
                k_old, v_old = k4[prev, rows, :], v4[prev, rows, :]
            kk = jnp.concatenate([k_old, k_new], axis=0).astype(BF16)
            vv = jnp.concatenate([v_old, v_new], axis=0).astype(BF16)
            qm = jnp.concatenate([jnp.where(low, q, 0.0), jnp.where(low, 0.0, q)],
                                 axis=0).astype(BF16)
            s = lax.dot_general(qm, kk, (((1,), (1,)), ((), ())),
                                preferred_element_type=F32)
            first = jnp.logical_and(j == 0, n == 0).astype(jnp.int32)
            s = s + bias_ref[di, first]
            m = jnp.max(s, axis=-1, keepdims=True)
            p = jnp.exp(s - m).astype(BF16)
            pv = jnp.dot(p, jnp.concatenate([vv, ones], axis=1),
                         preferred_element_type=F32)
            o_sel = jnp.where(low, pv[0:Q_BLOCK, 0:LANES], pv[Q_BLOCK:, 0:LANES])
            l_sel = jnp.where(low, pv[0:Q_BLOCK, LANES:], pv[Q_BLOCK:, LANES:])
            m_sel = jnp.where(low, jnp.broadcast_to(m[0:Q_BLOCK], (Q_BLOCK, LANES)),
                              jnp.broadcast_to(m[Q_BLOCK:], (Q_BLOCK, LANES)))
            oacc[di, pl.ds(qstart, Q_BLOCK, stride=d), :] = o_sel / l_sel
            lacc[di, pl.ds(qstart, Q_BLOCK, stride=d), :] = m_sel + jnp.log(l_sel)

        def group(g, carry, unit=unit):
            for uu in range(UNITS_PER_TRIP):
                unit(g * UNITS_PER_TRIP + uu)
            return carry

        lax.fori_loop(0, sb // Q_BLOCK // UNITS_PER_TRIP, group, 0)

    for c in range(sb // 256):
        rows = slice(c * 256, (c + 1) * 256)
        l0, l1, l2 = lacc[0, rows, :], lacc[1, rows, :], lacc[2, rows, :]
        mm = jnp.maximum(jnp.maximum(l0, l1), l2)
        w0, w1, w2 = jnp.exp(l0 - mm), jnp.exp(l1 - mm), jnp.exp(l2 - mm)
        num = w0 * oacc[0, rows, :] + w1 * oacc[1, rows, :] + w2 * oacc[2, rows, :]
        o_ref[rows, :] = num / (w0 + w1 + w2)
    ktail[...] = k_ref[sb - Q_BLOCK:sb, :]
    vtail[...] = v_ref[sb - Q_BLOCK:sb, :]


def _step_bias(rel_bias):
    steps = jnp.arange(SEG_KEYS + 1)
    return [rel_bias[_rel_bucket(steps * d)].astype(F32) for d in DILATIONS]


def _prompt_bias_tables(step_bias):
    width = 3 * Q_BLOCK
    k_loc = jnp.arange(2 * Q_BLOCK)
    tables = []
    for b in step_bias:
        g = jnp.concatenate([b[::-1].T, jnp.full((N_HEADS, width - SEG_KEYS), NEG, F32)], axis=1)
        flat = jnp.broadcast_to(g[:, None, :], (N_HEADS, Q_BLOCK, width + 1)).reshape(N_HEADS, -1)
        x = flat[:, :Q_BLOCK * width].reshape(N_HEADS, Q_BLOCK, width)
        t = x[:, :, :2 * Q_BLOCK]
        t_first = jnp.where(k_loc[None, None, :] >= Q_BLOCK, t, NEG)
        tables.append(jnp.stack([t.reshape(N_SLABS, 2 * Q_BLOCK, 2 * Q_BLOCK),
                                 t_first.reshape(N_SLABS, 2 * Q_BLOCK, 2 * Q_BLOCK)]))
    return jnp.stack(tables)


def _attention_prompt(q, k, v, bias_tab, n_seq, seq_len, sb=2048):
    assert DILATIONS == (1, REGROUP, REGROUP * REGROUP)
    n = q.shape[1]
    nsb = seq_len // sb
    blk = pl.BlockSpec((None, sb, LANES), lambda b, s, j: (s, b * nsb + j, 0))
    return pl.pallas_call(
        functools.partial(_attn_kernel, sb=sb),
        grid=(n_seq, N_SLABS, nsb),
        in_specs=[blk, blk, blk,
                  pl.BlockSpec((len(DILATIONS), 2, None, 2 * Q_BLOCK, 2 * Q_BLOCK),
                               lambda b, s, j: (0, 0, s, 0, 0))],
        out_specs=blk,
        out_shape=jax.ShapeDtypeStruct((N_SLABS, n, LANES), F32),
        scratch_shapes=[pltpu.VMEM((sb, LANES), F32),
                        pltpu.VMEM((2, sb, LANES), F32), pltpu.VMEM((2, sb, LANES), F32),
                        pltpu.VMEM((Q_BLOCK, LANES), F32), pltpu.VMEM((Q_BLOCK, LANES), F32),
                        pltpu.VMEM((len(DILATIONS), sb, LANES), F32),
                        pltpu.VMEM((len(DILATIONS), sb, LANES), F32)],
        compiler_params=_params("arbitrary", "arbitrary", "arbitrary"),
        name="attention_prompt",
    )(q, k, v, bias_tab)


ROUTER_ROWS = 48
MOE_TILE = 256
RUN_ALIGN = 2 * SUBLANES
TABLE_LANES = 128
EXPERT_BUFFERS = 6


def _local_rows(tm):
    return -(-(2 * tm + N_EXPERTS * (RUN_ALIGN - 1)) // LANES) * LANES


def _route(lg):
    tm = lg.shape[1]
    gl = lg[N_EXPERTS:N_EXPERTS + N_GROUPS]
    grow = lax.broadcasted_iota(jnp.int32, (N_GROUPS, tm), 0)
    gmax = jnp.max(gl, axis=0, keepdims=True)
    gsum = jnp.sum(jnp.exp(gl - gmax), axis=0, keepdims=True)
    gidx = jnp.min(jnp.where(gl == gmax, grow, N_GROUPS), axis=0, keepdims=True)
    p_grp = 1.0 / gsum
    el = lg[0:N_EXPERTS]
    e = lax.broadcasted_iota(jnp.int32, (N_EXPERTS, tm), 0)
    in_grp = (e // EXPERTS_PER_GROUP) == gidx
    elog = jnp.where(in_grp, el, -jnp.inf)
    emax = jnp.max(elog, axis=0, keepdims=True)
    eexp = jnp.exp(elog - emax)
    eprob = jnp.where(in_grp, eexp / jnp.sum(eexp, axis=0, keepdims=True), -1.0)
    p1 = jnp.max(eprob, axis=0, keepdims=True)
    i1 = jnp.min(jnp.where(eprob == p1, e, N_EXPERTS), axis=0, keepdims=True)
    rest = jnp.where(e == i1, -1.0, eprob)
    p2 = jnp.max(rest, axis=0, keepdims=True)
    i2 = jnp.min(jnp.where(rest == p2, e, N_EXPERTS), axis=0, keepdims=True)
    tot = p1 + p2
    return e == i1, e == i2, p_grp * (p1 / tot), p_grp * (p2 / tot)


def _split_bf16(x):
    hi = x.astype(BF16)
    return hi, (x - hi.astype(F32)).astype(BF16)


def _out_proj_kernel(attn_ref, conv_ref, h_ref, wo_ref, g_ref, wr_ref, upper_ref, lower_ref,
                     h1_ref, xn_ref, route_ref, routet_ref, cnt_ref, *, tm, sub):
    contract_last = (((1,), (1,)), ((), ()))
    mix = jnp.concatenate([attn_ref[s] for s in range(N_SLABS)] + [conv_ref[...]], axis=1)
    h1_ref[...] = h_ref[...] + jnp.dot(mix.astype(BF16), wo_ref[...], preferred_element_type=F32)
    for t in range(tm // sub):
        rs = slice(t * sub, (t + 1) * sub)
        xn = _rms(h1_ref[rs, :], g_ref[...])
        xh, xl = _split_bf16(xn)
        xn_ref[rs, :] = xh
        both = lax.dot_general(wr_ref[...], xh, contract_last, preferred_element_type=F32)
        lg = (both[:ROUTER_ROWS] + both[ROUTER_ROWS:]
              + lax.dot_general(wr_ref[:ROUTER_ROWS, :], xl, contract_last, preferred_element_type=F32))
        sel1, sel2, g1, g2 = _route(lg)
        chosen = jnp.where(sel1 | sel2, 1.0, 0.0)
        rank = jnp.dot(chosen.astype(BF16), upper_ref[...], preferred_element_type=F32)
        cnt = jnp.sum(chosen, axis=1, keepdims=True)
        cnt_pad = jnp.floor((cnt + (RUN_ALIGN - 1)) * (1.0 / RUN_ALIGN)) * RUN_ALIGN
        cnt_b = jnp.broadcast_to(cnt_pad, (N_EXPERTS, LANES))
        start = jnp.dot(lower_ref[...], cnt_b.astype(BF16), preferred_element_type=F32)
        pos = start[:, 0:1] + rank
        pos1 = jnp.sum(jnp.where(sel1, pos, 0.0), axis=0, keepdims=True)
        pos2 = jnp.sum(jnp.where(sel2, pos, 0.0), axis=0, keepdims=True)
        rows = jnp.concatenate([pos1, pos2, g1, g2, jnp.zeros((SUBLANES - 4, sub), F32)], axis=0)
        route_ref[t] = rows
        routet_ref[rs, :] = jnp.concatenate([rows, jnp.zeros((LANES - SUBLANES, sub), F32)], axis=0).T
        cnt_ref[t] = cnt_b


def _out_proj(attn, conv, h, wo_bf16, g, w_router_t, tm, sub):
    n = h.shape[0]
    w_router = jnp.concatenate(_split_bf16(w_router_t), axis=0)
    idx = jnp.arange(sub)
    upper = (idx[:, None] < idx[None, :]).astype(BF16)
    eidx = jnp.arange(N_EXPERTS)
    lower = (eidx[None, :] < eidx[:, None]).astype(BF16)
    const = lambda shape: pl.BlockSpec(shape, lambda i: (0,) * len(shape))
    per_step = tm // sub
    return pl.pallas_call(
        functools.partial(_out_proj_kernel, tm=tm, sub=sub),
        grid=(n // tm,),
        in_specs=[pl.BlockSpec((N_SLABS, tm, LANES), lambda i: (0, i, 0)),
                  pl.BlockSpec((tm, CONV_DIM), lambda i: (i, 0)),
                  pl.BlockSpec((tm, D_MODEL), lambda i: (i, 0)),
                  const((D_MODEL, D_MODEL)), const((1, D_MODEL)),
                  const((2 * ROUTER_ROWS, D_MODEL)),
                  const((sub, sub)), const((N_EXPERTS, N_EXPERTS))],
        out_specs=[pl.BlockSpec((tm, D_MODEL), lambda i: (i, 0)),
                   pl.BlockSpec((tm, D_MODEL), lambda i: (i, 0)),
                   pl.BlockSpec((per_step, SUBLANES, sub), lambda i: (i, 0, 0)),
                   pl.BlockSpec((tm, LANES), lambda i: (i, 0)),
                   pl.BlockSpec((per_step, N_EXPERTS, LANES), lambda i: (i, 0, 0))],
        out_shape=[jax.ShapeDtypeStruct((n, D_MODEL), F32),
                   jax.ShapeDtypeStruct((n, D_MODEL), BF16),
                   jax.ShapeDtypeStruct((n // sub, SUBLANES, sub), F32),
                   jax.ShapeDtypeStruct((n, LANES), F32),
                   jax.ShapeDtypeStruct((n // sub, N_EXPERTS, LANES), F32)],
        compiler_params=_params("arbitrary"),
        name="out_proj",
    )(attn, conv, h, wo_bf16, g, w_router, upper, lower)


def _moe_plan(cnt_tiles, max_tiles):
    c = cnt_tiles.astype(jnp.int32)
    rows_e = jnp.sum(c, axis=0)
    tiles_e = (rows_e + MOE_TILE - 1) // MOE_TILE
    cum = jnp.cumsum(tiles_e)
    total = cum[-1]
    first_tile = cum - tiles_e

    run_start = jnp.cumsum(c, axis=0) - c
    local_start = jnp.cumsum(c, axis=1) - c
    local_end = local_start + c
    row = jnp.arange(TABLE_LANES, dtype=jnp.int32) * RUN_ALIGN
    inside = (local_start[:, None, :] <= row[None, :, None]) & (row[None, :, None] < local_end[:, None, :])
    base = first_tile[None] * MOE_TILE + run_start - local_start
    n_chunks = jnp.sum(c, axis=1) // RUN_ALIGN
    table = jnp.sum(jnp.where(inside, base[:, None, :] + row[None, :, None], 0), axis=-1)
    live = jnp.arange(TABLE_LANES)[None] < n_chunks[:, None]
    table = jnp.where(live, table, max_tiles * MOE_TILE + row[None])
    i32 = lambda a: a.astype(jnp.int32)
    return (i32(table), i32(n_chunks), i32(first_tile), i32(tiles_e), i32(rows_e),
            i32(total).reshape(1))


def _dispatch_kernel(tbl_ref, xn_ref, route_ref, *rest, tm, rl, chained):
    xs_hbm, lsort, sem = rest[1:] if chained else rest
    i = pl.program_id(0)
    slot = i % 2
    rows = route_ref[0]
    pos1 = rows[0:1].astype(jnp.int32)
    pos2 = rows[1:2].astype(jnp.int32)
    p = lax.broadcasted_iota(jnp.int32, (rl, tm), 0)
    one_hot = jnp.where((p == pos1) | (p == pos2), 1.0, 0.0).astype(BF16)
    lsort[slot] = jnp.dot(one_hot, xn_ref[...], preferred_element_type=F32).astype(BF16)

    def tile_copies_done(s):
        pltpu.make_async_copy(lsort.at[s], xs_hbm.at[pl.ds(0, rl), :], sem.at[s]).wait()

    @pl.when(i > 0)
    def _():
        tile_copies_done(1 - slot)

    for q in range(rl // RUN_ALIGN):
        row = pl.multiple_of(tbl_ref[i * TABLE_LANES + q], RUN_ALIGN)
        pltpu.make_async_copy(lsort.at[slot, pl.ds(q * RUN_ALIGN, RUN_ALIGN), :],
                              xs_hbm.at[pl.ds(row, RUN_ALIGN), :], sem.at[slot]).start()

    @pl.when(i == pl.num_programs(0) - 1)
    def _():
        tile_copies_done(slot)


def _dispatch(table, xn, route, tm, total_rows, xs_prev=None):
    nt = xn.shape[0] // tm
    rl = _local_rows(tm)
    chained = xs_prev is not None
    in_specs = [pl.BlockSpec((tm, D_MODEL), lambda i, t: (i, 0)),
                pl.BlockSpec((1, SUBLANES, tm), lambda i, t: (i, 0, 0))]
    args = [table.reshape(-1), xn, route]
    if chained:
        in_specs.append(pl.BlockSpec(memory_space=pl.ANY))
        args.append(xs_prev)
    return pl.pallas_call(
        functools.partial(_dispatch_kernel, tm=tm, rl=rl, chained=chained),
        grid_spec=pltpu.PrefetchScalarGridSpec(
            num_scalar_prefetch=1, grid=(nt,), in_specs=in_specs,
            out_specs=pl.BlockSpec(memory_space=pl.ANY),
            scratch_shapes=[pltpu.VMEM((2, rl, D_MODEL), BF16), pltpu.SemaphoreType.DMA((2,))]),
        out_shape=jax.ShapeDtypeStruct((total_rows, D_MODEL), BF16),
        input_output_aliases={3: 0} if chained else {},
        compiler_params=_params("arbitrary"),
        name="moe_dispatch",
    )(*args)


def _expert_kernel(first_ref, tiles_ref, rows_ref, tot_ref, xs_hbm, wg_ref, wu_ref, wd_ref, eo_hbm,
                   xbuf, obuf, wg_b, wu_b, wd_b, sem_in, sem_out):
    e = pl.program_id(0)
    first_tile = first_ref[e]
    n_tiles = tiles_ref[e]
    total = tot_ref[0]

    def in_copy(g):
        return pltpu.make_async_copy(
            xs_hbm.at[pl.ds(pl.multiple_of(g * MOE_TILE, MOE_TILE), MOE_TILE), :],
            xbuf.at[g % EXPERT_BUFFERS], sem_in.at[g % EXPERT_BUFFERS])

    def out_copy(g):
        return pltpu.make_async_copy(
            obuf.at[g % EXPERT_BUFFERS],
            eo_hbm.at[pl.ds(pl.multiple_of(g * MOE_TILE, MOE_TILE), MOE_TILE), :],
            sem_out.at[g % EXPERT_BUFFERS])

    lookahead = EXPERT_BUFFERS - 2

    @pl.when(e == 0)
    def _():
        for g0 in range(lookahead):
            @pl.when(g0 < total)
            def _():
                in_copy(g0).start()

    @pl.when(n_tiles > 0)
    def _():
        wg_b[...] = wg_ref[0, 0].astype(BF16)
        wu_b[...] = wu_ref[0, 0].astype(BF16)
        wd_b[...] = wd_ref[0, 0].astype(BF16)

    def acquire(g, j):
        @pl.when(g + lookahead < total)
        def _():
            in_copy(g + lookahead).start()

        in_copy(g).wait()

        @pl.when(g >= EXPERT_BUFFERS)
        def _():
            out_copy(g - EXPERT_BUFFERS).wait()

        row = lax.broadcasted_iota(jnp.int32, (MOE_TILE, 1), 0)
        x = xbuf[g % EXPERT_BUFFERS]
        return jnp.where(row < rows_ref[e] - j * MOE_TILE, x, jnp.zeros_like(x))

    def mlp(x):
        gate = jnp.dot(x, wg_b[...], preferred_element_type=F32)
        up = jnp.dot(x, wu_b[...], preferred_element_type=F32)
        hdn = gate * jax.nn.sigmoid(gate) * up
        return jnp.dot(hdn.astype(BF16), wd_b[...], preferred_element_type=F32).astype(BF16)

    def pair(jp, carry):
        j = 2 * jp
        g = first_tile + j
        x0 = acquire(g, j)
        x1 = acquire(g + 1, j + 1)
        y = mlp(jnp.concatenate([x0, x1], axis=0))
        obuf[g % EXPERT_BUFFERS] = y[:MOE_TILE]
        obuf[(g + 1) % EXPERT_BUFFERS] = y[MOE_TILE:]
        out_copy(g).start()
        out_copy(g + 1).start()
        return carry

    lax.fori_loop(0, n_tiles // 2, pair, 0)

    @pl.when(n_tiles % 2 == 1)
    def _():
        j = n_tiles - 1
        g = first_tile + j
        obuf[g % EXPERT_BUFFERS] = mlp(acquire(g, j))
        out_copy(g).start()

    @pl.when(e == pl.num_programs(0) - 1)
    def _():
        for back in range(EXPERT_BUFFERS, 0, -1):
            @pl.when(total >= back)
            def _():
                out_copy(total - back).wait()


def _experts(first_tile, tiles_e, rows_e, total, xs, w_gate, w_up, w_down, layer):
    weight = lambda e, f, t, r, n: (layer, e, 0, 0)
    return pl.pallas_call(
        _expert_kernel,
        grid_spec=pltpu.PrefetchScalarGridSpec(
            num_scalar_prefetch=4, grid=(N_EXPERTS,),
            in_specs=[pl.BlockSpec(memory_space=pl.ANY),
                      pl.BlockSpec((1, 1, D_MODEL, D_EXPERT), weight),
                      pl.BlockSpec((1, 1, D_MODEL, D_EXPERT), weight),
                      pl.BlockSpec((1, 1, D_EXPERT, D_MODEL), weight)],
            out_specs=pl.BlockSpec(memory_space=pl.ANY),
            scratch_shapes=[pltpu.VMEM((EXPERT_BUFFERS, MOE_TILE, D_MODEL), BF16),
                            pltpu.VMEM((EXPERT_BUFFERS, MOE_TILE, D_MODEL), BF16),
                            pltpu.VMEM((D_MODEL, D_EXPERT), BF16),
                            pltpu.VMEM((D_MODEL, D_EXPERT), BF16),
                            pltpu.VMEM((D_EXPERT, D_MODEL), BF16),
                            pltpu.SemaphoreType.DMA((EXPERT_BUFFERS,)),
                            pltpu.SemaphoreType.DMA((EXPERT_BUFFERS,))]),
        out_shape=jax.ShapeDtypeStruct(xs.shape, BF16),
        compiler_params=_params("arbitrary"),
        name="moe_experts",
    )(first_tile, tiles_e, rows_e, total, xs, w_gate, w_up, w_down)


def _combine_kernel(tbl_ref, nch_ref, h1_ref, routet_ref, gf_ref, eo_hbm, o_ref, leo, sem,
                    *, tm, rl, final_norm):
    i = pl.program_id(0)
    n = pl.num_programs(0)
    slot = i % 2

    def gather(tile, s):
        for q in range(rl // RUN_ALIGN):
            row = jnp.where(q < nch_ref[tile], tbl_ref[tile * TABLE_LANES + q], q * RUN_ALIGN)
            pltpu.make_async_copy(eo_hbm.at[pl.ds(pl.multiple_of(row, RUN_ALIGN), RUN_ALIGN), :],
                                  leo.at[s, pl.ds(q * RUN_ALIGN, RUN_ALIGN), :], sem.at[s]).start()

    @pl.when(i == 0)
    def _():
        gather(0, 0)

    @pl.when(i + 1 < n)
    def _():
        gather(i + 1, 1 - slot)

    pltpu.make_async_copy(eo_hbm.at[pl.ds(0, rl), :], leo.at[slot], sem.at[slot]).wait()

    rt = routet_ref[...]
    pos1 = rt[:, 0:1].astype(jnp.int32)
    pos2 = rt[:, 1:2].astype(jnp.int32)
    lane = lax.broadcasted_iota(jnp.int32, (tm, rl), 1)
    weights = jnp.where(lane == pos1, rt[:, 2:3], 0.0) + jnp.where(lane == pos2, rt[:, 3:4], 0.0)
    y = jnp.dot(weights.astype(BF16), leo[slot], preferred_element_type=F32)
    out = h1_ref[...] + y
    if final_norm:
        out = _rms(out, gf_ref[...])
    o_ref[...] = out


def _combine(table, n_chunks, h1, routet, g_final, eo, tm, final_norm):
    n = h1.shape[0]
    rl = _local_rows(tm)
    return pl.pallas_call(
        functools.partial(_combine_kernel, tm=tm, rl=rl, final_norm=final_norm),
        grid_spec=pltpu.PrefetchScalarGridSpec(
            num_scalar_prefetch=2, grid=(n // tm,),
            in_specs=[pl.BlockSpec((tm, D_MODEL), lambda i, t, c: (i, 0)),
                      pl.BlockSpec((tm, LANES), lambda i, t, c: (i, 0)),
                      pl.BlockSpec((1, D_MODEL), lambda i, t, c: (0, 0)),
                      pl.BlockSpec(memory_space=pl.ANY)],
            out_specs=pl.BlockSpec((tm, D_MODEL), lambda i, t, c: (i, 0)),
            scratch_shapes=[pltpu.VMEM((2, rl, D_MODEL), BF16), pltpu.SemaphoreType.DMA((2,))]),
        out_shape=jax.ShapeDtypeStruct((n, D_MODEL), F32),
        compiler_params=_params("arbitrary"),
        name="moe_combine",
    )(table.reshape(-1), n_chunks, h1, routet, g_final, eo)


def _in_proj_sample_kernel(x_ref, g_ref, w_ref, o_ref):
    xn = _rms(x_ref[...], g_ref[...])
    o_ref[...] = jnp.dot(xn.astype(BF16), w_ref[...], preferred_element_type=F32)


def _in_proj_sample(x, g, w_bf16):
    n = x.shape[0]
    return pl.pallas_call(
        _in_proj_sample_kernel,
        out_shape=jax.ShapeDtypeStruct((n, PROJ_WIDTH), F32),
        compiler_params=pltpu.CompilerParams(vmem_limit_bytes=VMEM_LIMIT),
        name="in_proj_sample",
    )(x, g, w_bf16)


def _sample_mix_kernel(q_ref, kn_ref, vn_ref, kt_ref, vt_ref,
                       bc_ref, bn_ref, gb_ref, gc_ref, hc_ref, st_ref, cw_ref,
                       o_ref, conv_ref, st_out_ref, *, t_new):
    contract_last = (((1,), (1,)), ((), ()))
    no_rows = jnp.zeros((LANES - SUBLANES, HEAD_DIM), F32)
    for h in range(N_HEADS):
        q = (q_ref[0, h] * SCALE).astype(BF16)
        s_c = jnp.dot(q, kt_ref[0, 0, h].astype(BF16), preferred_element_type=F32) + bc_ref[h]
        kn = jnp.concatenate([kn_ref[0, h], no_rows], axis=0).astype(BF16)
        vn = jnp.concatenate([vn_ref[0, h], no_rows], axis=0).astype(BF16)
        s_n = lax.dot_general(q, kn, contract_last, preferred_element_type=F32) + bn_ref[h]
        m = jnp.maximum(jnp.max(s_c, axis=-1, keepdims=True), jnp.max(s_n, axis=-1, keepdims=True))
        p_c = jnp.exp(s_c - m)
        p_n = jnp.exp(s_n - m)
        den = jnp.sum(p_c, axis=-1, keepdims=True) + jnp.sum(p_n, axis=-1, keepdims=True)
        num = (lax.dot_general(p_c.astype(BF16), vt_ref[0, 0, h].astype(BF16), contract_last,
                               preferred_element_type=F32)
               + jnp.dot(p_n.astype(BF16), vn, preferred_element_type=F32))
        out = num / den
        lse = m + jnp.log(den)
        ls = [lse[i * t_new:(i + 1) * t_new] for i in range(len(DILATIONS))]
        os_ = [out[i * t_new:(i + 1) * t_new] for i in range(len(DILATIONS))]
        mm = jnp.maximum(jnp.maximum(ls[0], ls[1]), ls[2])
        w = [jnp.exp(l - mm) for l in ls]
        o_ref[0, h] = (w[0] * os_[0] + w[1] * os_[1] + w[2] * os_[2]) / (w[0] + w[1] + w[2])
    cw = cw_ref[...]
    u = gc_ref[0] * hc_ref[0]
    st = st_ref[0]
    rows = [st[j:j + 1] for j in range(CONV_K - 1)] + [u[t:t + 1] for t in range(t_new)]
    gb = gb_ref[0]
    conv_ref[0] = jnp.concatenate(
        [gb[t:t + 1] * sum(cw[j:j + 1] * rows[t + j] for j in range(CONV_K)) for t in range(t_new)],
        axis=0)
    st_out_ref[0] = jnp.concatenate(rows[t_new:], axis=0)


SAMPLE_ROWS = 16


def _sample_bias_tables(step_bias, t_new, w_buf):
    cache_rows = []
    for b, d in zip(step_bias, DILATIONS):
        back = b[1:][::-1].T
        if d >= t_new:
            residue = np.eye(d, dtype=bool)[:t_new]
            span = jnp.where(residue[None, :, None, :], back[:, None, :, None], NEG)
            span = span.reshape(N_HEADS, t_new, SEG_KEYS * d)
        else:
            assert d == 1, "dilations between 1 and the number of new tokens are not supported"
            span = jnp.stack([jnp.pad(back[:, :SEG_KEYS - t], ((0, 0), (t, 0)), constant_values=NEG)
                              for t in range(t_new)], axis=1)
        cache_rows.append(jnp.pad(span, ((0, 0), (0, 0), (w_buf - SEG_KEYS * d, 0)),
                                  constant_values=NEG))
    n_rows = len(DILATIONS) * t_new
    bias_c = jnp.pad(jnp.concatenate(cache_rows, axis=1), ((0, 0), (0, SAMPLE_ROWS - n_rows), (0, 0)))

    rows, cols, which, steps = [], [], [], []
    for di, d in enumerate(DILATIONS):
        for t in range(t_new):
            for t2 in range(t + 1):
                if (t - t2) % d == 0:
                    rows.append(di * t_new + t)
                    cols.append(t2)
                    which.append(di)
                    steps.append((t - t2) // d)
    vals = jnp.stack(step_bias)[np.array(which), np.array(steps)]
    bias_n = jnp.full((N_HEADS, n_rows, LANES), NEG, F32).at[:, np.array(rows), np.array(cols)].set(vals.T)
    bias_n = jnp.pad(bias_n, ((0, 0), (0, SAMPLE_ROWS - n_rows), (0, 0)))
    return bias_c, bias_n


def _sample_mix(q, kn, vn, cache_k, cache_v, layer, bias_c, bias_n, gb, gc, hc, state, conv_w):
    db, t_new = q.shape[0], q.shape[1]
    w_buf = cache_k.shape[2]
    kt = cache_k.transpose(0, 1, 3, 4, 2)
    vt = cache_v.transpose(0, 1, 3, 4, 2)
    n_rows = len(DILATIONS) * t_new
    q_rows = jnp.pad(jnp.tile(q.transpose(0, 2, 1, 3), (1, 1, len(DILATIONS), 1)),
                     ((0, 0), (0, 0), (0, SAMPLE_ROWS - n_rows), (0, 0)))
    kn_pad = jnp.pad(kn.transpose(0, 2, 1, 3), ((0, 0), (0, 0), (0, SUBLANES - t_new), (0, 0)))
    vn_pad = jnp.pad(vn.transpose(0, 2, 1, 3), ((0, 0), (0, 0), (0, SUBLANES - t_new), (0, 0)))
    row = pl.BlockSpec((1, t_new, CONV_DIM), lambda b: (b, 0, 0))
    new_spec = pl.BlockSpec((1, N_HEADS, SUBLANES, HEAD_DIM), lambda b: (b, 0, 0, 0))
    win_spec = pl.BlockSpec((1, 1, N_HEADS, HEAD_DIM, w_buf), lambda b: (layer, b, 0, 0, 0))
    attn, conv, st = pl.pallas_call(
        functools.partial(_sample_mix_kernel, t_new=t_new),
        grid=(db,),
        in_specs=[pl.BlockSpec((1, N_HEADS, SAMPLE_ROWS, HEAD_DIM), lambda b: (b, 0, 0, 0)),
                  new_spec, new_spec, win_spec, win_spec,
                  pl.BlockSpec(bias_c.shape, lambda b: (0, 0, 0)),
                  pl.BlockSpec(bias_n.shape, lambda b: (0, 0, 0)),
                  row, row, row,
                  pl.BlockSpec((1, CONV_K - 1, CONV_DIM), lambda b: (b, 0, 0)),
                  pl.BlockSpec((CONV_K, CONV_DIM), lambda b: (0, 0))],
        out_specs=[pl.BlockSpec((1, N_HEADS, t_new, HEAD_DIM), lambda b: (b, 0, 0, 0)), row,
                   pl.BlockSpec((1, CONV_K - 1, CONV_DIM), lambda b: (b, 0, 0))],
        out_shape=[jax.ShapeDtypeStruct((db, N_HEADS, t_new, HEAD_DIM), F32),
                   jax.ShapeDtypeStruct((db, t_new, CONV_DIM), F32),
                   jax.ShapeDtypeStruct((db, CONV_K - 1, CONV_DIM), F32)],
        compiler_params=_params("arbitrary"),
        name="sample_mix",
    )(q_rows, kn_pad, vn_pad, kt, vt, bias_c, bias_n, gb, gc, hc, state, conv_w)
    return attn.transpose(0, 2, 1, 3), conv, st


def _to_slabs(x):
    n = x.shape[0]
    return x.reshape(n, N_SLABS, LANES).transpose(1, 0, 2)


def kernel(x_prompt, x_sample, cache_k, cache_v, state_conv, rel_bias, norm_mix, norm_ffn,
           norm_final, w_in, conv_w, w_out, w_router_group, w_router_expert, w_gate, w_up,
           w_down):
    batch, seq, _ = x_prompt.shape
    db, t_new, _ = x_sample.shape
    depth = w_in.shape[0]
    w_keep = min(MAX_WINDOW, seq)

    hp = x_prompt.reshape(batch * seq, D_MODEL)
    hs = x_sample.reshape(db * t_new, D_MODEL)
    step_bias = _step_bias(rel_bias)
    bias_prompt = _prompt_bias_tables(step_bias)
    bias_c, bias_n = _sample_bias_tables(step_bias, t_new, cache_k.shape[2])
    g_final = norm_final.reshape(1, D_MODEL)

    tm_p = 256
    n_s = db * t_new
    nt_p = batch * seq // tm_p
    pad_per_tile = N_EXPERTS * (RUN_ALIGN - 1)
    max_tiles = -(-(2 * (batch * seq + n_s) + (nt_p + 1) * pad_per_tile) // MOE_TILE) + N_EXPERTS
    buf_rows = max_tiles * MOE_TILE + _local_rows(tm_p)

    pc, sk, sv, sc = [], [], [], []
    windows = None
    for l in range(depth):
        w_in_b = w_in[l].astype(BF16)
        w_out_b = w_out[l].astype(BF16)
        w_router_t = jnp.pad(jnp.concatenate([w_router_expert[l], w_router_group[l]], axis=1).T,
                             ((0, ROUTER_ROWS - N_EXPERTS - N_GROUPS), (0, 0)))
        g_mix = norm_mix[l].reshape(1, D_MODEL)
        g_ffn = norm_ffn[l].reshape(1, D_MODEL)
        last = l == depth - 1

        q, k, v, conv, u_last, *windows = _in_proj(hp, g_mix, w_in_b, conv_w[l], seq, w_keep, l, depth,
                                                   windows)
        attn = _attention_prompt(q, k, v, bias_prompt, batch, seq)
        h1, xn, route, routet, cnt = _out_proj(attn, conv, hp, w_out_b, g_ffn, w_router_t,
                                               tm=4 * tm_p, sub=tm_p)
        pc.append(u_last)

        proj = _in_proj_sample(hs, g_mix, w_in_b)
        qs = proj[:, :ATTN_WIDTH].reshape(db, t_new, N_HEADS, HEAD_DIM)
        ks = proj[:, ATTN_WIDTH:2 * ATTN_WIDTH].reshape(db, t_new, N_HEADS, HEAD_DIM)
        vs = proj[:, 2 * ATTN_WIDTH:3 * ATTN_WIDTH].reshape(db, t_new, N_HEADS, HEAD_DIM)
        c0 = 3 * ATTN_WIDTH
        gb = proj[:, c0:c0 + CONV_DIM].reshape(db, t_new, CONV_DIM)
        gc = proj[:, c0 + CONV_DIM:c0 + 2 * CONV_DIM].reshape(db, t_new, CONV_DIM)
        hc = proj[:, c0 + 2 * CONV_DIM:].reshape(db, t_new, CONV_DIM)
        attn_s, conv_s, state_s = _sample_mix(qs, ks, vs, cache_k, cache_v, l, bias_c, bias_n,
                                              gb, gc, hc, state_conv[l], conv_w[l])
        h1s, xns, route_s, routet_s, cnt_s = _out_proj(
            _to_slabs(attn_s.reshape(n_s, ATTN_WIDTH)), conv_s.reshape(n_s, CONV_DIM), hs,
            w_out_b, g_ffn, w_router_t, tm=n_s, sub=n_s)

        table, n_chunks, first_tile, tiles_e, rows_e, total = _moe_plan(
            jnp.concatenate([cnt[:, :, 0], cnt_s[:, :, 0]], axis=0), max_tiles)
        xs = _dispatch(table[:nt_p], xn, route, tm_p, buf_rows)
        xs = _dispatch(table[nt_p:], xns, route_s, n_s, buf_rows, xs_prev=xs)
        eo = _experts(first_tile, tiles_e, rows_e, total, xs, w_gate, w_up, w_down, l)
        hp = _combine(table[:nt_p], n_chunks[:nt_p], h1, routet, g_final, eo, tm_p, last)
        hs = _combine(table[nt_p:], n_chunks[nt_p:], h1s, routet_s, g_final, eo, n_s, last)
        sk.append(ks)
        sv.append(vs)
        sc.append(state_s)

    k_win, v_win = (w.transpose(0, 1, 4, 2, 3) for w in windows)
    return (hp.reshape(batch, seq, D_MODEL), hs.reshape(db, t_new, D_MODEL),
            k_win, v_win, jnp.stack(pc),
            jnp.stack(sk), jnp.stack(sv), jnp.stack(sc))
```

```python
import functools
import math

import jax
import jax.numpy as jnp
import numpy as np
from jax import lax
from jax.experimental import pallas as pl
from jax.experimental.pallas import tpu as pltpu

F32 = jnp.float32
BF16 = jnp.bfloat16

D_MODEL = 1024
N_HEADS = 8
HEAD_DIM = 64
ATTN_WIDTH = N_HEADS * HEAD_DIM
CONV_DIM = D_MODEL - ATTN_WIDTH
CONV_K = 3
PROJ_WIDTH = 3 * ATTN_WIDTH + 3 * CONV_DIM
DILATIONS = (1, 4, 16)
SEG_KEYS = 128
Q_BLOCK = 128
MAX_WINDOW = 2048
N_BUCKETS = 32
MAX_DISTANCE = 2048
N_GROUPS = 4
EXPERTS_PER_GROUP = 8
N_EXPERTS = N_GROUPS * EXPERTS_PER_GROUP
D_EXPERT = D_MODEL // 4
EPS = 1e-6
NEG = -1e30
SCALE = 1.0 / math.sqrt(HEAD_DIM)

LANES = 128
SUBLANES = 8
N_SLABS = ATTN_WIDTH // LANES
VMEM_LIMIT = 48 * 1024 * 1024
REGROUP = 4
UNITS_PER_TRIP = 16


def _params(*sem):
    return pltpu.CompilerParams(dimension_semantics=sem, vmem_limit_bytes=VMEM_LIMIT)


def _rms(x, g):
    return x * lax.rsqrt(jnp.mean(x * x, axis=-1, keepdims=True) + EPS) * g


def _rel_bucket(dist):
    max_exact = N_BUCKETS // 2
    d_f = jnp.maximum(dist, 1).astype(F32)
    large = max_exact + (jnp.log(d_f / max_exact) / math.log(MAX_DISTANCE / max_exact)
                         * (N_BUCKETS - max_exact)).astype(jnp.int32)
    large = jnp.minimum(large, N_BUCKETS - 1)
    return jnp.where(dist < max_exact, dist, large)


def _in_proj_kernel(x_ref, g_ref, w_ref, cw_ref, *rest, tm, tiles_per_seq, win_tiles, chained):
    (q_ref, k_ref, v_ref, conv_ref, ulast_ref, kwin_ref, vwin_ref, ext_ref) = rest[2:] if chained else rest
    i = pl.program_id(0)
    xn = _rms(x_ref[...], g_ref[...])
    proj = jnp.dot(xn.astype(BF16), w_ref[...], preferred_element_type=F32)
    for s in range(N_SLABS):
        q_ref[s] = proj[:, s * LANES:(s + 1) * LANES]
        k_ref[s] = proj[:, ATTN_WIDTH + s * LANES:ATTN_WIDTH + (s + 1) * LANES]
        v_ref[s] = proj[:, 2 * ATTN_WIDTH + s * LANES:2 * ATTN_WIDTH + (s + 1) * LANES]

    @pl.when(i % tiles_per_seq >= tiles_per_seq - win_tiles)
    def _():
        heads_per_slab = LANES // HEAD_DIM
        for s in range(N_SLABS):
            for src, dst in ((k_ref, kwin_ref), (v_ref, vwin_ref)):
                dst[0, 0, heads_per_slab * s:heads_per_slab * (s + 1)] = (
                    src[s].T.reshape(heads_per_slab, HEAD_DIM, tm))
    c0 = 3 * ATTN_WIDTH
    gb = proj[:, c0:c0 + CONV_DIM]
    u = proj[:, c0 + CONV_DIM:c0 + 2 * CONV_DIM] * proj[:, c0 + 2 * CONV_DIM:c0 + 3 * CONV_DIM]

    @pl.when(i % tiles_per_seq == 0)
    def _():
        ext_ref[0:SUBLANES, :] = jnp.zeros((SUBLANES, CONV_DIM), F32)

    ext_ref[SUBLANES:SUBLANES + tm, :] = u
    u1 = ext_ref[SUBLANES - 1:SUBLANES - 1 + tm, :]
    u2 = ext_ref[SUBLANES - 2:SUBLANES - 2 + tm, :]
    cw = cw_ref[...]
    conv_ref[...] = gb * (cw[0:1] * u2 + cw[1:2] * u1 + cw[2:3] * u)
    ulast_ref[0] = ext_ref[tm + SUBLANES - 2:tm + SUBLANES, :]
    ext_ref[0:SUBLANES, :] = ext_ref[tm:tm + SUBLANES, :]


def _in_proj(x, g, w_bf16, conv_w, seq_len, w_keep, layer, depth, windows=None, tm=1024):
    n = x.shape[0]
    n_seq = n // seq_len
    slab = jax.ShapeDtypeStruct((N_SLABS, n, LANES), F32)
    slab_spec = pl.BlockSpec((N_SLABS, tm, LANES), lambda i: (0, i, 0))
    tiles_per_seq = seq_len // tm
    win_tiles = w_keep // tm
    win = jax.ShapeDtypeStruct((depth, n_seq, N_HEADS, HEAD_DIM, w_keep), F32)
    win_spec = pl.BlockSpec(
        (1, 1, N_HEADS, HEAD_DIM, tm),
        lambda i: (layer, i // tiles_per_seq, 0, 0,
                   jnp.maximum(i % tiles_per_seq - (tiles_per_seq - win_tiles), 0)))
    chained = windows is not None
    in_specs = [pl.BlockSpec((tm, D_MODEL), lambda i: (i, 0)),
                pl.BlockSpec((1, D_MODEL), lambda i: (0, 0)),
                pl.BlockSpec((D_MODEL, PROJ_WIDTH), lambda i: (0, 0)),
                pl.BlockSpec((CONV_K, CONV_DIM), lambda i: (0, 0))]
    args = [x, g, w_bf16, conv_w]
    if chained:
        in_specs += [pl.BlockSpec(memory_space=pl.ANY)] * 2
        args += list(windows)
    return pl.pallas_call(
        functools.partial(_in_proj_kernel, tm=tm, tiles_per_seq=tiles_per_seq, win_tiles=win_tiles,
                          chained=chained),
        grid=(n // tm,),
        in_specs=in_specs,
        out_specs=[slab_spec, slab_spec, slab_spec,
                   pl.BlockSpec((tm, CONV_DIM), lambda i: (i, 0)),
                   pl.BlockSpec((1, CONV_K - 1, CONV_DIM), lambda i: (i // tiles_per_seq, 0, 0)),
                   win_spec, win_spec],
        out_shape=[slab, slab, slab,
                   jax.ShapeDtypeStruct((n, CONV_DIM), F32),
                   jax.ShapeDtypeStruct((n_seq, CONV_K - 1, CONV_DIM), F32), win, win],
        scratch_shapes=[pltpu.VMEM((tm + SUBLANES, CONV_DIM), F32)],
        input_output_aliases={4: 5, 5: 6} if chained else {},
        compiler_params=_params("arbitrary"),
        name="in_proj_prompt",
    )(*args)


def _attn_kernel(q_ref, k_ref, v_ref, bias_ref, o_ref, q4, k4, v4, ktail, vtail, oacc, lacc, *, sb):
    j = pl.program_id(2)
    slot = j % 2
    prev = 1 - slot
    seg = sb // REGROUP

    @pl.when(j == 0)
    def _():
        k4[1] = jnp.zeros((sb, LANES), F32)
        v4[1] = jnp.zeros((sb, LANES), F32)
        ktail[...] = jnp.zeros((Q_BLOCK, LANES), F32)
        vtail[...] = jnp.zeros((Q_BLOCK, LANES), F32)

    for r in range(REGROUP):
        q4[r * seg:(r + 1) * seg, :] = q_ref[pl.ds(r, seg, stride=REGROUP), :] * SCALE
        k4[slot, r * seg:(r + 1) * seg, :] = k_ref[pl.ds(r, seg, stride=REGROUP), :]
        v4[slot, r * seg:(r + 1) * seg, :] = v_ref[pl.ds(r, seg, stride=REGROUP), :]

    lane = lax.broadcasted_iota(jnp.int32, (Q_BLOCK, LANES), 1)
    low = lane < HEAD_DIM
    ones = jnp.ones((2 * Q_BLOCK, LANES), BF16)

    def block(start):
        return pl.ds(pl.multiple_of(start, Q_BLOCK), Q_BLOCK)

    for di, d in enumerate(DILATIONS):
        nq = sb // (Q_BLOCK * d)

        def unit(u, di=di, d=d, nq=nq):
            r = u // nq
            n = u % nq
            qstart = r + d * Q_BLOCK * n
            if d == 1:
                here = block(n * Q_BLOCK)
                before = block(jnp.maximum(n - 1, 0) * Q_BLOCK)
                q = q_ref[here, :] * SCALE
                k_new, v_new = k_ref[here, :], v_ref[here, :]
                k_old = jnp.where(n == 0, ktail[...], k_ref[before, :])
                v_old = jnp.where(n == 0, vtail[...], v_ref[before, :])
            elif d == REGROUP:
                here = block(r * seg + n * Q_BLOCK)
                before = block(r * seg + jnp.maximum(n - 1, 0) * Q_BLOCK)
                last = block(r * seg + seg - Q_BLOCK)
                q = q4[here, :]
                k_new, v_new = k4[slot, here, :], v4[slot, here, :]
                k_old = jnp.where(n == 0, k4[prev, last, :], k4[slot, before, :])
                v_old = jnp.where(n == 0, v4[prev, last, :], v4[slot, before, :])
            else:
                rows = pl.ds((u % REGROUP) * seg + u // REGROUP, Q_BLOCK, stride=REGROUP)
                q = q4[rows, :]
                k_new, v_new = k4[slot, rows, :], v4[slot, rows, :]
                k_old, v_old = k4[prev, rows, :], v4[prev, rows, :]
            kk = jnp.concatenate([k_old, k_new], axis=0).astype(BF16)
            vv = jnp.concatenate([v_old, v_new], axis=0).astype(BF16)
            qm = jnp.concatenate([jnp.where(low, q, 0.0), jnp.where(low, 0.0, q)],
                                 axis=0).astype(BF16)
            s = lax.dot_general(qm, kk, (((1,), (1,)), ((), ())),
                                preferred_element_type=F32)
            first = jnp.logical_and(j == 0, n == 0).astype(jnp.int32)
            s = s + bias_ref[di, first]
            m = jnp.max(s, axis=-1, keepdims=True)
            p = jnp.exp(s - m).astype(BF16)
            pv = jnp.dot(p, jnp.concatenate([vv, ones], axis=1),
                         preferred_element_type=F32)
            o_sel = jnp.where(low, pv[0:Q_BLOCK, 0:LANES], pv[Q_BLOCK:, 0:LANES])
            l_sel = jnp.where(low, pv[0:Q_BLOCK, LANES:], pv[Q_BLOCK:, LANES:])
            m_sel = jnp.where(low, jnp.broadcast_to(m[0:Q_BLOCK], (Q_BLOCK, LANES)),
                              jnp.broadcast_to(m[Q_BLOCK:], (Q_BLOCK, LANES)))
            oacc[di, pl.ds(qstart, Q_BLOCK, stride=d), :] = o_sel / l_sel
            lacc[di, pl.ds(qstart, Q_BLOCK, stride=d), :] = m_sel + jnp.log(l_sel)

        def group(g, carry, unit=unit):
            for uu in range(UNITS_PER_TRIP):
                unit(g * UNITS_PER_TRIP + uu)
            return carry

        lax.fori_loop(0, sb // Q_BLOCK // UNITS_PER_TRIP, group, 0)

    for c in range(sb // 256):
        rows = slice(c * 256, (c + 1) * 256)
        l0, l1, l2 = lacc[0, rows, :], lacc[1, rows, :], lacc[2, rows, :]
        mm = jnp.maximum(jnp.maximum(l0, l1), l2)
        w0, w1, w2 = jnp.exp(l0 - mm), jnp.exp(l1 - mm), jnp.exp(l2 - mm)
        num = w0 * oacc[0, rows, :] + w1 * oacc[1, rows, :] + w2 * oacc[2, rows, :]
        o_ref[rows, :] = num / (w0 + w1 + w2)
    ktail[...] = k_ref[sb - Q_BLOCK:sb, :]
    vtail[...] = v_ref[sb - Q_BLOCK:sb, :]


def _step_bias(rel_bias):
    steps = jnp.arange(SEG_KEYS + 1)
    return [rel_bias[_rel_bucket(steps * d)].astype(F32) for d in DILATIONS]


def _prompt_bias_tables(step_bias):
    width = 3 * Q_BLOCK
    k_loc = jnp.arange(2 * Q_BLOCK)
    tables = []
    for b in step_bias:
        g = jnp.concatenate([b[::-1].T, jnp.full((N_HEADS, width - SEG_KEYS), NEG, F32)], axis=1)
        flat = jnp.broadcast_to(g[:, None, :], (N_HEADS, Q_BLOCK, width + 1)).reshape(N_HEADS, -1)
        x = flat[:, :Q_BLOCK * width].reshape(N_HEADS, Q_BLOCK, width)
        t = x[:, :, :2 * Q_BLOCK]
        t_first = jnp.where(k_loc[None, None, :] >= Q_BLOCK, t, NEG)
        tables.append(jnp.stack([t.reshape(N_SLABS, 2 * Q_BLOCK, 2 * Q_BLOCK),
                                 t_first.reshape(N_SLABS, 2 * Q_BLOCK, 2 * Q_BLOCK)]))
    return jnp.stack(tables)


def _attention_prompt(q, k, v, bias_tab, n_seq, seq_len, sb=2048):
    assert DILATIONS == (1, REGROUP, REGROUP * REGROUP)
    n = q.shape[1]
    nsb = seq_len // sb
    blk = pl.BlockSpec((None, sb, LANES), lambda b, s, j: (s, b * nsb + j, 0))
    return pl.pallas_call(
        functools.partial(_attn_kernel, sb=sb),
        grid=(n_seq, N_SLABS, nsb),
        in_specs=[blk, blk, blk,
                  pl.BlockSpec((len(DILATIONS), 2, None, 2 * Q_BLOCK, 2 * Q_BLOCK),
                               lambda b, s, j: (0, 0, s, 0, 0))],
        out_specs=blk,
        out_shape=jax.ShapeDtypeStruct((N_SLABS, n, LANES), F32),
        scratch_shapes=[pltpu.VMEM((sb, LANES), F32),
                        pltpu.VMEM((2, sb, LANES), F32), pltpu.VMEM((2, sb, LANES), F32),
                        pltpu.VMEM((Q_BLOCK, LANES), F32), pltpu.VMEM((Q_BLOCK, LANES), F32),
                        pltpu.VMEM((len(DILATIONS), sb, LANES), F32),
                        pltpu.VMEM((len(DILATIONS), sb, LANES), F32)],
        compiler_params=_params("arbitrary", "arbitrary", "arbitrary"),
        name="attention_prompt",
    )(q, k, v, bias_tab)


ROUTER_ROWS = 48
MOE_TILE = 256
RUN_ALIGN = 2 * SUBLANES
TABLE_LANES = 128
EXPERT_BUFFERS = 6


def _local_rows(tm):
    return -(-(2 * tm + N_EXPERTS * (RUN_ALIGN - 1)) // LANES) * LANES


def _route(lg):
    tm = lg.shape[1]
    gl = lg[N_EXPERTS:N_EXPERTS + N_GROUPS]
    grow = lax.broadcasted_iota(jnp.int32, (N_GROUPS, tm), 0)
    gmax = jnp.max(gl, axis=0, keepdims=True)
    gsum = jnp.sum(jnp.exp(gl - gmax), axis=0, keepdims=True)
    gidx = jnp.min(jnp.where(gl == gmax, grow, N_GROUPS), axis=0, keepdims=True)
    p_grp = 1.0 / gsum
    el = lg[0:N_EXPERTS]
    e = lax.broadcasted_iota(jnp.int32, (N_EXPERTS, tm), 0)
    in_grp = (e // EXPERTS_PER_GROUP) == gidx
    elog = jnp.where(in_grp, el, -jnp.inf)
    emax = jnp.max(elog, axis=0, keepdims=True)
    eexp = jnp.exp(elog - emax)
    eprob = jnp.where(in_grp, eexp / jnp.sum(eexp, axis=0, keepdims=True), -1.0)
    p1 = jnp.max(eprob, axis=0, keepdims=True)
    i1 = jnp.min(jnp.where(eprob == p1, e, N_EXPERTS), axis=0, keepdims=True)
    rest = jnp.where(e == i1, -1.0, eprob)
    p2 = jnp.max(rest, axis=0, keepdims=True)
    i2 = jnp.min(jnp.where(rest == p2, e, N_EXPERTS), axis=0, keepdims=True)
    tot = p1 + p2
    return e == i1, e == i2, p_grp * (p1 / tot), p_grp * (p2 / tot)


def _split_bf16(x):
    hi = x.astype(BF16)
    return hi, (x - hi.astype(F32)).astype(BF16)


def _out_proj_kernel(attn_ref, conv_ref, h_ref, wo_ref, g_ref, wr_ref, upper_ref, lower_ref,
                     h1_ref, xn_ref, route_ref, routet_ref, cnt_ref, *, tm, sub):
    contract_last = (((1,), (1,)), ((), ()))
    mix = jnp.concatenate([attn_ref[s] for s in range(N_SLABS)] + [conv_ref[...]], axis=1)
    h1_ref[...] = h_ref[...] + jnp.dot(mix.astype(BF16), wo_ref[...], preferred_element_type=F32)
    for t in range(tm // sub):
        rs = slice(t * sub, (t + 1) * sub)
        xn = _rms(h1_ref[rs, :], g_ref[...])
        xh, xl = _split_bf16(xn)
        xn_ref[rs, :] = xh
        both = lax.dot_general(wr_ref[...], xh, contract_last, preferred_element_type=F32)
        lg = (both[:ROUTER_ROWS] + both[ROUTER_ROWS:]
              + lax.dot_general(wr_ref[:ROUTER_ROWS, :], xl, contract_last, preferred_element_type=F32))
        sel1, sel2, g1, g2 = _route(lg)
        chosen = jnp.where(sel1 | sel2, 1.0, 0.0)
        rank = jnp.dot(chosen.astype(BF16), upper_ref[...], preferred_element_type=F32)
        cnt = jnp.sum(chosen, axis=1, keepdims=True)
        cnt_pad = jnp.floor((cnt + (RUN_ALIGN - 1)) * (1.0 / RUN_ALIGN)) * RUN_ALIGN
        cnt_b = jnp.broadcast_to(cnt_pad, (N_EXPERTS, LANES))
        start = jnp.dot(lower_ref[...], cnt_b.astype(BF16), preferred_element_type=F32)
        pos = start[:, 0:1] + rank
        pos1 = jnp.sum(jnp.where(sel1, pos, 0.0), axis=0, keepdims=True)
        pos2 = jnp.sum(jnp.where(sel2, pos, 0.0), axis=0, keepdims=True)
        rows = jnp.concatenate([pos1, pos2, g1, g2, jnp.zeros((SUBLANES - 4, sub), F32)], axis=0)
        route_ref[t] = rows
        routet_ref[rs, :] = jnp.concatenate([rows, jnp.zeros((LANES - SUBLANES, sub), F32)], axis=0).T
        cnt_ref[t] = cnt_b


def _out_proj(attn, conv, h, wo_bf16, g, w_router_t, tm, sub):
    n = h.shape[0]
    w_router = jnp.concatenate(_split_bf16(w_router_t), axis=0)
    idx = jnp.arange(sub)
    upper = (idx[:, None] < idx[None, :]).astype(BF16)
    eidx = jnp.arange(N_EXPERTS)
    lower = (eidx[None, :] < eidx[:, None]).astype(BF16)
    const = lambda shape: pl.BlockSpec(shape, lambda i: (0,) * len(shape))
    per_step = tm // sub
    return pl.pallas_call(
        functools.partial(_out_proj_kernel, tm=tm, sub=sub),
        grid=(n // tm,),
        in_specs=[pl.BlockSpec((N_SLABS, tm, LANES), lambda i: (0, i, 0)),
                  pl.BlockSpec((tm, CONV_DIM), lambda i: (i, 0)),
                  pl.BlockSpec((tm, D_MODEL), lambda i: (i, 0)),
                  const((D_MODEL, D_MODEL)), const((1, D_MODEL)),
                  const((2 * ROUTER_ROWS, D_MODEL)),
                  const((sub, sub)), const((N_EXPERTS, N_EXPERTS))],
        out_specs=[pl.BlockSpec((tm, D_MODEL), lambda i: (i, 0)),
                   pl.BlockSpec((tm, D_MODEL), lambda i: (i, 0)),
                   pl.BlockSpec((per_step, SUBLANES, sub), lambda i: (i, 0, 0)),
                   pl.BlockSpec((tm, LANES), lambda i: (i, 0)),
                   pl.BlockSpec((per_step, N_EXPERTS, LANES), lambda i: (i, 0, 0))],
        out_shape=[jax.ShapeDtypeStruct((n, D_MODEL), F32),
                   jax.ShapeDtypeStruct((n, D_MODEL), BF16),
                   jax.ShapeDtypeStruct((n // sub, SUBLANES, sub), F32),
                   jax.ShapeDtypeStruct((n, LANES), F32),
                   jax.ShapeDtypeStruct((n // sub, N_EXPERTS, LANES), F32)],
        compiler_params=_params("arbitrary"),
        name="out_proj",
    )(attn, conv, h, wo_bf16, g, w_router, upper, lower)


def _moe_plan(cnt_tiles, max_tiles):
    c = cnt_tiles.astype(jnp.int32)
    rows_e = jnp.sum(c, axis=0)
    tiles_e = (rows_e + MOE_TILE - 1) // MOE_TILE
    cum = jnp.cumsum(tiles_e)
    total = cum[-1]
    first_tile = cum - tiles_e

    run_start = jnp.cumsum(c, axis=0) - c
    local_start = jnp.cumsum(c, axis=1) - c
    local_end = local_start + c
    row = jnp.arange(TABLE_LANES, dtype=jnp.int32) * RUN_ALIGN
    inside = (local_start[:, None, :] <= row[None, :, None]) & (row[None, :, None] < local_end[:, None, :])
    base = first_tile[None] * MOE_TILE + run_start - local_start
    n_chunks = jnp.sum(c, axis=1) // RUN_ALIGN
    table = jnp.sum(jnp.where(inside, base[:, None, :] + row[None, :, None], 0), axis=-1)
    live = jnp.arange(TABLE_LANES)[None] < n_chunks[:, None]
    table_out = jnp.where(live, table, max_tiles * MOE_TILE + row[None])
    table_in = jnp.where(live, table, row[None])
    i32 = lambda a: a.astype(jnp.int32)
    return (i32(table_out), i32(table_in), i32(first_tile), i32(tiles_e), i32(rows_e),
            i32(total).reshape(1))


def _dispatch_kernel(tbl_ref, xn_ref, route_ref, *rest, tm, rl, chained):
    xs_hbm, lsort, sem = rest[1:] if chained else rest
    i = pl.program_id(0)
    slot = i % 2
    rows = route_ref[0]
    pos1 = rows[0:1].astype(jnp.int32)
    pos2 = rows[1:2].astype(jnp.int32)
    p = lax.broadcasted_iota(jnp.int32, (rl, tm), 0)
    one_hot = jnp.where((p == pos1) | (p == pos2), 1.0, 0.0).astype(BF16)
    lsort[slot] = jnp.dot(one_hot, xn_ref[...], preferred_element_type=F32).astype(BF16)

    def tile_copies_done(s):
        pltpu.make_async_copy(lsort.at[s], xs_hbm.at[pl.ds(0, rl), :], sem.at[s]).wait()

    @pl.when(i > 0)
    def _():
        tile_copies_done(1 - slot)

    for q in range(rl // RUN_ALIGN):
        row = pl.multiple_of(tbl_ref[i * TABLE_LANES + q], RUN_ALIGN)
        pltpu.make_async_copy(lsort.at[slot, pl.ds(q * RUN_ALIGN, RUN_ALIGN), :],
                              xs_hbm.at[pl.ds(row, RUN_ALIGN), :], sem.at[slot]).start()

    @pl.when(i == pl.num_programs(0) - 1)
    def _():
        tile_copies_done(slot)


def _dispatch(table, xn, route, tm, total_rows, xs_prev=None):
    nt = xn.shape[0] // tm
    rl = _local_rows(tm)
    chained = xs_prev is not None
    in_specs = [pl.BlockSpec((tm, D_MODEL), lambda i, t: (i, 0)),
                pl.BlockSpec((1, SUBLANES, tm), lambda i, t: (i, 0, 0))]
    args = [table.reshape(-1), xn, route]
    if chained:
        in_specs.append(pl.BlockSpec(memory_space=pl.ANY))
        args.append(xs_prev)
    return pl.pallas_call(
        functools.partial(_dispatch_kernel, tm=tm, rl=rl, chained=chained),
        grid_spec=pltpu.PrefetchScalarGridSpec(
            num_scalar_prefetch=1, grid=(nt,), in_specs=in_specs,
            out_specs=pl.BlockSpec(memory_space=pl.ANY),
            scratch_shapes=[pltpu.VMEM((2, rl, D_MODEL), BF16), pltpu.SemaphoreType.DMA((2,))]),
        out_shape=jax.ShapeDtypeStruct((total_rows, D_MODEL), BF16),
        input_output_aliases={3: 0} if chained else {},
        compiler_params=_params("arbitrary"),
        name="moe_dispatch",
    )(*args)


def _expert_kernel(first_ref, tiles_ref, rows_ref, tot_ref, xs_hbm, wg_ref, wu_ref, wd_ref, eo_hbm,
                   xbuf, obuf, wg_b, wu_b, wd_b, sem_in, sem_out):
    e = pl.program_id(0)
    first_tile = first_ref[e]
    n_tiles = tiles_ref[e]
    total = tot_ref[0]

    def in_copy(g):
        return pltpu.make_async_copy(
            xs_hbm.at[pl.ds(pl.multiple_of(g * MOE_TILE, MOE_TILE), MOE_TILE), :],
            xbuf.at[g % EXPERT_BUFFERS], sem_in.at[g % EXPERT_BUFFERS])

    def out_copy(g):
        return pltpu.make_async_copy(
            obuf.at[g % EXPERT_BUFFERS],
            eo_hbm.at[pl.ds(pl.multiple_of(g * MOE_TILE, MOE_TILE), MOE_TILE), :],
            sem_out.at[g % EXPERT_BUFFERS])

    lookahead = EXPERT_BUFFERS - 2

    @pl.when(e == 0)
    def _():
        for g0 in range(lookahead):
            @pl.when(g0 < total)
            def _():
                in_copy(g0).start()

    @pl.when(n_tiles > 0)
    def _():
        wg_b[...] = wg_ref[0, 0].astype(BF16)
        wu_b[...] = wu_ref[0, 0].astype(BF16)
        wd_b[...] = wd_ref[0, 0].astype(BF16)

    def acquire(g, j):
        @pl.when(g + lookahead < total)
        def _():
            in_copy(g + lookahead).start()

        in_copy(g).wait()

        @pl.when(g >= EXPERT_BUFFERS)
        def _():
            out_copy(g - EXPERT_BUFFERS).wait()

        row = lax.broadcasted_iota(jnp.int32, (MOE_TILE, 1), 0)
        x = xbuf[g % EXPERT_BUFFERS]
        return jnp.where(row < rows_ref[e] - j * MOE_TILE, x, jnp.zeros_like(x))

    def mlp(x):
        gate = jnp.dot(x, wg_b[...], preferred_element_type=F32)
        up = jnp.dot(x, wu_b[...], preferred_element_type=F32)
        hdn = gate * jax.nn.sigmoid(gate) * up
        return jnp.dot(hdn.astype(BF16), wd_b[...], preferred_element_type=F32).astype(BF16)

    def pair(jp, carry):
        j = 2 * jp
        g = first_tile + j
        x0 = acquire(g, j)
        x1 = acquire(g + 1, j + 1)
        y = mlp(jnp.concatenate([x0, x1], axis=0))
        obuf[g % EXPERT_BUFFERS] = y[:MOE_TILE]
        obuf[(g + 1) % EXPERT_BUFFERS] = y[MOE_TILE:]
        out_copy(g).start()
        out_copy(g + 1).start()
        return carry

    lax.fori_loop(0, n_tiles // 2, pair, 0)

    @pl.when(n_tiles % 2 == 1)
    def _():
        j = n_tiles - 1
        g = first_tile + j
        obuf[g % EXPERT_BUFFERS] = mlp(acquire(g, j))
        out_copy(g).start()

    @pl.when(e == pl.num_programs(0) - 1)
    def _():
        for back in range(EXPERT_BUFFERS, 0, -1):
            @pl.when(total >= back)
            def _():
                out_copy(total - back).wait()


def _experts(first_tile, tiles_e, rows_e, total, xs, w_gate, w_up, w_down, layer):
    weight = lambda e, f, t, r, n: (layer, e, 0, 0)
    return pl.pallas_call(
        _expert_kernel,
        grid_spec=pltpu.PrefetchScalarGridSpec(
            num_scalar_prefetch=4, grid=(N_EXPERTS,),
            in_specs=[pl.BlockSpec(memory_space=pl.ANY),
                      pl.BlockSpec((1, 1, D_MODEL, D_EXPERT), weight),
                      pl.BlockSpec((1, 1, D_MODEL, D_EXPERT), weight),
                      pl.BlockSpec((1, 1, D_EXPERT, D_MODEL), weight)],
            out_specs=pl.BlockSpec(memory_space=pl.ANY),
            scratch_shapes=[pltpu.VMEM((EXPERT_BUFFERS, MOE_TILE, D_MODEL), BF16),
                            pltpu.VMEM((EXPERT_BUFFERS, MOE_TILE, D_MODEL), BF16),
                            pltpu.VMEM((D_MODEL, D_EXPERT), BF16),
                            pltpu.VMEM((D_MODEL, D_EXPERT), BF16),
                            pltpu.VMEM((D_EXPERT, D_MODEL), BF16),
                            pltpu.SemaphoreType.DMA((EXPERT_BUFFERS,)),
                            pltpu.SemaphoreType.DMA((EXPERT_BUFFERS,))]),
        out_shape=jax.ShapeDtypeStruct(xs.shape, BF16),
        compiler_params=_params("arbitrary"),
        name="moe_experts",
    )(first_tile, tiles_e, rows_e, total, xs, w_gate, w_up, w_down)


def _combine_kernel(tbl_ref, h1_ref, routet_ref, gf_ref, eo_hbm, o_ref, leo, sem,
                    *, tm, rl, final_norm):
    i = pl.program_id(0)
    n = pl.num_programs(0)
    slot = i % 2

    def gather(tile, s):
        for q in range(rl // RUN_ALIGN):
            row = pl.multiple_of(tbl_ref[tile * TABLE_LANES + q], RUN_ALIGN)
            pltpu.make_async_copy(eo_hbm.at[pl.ds(row, RUN_ALIGN), :],
                                  leo.at[s, pl.ds(q * RUN_ALIGN, RUN_ALIGN), :], sem.at[s]).start()

    @pl.when(i == 0)
    def _():
        gather(0, 0)

    @pl.when(i + 1 < n)
    def _():
        gather(i + 1, 1 - slot)

    pltpu.make_async_copy(eo_hbm.at[pl.ds(0, rl), :], leo.at[slot], sem.at[slot]).wait()

    rt = routet_ref[...]
    pos1 = rt[:, 0:1].astype(jnp.int32)
    pos2 = rt[:, 1:2].astype(jnp.int32)
    lane = lax.broadcasted_iota(jnp.int32, (tm, rl), 1)
    weights = jnp.where(lane == pos1, rt[:, 2:3], 0.0) + jnp.where(lane == pos2, rt[:, 3:4], 0.0)
    y = jnp.dot(weights.astype(BF16), leo[slot], preferred_element_type=F32)
    out = h1_ref[...] + y
    if final_norm:
        out = _rms(out, gf_ref[...])
    o_ref[...] = out


def _combine(table, h1, routet, g_final, eo, tm, final_norm):
    n = h1.shape[0]
    rl = _local_rows(tm)
    return pl.pallas_call(
        functools.partial(_combine_kernel, tm=tm, rl=rl, final_norm=final_norm),
        grid_spec=pltpu.PrefetchScalarGridSpec(
            num_scalar_prefetch=1, grid=(n // tm,),
            in_specs=[pl.BlockSpec((tm, D_MODEL), lambda i, t: (i, 0)),
                      pl.BlockSpec((tm, LANES), lambda i, t: (i, 0)),
                      pl.BlockSpec((1, D_MODEL), lambda i, t: (0, 0)),
                      pl.BlockSpec(memory_space=pl.ANY)],
            out_specs=pl.BlockSpec((tm, D_MODEL), lambda i, t: (i, 0)),
            scratch_shapes=[pltpu.VMEM((2, rl, D_MODEL), BF16), pltpu.SemaphoreType.DMA((2,))]),
        out_shape=jax.ShapeDtypeStruct((n, D_MODEL), F32),
        compiler_params=_params("arbitrary"),
        name="moe_combine",
    )(table.reshape(-1), h1, routet, g_final, eo)


def _in_proj_sample_kernel(x_ref, g_ref, w_ref, o_ref):
    xn = _rms(x_ref[...], g_ref[...])
    o_ref[...] = jnp.dot(xn.astype(BF16), w_ref[...], preferred_element_type=F32)


def _in_proj_sample(x, g, w_bf16):
    n = x.shape[0]
    return pl.pallas_call(
        _in_proj_sample_kernel,
        out_shape=jax.ShapeDtypeStruct((n, PROJ_WIDTH), F32),
        compiler_params=pltpu.CompilerParams(vmem_limit_bytes=VMEM_LIMIT),
        name="in_proj_sample",
    )(x, g, w_bf16)


def _sample_mix_kernel(q_ref, kn_ref, vn_ref, kt_ref, vt_ref,
                       bc_ref, bn_ref, gb_ref, gc_ref, hc_ref, st_ref, cw_ref,
                       o_ref, conv_ref, st_out_ref, *, t_new):
    contract_last = (((1,), (1,)), ((), ()))
    no_rows = jnp.zeros((LANES - SUBLANES, HEAD_DIM), F32)
    for h in range(N_HEADS):
        q = (q_ref[0, h] * SCALE).astype(BF16)
        s_c = jnp.dot(q, kt_ref[0, 0, h].astype(BF16), preferred_element_type=F32) + bc_ref[h]
        kn = jnp.concatenate([kn_ref[0, h], no_rows], axis=0).astype(BF16)
        vn = jnp.concatenate([vn_ref[0, h], no_rows], axis=0).astype(BF16)
        s_n = lax.dot_general(q, kn, contract_last, preferred_element_type=F32) + bn_ref[h]
        m = jnp.maximum(jnp.max(s_c, axis=-1, keepdims=True), jnp.max(s_n, axis=-1, keepdims=True))
        p_c = jnp.exp(s_c - m)
        p_n = jnp.exp(s_n - m)
        den = jnp.sum(p_c, axis=-1, keepdims=True) + jnp.sum(p_n, axis=-1, keepdims=True)
        num = (lax.dot_general(p_c.astype(BF16), vt_ref[0, 0, h].astype(BF16), contract_last,
                               preferred_element_type=F32)
               + jnp.dot(p_n.astype(BF16), vn, preferred_element_type=F32))
        out = num / den
        lse = m + jnp.log(den)
        ls = [lse[i * t_new:(i + 1) * t_new] for i in range(len(DILATIONS))]
        os_ = [out[i * t_new:(i + 1) * t_new] for i in range(len(DILATIONS))]
        mm = jnp.maximum(jnp.maximum(ls[0], ls[1]), ls[2])
        w = [jnp.exp(l - mm) for l in ls]
        o_ref[0, h] = (w[0] * os_[0] + w[1] * os_[1] + w[2] * os_[2]) / (w[0] + w[1] + w[2])
    cw = cw_ref[...]
    u = gc_ref[0] * hc_ref[0]
    st = st_ref[0]
    rows = [st[j:j + 1] for j in range(CONV_K - 1)] + [u[t:t + 1] for t in range(t_new)]
    gb = gb_ref[0]
    conv_ref[0] = jnp.concatenate(
        [gb[t:t + 1] * sum(cw[j:j + 1] * rows[t + j] for j in range(CONV_K)) for t in range(t_new)],
        axis=0)
    st_out_ref[0] = jnp.concatenate(rows[t_new:], axis=0)


SAMPLE_ROWS = 16


def _sample_bias_tables(step_bias, t_new, w_buf):
    cache_rows = []
    for b, d in zip(step_bias, DILATIONS):
        back = b[1:][::-1].T
        if d >= t_new:
            residue = np.eye(d, dtype=bool)[:t_new]
            span = jnp.where(residue[None, :, None, :], back[:, None, :, None], NEG)
            span = span.reshape(N_HEADS, t_new, SEG_KEYS * d)
        else:
            assert d == 1, "dilations between 1 and the number of new tokens are not supported"
            span = jnp.stack([jnp.pad(back[:, :SEG_KEYS - t], ((0, 0), (t, 0)), constant_values=NEG)
                              for t in range(t_new)], axis=1)
        cache_rows.append(jnp.pad(span, ((0, 0), (0, 0), (w_buf - SEG_KEYS * d, 0)),
                                  constant_values=NEG))
    n_rows = len(DILATIONS) * t_new
    bias_c = jnp.pad(jnp.concatenate(cache_rows, axis=1), ((0, 0), (0, SAMPLE_ROWS - n_rows), (0, 0)))

    rows, cols, which, steps = [], [], [], []
    for di, d in enumerate(DILATIONS):
        for t in range(t_new):
            for t2 in range(t + 1):
                if (t - t2) % d == 0:
                    rows.append(di * t_new + t)
                    cols.append(t2)
                    which.append(di)
                    steps.append((t - t2) // d)
    vals = jnp.stack(step_bias)[np.array(which), np.array(steps)]
    bias_n = jnp.full((N_HEADS, n_rows, LANES), NEG, F32).at[:, np.array(rows), np.array(cols)].set(vals.T)
    bias_n = jnp.pad(bias_n, ((0, 0), (0, SAMPLE_ROWS - n_rows), (0, 0)))
    return bias_c, bias_n


def _sample_mix(q, kn, vn, cache_k, cache_v, layer, bias_c, bias_n, gb, gc, hc, state, conv_w):
    db, t_new = q.shape[0], q.shape[1]
    w_buf = cache_k.shape[2]
    kt = cache_k.transpose(0, 1, 3, 4, 2)
    vt = cache_v.transpose(0, 1, 3, 4, 2)
    n_rows = len(DILATIONS) * t_new
    q_rows = jnp.pad(jnp.tile(q.transpose(0, 2, 1, 3), (1, 1, len(DILATIONS), 1)),
                     ((0, 0), (0, 0), (0, SAMPLE_ROWS - n_rows), (0, 0)))
    kn_pad = jnp.pad(kn.transpose(0, 2, 1, 3), ((0, 0), (0, 0), (0, SUBLANES - t_new), (0, 0)))
    vn_pad = jnp.pad(vn.transpose(0, 2, 1, 3), ((0, 0), (0, 0), (0, SUBLANES - t_new), (0, 0)))
    row = pl.BlockSpec((1, t_new, CONV_DIM), lambda b: (b, 0, 0))
    new_spec = pl.BlockSpec((1, N_HEADS, SUBLANES, HEAD_DIM), lambda b: (b, 0, 0, 0))
    win_spec = pl.BlockSpec((1, 1, N_HEADS, HEAD_DIM, w_buf), lambda b: (layer, b, 0, 0, 0))
    attn, conv, st = pl.pallas_call(
        functools.partial(_sample_mix_kernel, t_new=t_new),
        grid=(db,),
        in_specs=[pl.BlockSpec((1, N_HEADS, SAMPLE_ROWS, HEAD_DIM), lambda b: (b, 0, 0, 0)),
                  new_spec, new_spec, win_spec, win_spec,
                  pl.BlockSpec(bias_c.shape, lambda b: (0, 0, 0)),
                  pl.BlockSpec(bias_n.shape, lambda b: (0, 0, 0)),
                  row, row, row,
                  pl.BlockSpec((1, CONV_K - 1, CONV_DIM), lambda b: (b, 0, 0)),
                  pl.BlockSpec((CONV_K, CONV_DIM), lambda b: (0, 0))],
        out_specs=[pl.BlockSpec((1, N_HEADS, t_new, HEAD_DIM), lambda b: (b, 0, 0, 0)), row,
                   pl.BlockSpec((1, CONV_K - 1, CONV_DIM), lambda b: (b, 0, 0))],
        out_shape=[jax.ShapeDtypeStruct((db, N_HEADS, t_new, HEAD_DIM), F32),
                   jax.ShapeDtypeStruct((db, t_new, CONV_DIM), F32),
                   jax.ShapeDtypeStruct((db, CONV_K - 1, CONV_DIM), F32)],
        compiler_params=_params("arbitrary"),
        name="sample_mix",
    )(q_rows, kn_pad, vn_pad, kt, vt, bias_c, bias_n, gb, gc, hc, state, conv_w)
    return attn.transpose(0, 2, 1, 3), conv, st


def _to_slabs(x):
    n = x.shape[0]
    return x.reshape(n, N_SLABS, LANES).transpose(1, 0, 2)


def kernel(x_prompt, x_sample, cache_k, cache_v, state_conv, rel_bias, norm_mix, norm_ffn,
           norm_final, w_in, conv_w, w_out, w_router_group, w_router_expert, w_gate, w_up,
           w_down):
    batch, seq, _ = x_prompt.shape
    db, t_new, _ = x_sample.shape
    depth = w_in.shape[0]
    w_keep = min(MAX_WINDOW, seq)

    hp = x_prompt.reshape(batch * seq, D_MODEL)
    hs = x_sample.reshape(db * t_new, D_MODEL)
    step_bias = _step_bias(rel_bias)
    bias_prompt = _prompt_bias_tables(step_bias)
    bias_c, bias_n = _sample_bias_tables(step_bias, t_new, cache_k.shape[2])
    g_final = norm_final.reshape(1, D_MODEL)

    tm_p = 256
    n_s = db * t_new
    nt_p = batch * seq // tm_p
    pad_per_tile = N_EXPERTS * (RUN_ALIGN - 1)
    max_tiles = -(-(2 * (batch * seq + n_s) + (nt_p + 1) * pad_per_tile) // MOE_TILE) + N_EXPERTS
    buf_rows = max_tiles * MOE_TILE + _local_rows(tm_p)

    pc, sk, sv, sc = [], [], [], []
    windows = None
    for l in range(depth):
        w_in_b = w_in[l].astype(BF16)
        w_out_b = w_out[l].astype(BF16)
        w_router_t = jnp.pad(jnp.concatenate([w_router_expert[l], w_router_group[l]], axis=1).T,
                             ((0, ROUTER_ROWS - N_EXPERTS - N_GROUPS), (0, 0)))
        g_mix = norm_mix[l].reshape(1, D_MODEL)
        g_ffn = norm_ffn[l].reshape(1, D_MODEL)
        last = l == depth - 1

        q, k, v, conv, u_last, *windows = _in_proj(hp, g_mix, w_in_b, conv_w[l], seq, w_keep, l, depth,
                                                   windows)
        attn = _attention_prompt(q, k, v, bias_prompt, batch, seq)
        h1, xn, route, routet, cnt = _out_proj(attn, conv, hp, w_out_b, g_ffn, w_router_t,
                                               tm=4 * tm_p, sub=tm_p)
        pc.append(u_last)

        proj = _in_proj_sample(hs, g_mix, w_in_b)
        qs = proj[:, :ATTN_WIDTH].reshape(db, t_new, N_HEADS, HEAD_DIM)
        ks = proj[:, ATTN_WIDTH:2 * ATTN_WIDTH].reshape(db, t_new, N_HEADS, HEAD_DIM)
        vs = proj[:, 2 * ATTN_WIDTH:3 * ATTN_WIDTH].reshape(db, t_new, N_HEADS, HEAD_DIM)
        c0 = 3 * ATTN_WIDTH
        gb = proj[:, c0:c0 + CONV_DIM].reshape(db, t_new, CONV_DIM)
        gc = proj[:, c0 + CONV_DIM:c0 + 2 * CONV_DIM].reshape(db, t_new, CONV_DIM)
        hc = proj[:, c0 + 2 * CONV_DIM:].reshape(db, t_new, CONV_DIM)
        attn_s, conv_s, state_s = _sample_mix(qs, ks, vs, cache_k, cache_v, l, bias_c, bias_n,
                                              gb, gc, hc, state_conv[l], conv_w[l])
        h1s, xns, route_s, routet_s, cnt_s = _out_proj(
            _to_slabs(attn_s.reshape(n_s, ATTN_WIDTH)), conv_s.reshape(n_s, CONV_DIM), hs,
            w_out_b, g_ffn, w_router_t, tm=n_s, sub=n_s)

        table_out, table_in, first_tile, tiles_e, rows_e, total = _moe_plan(
            jnp.concatenate([cnt[:, :, 0], cnt_s[:, :, 0]], axis=0), max_tiles)
        xs = _dispatch(table_out[:nt_p], xn, route, tm_p, buf_rows)
        xs = _dispatch(table_out[nt_p:], xns, route_s, n_s, buf_rows, xs_prev=xs)
        eo = _experts(first_tile, tiles_e, rows_e, total, xs, w_gate, w_up, w_down, l)
        hp = _combine(table_in[:nt_p], h1, routet, g_final, eo, tm_p, last)
        hs = _combine(table_in[nt_p:], h1s, routet_s, g_final, eo, n_s, last)
        sk.append(ks)
        sv.append(vs)
        sc.append(state_s)

    k_win, v_win = (w.transpose(0, 1, 4, 2, 3) for w in windows)
    return (hp.reshape(batch, seq, D_MODEL), hs.reshape(db, t_new, D_MODEL),
            k_win, v_win, jnp.stack(pc),
            jnp.stack(sk), jnp.stack(sv), jnp.stack(sc))
```

```python
import functools
import math

import jax
import jax.numpy as jnp
import numpy as np
from jax import lax
from jax.experimental import pallas as pl
from jax.experimental.pallas import tpu as pltpu

F32 = jnp.float32
BF16 = jnp.bfloat16

D_MODEL = 1024
N_HEADS = 8
HEAD_DIM = 64
ATTN_WIDTH = N_HEADS * HEAD_DIM
CONV_DIM = D_MODEL - ATTN_WIDTH
CONV_K = 3
PROJ_WIDTH = 3 * ATTN_WIDTH + 3 * CONV_DIM
DILATIONS = (1, 4, 16)
SEG_KEYS = 128
Q_BLOCK = 128
MAX_WINDOW = 2048
N_BUCKETS = 32
MAX_DISTANCE = 2048
N_GROUPS = 4
EXPERTS_PER_GROUP = 8
N_EXPERTS = N_GROUPS * EXPERTS_PER_GROUP
D_EXPERT = D_MODEL // 4
EPS = 1e-6
NEG = -1e30
SCALE = 1.0 / math.sqrt(HEAD_DIM)

LANES = 128
SUBLANES = 8
N_SLABS = ATTN_WIDTH // LANES
VMEM_LIMIT = 48 * 1024 * 1024
REGROUP = 4


def _params(*sem):
    return pltpu.CompilerParams(dimension_semantics=sem, vmem_limit_bytes=VMEM_LIMIT)


def _rms(x, g):
    return x * lax.rsqrt(jnp.mean(x * x, axis=-1, keepdims=True) + EPS) * g


def _rel_bucket(dist):
    max_exact = N_BUCKETS // 2
    d_f = jnp.maximum(dist, 1).astype(F32)
    large = max_exact + (jnp.log(d_f / max_exact) / math.log(MAX_DISTANCE / max_exact)
                         * (N_BUCKETS - max_exact)).astype(jnp.int32)
    large = jnp.minimum(large, N_BUCKETS - 1)
    return jnp.where(dist < max_exact, dist, large)


def _in_proj_kernel(x_ref, g_ref, w_ref, cw_ref, *rest, tm, tiles_per_seq, win_tiles, chained):
    (q_ref, k_ref, v_ref, conv_ref, ulast_ref, kwin_ref, vwin_ref, ext_ref) = rest[2:] if chained else rest
    i = pl.program_id(0)
    xn = _rms(x_ref[...], g_ref[...])
    proj = jnp.dot(xn.astype(BF16), w_ref[...], preferred_element_type=F32)
    for s in range(N_SLABS):
        q_ref[s] = proj[:, s * LANES:(s + 1) * LANES]
        k_ref[s] = proj[:, ATTN_WIDTH + s * LANES:ATTN_WIDTH + (s + 1) * LANES]
        v_ref[s] = proj[:, 2 * ATTN_WIDTH + s * LANES:2 * ATTN_WIDTH + (s + 1) * LANES]

    @pl.when(i % tiles_per_seq >= tiles_per_seq - win_tiles)
    def _():
        heads_per_slab = LANES // HEAD_DIM
        for s in range(N_SLABS):
            for src, dst in ((k_ref, kwin_ref), (v_ref, vwin_ref)):
                dst[0, 0, heads_per_slab * s:heads_per_slab * (s + 1)] = (
                    src[s].T.reshape(heads_per_slab, HEAD_DIM, tm))
    c0 = 3 * ATTN_WIDTH
    gb = proj[:, c0:c0 + CONV_DIM]
    u = proj[:, c0 + CONV_DIM:c0 + 2 * CONV_DIM] * proj[:, c0 + 2 * CONV_DIM:c0 + 3 * CONV_DIM]

    @pl.when(i % tiles_per_seq == 0)
    def _():
        ext_ref[0:SUBLANES, :] = jnp.zeros((SUBLANES, CONV_DIM), F32)

    ext_ref[SUBLANES:SUBLANES + tm, :] = u
    u1 = ext_ref[SUBLANES - 1:SUBLANES - 1 + tm, :]
    u2 = ext_ref[SUBLANES - 2:SUBLANES - 2 + tm, :]
    cw = cw_ref[...]
    conv_ref[...] = gb * (cw[0:1] * u2 + cw[1:2] * u1 + cw[2:3] * u)
    ulast_ref[0] = ext_ref[tm + SUBLANES - 2:tm + SUBLANES, :]
    ext_ref[0:SUBLANES, :] = ext_ref[tm:tm + SUBLANES, :]


def _in_proj(x, g, w_bf16, conv_w, seq_len, w_keep, layer, depth, windows=None, tm=1024):
    n = x.shape[0]
    n_seq = n // seq_len
    slab = jax.ShapeDtypeStruct((N_SLABS, n, LANES), F32)
    slab_spec = pl.BlockSpec((N_SLABS, tm, LANES), lambda i: (0, i, 0))
    tiles_per_seq = seq_len // tm
    win_tiles = w_keep // tm
    win = jax.ShapeDtypeStruct((depth, n_seq, N_HEADS, HEAD_DIM, w_keep), F32)
    win_spec = pl.BlockSpec(
        (1, 1, N_HEADS, HEAD_DIM, tm),
        lambda i: (layer, i // tiles_per_seq, 0, 0,
                   jnp.maximum(i % tiles_per_seq - (tiles_per_seq - win_tiles), 0)))
    chained = windows is not None
    in_specs = [pl.BlockSpec((tm, D_MODEL), lambda i: (i, 0)),
                pl.BlockSpec((1, D_MODEL), lambda i: (0, 0)),
                pl.BlockSpec((D_MODEL, PROJ_WIDTH), lambda i: (0, 0)),
                pl.BlockSpec((CONV_K, CONV_DIM), lambda i: (0, 0))]
    args = [x, g, w_bf16, conv_w]
    if chained:
        in_specs += [pl.BlockSpec(memory_space=pl.ANY)] * 2
        args += list(windows)
    return pl.pallas_call(
        functools.partial(_in_proj_kernel, tm=tm, tiles_per_seq=tiles_per_seq, win_tiles=win_tiles,
                          chained=chained),
        grid=(n // tm,),
        in_specs=in_specs,
        out_specs=[slab_spec, slab_spec, slab_spec,
                   pl.BlockSpec((tm, CONV_DIM), lambda i: (i, 0)),
                   pl.BlockSpec((1, CONV_K - 1, CONV_DIM), lambda i: (i // tiles_per_seq, 0, 0)),
                   win_spec, win_spec],
        out_shape=[slab, slab, slab,
                   jax.ShapeDtypeStruct((n, CONV_DIM), F32),
                   jax.ShapeDtypeStruct((n_seq, CONV_K - 1, CONV_DIM), F32), win, win],
        scratch_shapes=[pltpu.VMEM((tm + SUBLANES, CONV_DIM), F32)],
        input_output_aliases={4: 5, 5: 6} if chained else {},
        compiler_params=_params("arbitrary"),
        name="in_proj_prompt",
    )(*args)


def _attn_kernel(q_ref, k_ref, v_ref, bias_ref, o_ref, q4, k4, v4, ktail, vtail, oacc, lacc, *, sb):
    j = pl.program_id(2)
    slot = j % 2
    prev = 1 - slot
    seg = sb // REGROUP

    @pl.when(j == 0)
    def _():
        k4[1] = jnp.zeros((sb, LANES), F32)
        v4[1] = jnp.zeros((sb, LANES), F32)
        ktail[...] = jnp.zeros((Q_BLOCK, LANES), F32)
        vtail[...] = jnp.zeros((Q_BLOCK, LANES), F32)

    for r in range(REGROUP):
        q4[r * seg:(r + 1) * seg, :] = q_ref[pl.ds(r, seg, stride=REGROUP), :] * SCALE
        k4[slot, r * seg:(r + 1) * seg, :] = k_ref[pl.ds(r, seg, stride=REGROUP), :]
        v4[slot, r * seg:(r + 1) * seg, :] = v_ref[pl.ds(r, seg, stride=REGROUP), :]

    lane = lax.broadcasted_iota(jnp.int32, (Q_BLOCK, LANES), 1)
    low = lane < HEAD_DIM
    ones = jnp.ones((2 * Q_BLOCK, LANES), BF16)

    def block(start):
        return pl.ds(start, Q_BLOCK)

    for di, d in enumerate(DILATIONS):
        nq = sb // (Q_BLOCK * d)

        def unit(u, di=di, d=d, nq=nq):
            r = u // nq
            n = u % nq
            qstart = r + d * Q_BLOCK * n
            if d == 1:
                here = block(n * Q_BLOCK)
                q = q_ref[here, :] * SCALE
                k_new, v_new = k_ref[here, :], v_ref[here, :]
                if n == 0:
                    k_old, v_old = ktail[...], vtail[...]
                else:
                    before = block((n - 1) * Q_BLOCK)
                    k_old, v_old = k_ref[before, :], v_ref[before, :]
            elif d == REGROUP:
                here = block(r * seg + n * Q_BLOCK)
                q = q4[here, :]
                k_new, v_new = k4[slot, here, :], v4[slot, here, :]
                if n == 0:
                    last = block(r * seg + seg - Q_BLOCK)
                    k_old, v_old = k4[prev, last, :], v4[prev, last, :]
                else:
                    before = block(r * seg + (n - 1) * Q_BLOCK)
                    k_old, v_old = k4[slot, before, :], v4[slot, before, :]
            else:
                rows = pl.ds((u % REGROUP) * seg + u // REGROUP, Q_BLOCK, stride=REGROUP)
                q = q4[rows, :]
                k_new, v_new = k4[slot, rows, :], v4[slot, rows, :]
                k_old, v_old = k4[prev, rows, :], v4[prev, rows, :]
            kk = jnp.concatenate([k_old, k_new], axis=0).astype(BF16)
            vv = jnp.concatenate([v_old, v_new], axis=0).astype(BF16)
            qm = jnp.concatenate([jnp.where(low, q, 0.0), jnp.where(low, 0.0, q)],
                                 axis=0).astype(BF16)
            s = lax.dot_general(qm, kk, (((1,), (1,)), ((), ())),
                                preferred_element_type=F32)
            first = (j == 0).astype(jnp.int32) if n == 0 else 0
            s = s + bias_ref[di, first]
            m = jnp.max(s, axis=-1, keepdims=True)
            p = jnp.exp(s - m).astype(BF16)
            pv = jnp.dot(p, jnp.concatenate([vv, ones], axis=1),
                         preferred_element_type=F32)
            o_sel = jnp.where(low, pv[0:Q_BLOCK, 0:LANES], pv[Q_BLOCK:, 0:LANES])
            l_sel = jnp.where(low, pv[0:Q_BLOCK, LANES:], pv[Q_BLOCK:, LANES:])
            m_sel = jnp.where(low, jnp.broadcast_to(m[0:Q_BLOCK], (Q_BLOCK, LANES)),
                              jnp.broadcast_to(m[Q_BLOCK:], (Q_BLOCK, LANES)))
            oacc[di, pl.ds(qstart, Q_BLOCK, stride=d), :] = o_sel / l_sel
            lacc[di, pl.ds(qstart, Q_BLOCK, stride=d), :] = m_sel + jnp.log(l_sel)

        for u in range(sb // Q_BLOCK):
            unit(u)

    for c in range(sb // 256):
        rows = slice(c * 256, (c + 1) * 256)
        l0, l1, l2 = lacc[0, rows, :], lacc[1, rows, :], lacc[2, rows, :]
        mm = jnp.maximum(jnp.maximum(l0, l1), l2)
        w0, w1, w2 = jnp.exp(l0 - mm), jnp.exp(l1 - mm), jnp.exp(l2 - mm)
        num = w0 * oacc[0, rows, :] + w1 * oacc[1, rows, :] + w2 * oacc[2, rows, :]
        o_ref[rows, :] = num / (w0 + w1 + w2)
    ktail[...] = k_ref[sb - Q_BLOCK:sb, :]
    vtail[...] = v_ref[sb - Q_BLOCK:sb, :]


def _step_bias(rel_bias):
    steps = jnp.arange(SEG_KEYS + 1)
    return [rel_bias[_rel_bucket(steps * d)].astype(F32) for d in DILATIONS]


def _prompt_bias_tables(step_bias):
    width = 3 * Q_BLOCK
    k_loc = jnp.arange(2 * Q_BLOCK)
    tables = []
    for b in step_bias:
        g = jnp.concatenate([b[::-1].T, jnp.full((N_HEADS, width - SEG_KEYS), NEG, F32)], axis=1)
        flat = jnp.broadcast_to(g[:, None, :], (N_HEADS, Q_BLOCK, width + 1)).reshape(N_HEADS, -1)
        x = flat[:, :Q_BLOCK * width].reshape(N_HEADS, Q_BLOCK, width)
        t = x[:, :, :2 * Q_BLOCK]
        t_first = jnp.where(k_loc[None, None, :] >= Q_BLOCK, t, NEG)
        tables.append(jnp.stack([t.reshape(N_SLABS, 2 * Q_BLOCK, 2 * Q_BLOCK),
                                 t_first.reshape(N_SLABS, 2 * Q_BLOCK, 2 * Q_BLOCK)]))
    return jnp.stack(tables)


def _attention_prompt(q, k, v, bias_tab, n_seq, seq_len, sb=2048):
    assert DILATIONS == (1, REGROUP, REGROUP * REGROUP)
    n = q.shape[1]
    nsb = seq_len // sb
    blk = pl.BlockSpec((None, sb, LANES), lambda b, s, j: (s, b * nsb + j, 0))
    return pl.pallas_call(
        functools.partial(_attn_kernel, sb=sb),
        grid=(n_seq, N_SLABS, nsb),
        in_specs=[blk, blk, blk,
                  pl.BlockSpec((len(DILATIONS), 2, None, 2 * Q_BLOCK, 2 * Q_BLOCK),
                               lambda b, s, j: (0, 0, s, 0, 0))],
        out_specs=blk,
        out_shape=jax.ShapeDtypeStruct((N_SLABS, n, LANES), F32),
        scratch_shapes=[pltpu.VMEM((sb, LANES), F32),
                        pltpu.VMEM((2, sb, LANES), F32), pltpu.VMEM((2, sb, LANES), F32),
                        pltpu.VMEM((Q_BLOCK, LANES), F32), pltpu.VMEM((Q_BLOCK, LANES), F32),
                        pltpu.VMEM((len(DILATIONS), sb, LANES), F32),
                        pltpu.VMEM((len(DILATIONS), sb, LANES), F32)],
        compiler_params=_params("arbitrary", "arbitrary", "arbitrary"),
        name="attention_prompt",
    )(q, k, v, bias_tab)


ROUTER_ROWS = 48
MOE_TILE = 256
RUN_ALIGN = 2 * SUBLANES
TABLE_LANES = 128
EXPERT_BUFFERS = 6


def _local_rows(tm):
    return -(-(2 * tm + N_EXPERTS * (RUN_ALIGN - 1)) // LANES) * LANES


def _route(lg):
    tm = lg.shape[1]
    gl = lg[N_EXPERTS:N_EXPERTS + N_GROUPS]
    grow = lax.broadcasted_iota(jnp.int32, (N_GROUPS, tm), 0)
    gmax = jnp.max(gl, axis=0, keepdims=True)
    gsum = jnp.sum(jnp.exp(gl - gmax), axis=0, keepdims=True)
    gidx = jnp.min(jnp.where(gl == gmax, grow, N_GROUPS), axis=0, keepdims=True)
    p_grp = 1.0 / gsum
    el = lg[0:N_EXPERTS]
    e = lax.broadcasted_iota(jnp.int32, (N_EXPERTS, tm), 0)
    in_grp = (e // EXPERTS_PER_GROUP) == gidx
    elog = jnp.where(in_grp, el, -jnp.inf)
    emax = jnp.max(elog, axis=0, keepdims=True)
    eexp = jnp.exp(elog - emax)
    eprob = jnp.where(in_grp, eexp / jnp.sum(eexp, axis=0, keepdims=True), -1.0)
    p1 = jnp.max(eprob, axis=0, keepdims=True)
    i1 = jnp.min(jnp.where(eprob == p1, e, N_EXPERTS), axis=0, keepdims=True)
    rest = jnp.where(e == i1, -1.0, eprob)
    p2 = jnp.max(rest, axis=0, keepdims=True)
    i2 = jnp.min(jnp.where(rest == p2, e, N_EXPERTS), axis=0, keepdims=True)
    tot = p1 + p2
    return e == i1, e == i2, p_grp * (p1 / tot), p_grp * (p2 / tot)


def _split_bf16(x):
    hi = x.astype(BF16)
    return hi, (x - hi.astype(F32)).astype(BF16)


def _out_proj_kernel(attn_ref, conv_ref, h_ref, wo_ref, g_ref, wr_ref, upper_ref, lower_ref,
                     h1_ref, xn_ref, route_ref, routet_ref, cnt_ref, *, tm, sub):
    contract_last = (((1,), (1,)), ((), ()))
    mix = jnp.concatenate([attn_ref[s] for s in range(N_SLABS)] + [conv_ref[...]], axis=1)
    h1_ref[...] = h_ref[...] + jnp.dot(mix.astype(BF16), wo_ref[...], preferred_element_type=F32)
    for t in range(tm // sub):
        rs = slice(t * sub, (t + 1) * sub)
        xn = _rms(h1_ref[rs, :], g_ref[...])
        xh, xl = _split_bf16(xn)
        xn_ref[rs, :] = xh
        both = lax.dot_general(wr_ref[...], xh, contract_last, preferred_element_type=F32)
        lg = (both[:ROUTER_ROWS] + both[ROUTER_ROWS:]
              + lax.dot_general(wr_ref[:ROUTER_ROWS, :], xl, contract_last, preferred_element_type=F32))
        sel1, sel2, g1, g2 = _route(lg)
        chosen = jnp.where(sel1 | sel2, 1.0, 0.0)
        rank = jnp.dot(chosen.astype(BF16), upper_ref[...], preferred_element_type=F32)
        cnt = jnp.sum(chosen, axis=1, keepdims=True)
        cnt_pad = jnp.floor((cnt + (RUN_ALIGN - 1)) * (1.0 / RUN_ALIGN)) * RUN_ALIGN
        cnt_b = jnp.broadcast_to(cnt_pad, (N_EXPERTS, LANES))
        start = jnp.dot(lower_ref[...], cnt_b.astype(BF16), preferred_element_type=F32)
        pos = start[:, 0:1] + rank
        pos1 = jnp.sum(jnp.where(sel1, pos, 0.0), axis=0, keepdims=True)
        pos2 = jnp.sum(jnp.where(sel2, pos, 0.0), axis=0, keepdims=True)
        rows = jnp.concatenate([pos1, pos2, g1, g2, jnp.zeros((SUBLANES - 4, sub), F32)], axis=0)
        route_ref[t] = rows
        routet_ref[rs, :] = jnp.concatenate([rows, jnp.zeros((LANES - SUBLANES, sub), F32)], axis=0).T
        cnt_ref[t] = cnt_b


def _out_proj(attn, conv, h, wo_bf16, g, w_router_t, tm, sub):
    n = h.shape[0]
    w_router = jnp.concatenate(_split_bf16(w_router_t), axis=0)
    idx = jnp.arange(sub)
    upper = (idx[:, None] < idx[None, :]).astype(BF16)
    eidx = jnp.arange(N_EXPERTS)
    lower = (eidx[None, :] < eidx[:, None]).astype(BF16)
    const = lambda shape: pl.BlockSpec(shape, lambda i: (0,) * len(shape))
    per_step = tm // sub
    return pl.pallas_call(
        functools.partial(_out_proj_kernel, tm=tm, sub=sub),
        grid=(n // tm,),
        in_specs=[pl.BlockSpec((N_SLABS, tm, LANES), lambda i: (0, i, 0)),
                  pl.BlockSpec((tm, CONV_DIM), lambda i: (i, 0)),
                  pl.BlockSpec((tm, D_MODEL), lambda i: (i, 0)),
                  const((D_MODEL, D_MODEL)), const((1, D_MODEL)),
                  const((2 * ROUTER_ROWS, D_MODEL)),
                  const((sub, sub)), const((N_EXPERTS, N_EXPERTS))],
        out_specs=[pl.BlockSpec((tm, D_MODEL), lambda i: (i, 0)),
                   pl.BlockSpec((tm, D_MODEL), lambda i: (i, 0)),
                   pl.BlockSpec((per_step, SUBLANES, sub), lambda i: (i, 0, 0)),
                   pl.BlockSpec((tm, LANES), lambda i: (i, 0)),
                   pl.BlockSpec((per_step, N_EXPERTS, LANES), lambda i: (i, 0, 0))],
        out_shape=[jax.ShapeDtypeStruct((n, D_MODEL), F32),
                   jax.ShapeDtypeStruct((n, D_MODEL), BF16),
                   jax.ShapeDtypeStruct((n // sub, SUBLANES, sub), F32),
                   jax.ShapeDtypeStruct((n, LANES), F32),
                   jax.ShapeDtypeStruct((n // sub, N_EXPERTS, LANES), F32)],
        compiler_params=_params("arbitrary"),
        name="out_proj",
    )(attn, conv, h, wo_bf16, g, w_router, upper, lower)


def _moe_plan(cnt_tiles, max_tiles):
    c = cnt_tiles.astype(jnp.int32)
    rows_e = jnp.sum(c, axis=0)
    tiles_e = (rows_e + MOE_TILE - 1) // MOE_TILE
    cum = jnp.cumsum(tiles_e)
    total = cum[-1]
    first_tile = cum - tiles_e

    run_start = jnp.cumsum(c, axis=0) - c
    local_start = jnp.cumsum(c, axis=1) - c
    local_end = local_start + c
    row = jnp.arange(TABLE_LANES, dtype=jnp.int32) * RUN_ALIGN
    inside = (local_start[:, None, :] <= row[None, :, None]) & (row[None, :, None] < local_end[:, None, :])
    base = first_tile[None] * MOE_TILE + run_start - local_start
    n_chunks = jnp.sum(c, axis=1) // RUN_ALIGN
    table = jnp.sum(jnp.where(inside, base[:, None, :] + row[None, :, None], 0), axis=-1)
    live = jnp.arange(TABLE_LANES)[None] < n_chunks[:, None]
    table_out = jnp.where(live, table, max_tiles * MOE_TILE + row[None])
    table_in = jnp.where(live, table, row[None])
    i32 = lambda a: a.astype(jnp.int32)
    return (i32(table_out), i32(table_in), i32(first_tile), i32(tiles_e), i32(rows_e),
            i32(total).reshape(1))


def _dispatch_kernel(tbl_ref, xn_ref, route_ref, *rest, tm, rl, chained):
    xs_hbm, lsort, sem = rest[1:] if chained else rest
    i = pl.program_id(0)
    slot = i % 2
    rows = route_ref[0]
    pos1 = rows[0:1].astype(jnp.int32)
    pos2 = rows[1:2].astype(jnp.int32)
    p = lax.broadcasted_iota(jnp.int32, (rl, tm), 0)
    one_hot = jnp.where((p == pos1) | (p == pos2), 1.0, 0.0).astype(BF16)
    lsort[slot] = jnp.dot(one_hot, xn_ref[...], preferred_element_type=F32).astype(BF16)

    def tile_copies_done(s):
        pltpu.make_async_copy(lsort.at[s], xs_hbm.at[pl.ds(0, rl), :], sem.at[s]).wait()

    @pl.when(i > 0)
    def _():
        tile_copies_done(1 - slot)

    for q in range(rl // RUN_ALIGN):
        row = pl.multiple_of(tbl_ref[i * TABLE_LANES + q], RUN_ALIGN)
        pltpu.make_async_copy(lsort.at[slot, pl.ds(q * RUN_ALIGN, RUN_ALIGN), :],
                              xs_hbm.at[pl.ds(row, RUN_ALIGN), :], sem.at[slot]).start()

    @pl.when(i == pl.num_programs(0) - 1)
    def _():
        tile_copies_done(slot)


def _dispatch(table, xn, route, tm, total_rows, xs_prev=None):
    nt = xn.shape[0] // tm
    rl = _local_rows(tm)
    chained = xs_prev is not None
    in_specs = [pl.BlockSpec((tm, D_MODEL), lambda i, t: (i, 0)),
                pl.BlockSpec((1, SUBLANES, tm), lambda i, t: (i, 0, 0))]
    args = [table.reshape(-1), xn, route]
    if chained:
        in_specs.append(pl.BlockSpec(memory_space=pl.ANY))
        args.append(xs_prev)
    return pl.pallas_call(
        functools.partial(_dispatch_kernel, tm=tm, rl=rl, chained=chained),
        grid_spec=pltpu.PrefetchScalarGridSpec(
            num_scalar_prefetch=1, grid=(nt,), in_specs=in_specs,
            out_specs=pl.BlockSpec(memory_space=pl.ANY),
            scratch_shapes=[pltpu.VMEM((2, rl, D_MODEL), BF16), pltpu.SemaphoreType.DMA((2,))]),
        out_shape=jax.ShapeDtypeStruct((total_rows, D_MODEL), BF16),
        input_output_aliases={3: 0} if chained else {},
        compiler_params=_params("arbitrary"),
        name="moe_dispatch",
    )(*args)


def _expert_kernel(first_ref, tiles_ref, rows_ref, tot_ref, xs_hbm, wg_ref, wu_ref, wd_ref, eo_hbm,
                   xbuf, obuf, sem_in, sem_out):
    e = pl.program_id(0)
    first_tile = first_ref[e]
    n_tiles = tiles_ref[e]
    total = tot_ref[0]

    def in_copy(g):
        return pltpu.make_async_copy(
            xs_hbm.at[pl.ds(pl.multiple_of(g * MOE_TILE, MOE_TILE), MOE_TILE), :],
            xbuf.at[g % EXPERT_BUFFERS], sem_in.at[g % EXPERT_BUFFERS])

    def out_copy(g):
        return pltpu.make_async_copy(
            obuf.at[g % EXPERT_BUFFERS],
            eo_hbm.at[pl.ds(pl.multiple_of(g * MOE_TILE, MOE_TILE), MOE_TILE), :],
            sem_out.at[g % EXPERT_BUFFERS])

    lookahead = EXPERT_BUFFERS - 2

    @pl.when(e == 0)
    def _():
        for g0 in range(lookahead):
            @pl.when(g0 < total)
            def _():
                in_copy(g0).start()

    def acquire(g, j):
        @pl.when(g + lookahead < total)
        def _():
            in_copy(g + lookahead).start()

        in_copy(g).wait()

        @pl.when(g >= EXPERT_BUFFERS)
        def _():
            out_copy(g - EXPERT_BUFFERS).wait()

        row = lax.broadcasted_iota(jnp.int32, (MOE_TILE, 1), 0)
        x = xbuf[g % EXPERT_BUFFERS]
        return jnp.where(row < rows_ref[e] - j * MOE_TILE, x, jnp.zeros_like(x))

    def mlp(x):
        gate = jnp.dot(x, wg_ref[0, 0].astype(BF16), preferred_element_type=F32)
        up = jnp.dot(x, wu_ref[0, 0].astype(BF16), preferred_element_type=F32)
        hdn = gate * jax.nn.sigmoid(gate) * up
        return jnp.dot(hdn.astype(BF16), wd_ref[0, 0].astype(BF16),
                       preferred_element_type=F32).astype(BF16)

    def pair(jp, carry):
        j = 2 * jp
        g = first_tile + j
        x0 = acquire(g, j)
        x1 = acquire(g + 1, j + 1)
        y = mlp(jnp.concatenate([x0, x1], axis=0))
        obuf[g % EXPERT_BUFFERS] = y[:MOE_TILE]
        obuf[(g + 1) % EXPERT_BUFFERS] = y[MOE_TILE:]
        out_copy(g).start()
        out_copy(g + 1).start()
        return carry

    lax.fori_loop(0, n_tiles // 2, pair, 0)

    @pl.when(n_tiles % 2 == 1)
    def _():
        j = n_tiles - 1
        g = first_tile + j
        obuf[g % EXPERT_BUFFERS] = mlp(acquire(g, j))
        out_copy(g).start()

    @pl.when(e == pl.num_programs(0) - 1)
    def _():
        for back in range(EXPERT_BUFFERS, 0, -1):
            @pl.when(total >= back)
            def _():
                out_copy(total - back).wait()


def _experts(first_tile, tiles_e, rows_e, total, xs, w_gate, w_up, w_down, layer):
    weight = lambda e, f, t, r, n: (layer, e, 0, 0)
    return pl.pallas_call(
        _expert_kernel,
        grid_spec=pltpu.PrefetchScalarGridSpec(
            num_scalar_prefetch=4, grid=(N_EXPERTS,),
            in_specs=[pl.BlockSpec(memory_space=pl.ANY),
                      pl.BlockSpec((1, 1, D_MODEL, D_EXPERT), weight),
                      pl.BlockSpec((1, 1, D_MODEL, D_EXPERT), weight),
                      pl.BlockSpec((1, 1, D_EXPERT, D_MODEL), weight)],
            out_specs=pl.BlockSpec(memory_space=pl.ANY),
            scratch_shapes=[pltpu.VMEM((EXPERT_BUFFERS, MOE_TILE, D_MODEL), BF16),
                            pltpu.VMEM((EXPERT_BUFFERS, MOE_TILE, D_MODEL), BF16),
                            pltpu.SemaphoreType.DMA((EXPERT_BUFFERS,)),
                            pltpu.SemaphoreType.DMA((EXPERT_BUFFERS,))]),
        out_shape=jax.ShapeDtypeStruct(xs.shape, BF16),
        compiler_params=_params("arbitrary"),
        name="moe_experts",
    )(first_tile, tiles_e, rows_e, total, xs, w_gate, w_up, w_down)


def _combine_kernel(tbl_ref, h1_ref, routet_ref, gf_ref, eo_hbm, o_ref, leo, sem,
                    *, tm, rl, final_norm):
    i = pl.program_id(0)
    n = pl.num_programs(0)
    slot = i % 2

    def gather(tile, s):
        for q in range(rl // RUN_ALIGN):
            row = pl.multiple_of(tbl_ref[tile * TABLE_LANES + q], RUN_ALIGN)
            pltpu.make_async_copy(eo_hbm.at[pl.ds(row, RUN_ALIGN), :],
                                  leo.at[s, pl.ds(q * RUN_ALIGN, RUN_ALIGN), :], sem.at[s]).start()

    @pl.when(i == 0)
    def _():
        gather(0, 0)

    @pl.when(i + 1 < n)
    def _():
        gather(i + 1, 1 - slot)

    pltpu.make_async_copy(eo_hbm.at[pl.ds(0, rl), :], leo.at[slot], sem.at[slot]).wait()

    rt = routet_ref[...]
    pos1 = rt[:, 0:1].astype(jnp.int32)
    pos2 = rt[:, 1:2].astype(jnp.int32)
    lane = lax.broadcasted_iota(jnp.int32, (tm, rl), 1)
    weights = jnp.where(lane == pos1, rt[:, 2:3], 0.0) + jnp.where(lane == pos2, rt[:, 3:4], 0.0)
    y = jnp.dot(weights.astype(BF16), leo[slot], preferred_element_type=F32)
    out = h1_ref[...] + y
    if final_norm:
        out = _rms(out, gf_ref[...])
    o_ref[...] = out


def _combine(table, h1, routet, g_final, eo, tm, final_norm):
    n = h1.shape[0]
    rl = _local_rows(tm)
    return pl.pallas_call(
        functools.partial(_combine_kernel, tm=tm, rl=rl, final_norm=final_norm),
        grid_spec=pltpu.PrefetchScalarGridSpec(
            num_scalar_prefetch=1, grid=(n // tm,),
            in_specs=[pl.BlockSpec((tm, D_MODEL), lambda i, t: (i, 0)),
                      pl.BlockSpec((tm, LANES), lambda i, t: (i, 0)),
                      pl.BlockSpec((1, D_MODEL), lambda i, t: (0, 0)),
                      pl.BlockSpec(memory_space=pl.ANY)],
            out_specs=pl.BlockSpec((tm, D_MODEL), lambda i, t: (i, 0)),
            scratch_shapes=[pltpu.VMEM((2, rl, D_MODEL), BF16), pltpu.SemaphoreType.DMA((2,))]),
        out_shape=jax.ShapeDtypeStruct((n, D_MODEL), F32),
        compiler_params=_params("arbitrary"),
        name="moe_combine",
    )(table.reshape(-1), h1, routet, g_final, eo)


def _in_proj_sample_kernel(x_ref, g_ref, w_ref, o_ref):
    xn = _rms(x_ref[...], g_ref[...])
    o_ref[...] = jnp.dot(xn.astype(BF16), w_ref[...], preferred_element_type=F32)


def _in_proj_sample(x, g, w_bf16):
    n = x.shape[0]
    return pl.pallas_call(
        _in_proj_sample_kernel,
        out_shape=jax.ShapeDtypeStruct((n, PROJ_WIDTH), F32),
        compiler_params=pltpu.CompilerParams(vmem_limit_bytes=VMEM_LIMIT),
        name="in_proj_sample",
    )(x, g, w_bf16)


def _sample_mix_kernel(q_ref, kn_ref, vn_ref, kt_ref, vt_ref,
                       bc_ref, bn_ref, gb_ref, gc_ref, hc_ref, st_ref, cw_ref,
                       o_ref, conv_ref, st_out_ref, *, t_new):
    contract_last = (((1,), (1,)), ((), ()))
    no_rows = jnp.zeros((LANES - SUBLANES, HEAD_DIM), F32)
    for h in range(N_HEADS):
        q = (q_ref[0, h] * SCALE).astype(BF16)
        s_c = jnp.dot(q, kt_ref[0, 0, h].astype(BF16), preferred_element_type=F32) + bc_ref[h]
        kn = jnp.concatenate([kn_ref[0, h], no_rows], axis=0).astype(BF16)
        vn = jnp.concatenate([vn_ref[0, h], no_rows], axis=0).astype(BF16)
        s_n = lax.dot_general(q, kn, contract_last, preferred_element_type=F32) + bn_ref[h]
        m = jnp.maximum(jnp.max(s_c, axis=-1, keepdims=True), jnp.max(s_n, axis=-1, keepdims=True))
        p_c = jnp.exp(s_c - m)
        p_n = jnp.exp(s_n - m)
        den = jnp.sum(p_c, axis=-1, keepdims=True) + jnp.sum(p_n, axis=-1, keepdims=True)
        num = (lax.dot_general(p_c.astype(BF16), vt_ref[0, 0, h].astype(BF16), contract_last,
                               preferred_element_type=F32)
               + jnp.dot(p_n.astype(BF16), vn, preferred_element_type=F32))
        out = num / den
        lse = m + jnp.log(den)
        ls = [lse[i * t_new:(i + 1) * t_new] for i in range(len(DILATIONS))]
        os_ = [out[i * t_new:(i + 1) * t_new] for i in range(len(DILATIONS))]
        mm = jnp.maximum(jnp.maximum(ls[0], ls[1]), ls[2])
        w = [jnp.exp(l - mm) for l in ls]
        o_ref[0, h] = (w[0] * os_[0] + w[1] * os_[1] + w[2] * os_[2]) / (w[0] + w[1] + w[2])
    cw = cw_ref[...]
    u = gc_ref[0] * hc_ref[0]
    st = st_ref[0]
    rows = [st[j:j + 1] for j in range(CONV_K - 1)] + [u[t:t + 1] for t in range(t_new)]
    gb = gb_ref[0]
    conv_ref[0] = jnp.concatenate(
        [gb[t:t + 1] * sum(cw[j:j + 1] * rows[t + j] for j in range(CONV_K)) for t in range(t_new)],
        axis=0)
    st_out_ref[0] = jnp.concatenate(rows[t_new:], axis=0)


SAMPLE_ROWS = 16


def _sample_bias_tables(step_bias, t_new, w_buf):
    cache_rows = []
    for b, d in zip(step_bias, DILATIONS):
        back = b[1:][::-1].T
        if d >= t_new:
            residue = np.eye(d, dtype=bool)[:t_new]
            span = jnp.where(residue[None, :, None, :], back[:, None, :, None], NEG)
            span = span.reshape(N_HEADS, t_new, SEG_KEYS * d)
        else:
            assert d == 1, "dilations between 1 and the number of new tokens are not supported"
            span = jnp.stack([jnp.pad(back[:, :SEG_KEYS - t], ((0, 0), (t, 0)), constant_values=NEG)
                              for t in range(t_new)], axis=1)
        cache_rows.append(jnp.pad(span, ((0, 0), (0, 0), (w_buf - SEG_KEYS * d, 0)),
                                  constant_values=NEG))
    n_rows = len(DILATIONS) * t_new
    bias_c = jnp.pad(jnp.concatenate(cache_rows, axis=1), ((0, 0), (0, SAMPLE_ROWS - n_rows), (0, 0)))

    rows, cols, which, steps = [], [], [], []
    for di, d in enumerate(DILATIONS):
        for t in range(t_new):
            for t2 in range(t + 1):
                if (t - t2) % d == 0:
                    rows.append(di * t_new + t)
                    cols.append(t2)
                    which.append(di)
                    steps.append((t - t2) // d)
    vals = jnp.stack(step_bias)[np.array(which), np.array(steps)]
    bias_n = jnp.full((N_HEADS, n_rows, LANES), NEG, F32).at[:, np.array(rows), np.array(cols)].set(vals.T)
    bias_n = jnp.pad(bias_n, ((0, 0), (0, SAMPLE_ROWS - n_rows), (0, 0)))
    return bias_c, bias_n


def _sample_mix(q, kn, vn, cache_k, cache_v, layer, bias_c, bias_n, gb, gc, hc, state, conv_w):
    db, t_new = q.shape[0], q.shape[1]
    w_buf = cache_k.shape[2]
    kt = cache_k.transpose(0, 1, 3, 4, 2)
    vt = cache_v.transpose(0, 1, 3, 4, 2)
    n_rows = len(DILATIONS) * t_new
    q_rows = jnp.pad(jnp.tile(q.transpose(0, 2, 1, 3), (1, 1, len(DILATIONS), 1)),
                     ((0, 0), (0, 0), (0, SAMPLE_ROWS - n_rows), (0, 0)))
    kn_pad = jnp.pad(kn.transpose(0, 2, 1, 3), ((0, 0), (0, 0), (0, SUBLANES - t_new), (0, 0)))
    vn_pad = jnp.pad(vn.transpose(0, 2, 1, 3), ((0, 0), (0, 0), (0, SUBLANES - t_new), (0, 0)))
    row = pl.BlockSpec((1, t_new, CONV_DIM), lambda b: (b, 0, 0))
    new_spec = pl.BlockSpec((1, N_HEADS, SUBLANES, HEAD_DIM), lambda b: (b, 0, 0, 0))
    win_spec = pl.BlockSpec((1, 1, N_HEADS, HEAD_DIM, w_buf), lambda b: (layer, b, 0, 0, 0))
    attn, conv, st = pl.pallas_call(
        functools.partial(_sample_mix_kernel, t_new=t_new),
        grid=(db,),
        in_specs=[pl.BlockSpec((1, N_HEADS, SAMPLE_ROWS, HEAD_DIM), lambda b: (b, 0, 0, 0)),
                  new_spec, new_spec, win_spec, win_spec,
                  pl.BlockSpec(bias_c.shape, lambda b: (0, 0, 0)),
                  pl.BlockSpec(bias_n.shape, lambda b: (0, 0, 0)),
                  row, row, row,
                  pl.BlockSpec((1, CONV_K - 1, CONV_DIM), lambda b: (b, 0, 0)),
                  pl.BlockSpec((CONV_K, CONV_DIM), lambda b: (0, 0))],
        out_specs=[pl.BlockSpec((1, N_HEADS, t_new, HEAD_DIM), lambda b: (b, 0, 0, 0)), row,
                   pl.BlockSpec((1, CONV_K - 1, CONV_DIM), lambda b: (b, 0, 0))],
        out_shape=[jax.ShapeDtypeStruct((db, N_HEADS, t_new, HEAD_DIM), F32),
                   jax.ShapeDtypeStruct((db, t_new, CONV_DIM), F32),
                   jax.ShapeDtypeStruct((db, CONV_K - 1, CONV_DIM), F32)],
        compiler_params=_params("arbitrary"),
        name="sample_mix",
    )(q_rows, kn_pad, vn_pad, kt, vt, bias_c, bias_n, gb, gc, hc, state, conv_w)
    return attn.transpose(0, 2, 1, 3), conv, st


def _to_slabs(x):
    n = x.shape[0]
    return x.reshape(n, N_SLABS, LANES).transpose(1, 0, 2)


def kernel(x_prompt, x_sample, cache_k, cache_v, state_conv, rel_bias, norm_mix, norm_ffn,
           norm_final, w_in, conv_w, w_out, w_router_group, w_router_expert, w_gate, w_up,
           w_down):
    batch, seq, _ = x_prompt.shape
    db, t_new, _ = x_sample.shape
    depth = w_in.shape[0]
    w_keep = min(MAX_WINDOW, seq)

    hp = x_prompt.reshape(batch * seq, D_MODEL)
    hs = x_sample.reshape(db * t_new, D_MODEL)
    step_bias = _step_bias(rel_bias)
    bias_prompt = _prompt_bias_tables(step_bias)
    bias_c, bias_n = _sample_bias_tables(step_bias, t_new, cache_k.shape[2])
    g_final = norm_final.reshape(1, D_MODEL)

    tm_p = 256
    n_s = db * t_new
    nt_p = batch * seq // tm_p
    pad_per_tile = N_EXPERTS * (RUN_ALIGN - 1)
    max_tiles = -(-(2 * (batch * seq + n_s) + (nt_p + 1) * pad_per_tile) // MOE_TILE) + N_EXPERTS
    buf_rows = max_tiles * MOE_TILE + _local_rows(tm_p)

    pc, sk, sv, sc = [], [], [], []
    windows = None
    for l in range(depth):
        w_in_b = w_in[l].astype(BF16)
        w_out_b = w_out[l].astype(BF16)
        w_router_t = jnp.pad(jnp.concatenate([w_router_expert[l], w_router_group[l]], axis=1).T,
                             ((0, ROUTER_ROWS - N_EXPERTS - N_GROUPS), (0, 0)))
        g_mix = norm_mix[l].reshape(1, D_MODEL)
        g_ffn = norm_ffn[l].reshape(1, D_MODEL)
        last = l == depth - 1

        q, k, v, conv, u_last, *windows = _in_proj(hp, g_mix, w_in_b, conv_w[l], seq, w_keep, l, depth,
                                                   windows)
        attn = _attention_prompt(q, k, v, bias_prompt, batch, seq)
        h1, xn, route, routet, cnt = _out_proj(attn, conv, hp, w_out_b, g_ffn, w_router_t,
                                               tm=4 * tm_p, sub=tm_p)
        pc.append(u_last)

        proj = _in_proj_sample(hs, g_mix, w_in_b)
        qs = proj[:, :ATTN_WIDTH].reshape(db, t_new, N_HEADS, HEAD_DIM)
        ks = proj[:, ATTN_WIDTH:2 * ATTN_WIDTH].reshape(db, t_new, N_HEADS, HEAD_DIM)
        vs = proj[:, 2 * ATTN_WIDTH:3 * ATTN_WIDTH].reshape(db, t_new, N_HEADS, HEAD_DIM)
        c0 = 3 * ATTN_WIDTH
        gb = proj[:, c0:c0 + CONV_DIM].reshape(db, t_new, CONV_DIM)
        gc = proj[:, c0 + CONV_DIM:c0 + 2 * CONV_DIM].reshape(db, t_new, CONV_DIM)
        hc = proj[:, c0 + 2 * CONV_DIM:].reshape(db, t_new, CONV_DIM)
        attn_s, conv_s, state_s = _sample_mix(qs, ks, vs, cache_k, cache_v, l, bias_c, bias_n,
                                              gb, gc, hc, state_conv[l], conv_w[l])
        h1s, xns, route_s, routet_s, cnt_s = _out_proj(
            _to_slabs(attn_s.reshape(n_s, ATTN_WIDTH)), conv_s.reshape(n_s, CONV_DIM), hs,
            w_out_b, g_ffn, w_router_t, tm=n_s, sub=n_s)

        table_out, table_in, first_tile, tiles_e, rows_e, total = _moe_plan(
            jnp.concatenate([cnt[:, :, 0], cnt_s[:, :, 0]], axis=0), max_tiles)
        xs = _dispatch(table_out[:nt_p], xn, route, tm_p, buf_rows)
        xs = _dispatch(table_out[nt_p:], xns, route_s, n_s, buf_rows, xs_prev=xs)
        eo = _experts(first_tile, tiles_e, rows_e, total, xs, w_gate, w_up, w_down, l)
        hp = _combine(table_in[:nt_p], h1, routet, g_final, eo, tm_p, last)
        hs = _combine(table_in[nt_p:], h1s, routet_s, g_final, eo, n_s, last)
        sk.append(ks)
        sv.append(vs)
        sc.append(state_s)

    k_win, v_win = (w.transpose(0, 1, 4, 2, 3) for w in windows)
    return (hp.reshape(batch, seq, D_MODEL), hs.reshape(db, t_new, D_MODEL),
            k_win, v_win, jnp.stack(pc),
            jnp.stack(sk), jnp.stack(sv), jnp.stack(sc))
```

```python
import functools
import math

import jax
import jax.numpy as jnp
import numpy as np
from jax import lax
from jax.experimental import pallas as pl
from jax.experimental.pallas import tpu as pltpu

F32 = jnp.float32
BF16 = jnp.bfloat16

D_MODEL = 1024
N_HEADS = 8
HEAD_DIM = 64
ATTN_WIDTH = N_HEADS * HEAD_DIM
CONV_DIM = D_MODEL - ATTN_WIDTH
CONV_K = 3
PROJ_WIDTH = 3 * ATTN_WIDTH + 3 * CONV_DIM
DILATIONS = (1, 4, 16)
SEG_KEYS = 128
Q_BLOCK = 128
MAX_WINDOW = 2048
N_BUCKETS = 32
MAX_DISTANCE = 2048
N_GROUPS = 4
EXPERTS_PER_GROUP = 8
N_EXPERTS = N_GROUPS * EXPERTS_PER_GROUP
D_EXPERT = D_MODEL // 4
EPS = 1e-6
NEG = -1e30
SCALE = 1.0 / math.sqrt(HEAD_DIM)

LANES = 128
SUBLANES = 8
N_SLABS = ATTN_WIDTH // LANES
VMEM_LIMIT = 48 * 1024 * 1024
REGROUP = 4


def _params(*sem):
    return pltpu.CompilerParams(dimension_semantics=sem, vmem_limit_bytes=VMEM_LIMIT)


def _rms(x, g):
    return x * lax.rsqrt(jnp.mean(x * x, axis=-1, keepdims=True) + EPS) * g


def _rel_bucket(dist):
    max_exact = N_BUCKETS // 2
    d_f = jnp.maximum(dist, 1).astype(F32)
    large = max_exact + (jnp.log(d_f / max_exact) / math.log(MAX_DISTANCE / max_exact)
                         * (N_BUCKETS - max_exact)).astype(jnp.int32)
    large = jnp.minimum(large, N_BUCKETS - 1)
    return jnp.where(dist < max_exact, dist, large)


def _in_proj_kernel(x_ref, g_ref, w_ref, cw_ref, *rest, tm, tiles_per_seq, win_tiles, chained):
    (q_ref, k_ref, v_ref, conv_ref, ulast_ref, kwin_ref, vwin_ref, ext_ref) = rest[2:] if chained else rest
    i = pl.program_id(0)
    xn = _rms(x_ref[...], g_ref[...])
    proj = jnp.dot(xn.astype(BF16), w_ref[...], preferred_element_type=F32)
    for s in range(N_SLABS):
        q_ref[s] = proj[:, s * LANES:(s + 1) * LANES]
        k_ref[s] = proj[:, ATTN_WIDTH + s * LANES:ATTN_WIDTH + (s + 1) * LANES]
        v_ref[s] = proj[:, 2 * ATTN_WIDTH + s * LANES:2 * ATTN_WIDTH + (s + 1) * LANES]

    @pl.when(i % tiles_per_seq >= tiles_per_seq - win_tiles)
    def _():
        heads_per_slab = LANES // HEAD_DIM
        for s in range(N_SLABS):
            for src, dst in ((k_ref, kwin_ref), (v_ref, vwin_ref)):
                dst[0, 0, heads_per_slab * s:heads_per_slab * (s + 1)] = (
                    src[s].T.reshape(heads_per_slab, HEAD_DIM, tm))
    c0 = 3 * ATTN_WIDTH
    gb = proj[:, c0:c0 + CONV_DIM]
    u = proj[:, c0 + CONV_DIM:c0 + 2 * CONV_DIM] * proj[:, c0 + 2 * CONV_DIM:c0 + 3 * CONV_DIM]

    @pl.when(i % tiles_per_seq == 0)
    def _():
        ext_ref[0:SUBLANES, :] = jnp.zeros((SUBLANES, CONV_DIM), F32)

    ext_ref[SUBLANES:SUBLANES + tm, :] = u
    u1 = ext_ref[SUBLANES - 1:SUBLANES - 1 + tm, :]
    u2 = ext_ref[SUBLANES - 2:SUBLANES - 2 + tm, :]
    cw = cw_ref[...]
    conv_ref[...] = gb * (cw[0:1] * u2 + cw[1:2] * u1 + cw[2:3] * u)
    ulast_ref[0] = ext_ref[tm + SUBLANES - 2:tm + SUBLANES, :]
    ext_ref[0:SUBLANES, :] = ext_ref[tm:tm + SUBLANES, :]


def _in_proj(x, g, w_bf16, conv_w, seq_len, w_keep, layer, depth, windows=None, tm=1024):
    n = x.shape[0]
    n_seq = n // seq_len
    slab = jax.ShapeDtypeStruct((N_SLABS, n, LANES), F32)
    slab_spec = pl.BlockSpec((N_SLABS, tm, LANES), lambda i: (0, i, 0))
    tiles_per_seq = seq_len // tm
    win_tiles = w_keep // tm
    win = jax.ShapeDtypeStruct((depth, n_seq, N_HEADS, HEAD_DIM, w_keep), F32)
    win_spec = pl.BlockSpec(
        (1, 1, N_HEADS, HEAD_DIM, tm),
        lambda i: (layer, i // tiles_per_seq, 0, 0,
                   jnp.maximum(i % tiles_per_seq - (tiles_per_seq - win_tiles), 0)))
    chained = windows is not None
    in_specs = [pl.BlockSpec((tm, D_MODEL), lambda i: (i, 0)),
                pl.BlockSpec((1, D_MODEL), lambda i: (0, 0)),
                pl.BlockSpec((D_MODEL, PROJ_WIDTH), lambda i: (0, 0)),
                pl.BlockSpec((CONV_K, CONV_DIM), lambda i: (0, 0))]
    args = [x, g, w_bf16, conv_w]
    if chained:
        in_specs += [pl.BlockSpec(memory_space=pl.ANY)] * 2
        args += list(windows)
    return pl.pallas_call(
        functools.partial(_in_proj_kernel, tm=tm, tiles_per_seq=tiles_per_seq, win_tiles=win_tiles,
                          chained=chained),
        grid=(n // tm,),
        in_specs=in_specs,
        out_specs=[slab_spec, slab_spec, slab_spec,
                   pl.BlockSpec((tm, CONV_DIM), lambda i: (i, 0)),
                   pl.BlockSpec((1, CONV_K - 1, CONV_DIM), lambda i: (i // tiles_per_seq, 0, 0)),
                   win_spec, win_spec],
        out_shape=[slab, slab, slab,
                   jax.ShapeDtypeStruct((n, CONV_DIM), F32),
                   jax.ShapeDtypeStruct((n_seq, CONV_K - 1, CONV_DIM), F32), win, win],
        scratch_shapes=[pltpu.VMEM((tm + SUBLANES, CONV_DIM), F32)],
        input_output_aliases={4: 5, 5: 6} if chained else {},
        compiler_params=_params("arbitrary"),
        name="in_proj_prompt",
    )(*args)


def _attn_kernel(q_ref, k_ref, v_ref, bias_ref, o_ref, q4, k4, v4, ktail, vtail, oacc, lacc, *, sb):
    j = pl.program_id(2)
    slot = j % 2
    prev = 1 - slot
    seg = sb // REGROUP

    @pl.when(j == 0)
    def _():
        k4[1] = jnp.zeros((sb, LANES), F32)
        v4[1] = jnp.zeros((sb, LANES), F32)
        ktail[...] = jnp.zeros((Q_BLOCK, LANES), F32)
        vtail[...] = jnp.zeros((Q_BLOCK, LANES), F32)

    for r in range(REGROUP):
        q4[r * seg:(r + 1) * seg, :] = q_ref[pl.ds(r, seg, stride=REGROUP), :] * SCALE
        k4[slot, r * seg:(r + 1) * seg, :] = k_ref[pl.ds(r, seg, stride=REGROUP), :]
        v4[slot, r * seg:(r + 1) * seg, :] = v_ref[pl.ds(r, seg, stride=REGROUP), :]

    lane = lax.broadcasted_iota(jnp.int32, (Q_BLOCK, LANES), 1)
    low = lane < HEAD_DIM
    ones = jnp.ones((2 * Q_BLOCK, LANES), BF16)

    def block(start):
        return pl.ds(start, Q_BLOCK)

    for di, d in enumerate(DILATIONS):
        nq = sb // (Q_BLOCK * d)

        def unit(u, di=di, d=d, nq=nq):
            r = u // nq
            n = u % nq
            qstart = r + d * Q_BLOCK * n
            if d == 1:
                here = block(n * Q_BLOCK)
                q = q_ref[here, :] * SCALE
                k_new, v_new = k_ref[here, :], v_ref[here, :]
                if n == 0:
                    k_old, v_old = ktail[...], vtail[...]
                else:
                    before = block((n - 1) * Q_BLOCK)
                    k_old, v_old = k_ref[before, :], v_ref[before, :]
            elif d == REGROUP:
                here = block(r * seg + n * Q_BLOCK)
                q = q4[here, :]
                k_new, v_new = k4[slot, here, :], v4[slot, here, :]
                if n == 0:
                    last = block(r * seg + seg - Q_BLOCK)
                    k_old, v_old = k4[prev, last, :], v4[prev, last, :]
                else:
                    before = block(r * seg + (n - 1) * Q_BLOCK)
                    k_old, v_old = k4[slot, before, :], v4[slot, before, :]
            else:
                rows = pl.ds((u % REGROUP) * seg + u // REGROUP, Q_BLOCK, stride=REGROUP)
                q = q4[rows, :]
                k_new, v_new = k4[slot, rows, :], v4[slot, rows, :]
                k_old, v_old = k4[prev, rows, :], v4[prev, rows, :]
            kk = jnp.concatenate([k_old, k_new], axis=0).astype(BF16)
            vv = jnp.concatenate([v_old, v_new], axis=0).astype(BF16)
            qm = jnp.concatenate([jnp.where(low, q, 0.0), jnp.where(low, 0.0, q)],
                                 axis=0).astype(BF16)
            s = lax.dot_general(qm, kk, (((1,), (1,)), ((), ())),
                                preferred_element_type=F32)
            first = (j == 0).astype(jnp.int32) if n == 0 else 0
            s = s + bias_ref[di, first]
            m = jnp.max(s, axis=-1, keepdims=True)
            p = jnp.exp(s - m).astype(BF16)
            pv = jnp.dot(p, jnp.concatenate([vv, ones], axis=1),
                         preferred_element_type=F32)
            o_sel = jnp.where(low, pv[0:Q_BLOCK, 0:LANES], pv[Q_BLOCK:, 0:LANES])
            l_sel = jnp.where(low, pv[0:Q_BLOCK, LANES:], pv[Q_BLOCK:, LANES:])
            m_sel = jnp.where(low, jnp.broadcast_to(m[0:Q_BLOCK], (Q_BLOCK, LANES)),
                              jnp.broadcast_to(m[Q_BLOCK:], (Q_BLOCK, LANES)))
            oacc[di, pl.ds(qstart, Q_BLOCK, stride=d), :] = o_sel / l_sel
            lacc[di, pl.ds(qstart, Q_BLOCK, stride=d), :] = m_sel + jnp.log(l_sel)

        for u in range(sb // Q_BLOCK):
            unit(u)

    for c in range(sb // 256):
        rows = slice(c * 256, (c + 1) * 256)
        l0, l1, l2 = lacc[0, rows, :], lacc[1, rows, :], lacc[2, rows, :]
        mm = jnp.maximum(jnp.maximum(l0, l1), l2)
        w0, w1, w2 = jnp.exp(l0 - mm), jnp.exp(l1 - mm), jnp.exp(l2 - mm)
        num = w0 * oacc[0, rows, :] + w1 * oacc[1, rows, :] + w2 * oacc[2, rows, :]
        o_ref[rows, :] = num / (w0 + w1 + w2)
    ktail[...] = k_ref[sb - Q_BLOCK:sb, :]
    vtail[...] = v_ref[sb - Q_BLOCK:sb, :]


def _step_bias(rel_bias):
    steps = jnp.arange(SEG_KEYS + 1)
    return [rel_bias[_rel_bucket(steps * d)].astype(F32) for d in DILATIONS]


def _prompt_bias_tables(step_bias):
    width = 3 * Q_BLOCK
    k_loc = jnp.arange(2 * Q_BLOCK)
    tables = []
    for b in step_bias:
        g = jnp.concatenate([b[::-1].T, jnp.full((N_HEADS, width - SEG_KEYS), NEG, F32)], axis=1)
        flat = jnp.broadcast_to(g[:, None, :], (N_HEADS, Q_BLOCK, width + 1)).reshape(N_HEADS, -1)
        x = flat[:, :Q_BLOCK * width].reshape(N_HEADS, Q_BLOCK, width)
        t = x[:, :, :2 * Q_BLOCK]
        t_first = jnp.where(k_loc[None, None, :] >= Q_BLOCK, t, NEG)
        tables.append(jnp.stack([t.reshape(N_SLABS, 2 * Q_BLOCK, 2 * Q_BLOCK),
                                 t_first.reshape(N_SLABS, 2 * Q_BLOCK, 2 * Q_BLOCK)]))
    return jnp.stack(tables)


def _attention_prompt(q, k, v, bias_tab, n_seq, seq_len, sb=2048):
    assert DILATIONS == (1, REGROUP, REGROUP * REGROUP)
    n = q.shape[1]
    nsb = seq_len // sb
    blk = pl.BlockSpec((None, sb, LANES), lambda b, s, j: (s, b * nsb + j, 0))
    return pl.pallas_call(
        functools.partial(_attn_kernel, sb=sb),
        grid=(n_seq, N_SLABS, nsb),
        in_specs=[blk, blk, blk,
                  pl.BlockSpec((len(DILATIONS), 2, None, 2 * Q_BLOCK, 2 * Q_BLOCK),
                               lambda b, s, j: (0, 0, s, 0, 0))],
        out_specs=blk,
        out_shape=jax.ShapeDtypeStruct((N_SLABS, n, LANES), F32),
        scratch_shapes=[pltpu.VMEM((sb, LANES), F32),
                        pltpu.VMEM((2, sb, LANES), F32), pltpu.VMEM((2, sb, LANES), F32),
                        pltpu.VMEM((Q_BLOCK, LANES), F32), pltpu.VMEM((Q_BLOCK, LANES), F32),
                        pltpu.VMEM((len(DILATIONS), sb, LANES), F32),
                        pltpu.VMEM((len(DILATIONS), sb, LANES), F32)],
        compiler_params=_params("arbitrary", "arbitrary", "arbitrary"),
        name="attention_prompt",
    )(q, k, v, bias_tab)


ROUTER_ROWS = 48
MOE_TILE = 256
RUN_ALIGN = 2 * SUBLANES
TABLE_LANES = 128
EXPERT_BUFFERS = 6


def _local_rows(tm):
    return -(-(2 * tm + N_EXPERTS * (RUN_ALIGN - 1)) // LANES) * LANES


def _route(lg):
    tm = lg.shape[1]
    gl = lg[N_EXPERTS:N_EXPERTS + N_GROUPS]
    grow = lax.broadcasted_iota(jnp.int32, (N_GROUPS, tm), 0)
    gmax = jnp.max(gl, axis=0, keepdims=True)
    gsum = jnp.sum(jnp.exp(gl - gmax), axis=0, keepdims=True)
    gidx = jnp.min(jnp.where(gl == gmax, grow, N_GROUPS), axis=0, keepdims=True)
    p_grp = 1.0 / gsum
    el = lg[0:N_EXPERTS]
    e = lax.broadcasted_iota(jnp.int32, (N_EXPERTS, tm), 0)
    in_grp = (e // EXPERTS_PER_GROUP) == gidx
    elog = jnp.where(in_grp, el, -jnp.inf)
    emax = jnp.max(elog, axis=0, keepdims=True)
    eexp = jnp.exp(elog - emax)
    eprob = jnp.where(in_grp, eexp / jnp.sum(eexp, axis=0, keepdims=True), -1.0)
    p1 = jnp.max(eprob, axis=0, keepdims=True)
    i1 = jnp.min(jnp.where(eprob == p1, e, N_EXPERTS), axis=0, keepdims=True)
    rest = jnp.where(e == i1, -1.0, eprob)
    p2 = jnp.max(rest, axis=0, keepdims=True)
    i2 = jnp.min(jnp.where(rest == p2, e, N_EXPERTS), axis=0, keepdims=True)
    tot = p1 + p2
    return e == i1, e == i2, p_grp * (p1 / tot), p_grp * (p2 / tot)


def _split_bf16(x):
    hi = x.astype(BF16)
    return hi, (x - hi.astype(F32)).astype(BF16)


def _out_proj_kernel(attn_ref, conv_ref, h_ref, wo_ref, g_ref, wr_ref, upper_ref, lower_ref,
                     h1_ref, xn_ref, route_ref, routet_ref, cnt_ref, *, tm, sub):
    contract_last = (((1,), (1,)), ((), ()))
    mix = jnp.concatenate([attn_ref[s] for s in range(N_SLABS)] + [conv_ref[...]], axis=1)
    h1_ref[...] = h_ref[...] + jnp.dot(mix.astype(BF16), wo_ref[...], preferred_element_type=F32)
    for t in range(tm // sub):
        rs = slice(t * sub, (t + 1) * sub)
        xn = _rms(h1_ref[rs, :], g_ref[...])
        xh, xl = _split_bf16(xn)
        xn_ref[rs, :] = xh
        both = lax.dot_general(wr_ref[...], xh, contract_last, preferred_element_type=F32)
        lg = (both[:ROUTER_ROWS] + both[ROUTER_ROWS:]
              + lax.dot_general(wr_ref[:ROUTER_ROWS, :], xl, contract_last, preferred_element_type=F32))
        sel1, sel2, g1, g2 = _route(lg)
        chosen = jnp.where(sel1 | sel2, 1.0, 0.0)
        rank = jnp.dot(chosen.astype(BF16), upper_ref[...], preferred_element_type=F32)
        cnt = jnp.sum(chosen, axis=1, keepdims=True)
        cnt_pad = jnp.floor((cnt + (RUN_ALIGN - 1)) * (1.0 / RUN_ALIGN)) * RUN_ALIGN
        cnt_b = jnp.broadcast_to(cnt_pad, (N_EXPERTS, LANES))
        start = jnp.dot(lower_ref[...], cnt_b.astype(BF16), preferred_element_type=F32)
        pos = start[:, 0:1] + rank
        pos1 = jnp.sum(jnp.where(sel1, pos, 0.0), axis=0, keepdims=True)
        pos2 = jnp.sum(jnp.where(sel2, pos, 0.0), axis=0, keepdims=True)
        rows = jnp.concatenate([pos1, pos2, g1, g2, jnp.zeros((SUBLANES - 4, sub), F32)], axis=0)
        route_ref[t] = rows
        routet_ref[rs, :] = jnp.concatenate([rows, jnp.zeros((LANES - SUBLANES, sub), F32)], axis=0).T
        cnt_ref[t] = cnt_b


def _out_proj(attn, conv, h, wo_bf16, g, w_router_t, tm, sub):
    n = h.shape[0]
    w_router = jnp.concatenate(_split_bf16(w_router_t), axis=0)
    idx = jnp.arange(sub)
    upper = (idx[:, None] < idx[None, :]).astype(BF16)
    eidx = jnp.arange(N_EXPERTS)
    lower = (eidx[None, :] < eidx[:, None]).astype(BF16)
    const = lambda shape: pl.BlockSpec(shape, lambda i: (0,) * len(shape))
    per_step = tm // sub
    return pl.pallas_call(
        functools.partial(_out_proj_kernel, tm=tm, sub=sub),
        grid=(n // tm,),
        in_specs=[pl.BlockSpec((N_SLABS, tm, LANES), lambda i: (0, i, 0)),
                  pl.BlockSpec((tm, CONV_DIM), lambda i: (i, 0)),
                  pl.BlockSpec((tm, D_MODEL), lambda i: (i, 0)),
                  const((D_MODEL, D_MODEL)), const((1, D_MODEL)),
                  const((2 * ROUTER_ROWS, D_MODEL)),
                  const((sub, sub)), const((N_EXPERTS, N_EXPERTS))],
        out_specs=[pl.BlockSpec((tm, D_MODEL), lambda i: (i, 0)),
                   pl.BlockSpec((tm, D_MODEL), lambda i: (i, 0)),
                   pl.BlockSpec((per_step, SUBLANES, sub), lambda i: (i, 0, 0)),
                   pl.BlockSpec((tm, LANES), lambda i: (i, 0)),
                   pl.BlockSpec((per_step, N_EXPERTS, LANES), lambda i: (i, 0, 0))],
        out_shape=[jax.ShapeDtypeStruct((n, D_MODEL), F32),
                   jax.ShapeDtypeStruct((n, D_MODEL), BF16),
                   jax.ShapeDtypeStruct((n // sub, SUBLANES, sub), F32),
                   jax.ShapeDtypeStruct((n, LANES), F32),
                   jax.ShapeDtypeStruct((n // sub, N_EXPERTS, LANES), F32)],
        compiler_params=_params("arbitrary"),
        name="out_proj",
    )(attn, conv, h, wo_bf16, g, w_router, upper, lower)


def _moe_plan(cnt_tiles, max_tiles):
    c = cnt_tiles.astype(jnp.int32)
    rows_e = jnp.sum(c, axis=0)
    tiles_e = (rows_e + MOE_TILE - 1) // MOE_TILE
    cum = jnp.cumsum(tiles_e)
    total = cum[-1]
    first_tile = cum - tiles_e

    run_start = jnp.cumsum(c, axis=0) - c
    local_start = jnp.cumsum(c, axis=1) - c
    local_end = local_start + c
    row = jnp.arange(TABLE_LANES, dtype=jnp.int32) * RUN_ALIGN
    inside = (local_start[:, None, :] <= row[None, :, None]) & (row[None, :, None] < local_end[:, None, :])
    base = first_tile[None] * MOE_TILE + run_start - local_start
    n_chunks = jnp.sum(c, axis=1) // RUN_ALIGN
    table = jnp.sum(jnp.where(inside, base[:, None, :] + row[None, :, None], 0), axis=-1)
    live = jnp.arange(TABLE_LANES)[None] < n_chunks[:, None]
    table_out = jnp.where(live, table, max_tiles * MOE_TILE + row[None])
    table_in = jnp.where(live, table, row[None])
    i32 = lambda a: a.astype(jnp.int32)
    return (i32(table_out), i32(table_in), i32(first_tile), i32(tiles_e), i32(rows_e),
            i32(total).reshape(1))


def _dispatch_kernel(tbl_ref, xn_ref, route_ref, *rest, tm, rl, chained):
    xs_hbm, lsort, sem = rest[1:] if chained else rest
    i = pl.program_id(0)
    slot = i % 2
    rows = route_ref[0]
    pos1 = rows[0:1].astype(jnp.int32)
    pos2 = rows[1:2].astype(jnp.int32)
    p = lax.broadcasted_iota(jnp.int32, (rl, tm), 0)
    one_hot = jnp.where((p == pos1) | (p == pos2), 1.0, 0.0).astype(BF16)
    lsort[slot] = jnp.dot(one_hot, xn_ref[...], preferred_element_type=F32).astype(BF16)

    def tile_copies_done(s):
        pltpu.make_async_copy(lsort.at[s], xs_hbm.at[pl.ds(0, rl), :], sem.at[s]).wait()

    @pl.when(i > 0)
    def _():
        tile_copies_done(1 - slot)

    for q in range(rl // RUN_ALIGN):
        row = pl.multiple_of(tbl_ref[i * TABLE_LANES + q], RUN_ALIGN)
        pltpu.make_async_copy(lsort.at[slot, pl.ds(q * RUN_ALIGN, RUN_ALIGN), :],
                              xs_hbm.at[pl.ds(row, RUN_ALIGN), :], sem.at[slot]).start()

    @pl.when(i == pl.num_programs(0) - 1)
    def _():
        tile_copies_done(slot)


def _dispatch(table, xn, route, tm, total_rows, xs_prev=None):
    nt = xn.shape[0] // tm
    rl = _local_rows(tm)
    chained = xs_prev is not None
    in_specs = [pl.BlockSpec((tm, D_MODEL), lambda i, t: (i, 0)),
                pl.BlockSpec((1, SUBLANES, tm), lambda i, t: (i, 0, 0))]
    args = [table.reshape(-1), xn, route]
    if chained:
        in_specs.append(pl.BlockSpec(memory_space=pl.ANY))
        args.append(xs_prev)
    return pl.pallas_call(
        functools.partial(_dispatch_kernel, tm=tm, rl=rl, chained=chained),
        grid_spec=pltpu.PrefetchScalarGridSpec(
            num_scalar_prefetch=1, grid=(nt,), in_specs=in_specs,
            out_specs=pl.BlockSpec(memory_space=pl.ANY),
            scratch_shapes=[pltpu.VMEM((2, rl, D_MODEL), BF16), pltpu.SemaphoreType.DMA((2,))]),
        out_shape=jax.ShapeDtypeStruct((total_rows, D_MODEL), BF16),
        input_output_aliases={3: 0} if chained else {},
        compiler_params=_params("arbitrary"),
        name="moe_dispatch",
    )(*args)


def _expert_kernel(first_ref, tiles_ref, rows_ref, tot_ref, xs_hbm, wg_ref, wu_ref, wd_ref, eo_hbm,
                   xbuf, obuf, sem_in, sem_out):
    e = pl.program_id(0)
    first_tile = first_ref[e]
    n_tiles = tiles_ref[e]
    total = tot_ref[0]

    def in_copy(g):
        return pltpu.make_async_copy(
            xs_hbm.at[pl.ds(pl.multiple_of(g * MOE_TILE, MOE_TILE), MOE_TILE), :],
            xbuf.at[g % EXPERT_BUFFERS], sem_in.at[g % EXPERT_BUFFERS])

    def out_copy(g):
        return pltpu.make_async_copy(
            obuf.at[g % EXPERT_BUFFERS],
            eo_hbm.at[pl.ds(pl.multiple_of(g * MOE_TILE, MOE_TILE), MOE_TILE), :],
            sem_out.at[g % EXPERT_BUFFERS])

    lookahead = EXPERT_BUFFERS - 2

    @pl.when(e == 0)
    def _():
        for g0 in range(lookahead):
            @pl.when(g0 < total)
            def _():
                in_copy(g0).start()

    def acquire(g, j):
        @pl.when(g + lookahead < total)
        def _():
            in_copy(g + lookahead).start()

        in_copy(g).wait()

        @pl.when(g >= EXPERT_BUFFERS)
        def _():
            out_copy(g - EXPERT_BUFFERS).wait()

        row = lax.broadcasted_iota(jnp.int32, (MOE_TILE, 1), 0)
        x = xbuf[g % EXPERT_BUFFERS]
        return jnp.where(row < rows_ref[e] - j * MOE_TILE, x, jnp.zeros_like(x))

    def mlp(x):
        gate = jnp.dot(x, wg_ref[0, 0].astype(BF16), preferred_element_type=F32)
        up = jnp.dot(x, wu_ref[0, 0].astype(BF16), preferred_element_type=F32)
        hdn = gate * jax.nn.sigmoid(gate) * up
        return jnp.dot(hdn.astype(BF16), wd_ref[0, 0].astype(BF16),
                       preferred_element_type=F32).astype(BF16)

    def pair(jp, carry):
        j = 2 * jp
        g = first_tile + j
        x0 = acquire(g, j)
        x1 = acquire(g + 1, j + 1)
        y = mlp(jnp.concatenate([x0, x1], axis=0))
        obuf[g % EXPERT_BUFFERS] = y[:MOE_TILE]
        obuf[(g + 1) % EXPERT_BUFFERS] = y[MOE_TILE:]
        out_copy(g).start()
        out_copy(g + 1).start()
        return carry

    lax.fori_loop(0, n_tiles // 2, pair, 0)

    @pl.when(n_tiles % 2 == 1)
    def _():
        j = n_tiles - 1
        g = first_tile + j
        obuf[g % EXPERT_BUFFERS] = mlp(acquire(g, j))
        out_copy(g).start()

    @pl.when(e == pl.num_programs(0) - 1)
    def _():
        for back in range(EXPERT_BUFFERS, 0, -1):
            @pl.when(total >= back)
            def _():
                out_copy(total - back).wait()


def _experts(first_tile, tiles_e, rows_e, total, xs, w_gate, w_up, w_down, layer):
    weight = lambda e, f, t, r, n: (layer, e, 0, 0)
    return pl.pallas_call(
        _expert_kernel,
        grid_spec=pltpu.PrefetchScalarGridSpec(
            num_scalar_prefetch=4, grid=(N_EXPERTS,),
            in_specs=[pl.BlockSpec(memory_space=pl.ANY),
                      pl.BlockSpec((1, 1, D_MODEL, D_EXPERT), weight),
                      pl.BlockSpec((1, 1, D_MODEL, D_EXPERT), weight),
                      pl.BlockSpec((1, 1, D_EXPERT, D_MODEL), weight)],
            out_specs=pl.BlockSpec(memory_space=pl.ANY),
            scratch_shapes=[pltpu.VMEM((EXPERT_BUFFERS, MOE_TILE, D_MODEL), BF16),
                            pltpu.VMEM((EXPERT_BUFFERS, MOE_TILE, D_MODEL), BF16),
                            pltpu.SemaphoreType.DMA((EXPERT_BUFFERS,)),
                            pltpu.SemaphoreType.DMA((EXPERT_BUFFERS,))]),
        out_shape=jax.ShapeDtypeStruct(xs.shape, BF16),
        compiler_params=_params("arbitrary"),
        name="moe_experts",
    )(first_tile, tiles_e, rows_e, total, xs, w_gate, w_up, w_down)


def _combine_kernel(tbl_ref, h1_ref, routet_ref, gf_ref, eo_hbm, o_ref, leo, sem,
                    *, tm, rl, final_norm):
    i = pl.program_id(0)
    n = pl.num_programs(0)
    slot = i % 2

    def gather(tile, s):
        for q in range(rl // RUN_ALIGN):
            row = pl.multiple_of(tbl_ref[tile * TABLE_LANES + q], RUN_ALIGN)
            pltpu.make_async_copy(eo_hbm.at[pl.ds(row, RUN_ALIGN), :],
                                  leo.at[s, pl.ds(q * RUN_ALIGN, RUN_ALIGN), :], sem.at[s]).start()

    @pl.when(i == 0)
    def _():
        gather(0, 0)

    @pl.when(i + 1 < n)
    def _():
        gather(i + 1, 1 - slot)

    pltpu.make_async_copy(eo_hbm.at[pl.ds(0, rl), :], leo.at[slot], sem.at[slot]).wait()

    rt = routet_ref[...]
    pos1 = rt[:, 0:1].astype(jnp.int32)
    pos2 = rt[:, 1:2].astype(jnp.int32)
    lane = lax.broadcasted_iota(jnp.int32, (tm, rl), 1)
    weights = jnp.where(lane == pos1, rt[:, 2:3], 0.0) + jnp.where(lane == pos2, rt[:, 3:4], 0.0)
    y = jnp.dot(weights.astype(BF16), leo[slot], preferred_element_type=F32)
    out = h1_ref[...] + y
    if final_norm:
        out = _rms(out, gf_ref[...])
    o_ref[...] = out


def _combine(table, h1, routet, g_final, eo, tm, final_norm):
    n = h1.shape[0]
    rl = _local_rows(tm)
    return pl.pallas_call(
        functools.partial(_combine_kernel, tm=tm, rl=rl, final_norm=final_norm),
        grid_spec=pltpu.PrefetchScalarGridSpec(
            num_scalar_prefetch=1, grid=(n // tm,),
            in_specs=[pl.BlockSpec((tm, D_MODEL), lambda i, t: (i, 0)),
                      pl.BlockSpec((tm, LANES), lambda i, t: (i, 0)),
                      pl.BlockSpec((1, D_MODEL), lambda i, t: (0, 0)),
                      pl.BlockSpec(memory_space=pl.ANY)],
            out_specs=pl.BlockSpec((tm, D_MODEL), lambda i, t: (i, 0)),
            scratch_shapes=[pltpu.VMEM((2, rl, D_MODEL), BF16), pltpu.SemaphoreType.DMA((2,))]),
        out_shape=jax.ShapeDtypeStruct((n, D_MODEL), F32),
        compiler_params=_params("arbitrary"),
        name="moe_combine",
    )(table.reshape(-1), h1, routet, g_final, eo)


def _in_proj_sample_kernel(x_ref, g_ref, w_ref, o_ref):
    xn = _rms(x_ref[...], g_ref[...])
    o_ref[...] = jnp.dot(xn.astype(BF16), w_ref[...], preferred_element_type=F32)


def _in_proj_sample(x, g, w_bf16):
    n = x.shape[0]
    return pl.pallas_call(
        _in_proj_sample_kernel,
        out_shape=jax.ShapeDtypeStruct((n, PROJ_WIDTH), F32),
        compiler_params=pltpu.CompilerParams(vmem_limit_bytes=VMEM_LIMIT),
        name="in_proj_sample",
    )(x, g, w_bf16)


def _sample_mix_kernel(q_ref, kn_ref, vn_ref, kt_ref, vt_ref,
                       bc_ref, bn_ref, gb_ref, gc_ref, hc_ref, st_ref, cw_ref,
                       o_ref, conv_ref, st_out_ref, *, t_new):
    contract_last = (((1,), (1,)), ((), ()))
    no_rows = jnp.zeros((LANES - SUBLANES, HEAD_DIM), F32)
    cw = cw_ref[...]
    for b in range(q_ref.shape[0]):
        for h in range(N_HEADS):
            q = (q_ref[b, h] * SCALE).astype(BF16)
            s_c = jnp.dot(q, kt_ref[0, b, h].astype(BF16), preferred_element_type=F32) + bc_ref[h]
            kn = jnp.concatenate([kn_ref[b, h], no_rows], axis=0).astype(BF16)
            vn = jnp.concatenate([vn_ref[b, h], no_rows], axis=0).astype(BF16)
            s_n = lax.dot_general(q, kn, contract_last, preferred_element_type=F32) + bn_ref[h]
            m = jnp.maximum(jnp.max(s_c, axis=-1, keepdims=True), jnp.max(s_n, axis=-1, keepdims=True))
            p_c = jnp.exp(s_c - m)
            p_n = jnp.exp(s_n - m)
            den = jnp.sum(p_c, axis=-1, keepdims=True) + jnp.sum(p_n, axis=-1, keepdims=True)
            num = (lax.dot_general(p_c.astype(BF16), vt_ref[0, b, h].astype(BF16), contract_last,
                                   preferred_element_type=F32)
                   + jnp.dot(p_n.astype(BF16), vn, preferred_element_type=F32))
            out = num / den
            lse = m + jnp.log(den)
            ls = [lse[i * t_new:(i + 1) * t_new] for i in range(len(DILATIONS))]
            os_ = [out[i * t_new:(i + 1) * t_new] for i in range(len(DILATIONS))]
            mm = jnp.maximum(jnp.maximum(ls[0], ls[1]), ls[2])
            w = [jnp.exp(l - mm) for l in ls]
            o_ref[b, h] = (w[0] * os_[0] + w[1] * os_[1] + w[2] * os_[2]) / (w[0] + w[1] + w[2])
        u = gc_ref[b] * hc_ref[b]
        st = st_ref[b]
        rows = [st[j:j + 1] for j in range(CONV_K - 1)] + [u[t:t + 1] for t in range(t_new)]
        gb = gb_ref[b]
        conv_ref[b] = jnp.concatenate(
            [gb[t:t + 1] * sum(cw[j:j + 1] * rows[t + j] for j in range(CONV_K)) for t in range(t_new)],
            axis=0)
        st_out_ref[b] = jnp.concatenate(rows[t_new:], axis=0)


SAMPLE_ROWS = 16
SAMPLE_SEQS_PER_STEP = 2


def _sample_bias_tables(step_bias, t_new, w_buf):
    cache_rows = []
    for b, d in zip(step_bias, DILATIONS):
        back = b[1:][::-1].T
        if d >= t_new:
            residue = np.eye(d, dtype=bool)[:t_new]
            span = jnp.where(residue[None, :, None, :], back[:, None, :, None], NEG)
            span = span.reshape(N_HEADS, t_new, SEG_KEYS * d)
        else:
            assert d == 1, "dilations between 1 and the number of new tokens are not supported"
            span = jnp.stack([jnp.pad(back[:, :SEG_KEYS - t], ((0, 0), (t, 0)), constant_values=NEG)
                              for t in range(t_new)], axis=1)
        cache_rows.append(jnp.pad(span, ((0, 0), (0, 0), (w_buf - SEG_KEYS * d, 0)),
                                  constant_values=NEG))
    n_rows = len(DILATIONS) * t_new
    bias_c = jnp.pad(jnp.concatenate(cache_rows, axis=1), ((0, 0), (0, SAMPLE_ROWS - n_rows), (0, 0)))

    rows, cols, which, steps = [], [], [], []
    for di, d in enumerate(DILATIONS):
        for t in range(t_new):
            for t2 in range(t + 1):
                if (t - t2) % d == 0:
                    rows.append(di * t_new + t)
                    cols.append(t2)
                    which.append(di)
                    steps.append((t - t2) // d)
    vals = jnp.stack(step_bias)[np.array(which), np.array(steps)]
    bias_n = jnp.full((N_HEADS, n_rows, LANES), NEG, F32).at[:, np.array(rows), np.array(cols)].set(vals.T)
    bias_n = jnp.pad(bias_n, ((0, 0), (0, SAMPLE_ROWS - n_rows), (0, 0)))
    return bias_c, bias_n


def _sample_mix(q, kn, vn, cache_k, cache_v, layer, bias_c, bias_n, gb, gc, hc, state, conv_w):
    db, t_new = q.shape[0], q.shape[1]
    w_buf = cache_k.shape[2]
    kt = cache_k.transpose(0, 1, 3, 4, 2)
    vt = cache_v.transpose(0, 1, 3, 4, 2)
    n_rows = len(DILATIONS) * t_new
    q_rows = jnp.pad(jnp.tile(q.transpose(0, 2, 1, 3), (1, 1, len(DILATIONS), 1)),
                     ((0, 0), (0, 0), (0, SAMPLE_ROWS - n_rows), (0, 0)))
    kn_pad = jnp.pad(kn.transpose(0, 2, 1, 3), ((0, 0), (0, 0), (0, SUBLANES - t_new), (0, 0)))
    vn_pad = jnp.pad(vn.transpose(0, 2, 1, 3), ((0, 0), (0, 0), (0, SUBLANES - t_new), (0, 0)))
    per = SAMPLE_SEQS_PER_STEP
    row = pl.BlockSpec((per, t_new, CONV_DIM), lambda b: (b, 0, 0))
    new_spec = pl.BlockSpec((per, N_HEADS, SUBLANES, HEAD_DIM), lambda b: (b, 0, 0, 0))
    win_spec = pl.BlockSpec((1, per, N_HEADS, HEAD_DIM, w_buf), lambda b: (layer, b, 0, 0, 0))
    attn, conv, st = pl.pallas_call(
        functools.partial(_sample_mix_kernel, t_new=t_new),
        grid=(db // per,),
        in_specs=[pl.BlockSpec((per, N_HEADS, SAMPLE_ROWS, HEAD_DIM), lambda b: (b, 0, 0, 0)),
                  new_spec, new_spec, win_spec, win_spec,
                  pl.BlockSpec(bias_c.shape, lambda b: (0, 0, 0)),
                  pl.BlockSpec(bias_n.shape, lambda b: (0, 0, 0)),
                  row, row, row,
                  pl.BlockSpec((per, CONV_K - 1, CONV_DIM), lambda b: (b, 0, 0)),
                  pl.BlockSpec((CONV_K, CONV_DIM), lambda b: (0, 0))],
        out_specs=[pl.BlockSpec((per, N_HEADS, t_new, HEAD_DIM), lambda b: (b, 0, 0, 0)), row,
                   pl.BlockSpec((per, CONV_K - 1, CONV_DIM), lambda b: (b, 0, 0))],
        out_shape=[jax.ShapeDtypeStruct((db, N_HEADS, t_new, HEAD_DIM), F32),
                   jax.ShapeDtypeStruct((db, t_new, CONV_DIM), F32),
                   jax.ShapeDtypeStruct((db, CONV_K - 1, CONV_DIM), F32)],
        compiler_params=_params("arbitrary"),
        name="sample_mix",
    )(q_rows, kn_pad, vn_pad, kt, vt, bias_c, bias_n, gb, gc, hc, state, conv_w)
    return attn.transpose(0, 2, 1, 3), conv, st


def _to_slabs(x):
    n = x.shape[0]
    return x.reshape(n, N_SLABS, LANES).transpose(1, 0, 2)


def kernel(x_prompt, x_sample, cache_k, cache_v, state_conv, rel_bias, norm_mix, norm_ffn,
           norm_final, w_in, conv_w, w_out, w_router_group, w_router_expert, w_gate, w_up,
           w_down):
    batch, seq, _ = x_prompt.shape
    db, t_new, _ = x_sample.shape
    depth = w_in.shape[0]
    w_keep = min(MAX_WINDOW, seq)

    hp = x_prompt.reshape(batch * seq, D_MODEL)
    hs = x_sample.reshape(db * t_new, D_MODEL)
    step_bias = _step_bias(rel_bias)
    bias_prompt = _prompt_bias_tables(step_bias)
    bias_c, bias_n = _sample_bias_tables(step_bias, t_new, cache_k.shape[2])
    g_final = norm_final.reshape(1, D_MODEL)

    tm_p = 256
    n_s = db * t_new
    nt_p = batch * seq // tm_p
    pad_per_tile = N_EXPERTS * (RUN_ALIGN - 1)
    max_tiles = -(-(2 * (batch * seq + n_s) + (nt_p + 1) * pad_per_tile) // MOE_TILE) + N_EXPERTS
    buf_rows = max_tiles * MOE_TILE + _local_rows(tm_p)

    pc, sk, sv, sc = [], [], [], []
    windows = None
    for l in range(depth):
        w_in_b = w_in[l].astype(BF16)
        w_out_b = w_out[l].astype(BF16)
        w_router_t = jnp.pad(jnp.concatenate([w_router_expert[l], w_router_group[l]], axis=1).T,
                             ((0, ROUTER_ROWS - N_EXPERTS - N_GROUPS), (0, 0)))
        g_mix = norm_mix[l].reshape(1, D_MODEL)
        g_ffn = norm_ffn[l].reshape(1, D_MODEL)
        last = l == depth - 1

        q, k, v, conv, u_last, *windows = _in_proj(hp, g_mix, w_in_b, conv_w[l], seq, w_keep, l, depth,
                                                   windows)
        attn = _attention_prompt(q, k, v, bias_prompt, batch, seq)
        h1, xn, route, routet, cnt = _out_proj(attn, conv, hp, w_out_b, g_ffn, w_router_t,
                                               tm=4 * tm_p, sub=tm_p)
        pc.append(u_last)

        proj = _in_proj_sample(hs, g_mix, w_in_b)
        qs = proj[:, :ATTN_WIDTH].reshape(db, t_new, N_HEADS, HEAD_DIM)
        ks = proj[:, ATTN_WIDTH:2 * ATTN_WIDTH].reshape(db, t_new, N_HEADS, HEAD_DIM)
        vs = proj[:, 2 * ATTN_WIDTH:3 * ATTN_WIDTH].reshape(db, t_new, N_HEADS, HEAD_DIM)
        c0 = 3 * ATTN_WIDTH
        gb = proj[:, c0:c0 + CONV_DIM].reshape(db, t_new, CONV_DIM)
        gc = proj[:, c0 + CONV_DIM:c0 + 2 * CONV_DIM].reshape(db, t_new, CONV_DIM)
        hc = proj[:, c0 + 2 * CONV_DIM:].reshape(db, t_new, CONV_DIM)
        attn_s, conv_s, state_s = _sample_mix(qs, ks, vs, cache_k, cache_v, l, bias_c, bias_n,
                                              gb, gc, hc, state_conv[l], conv_w[l])
        h1s, xns, route_s, routet_s, cnt_s = _out_proj(
            _to_slabs(attn_s.reshape(n_s, ATTN_WIDTH)), conv_s.reshape(n_s, CONV_DIM), hs,
            w_out_b, g_ffn, w_router_t, tm=n_s, sub=n_s)

        table_out, table_in, first_tile, tiles_e, rows_e, total = _moe_plan(
            jnp.concatenate([cnt[:, :, 0], cnt_s[:, :, 0]], axis=0), max_tiles)
        xs = _dispatch(table_out[:nt_p], xn, route, tm_p, buf_rows)
        xs = _dispatch(table_out[nt_p:], xns, route_s, n_s, buf_rows, xs_prev=xs)
        eo = _experts(first_tile, tiles_e, rows_e, total, xs, w_gate, w_up, w_down, l)
        hp = _combine(table_in[:nt_p], h1, routet, g_final, eo, tm_p, last)
        hs = _combine(table_in[nt_p:], h1s, routet_s, g_final, eo, n_s, last)
        sk.append(ks)
        sv.append(vs)
        sc.append(state_s)

    k_win, v_win = (w.transpose(0, 1, 4, 2, 3) for w in windows)
    return (hp.reshape(batch, seq, D_MODEL), hs.reshape(db, t_new, D_MODEL),
            k_win, v_win, jnp.stack(pc),
            jnp.stack(sk), jnp.stack(sv), jnp.stack(sc))
```
